```python
import jax, jax.numpy as jnp
from jax import lax
import numpy as np

D_MODEL = 2048
BATCH = 4
SEQ = 2048
DEPTH = 1
DEC_BATCH = 128
DEC_SEQ = 1
PAST_LEN = 16384
PAGE_SIZE = 128

N_META = 16
D_SSM = D_MODEL // 2
SSM_GROUP = 16
N_SSM_GROUPS = D_SSM // SSM_GROUP
SSM_STATE = 64
D_CONV = D_MODEL - D_SSM
N_CONV_HEADS = 8
CONV_W = 3
D_IN_PROJ = D_SSM + 3 * D_CONV
D_FF = -(-8 * D_MODEL // (3 * 256)) * 256
DEEPNORM_ALPHA = (2 * DEPTH) ** 0.25
DEEPNORM_BETA = (8 * DEPTH) ** -0.25
LN_EPS = 1e-5
RMS_EPS = 1e-6

kernel_name = "hymba_s5_shortconv_deepnorm_step"


def layer_norm(x, g, b):
    xf = x.astype(jnp.float32)
    mu = jnp.mean(xf, -1, keepdims=True)
    var = jnp.mean(jnp.square(xf - mu), -1, keepdims=True)
    y = (xf - mu) * lax.rsqrt(var + LN_EPS) * g.astype(jnp.float32) + b.astype(jnp.float32)
    return y.astype(x.dtype)


def rms_norm(x, g):
    xf = x.astype(jnp.float32)
    y = xf * lax.rsqrt(jnp.mean(jnp.square(xf), -1, keepdims=True) + RMS_EPS) * g.astype(jnp.float32)
    return y.astype(x.dtype)


def s5_discretise(a_re, a_im, log_dt, b_re, b_im):
    f32 = jnp.float32
    a_re = a_re.astype(f32); a_im = a_im.astype(f32)
    b_re = b_re.astype(f32); b_im = b_im.astype(f32)
    dt = jnp.exp(log_dt.astype(f32))[:, None]
    mag = jnp.exp(dt * a_re)
    ang = dt * a_im
    ab_re = mag * jnp.cos(ang)
    ab_im = mag * jnp.sin(ang)
    n_re = ab_re - 1.0
    n_im = ab_im
    den = a_re * a_re + a_im * a_im
    f_re = (n_re * a_re + n_im * a_im) / den
    f_im = (n_im * a_re - n_re * a_im) / den
    bb_re = f_re[..., None] * b_re - f_im[..., None] * b_im
    bb_im = f_re[..., None] * b_im + f_im[..., None] * b_re
    return ab_re, ab_im, bb_re, bb_im


def _complex_affine_combine(e1, e2):
    a1r, a1i, b1r, b1i = e1
    a2r, a2i, b2r, b2i = e2
    ar = a1r * a2r - a1i * a2i
    ai = a1r * a2i + a1i * a2r
    br = a2r * b1r - a2i * b1i + b2r
    bi = a2r * b1i + a2i * b1r + b2i
    return ar, ai, br, bi


def s5_mixer(u, s0_re, s0_im, a_re, a_im, log_dt, b_re, b_im, c_re, c_im, d_skip, w_glu):
    f32 = jnp.float32
    n, l, _ = u.shape
    uf = u.astype(f32)
    ug = uf.reshape(n, l, N_SSM_GROUPS, SSM_GROUP)
    ab_re, ab_im, bb_re, bb_im = s5_discretise(a_re, a_im, log_dt, b_re, b_im)
    x_re = jnp.einsum('nlgc,gpc->nlgp', ug, bb_re)
    x_im = jnp.einsum('nlgc,gpc->nlgp', ug, bb_im)
    s0_re = s0_re.astype(f32); s0_im = s0_im.astype(f32)
    x_re = x_re.at[:, 0].add(ab_re * s0_re - ab_im * s0_im)
    x_im = x_im.at[:, 0].add(ab_re * s0_im + ab_im * s0_re)
    a_re_t = jnp.broadcast_to(ab_re, (1, l, N_SSM_GROUPS, SSM_STATE))
    a_im_t = jnp.broadcast_to(ab_im, (1, l, N_SSM_GROUPS, SSM_STATE))
    _, _, s_re, s_im = lax.associative_scan(_complex_affine_combine, (a_re_t, a_im_t, x_re, x_im), axis=1)
    y = (jnp.einsum('nlgp,gcp->nlgc', s_re, c_re.astype(f32))
         - jnp.einsum('nlgp,gcp->nlgc', s_im, c_im.astype(f32)))
    y = y.reshape(n, l, D_SSM) + d_skip.astype(f32) * uf
    y = jax.nn.gelu(y).astype(u.dtype)
    y = y * jax.nn.sigmoid(y @ w_glu)
    return y, s_re[:, -1], s_im[:, -1]


def short_conv_mixer(b_gate, c_gate, h, buf, conv_w):
    v = c_gate * h
    vp = jnp.concatenate([buf.astype(v.dtype), v], axis=1)
    l = v.shape[1]
    conv = conv_w[0] * vp[:, 0:l]
    for k in range(1, CONV_W):
        conv = conv + conv_w[k] * vp[:, k:k + l]
    return b_gate * conv, vp[:, -(CONV_W - 1):]


def hybrid_layer(x, s_re, s_im, conv_buf, w_in, a_re, a_im, log_dt, b_re, b_im, c_re, c_im,
                 d_skip, w_glu, conv_w, g_ssm_out, g_conv_out, w_out, ln1_g, ln1_b,
                 w_gate, w_up, w_down, ln2_g, ln2_b):
    proj = x @ w_in
    u, b_gate, c_gate, h = jnp.split(proj, [D_SSM, D_SSM + D_CONV, D_SSM + 2 * D_CONV], axis=-1)
    y_ssm, s_re_new, s_im_new = s5_mixer(u, s_re, s_im, a_re, a_im, log_dt, b_re, b_im,
                                         c_re, c_im, d_skip, w_glu)
    y_conv, buf_new = short_conv_mixer(b_gate, c_gate, h, conv_buf, conv_w)
    merged = jnp.concatenate([rms_norm(y_ssm, g_ssm_out), rms_norm(y_conv, g_conv_out)], axis=-1)
    x = layer_norm(DEEPNORM_ALPHA * x + merged @ w_out, ln1_g, ln1_b)
    ffn = (jax.nn.silu(x @ w_gate) * (x @ w_up)) @ w_down
    x = layer_norm(DEEPNORM_ALPHA * x + ffn, ln2_g, ln2_b)
    return x, s_re_new, s_im_new, buf_new


def setup_inputs(seed: int = 0) -> dict:
    key = jax.random.key(seed)
    ks = jax.random.split(key, 32)
    f32 = jnp.float32
    nrm = lambda k, shape, s: jax.random.normal(k, shape, f32) * s
    G, P, C = N_SSM_GROUPS, SSM_STATE, SSM_GROUP
    n_idx = jnp.arange(P, dtype=f32)
    inp = {}
    inp['x_prompt'] = nrm(ks[0], (BATCH, SEQ, D_MODEL), 1.0)
    inp['x_sample'] = nrm(ks[1], (DEC_BATCH, DEC_SEQ, D_MODEL), 1.0)
    inp['state_ssm_re'] = nrm(ks[2], (DEPTH, DEC_BATCH, G, P), 0.5)
    inp['state_ssm_im'] = nrm(ks[3], (DEPTH, DEC_BATCH, G, P), 0.5)
    inp['state_conv'] = nrm(ks[4], (DEPTH, DEC_BATCH, CONV_W - 1, D_CONV), 1.0)
    inp['meta_tokens'] = nrm(ks[5], (N_META, D_MODEL), 1.0)
    inp['ln0_g'] = 1.0 + nrm(ks[6], (D_MODEL,), 0.02)
    inp['ln0_b'] = nrm(ks[7], (D_MODEL,), 0.02)
    inp['w_in'] = nrm(ks[8], (DEPTH, D_MODEL, D_IN_PROJ), D_MODEL ** -0.5)
    inp['ssm_a_re'] = -0.5 + nrm(ks[9], (DEPTH, G, P), 0.01)
    inp['ssm_a_im'] = jnp.pi * n_idx + nrm(ks[10], (DEPTH, G, P), 0.01)
    inp['ssm_log_dt'] = jax.random.uniform(ks[11], (DEPTH, G), f32, np.log(1e-3), np.log(1e-1))
    inp['ssm_b_re'] = nrm(ks[12], (DEPTH, G, P, C), (2 * C) ** -0.5)
    inp['ssm_b_im'] = nrm(ks[13], (DEPTH, G, P, C), (2 * C) ** -0.5)
    inp['ssm_c_re'] = nrm(ks[14], (DEPTH, G, C, P), (2 * P) ** -0.5)
    inp['ssm_c_im'] = nrm(ks[15], (DEPTH, G, C, P), (2 * P) ** -0.5)
    inp['ssm_d'] = nrm(ks[16], (DEPTH, D_SSM), 1.0)
    inp['ssm_w_glu'] = nrm(ks[17], (DEPTH, D_SSM, D_SSM), D_SSM ** -0.5)
    inp['conv_w'] = nrm(ks[18], (DEPTH, CONV_W, D_CONV), CONV_W ** -0.5)
    inp['g_ssm_out'] = 1.0 + nrm(ks[19], (DEPTH, D_SSM), 0.02)
    inp['g_conv_out'] = 1.0 + nrm(ks[20], (DEPTH, D_CONV), 0.02)
    inp['w_out'] = nrm(ks[21], (DEPTH, D_MODEL, D_MODEL), D_MODEL ** -0.5 * DEEPNORM_BETA)
    inp['ln1_g'] = 1.0 + nrm(ks[22], (DEPTH, D_MODEL), 0.02)
    inp['ln1_b'] = nrm(ks[23], (DEPTH, D_MODEL), 0.02)
    inp['w_gate'] = nrm(ks[24], (DEPTH, D_MODEL, D_FF), D_MODEL ** -0.5)
    inp['w_up'] = nrm(ks[25], (DEPTH, D_MODEL, D_FF), D_MODEL ** -0.5)
    inp['w_down'] = nrm(ks[26], (DEPTH, D_FF, D_MODEL), D_FF ** -0.5 * DEEPNORM_BETA)
    inp['ln2_g'] = 1.0 + nrm(ks[27], (DEPTH, D_MODEL), 0.02)
    inp['ln2_b'] = nrm(ks[28], (DEPTH, D_MODEL), 0.02)
    return inp


def reference(x_prompt, x_sample, state_ssm_re, state_ssm_im, state_conv, meta_tokens, ln0_g, ln0_b,
              w_in, ssm_a_re, ssm_a_im, ssm_log_dt, ssm_b_re, ssm_b_im, ssm_c_re, ssm_c_im, ssm_d,
              ssm_w_glu, conv_w, g_ssm_out, g_conv_out, w_out, ln1_g, ln1_b, w_gate, w_up, w_down,
              ln2_g, ln2_b):
    n_p = x_prompt.shape[0]
    n_s = x_sample.shape[0]
    meta = jnp.broadcast_to(meta_tokens.astype(x_prompt.dtype)[None], (n_p, N_META, D_MODEL))
    hp = layer_norm(jnp.concatenate([meta, x_prompt], axis=1), ln0_g, ln0_b)
    hs = layer_norm(x_sample, ln0_g, ln0_b)
    zeros_ssm = jnp.zeros((n_p, N_SSM_GROUPS, SSM_STATE), jnp.float32)
    zeros_conv = jnp.zeros((n_p, CONV_W - 1, D_CONV), x_prompt.dtype)
    p_re, p_im, p_cv, s_re_l, s_im_l, s_cv_l = [], [], [], [], [], []
    for l in range(DEPTH):
        w = (w_in[l], ssm_a_re[l], ssm_a_im[l], ssm_log_dt[l], ssm_b_re[l], ssm_b_im[l],
             ssm_c_re[l], ssm_c_im[l], ssm_d[l], ssm_w_glu[l], conv_w[l], g_ssm_out[l],
             g_conv_out[l], w_out[l], ln1_g[l], ln1_b[l], w_gate[l], w_up[l], w_down[l],
             ln2_g[l], ln2_b[l])
        hp, r, i, c = hybrid_layer(hp, zeros_ssm, zeros_ssm, zeros_conv, *w)
        p_re.append(r); p_im.append(i); p_cv.append(c)
        hs, r, i, c = hybrid_layer(hs, state_ssm_re[l], state_ssm_im[l], state_conv[l], *w)
        s_re_l.append(r); s_im_l.append(i); s_cv_l.append(c)
    y_prompt = hp[:, N_META:]
    y_sample = hs
    sd = state_ssm_re.dtype
    new_ssm_re_prompt = jnp.stack(p_re).astype(sd)
    new_ssm_im_prompt = jnp.stack(p_im).astype(sd)
    new_conv_prompt = jnp.stack(p_cv).astype(state_conv.dtype)
    new_ssm_re_sample = jnp.stack(s_re_l).astype(sd)
    new_ssm_im_sample = jnp.stack(s_im_l).astype(sd)
    new_conv_sample = jnp.stack(s_cv_l).astype(state_conv.dtype)
    return (y_prompt, y_sample, new_ssm_re_prompt, new_ssm_im_prompt, new_conv_prompt,
            new_ssm_re_sample, new_ssm_im_sample, new_conv_sample)
```

```python
import functools

import jax
import jax.numpy as jnp
import numpy as np
from jax import lax
from jax.experimental import pallas as pl
from jax.experimental.pallas import tpu as pltpu

D_MODEL = 2048
D_SSM = 1024
D_CONV = 1024
N_GROUPS = 64
GROUP = 16
N_STATE = 64
N_META = 16
CONV_W = 3
D_FF = 5632
LN_EPS = 1e-5
RMS_EPS = 1e-6
ALPHA = 2.0 ** 0.25

N_BATCH = 4
SEQ = 2048
N_SAMPLE = 128
CHUNK = 16
N_CHUNKS = SEQ // CHUNK + 1
ROW8 = 8
CW = CHUNK * GROUP
SW = 2 * N_STATE

TM = 512
N_PROMPT_ROWS = N_BATCH * SEQ
N_PROMPT_TILES = N_PROMPT_ROWS // TM
N_TILES = N_PROMPT_TILES + 1
M_PAD = N_TILES * TM
SAMPLE_ROW0 = N_META
TF = 512
GB = 8
VMEM_LIMIT = 56 * 1024 * 1024

F32 = jnp.float32
BF16 = jnp.bfloat16


def _layer_norm(x, g, b):
    mu = jnp.mean(x, axis=-1, keepdims=True)
    xc = x - mu
    var = jnp.mean(xc * xc, axis=-1, keepdims=True)
    return xc * lax.rsqrt(var + LN_EPS) * g + b


def _rms_norm(x, g):
    return x * lax.rsqrt(jnp.mean(x * x, axis=-1, keepdims=True) + RMS_EPS) * g


def _gelu_tanh(x):
    c = np.float32(np.sqrt(2.0 / np.pi))
    return 0.5 * x * (1.0 + jnp.tanh(c * (x + np.float32(0.044715) * (x * x * x))))


def _sigmoid(x):
    return 1.0 / (1.0 + jnp.exp(-x))


def _dot(a, b):
    return jnp.dot(a, b, preferred_element_type=F32)


def _dot_nt(a, b):
    return lax.dot_general(a, b, (((1,), (1,)), ((), ())), preferred_element_type=F32)


def _ssm_prep_kernel(logdt_ref, ac_ref, ar_ref, bt_ref, btT_ref, ct_ref,
                     m_ref, wst_ref, cpow_ref, c0_ref, avec_ref):
    dt = jnp.exp(logdt_ref[0])

    def discretise(a_re, a_im):
        mag = jnp.exp(dt * a_re)
        ang = dt * a_im
        ab_re = mag * jnp.cos(ang)
        ab_im = mag * jnp.sin(ang)
        n_re = ab_re - 1.0
        den = a_re * a_re + a_im * a_im
        f_re = (n_re * a_re + ab_im * a_im) / den
        f_im = (ab_im * a_re - n_re * a_im) / den
        return ab_re, ab_im, f_re, f_im

    acol = ac_ref[0]
    ab_re, ab_im, f_re, f_im = discretise(acol[:, 0:1], acol[:, 1:2])
    bt_re, bt_im = bt_ref[0, 0], bt_ref[0, 1]
    bb_re = f_re * bt_re - f_im * bt_im
    bb_im = f_re * bt_im + f_im * bt_re
    ct_re, ct_im = ct_ref[0, 0], ct_ref[0, 1]

    pw = [(jnp.ones_like(ab_re), jnp.zeros_like(ab_re))]
    for _ in range(CHUNK):
        pr, pi = pw[-1]
        pw.append((pr * ab_re - pi * ab_im, pr * ab_im + pi * ab_re))

    slot = lax.broadcasted_iota(jnp.int32, (N_STATE, CW), 1) // GROUP

    def spread(d_of_slot):
        re = jnp.zeros((N_STATE, CW), F32)
        im = jnp.zeros((N_STATE, CW), F32)
        for s in range(CHUNK):
            pr, pi = pw[d_of_slot(s)]
            re = jnp.where(slot == s, pr, re)
            im = jnp.where(slot == s, pi, im)
        return re, im

    wr, wi = spread(lambda s: CHUNK - 1 - s)
    wst_ref[0, 0:N_STATE, :] = (wr * bb_re - wi * bb_im).astype(BF16)
    wst_ref[0, N_STATE:SW, :] = (wr * bb_im + wi * bb_re).astype(BF16)

    qr, qi = spread(lambda s: s)
    cq_re = ct_re * qr - ct_im * qi
    cq_nim = -(ct_re * qi + ct_im * qr)
    c0_ref[0, 0:N_STATE, :] = cq_re.astype(BF16)
    c0_ref[0, N_STATE:SW, :] = cq_nim.astype(BF16)
    p1r = qr * ab_re - qi * ab_im
    p1i = qr * ab_im + qi * ab_re
    cpow_ref[0, 0:N_STATE, :] = (ct_re * p1r - ct_im * p1i).astype(BF16)
    cpow_ref[0, N_STATE:SW, :] = (-(ct_re * p1i + ct_im * p1r)).astype(BF16)

    arow = ar_ref[0]
    _, _, g_re, g_im = discretise(arow[0:1, :], arow[1:2, :])
    bT_re, bT_im = btT_ref[0, 0], btT_ref[0, 1]
    bbT = jnp.concatenate([g_re * bT_re - g_im * bT_im, g_re * bT_im + g_im * bT_re], axis=1)
    cq = jnp.concatenate([cq_re, cq_nim], axis=0)
    kt = jnp.dot(bbT, cq, precision=lax.Precision.HIGHEST, preferred_element_type=F32)
    lane = lax.broadcasted_iota(jnp.int32, (GROUP, CW), 1)
    for i in range(CHUNK):
        blk = kt if i == 0 else jnp.where(lane >= i * GROUP, pltpu.roll(kt, i * GROUP, axis=1), 0.0)
        m_ref[0, i * GROUP:(i + 1) * GROUP, :] = blk.astype(BF16)

    ab_r, ab_i, _, _ = discretise(arow[0:1, :], arow[1:2, :])
    cr, ci = ab_r, ab_i
    for _ in range(4):
        cr, ci = cr * cr - ci * ci, 2.0 * cr * ci
    avec_ref[0, 0:1, :] = jnp.concatenate([cr, cr], axis=1)
    avec_ref[0, 1:2, :] = jnp.concatenate([-ci, ci], axis=1)
    avec_ref[0, 2:3, :] = jnp.concatenate([ab_r, ab_r], axis=1)
    avec_ref[0, 3:4, :] = jnp.concatenate([-ab_i, ab_i], axis=1)


def _ssm_prep(log_dt, a_re, a_im, b_re, b_im, c_re, c_im):
    g, p, c = N_GROUPS, N_STATE, GROUP
    logdt = log_dt.reshape(g, 1, 1)
    ac = jnp.stack([a_re, a_im], axis=-1)
    ar = jnp.stack([a_re, a_im], axis=1)
    bt = jnp.tile(jnp.stack([b_re, b_im], axis=1), (1, 1, 1, CHUNK))
    btT = jnp.stack([b_re, b_im], axis=1).transpose(0, 1, 3, 2)
    ct = jnp.tile(jnp.stack([c_re, c_im], axis=1).transpose(0, 1, 3, 2), (1, 1, 1, CHUNK))
    blk = lambda *s: pl.BlockSpec((1,) + s, lambda i: (i,) + (0,) * len(s))
    return pl.pallas_call(
        _ssm_prep_kernel,
        grid=(g,),
        in_specs=[blk(1, 1), blk(p, 2), blk(2, p), blk(2, p, CW), blk(2, c, p), blk(2, p, CW)],
        out_specs=[blk(CW, CW), blk(SW, CW), blk(SW, CW), blk(SW, CW), blk(4, SW)],
        out_shape=[jax.ShapeDtypeStruct((g, CW, CW), BF16),
                   jax.ShapeDtypeStruct((g, SW, CW), BF16),
                   jax.ShapeDtypeStruct((g, SW, CW), BF16),
                   jax.ShapeDtypeStruct((g, SW, CW), BF16),
                   jax.ShapeDtypeStruct((g, 4, SW), F32)],
        name="ssm_prep",
    )(logdt, ac, ar, bt, btT, ct)


def _in_proj_kernel(xp_ref, xe_ref, g0_ref, b0_ref, w_ref, u_ref, bg_ref, v_ref, h_scr, cg_scr):
    i = pl.program_id(0)
    j = pl.program_id(1)

    @pl.when((j == 0) & (i < N_PROMPT_TILES))
    def _():
        h_scr[...] = _layer_norm(xp_ref[...], g0_ref[...], b0_ref[...]).astype(BF16)

    @pl.when((j == 0) & (i == N_PROMPT_TILES))
    def _():
        h_scr[...] = _layer_norm(xe_ref[...], g0_ref[...], b0_ref[...]).astype(BF16)

    r = _dot(h_scr[...], w_ref[...])

    @pl.when(j == 0)
    def _():
        u_ref[...] = r.astype(BF16)

    @pl.when(j == 1)
    def _():
        bg_ref[...] = r.astype(BF16)

    @pl.when(j == 2)
    def _():
        cg_scr[...] = r

    @pl.when(j == 3)
    def _():
        v_ref[...] = (cg_scr[...] * r).astype(BF16)


def _in_proj(xp, xe, g0, b0, w_in):
    row = lambda i, j: (i, 0)
    out = jax.ShapeDtypeStruct((M_PAD, D_SSM), BF16)
    return pl.pallas_call(
        _in_proj_kernel,
        grid=(N_TILES, 4),
        in_specs=[pl.BlockSpec((TM, D_MODEL), lambda i, j: (jnp.minimum(i, N_PROMPT_TILES - 1), 0)),
                  pl.BlockSpec((TM, D_MODEL), lambda i, j: (0, 0)),
                  pl.BlockSpec((1, D_MODEL), lambda i, j: (0, 0)),
                  pl.BlockSpec((1, D_MODEL), lambda i, j: (0, 0)),
                  pl.BlockSpec((D_MODEL, D_SSM), lambda i, j: (0, j))],
        out_specs=[pl.BlockSpec((TM, D_SSM), row)] * 3,
        out_shape=[out, out, out],
        scratch_shapes=[pltpu.VMEM((TM, D_MODEL), BF16), pltpu.VMEM((TM, D_CONV), F32)],
        compiler_params=pltpu.CompilerParams(
            dimension_semantics=("arbitrary", "arbitrary"), vmem_limit_bytes=VMEM_LIMIT),
        name="in_proj",
    )(xp, xe, g0, b0, w_in)


def _ssm_scan_kernel(u_ref, m_ref, wst_ref, cpow_ref, c0_ref, avec_ref, us_ref, s0_ref,
                     y_ref, sfin_ref, ys_ref, snew_ref, x_scr, sp_scr):
    rows = N_CHUNKS * ROW8

    def swap_halves(s):
        return pltpu.roll(s, N_STATE, axis=1)

    for gl in range(GB):
        x_scr[gl] = _dot_nt(u_ref[gl], wst_ref[gl])

    ar16 = [avec_ref[gl, 0:1, :] for gl in range(GB)]
    ax16 = [avec_ref[gl, 1:2, :] for gl in range(GB)]

    def step(k, carry):
        r0 = pl.multiple_of(k * ROW8, ROW8)
        new = []
        for gl in range(GB):
            s = carry[gl]
            sp_scr[gl, pl.ds(r0, ROW8), :] = s
            new.append(ar16[gl] * s + ax16[gl] * swap_halves(s) + x_scr[gl, pl.ds(r0, ROW8), :])
        return tuple(new)

    zero = jnp.zeros((ROW8, SW), F32)
    fin = lax.fori_loop(0, N_CHUNKS, step, (zero,) * GB)

    for gl in range(GB):
        sfin_ref[gl] = fin[gl]
        y_ref[gl] = (_dot(u_ref[gl], m_ref[gl])
                     + _dot(sp_scr[gl].astype(BF16), cpow_ref[gl]))
        bbar_t = wst_ref[gl, :, (CHUNK - 1) * GROUP:CW]
        s0 = s0_ref[gl]
        sn = (avec_ref[gl, 2:3, :] * s0 + avec_ref[gl, 3:4, :] * swap_halves(s0)
              + _dot_nt(us_ref[gl], bbar_t))
        snew_ref[gl] = sn
        ys_ref[gl] = _dot(sn.astype(BF16), c0_ref[gl, :, 0:GROUP])
    del rows


def _ssm_scan(u_chunks, m, wst, cpow, c0, avec, u_s, s0):
    rows = N_CHUNKS * ROW8
    blk = lambda *s: pl.BlockSpec((GB,) + s, lambda i: (i,) + (0,) * len(s))
    return pl.pallas_call(
        _ssm_scan_kernel,
        grid=(N_GROUPS // GB,),
        in_specs=[blk(rows, CW), blk(CW, CW), blk(SW, CW), blk(SW, CW), blk(SW, CW), blk(4, SW),
                  blk(N_SAMPLE, GROUP), blk(N_SAMPLE, SW)],
        out_specs=[blk(rows, CW), blk(ROW8, SW), blk(N_SAMPLE, GROUP), blk(N_SAMPLE, SW)],
        out_shape=[jax.ShapeDtypeStruct((N_GROUPS, rows, CW), F32),
                   jax.ShapeDtypeStruct((N_GROUPS, ROW8, SW), F32),
                   jax.ShapeDtypeStruct((N_GROUPS, N_SAMPLE, GROUP), F32),
                   jax.ShapeDtypeStruct((N_GROUPS, N_SAMPLE, SW), F32)],
        scratch_shapes=[pltpu.VMEM((GB, rows, SW), F32), pltpu.VMEM((GB, rows, SW), F32)],
        compiler_params=pltpu.CompilerParams(
            dimension_semantics=("arbitrary",), vmem_limit_bytes=VMEM_LIMIT),
        name="ssm_scan",
    )(u_chunks, m, wst, cpow, c0, avec, u_s, s0)


HALO = 16
TMX = 256
NX_PROMPT = N_PROMPT_ROWS // TMX
NX_TILES = M_PAD // TMX


def _mix_out_kernel(y_ref, u_ref, bg_ref, v_ref, vh_ref, p1_ref, p2_ref, xp_ref, xe_ref,
                    g0_ref, b0_ref, dskip_ref, wglu_ref, cw_ref, gs_ref, gc_ref, wout_ref,
                    g1_ref, b1_ref, x1_ref, x1b_ref, vpad_scr):
    i = pl.program_id(0)
    is_extra = i == NX_PROMPT

    ys = _gelu_tanh(y_ref[...] + dskip_ref[...] * u_ref[...].astype(F32))
    gate = _dot(ys.astype(BF16), wglu_ref[...])
    r_ssm = _rms_norm(ys * _sigmoid(gate), gs_ref[...])

    vf = v_ref[...].astype(F32)
    halo = jnp.where(is_extra, 0.0, vh_ref[...].astype(F32)[HALO - 2:HALO, :])
    vpad_scr[ROW8 - 2:ROW8, :] = halo
    vpad_scr[ROW8:ROW8 + TMX, :] = vf
    row = lax.broadcasted_iota(jnp.int32, (TMX, 1), 0)
    sample = is_extra & (row >= SAMPLE_ROW0)
    vm1 = jnp.where(sample, p1_ref[...], vpad_scr[ROW8 - 1:ROW8 - 1 + TMX, :])
    vm2 = jnp.where(sample, p2_ref[...], vpad_scr[ROW8 - 2:ROW8 - 2 + TMX, :])
    conv = cw_ref[0:1, :] * vm2 + cw_ref[1:2, :] * vm1 + cw_ref[2:3, :] * vf
    r_conv = _rms_norm(bg_ref[...].astype(F32) * conv, gc_ref[...])

    mo = (_dot(r_ssm.astype(BF16), wout_ref[0:D_SSM, :])
          + _dot(r_conv.astype(BF16), wout_ref[D_SSM:D_MODEL, :]))

    def finish(x_ref):
        h0 = _layer_norm(x_ref[...], g0_ref[...], b0_ref[...])
        x1 = _layer_norm(ALPHA * h0 + mo, g1_ref[...], b1_ref[...])
        x1_ref[...] = x1
        x1b_ref[...] = x1.astype(BF16)

    @pl.when(i < NX_PROMPT)
    def _():
        finish(xp_ref)

    @pl.when(i >= NX_PROMPT)
    def _():
        finish(xe_ref)


def _mix_out(y, u, bg, v, p1, p2, xp, xe, g0, b0, dskip, wglu, cw, gs, gc, wout, g1, b1):
    row = lambda i: (i, 0)
    const = lambda i: (0, 0)
    tiles_per_seq = SEQ // TMX
    meta_block = N_PROMPT_ROWS // HALO

    def halo_map(i):
        prev = i * (TMX // HALO) - 1
        return (jnp.where(i % tiles_per_seq == 0, meta_block, prev), 0)

    vec = lambda n: pl.BlockSpec((1, n), const)
    return pl.pallas_call(
        _mix_out_kernel,
        grid=(NX_TILES,),
        in_specs=[pl.BlockSpec((TMX, D_SSM), row), pl.BlockSpec((TMX, D_SSM), row),
                  pl.BlockSpec((TMX, D_CONV), row), pl.BlockSpec((TMX, D_CONV), row),
                  pl.BlockSpec((HALO, D_CONV), halo_map),
                  pl.BlockSpec((TMX, D_CONV), const), pl.BlockSpec((TMX, D_CONV), const),
                  pl.BlockSpec((TMX, D_MODEL), lambda i: (jnp.minimum(i, NX_PROMPT - 1), 0)),
                  pl.BlockSpec((TMX, D_MODEL), lambda i: (jnp.maximum(i - NX_PROMPT, 0), 0)),
                  vec(D_MODEL), vec(D_MODEL), vec(D_SSM),
                  pl.BlockSpec((D_SSM, D_SSM), const), pl.BlockSpec((CONV_W, D_CONV), const),
                  vec(D_SSM), vec(D_CONV), pl.BlockSpec((D_MODEL, D_MODEL), const),
                  vec(D_MODEL), vec(D_MODEL)],
        out_specs=[pl.BlockSpec((TMX, D_MODEL), row), pl.BlockSpec((TMX, D_MODEL), row)],
        out_shape=[jax.ShapeDtypeStruct((M_PAD, D_MODEL), F32),
                   jax.ShapeDtypeStruct((M_PAD, D_MODEL), BF16)],
        scratch_shapes=[pltpu.VMEM((ROW8 + TMX, D_CONV), F32)],
        compiler_params=pltpu.CompilerParams(
            dimension_semantics=("arbitrary",), vmem_limit_bytes=VMEM_LIMIT),
        name="mix_out",
    )(y, u, bg, v, v, p1, p2, xp, xe, g0, b0, dskip, wglu, cw, gs, gc, wout, g1, b1)


def _ffn_kernel(xb_ref, x_ref, wg_ref, wu_ref, wd_ref, g2_ref, b2_ref, yp_ref, ye_ref, acc_scr):
    i = pl.program_id(0)
    f = pl.program_id(1)

    @pl.when(f == 0)
    def _():
        acc_scr[...] = jnp.zeros_like(acc_scr)

    xb = xb_ref[...]
    gate = _dot(xb, wg_ref[...])
    up = _dot(xb, wu_ref[...])
    act = (gate * _sigmoid(gate) * up).astype(BF16)
    acc_scr[...] += _dot(act, wd_ref[...])

    last = f == pl.num_programs(1) - 1

    def finish(o_ref):
        o_ref[...] = _layer_norm(ALPHA * x_ref[...] + acc_scr[...], g2_ref[...], b2_ref[...])

    @pl.when(last & (i < N_PROMPT_TILES))
    def _():
        finish(yp_ref)

    @pl.when(last & (i == N_PROMPT_TILES))
    def _():
        finish(ye_ref)


def _ffn(x1b, x1, wg, wu, wd, g2, b2):
    row = lambda i, f: (i, 0)
    const = lambda i, f: (0, 0)
    return pl.pallas_call(
        _ffn_kernel,
        grid=(N_TILES, D_FF // TF),
        in_specs=[pl.BlockSpec((TM, D_MODEL), row), pl.BlockSpec((TM, D_MODEL), row),
                  pl.BlockSpec((D_MODEL, TF), lambda i, f: (0, f)),
                  pl.BlockSpec((D_MODEL, TF), lambda i, f: (0, f)),
                  pl.BlockSpec((TF, D_MODEL), lambda i, f: (f, 0)),
                  pl.BlockSpec((1, D_MODEL), const), pl.BlockSpec((1, D_MODEL), const)],
        out_specs=[pl.BlockSpec((TM, D_MODEL), lambda i, f: (jnp.minimum(i, N_PROMPT_TILES - 1), 0)),
                   pl.BlockSpec((TM, D_MODEL), const)],
        out_shape=[jax.ShapeDtypeStruct((N_PROMPT_ROWS, D_MODEL), F32),
                   jax.ShapeDtypeStruct((TM, D_MODEL), F32)],
        scratch_shapes=[pltpu.VMEM((TM, D_MODEL), F32)],
        compiler_params=pltpu.CompilerParams(
            dimension_semantics=("arbitrary", "arbitrary"), vmem_limit_bytes=VMEM_LIMIT),
        name="ffn",
    )(x1b, x1, wg, wu, wd, g2, b2)


def _pad_rows(a, n):
    return jnp.pad(a, ((0, n - a.shape[0]), (0, 0)))


def kernel(x_prompt, x_sample, state_ssm_re, state_ssm_im, state_conv, meta_tokens, ln0_g, ln0_b,
           w_in, ssm_a_re, ssm_a_im, ssm_log_dt, ssm_b_re, ssm_b_im, ssm_c_re, ssm_c_im, ssm_d,
           ssm_w_glu, conv_w, g_ssm_out, g_conv_out, w_out, ln1_g, ln1_b, w_gate, w_up, w_down,
           ln2_g, ln2_b):
    assert x_prompt.shape == (N_BATCH, SEQ, D_MODEL) and x_sample.shape == (N_SAMPLE, 1, D_MODEL)
    assert w_in.shape[0] == 1, "single layer"
    g, p, c = N_GROUPS, N_STATE, GROUP
    vec = lambda a: a.reshape(1, -1)

    xp = x_prompt.reshape(N_PROMPT_ROWS, D_MODEL)
    xe = _pad_rows(jnp.concatenate([meta_tokens, x_sample[:, 0, :]], axis=0), TM)
    g0, b0 = vec(ln0_g), vec(ln0_b)

    m, wst, cpow, c0, avec = _ssm_prep(ssm_log_dt[0], ssm_a_re[0], ssm_a_im[0], ssm_b_re[0],
                                       ssm_b_im[0], ssm_c_re[0], ssm_c_im[0])

    u, bg, v = _in_proj(xp, xe, g0, b0, w_in[0].astype(BF16))

    u_meta = jnp.broadcast_to(u[N_PROMPT_ROWS:N_PROMPT_ROWS + N_META].reshape(1, 1, CHUNK, g, c),
                              (N_BATCH, 1, CHUNK, g, c))
    u_seq = jnp.concatenate([u_meta, u[:N_PROMPT_ROWS].reshape(N_BATCH, SEQ // CHUNK, CHUNK, g, c)], axis=1)
    u_chunks = jnp.pad(u_seq.transpose(3, 1, 0, 2, 4), ((0, 0), (0, 0), (0, ROW8 - N_BATCH), (0, 0), (0, 0)))
    u_chunks = u_chunks.reshape(g, N_CHUNKS * ROW8, CW)
    s_lo = N_PROMPT_ROWS + SAMPLE_ROW0
    u_s = u[s_lo:s_lo + N_SAMPLE].reshape(N_SAMPLE, g, c).transpose(1, 0, 2)
    s0 = jnp.concatenate([state_ssm_re[0], state_ssm_im[0]], axis=-1).transpose(1, 0, 2)

    y_chunks, s_fin, y_s, s_new = _ssm_scan(u_chunks, m, wst, cpow, c0, avec, u_s, s0)

    y5 = y_chunks.reshape(g, N_CHUNKS, ROW8, CHUNK, c)
    y_prompt_rows = y5[:, 1:, :N_BATCH].transpose(2, 1, 3, 0, 4).reshape(N_PROMPT_ROWS, D_SSM)
    y_meta_rows = y5[:, 0, 0].transpose(1, 0, 2).reshape(N_META, D_SSM)
    y_sample_rows = y_s.transpose(1, 0, 2).reshape(N_SAMPLE, D_SSM)
    y_all = _pad_rows(jnp.concatenate([y_prompt_rows, y_meta_rows, y_sample_rows], axis=0), M_PAD)

    prev = jnp.pad(state_conv[0], ((SAMPLE_ROW0, TMX - SAMPLE_ROW0 - N_SAMPLE), (0, 0), (0, 0)))
    x1, x1b = _mix_out(y_all, u, bg, v, prev[:, 1], prev[:, 0], xp, xe, g0, b0, vec(ssm_d[0]),
                       ssm_w_glu[0].astype(BF16), conv_w[0], vec(g_ssm_out[0]), vec(g_conv_out[0]),
                       w_out[0].astype(BF16), vec(ln1_g[0]), vec(ln1_b[0]))

    yp, ye = _ffn(x1b, x1, w_gate[0].astype(BF16), w_up[0].astype(BF16), w_down[0].astype(BF16),
                  vec(ln2_g[0]), vec(ln2_b[0]))

    y_prompt = yp.reshape(N_BATCH, SEQ, D_MODEL)
    y_sample = ye[SAMPLE_ROW0:SAMPLE_ROW0 + N_SAMPLE].reshape(N_SAMPLE, 1, D_MODEL)

    sd = state_ssm_re.dtype
    s_p = s_fin[:, :N_BATCH].transpose(1, 0, 2)
    s_s = s_new.transpose(1, 0, 2)
    v_f = v.astype(state_conv.dtype)
    conv_p = v_f[:N_PROMPT_ROWS].reshape(N_BATCH, SEQ, D_CONV)[:, SEQ - (CONV_W - 1):]
    conv_s = jnp.stack([state_conv[0][:, 1], v_f[s_lo:s_lo + N_SAMPLE]], axis=1)
    return (y_prompt, y_sample,
            s_p[None, ..., :p].astype(sd), s_p[None, ..., p:].astype(sd), conv_p[None],
            s_s[None, ..., :p].astype(sd), s_s[None, ..., p:].astype(sd), conv_s[None])
```

```python
import functools

import jax
import jax.numpy as jnp
import numpy as np
from jax import lax
from jax.experimental import pallas as pl
from jax.experimental.pallas import tpu as pltpu

D_MODEL = 2048
D_SSM = 1024
D_CONV = 1024
N_GROUPS = 64
GROUP = 16
N_STATE = 64
N_META = 16
CONV_W = 3
D_FF = 5632
LN_EPS = 1e-5
RMS_EPS = 1e-6
ALPHA = 2.0 ** 0.25

N_BATCH = 4
SEQ = 2048
N_SAMPLE = 128
CHUNK = 16
N_CHUNKS = SEQ // CHUNK + 1
ROW8 = 8
CW = CHUNK * GROUP
SW = 2 * N_STATE

TM = 512
N_PROMPT_ROWS = N_BATCH * SEQ
N_PROMPT_TILES = N_PROMPT_ROWS // TM
N_TILES = N_PROMPT_TILES + 1
M_PAD = N_TILES * TM
SAMPLE_ROW0 = N_META
TF = 512
GB = 8
VMEM_LIMIT = 56 * 1024 * 1024

F32 = jnp.float32
BF16 = jnp.bfloat16


def _layer_norm(x, g, b):
    mu = jnp.mean(x, axis=-1, keepdims=True)
    xc = x - mu
    var = jnp.mean(xc * xc, axis=-1, keepdims=True)
    return xc * lax.rsqrt(var + LN_EPS) * g + b


def _rms_norm(x, g):
    return x * lax.rsqrt(jnp.mean(x * x, axis=-1, keepdims=True) + RMS_EPS) * g


def _gelu_tanh(x):
    c = np.float32(np.sqrt(2.0 / np.pi))
    return 0.5 * x * (1.0 + jnp.tanh(c * (x + np.float32(0.044715) * (x * x * x))))


def _sigmoid(x):
    return 1.0 / (1.0 + jnp.exp(-x))


def _dot(a, b):
    return jnp.dot(a, b, preferred_element_type=F32)


def _dot_nt(a, b):
    return lax.dot_general(a, b, (((1,), (1,)), ((), ())), preferred_element_type=F32)


def _ssm_prep_kernel(logdt_ref, ac_ref, ar_ref, bt_ref, btT_ref, ct_ref,
                     m_ref, wst_ref, cpow_ref, bpad_ref, cpad_ref, avec_ref):
    dt = jnp.exp(logdt_ref[0])

    def discretise(a_re, a_im):
        mag = jnp.exp(dt * a_re)
        ang = dt * a_im
        ab_re = mag * jnp.cos(ang)
        ab_im = mag * jnp.sin(ang)
        n_re = ab_re - 1.0
        den = a_re * a_re + a_im * a_im
        f_re = (n_re * a_re + ab_im * a_im) / den
        f_im = (ab_im * a_re - n_re * a_im) / den
        return ab_re, ab_im, f_re, f_im

    acol = ac_ref[0]
    ab_re, ab_im, f_re, f_im = discretise(acol[:, 0:1], acol[:, 1:2])
    bt_re, bt_im = bt_ref[0, 0], bt_ref[0, 1]
    bb_re = f_re * bt_re - f_im * bt_im
    bb_im = f_re * bt_im + f_im * bt_re
    ct_re, ct_im = ct_ref[0, 0], ct_ref[0, 1]

    pw = [(jnp.ones_like(ab_re), jnp.zeros_like(ab_re))]
    for _ in range(CHUNK):
        pr, pi = pw[-1]
        pw.append((pr * ab_re - pi * ab_im, pr * ab_im + pi * ab_re))

    slot = lax.broadcasted_iota(jnp.int32, (N_STATE, CW), 1) // GROUP

    def spread(d_of_slot):
        re = jnp.zeros((N_STATE, CW), F32)
        im = jnp.zeros((N_STATE, CW), F32)
        for s in range(CHUNK):
            pr, pi = pw[d_of_slot(s)]
            re = jnp.where(slot == s, pr, re)
            im = jnp.where(slot == s, pi, im)
        return re, im

    wr, wi = spread(lambda s: CHUNK - 1 - s)
    wst_ref[0, 0:N_STATE, :] = (wr * bb_re - wi * bb_im).astype(BF16)
    wst_ref[0, N_STATE:SW, :] = (wr * bb_im + wi * bb_re).astype(BF16)

    qr, qi = spread(lambda s: s)
    cq_re = ct_re * qr - ct_im * qi
    cq_nim = -(ct_re * qi + ct_im * qr)
    mine = lax.broadcasted_iota(jnp.int32, (N_STATE, SW), 1) // GROUP == pl.program_id(0) % GB
    bpad_ref[0, 0:N_STATE, :] = jnp.where(mine, bb_re[:, 0:SW], 0.0).astype(BF16)
    bpad_ref[0, N_STATE:SW, :] = jnp.where(mine, bb_im[:, 0:SW], 0.0).astype(BF16)
    cpad_ref[0, 0:N_STATE, :] = jnp.where(mine, ct_re[:, 0:SW], 0.0).astype(BF16)
    cpad_ref[0, N_STATE:SW, :] = jnp.where(mine, -ct_im[:, 0:SW], 0.0).astype(BF16)
    p1r = qr * ab_re - qi * ab_im
    p1i = qr * ab_im + qi * ab_re
    cpow_ref[0, 0:N_STATE, :] = (ct_re * p1r - ct_im * p1i).astype(BF16)
    cpow_ref[0, N_STATE:SW, :] = (-(ct_re * p1i + ct_im * p1r)).astype(BF16)

    arow = ar_ref[0]
    _, _, g_re, g_im = discretise(arow[0:1, :], arow[1:2, :])
    bT_re, bT_im = btT_ref[0, 0], btT_ref[0, 1]
    bbT = jnp.concatenate([g_re * bT_re - g_im * bT_im, g_re * bT_im + g_im * bT_re], axis=1)
    cq = jnp.concatenate([cq_re, cq_nim], axis=0)
    kt = jnp.dot(bbT, cq, precision=lax.Precision.HIGHEST, preferred_element_type=F32)
    lane = lax.broadcasted_iota(jnp.int32, (GROUP, CW), 1)
    for i in range(CHUNK):
        blk = kt if i == 0 else jnp.where(lane >= i * GROUP, pltpu.roll(kt, i * GROUP, axis=1), 0.0)
        m_ref[0, i * GROUP:(i + 1) * GROUP, :] = blk.astype(BF16)

    ab_r, ab_i, _, _ = discretise(arow[0:1, :], arow[1:2, :])
    cr, ci = ab_r, ab_i
    for _ in range(4):
        cr, ci = cr * cr - ci * ci, 2.0 * cr * ci
    avec_ref[0, 0:1, :] = jnp.concatenate([cr, cr], axis=1)
    avec_ref[0, 1:2, :] = jnp.concatenate([-ci, ci], axis=1)
    avec_ref[0, 2:3, :] = jnp.concatenate([ab_r, ab_r], axis=1)
    avec_ref[0, 3:4, :] = jnp.concatenate([-ab_i, ab_i], axis=1)


def _ssm_prep(log_dt, a_re, a_im, b_re, b_im, c_re, c_im):
    g, p, c = N_GROUPS, N_STATE, GROUP
    logdt = log_dt.reshape(g, 1, 1)
    ac = jnp.stack([a_re, a_im], axis=-1)
    ar = jnp.stack([a_re, a_im], axis=1)
    bt = jnp.tile(jnp.stack([b_re, b_im], axis=1), (1, 1, 1, CHUNK))
    btT = jnp.stack([b_re, b_im], axis=1).transpose(0, 1, 3, 2)
    ct = jnp.tile(jnp.stack([c_re, c_im], axis=1).transpose(0, 1, 3, 2), (1, 1, 1, CHUNK))
    blk = lambda *s: pl.BlockSpec((1,) + s, lambda i: (i,) + (0,) * len(s))
    return pl.pallas_call(
        _ssm_prep_kernel,
        grid=(g,),
        in_specs=[blk(1, 1), blk(p, 2), blk(2, p), blk(2, p, CW), blk(2, c, p), blk(2, p, CW)],
        out_specs=[blk(CW, CW), blk(SW, CW), blk(SW, CW), blk(SW, SW), blk(SW, SW), blk(4, SW)],
        out_shape=[jax.ShapeDtypeStruct((g, CW, CW), BF16),
                   jax.ShapeDtypeStruct((g, SW, CW), BF16),
                   jax.ShapeDtypeStruct((g, SW, CW), BF16),
                   jax.ShapeDtypeStruct((g, SW, SW), BF16),
                   jax.ShapeDtypeStruct((g, SW, SW), BF16),
                   jax.ShapeDtypeStruct((g, 4, SW), F32)],
        name="ssm_prep",
    )(logdt, ac, ar, bt, btT, ct)


def _in_proj_kernel(xp_ref, xe_ref, g0_ref, b0_ref, w_ref, u_ref, bg_ref, v_ref, h_scr, cg_scr):
    i = pl.program_id(0)
    j = pl.program_id(1)

    @pl.when((j == 0) & (i < N_PROMPT_TILES))
    def _():
        h_scr[...] = _layer_norm(xp_ref[...], g0_ref[...], b0_ref[...]).astype(BF16)

    @pl.when((j == 0) & (i == N_PROMPT_TILES))
    def _():
        h_scr[...] = _layer_norm(xe_ref[...], g0_ref[...], b0_ref[...]).astype(BF16)

    r = _dot(h_scr[...], w_ref[...])

    @pl.when(j == 0)
    def _():
        u_ref[...] = r

    @pl.when(j == 1)
    def _():
        bg_ref[...] = r.astype(BF16)

    @pl.when(j == 2)
    def _():
        cg_scr[...] = r

    @pl.when(j == 3)
    def _():
        v_ref[...] = (cg_scr[...] * r).astype(BF16)


def _in_proj(xp, xe, g0, b0, w_in):
    row = lambda i, j: (i, 0)
    out = jax.ShapeDtypeStruct((M_PAD, D_SSM), BF16)
    return pl.pallas_call(
        _in_proj_kernel,
        grid=(N_TILES, 4),
        in_specs=[pl.BlockSpec((TM, D_MODEL), lambda i, j: (jnp.minimum(i, N_PROMPT_TILES - 1), 0)),
                  pl.BlockSpec((TM, D_MODEL), lambda i, j: (0, 0)),
                  pl.BlockSpec((1, D_MODEL), lambda i, j: (0, 0)),
                  pl.BlockSpec((1, D_MODEL), lambda i, j: (0, 0)),
                  pl.BlockSpec((D_MODEL, D_SSM), lambda i, j: (0, j))],
        out_specs=[pl.BlockSpec((TM, D_SSM), row)] * 3,
        out_shape=[jax.ShapeDtypeStruct((M_PAD, D_SSM), F32), out, out],
        scratch_shapes=[pltpu.VMEM((TM, D_MODEL), BF16), pltpu.VMEM((TM, D_CONV), F32)],
        compiler_params=pltpu.CompilerParams(
            dimension_semantics=("arbitrary", "arbitrary"), vmem_limit_bytes=VMEM_LIMIT),
        name="in_proj",
    )(xp, xe, g0, b0, w_in)


SUB = 16
N_SUB = N_PROMPT_ROWS // (SUB * CHUNK)
SEQ_CHUNKS = SEQ // CHUNK
LHS_ROWS = N_BATCH * SEQ_CHUNKS + SUB
META_LHS = N_BATCH * SEQ_CHUNKS
PITCH = SEQ_CHUNKS + ROW8
REC_ROWS = N_BATCH * PITCH + ROW8
META_REC = N_BATCH * PITCH
SAMPLE_LO = N_PROMPT_ROWS + SAMPLE_ROW0
SAMPLE_HI = SAMPLE_LO + N_SAMPLE


def _block_transpose(a):
    a = list(a)
    blk = lax.broadcasted_iota(jnp.int32, a[0].shape, 1) // GROUP
    for d in (4, 2, 1):
        w = d * GROUP
        keep = (blk & d) == 0
        for j in range(GB):
            if j & d == 0:
                lo, hi = a[j], a[j + d]
                a[j] = jnp.where(keep, lo, pltpu.roll(hi, w, axis=1))
                a[j + d] = jnp.where(keep, pltpu.roll(lo, SW - w, axis=1), hi)
    return a


def _ssm_scan_kernel(u_ref, m_ref, wst_ref, cpow_ref, bpad_ref, cpad_ref, avec_ref, s0_ref,
                     y_ref, sfin_ref, snew_ref, lhs_scr, x_scr, sp_scr, yg_scr):
    def swap_halves(s):
        return pltpu.roll(s, N_STATE, axis=1)

    half = SW

    def regroup(t, _):
        r0 = pl.multiple_of(t * SUB, SUB)
        for h in range(2):
            src = [u_ref[pl.ds(t * (SUB * CHUNK) + 8 * h + j, SUB, stride=CHUNK), :] for j in range(GB)]
            for gl, w in enumerate(_block_transpose(src)):
                lhs_scr[gl, pl.ds(r0, SUB), h * half:(h + 1) * half] = w.astype(BF16)
        return 0

    lax.fori_loop(0, N_SUB + 1, regroup, 0)

    for gl in range(GB):
        x = _dot_nt(lhs_scr[gl], wst_ref[gl])
        for n in range(N_BATCH):
            x_scr[gl, n * PITCH:n * PITCH + SEQ_CHUNKS, :] = x[n * SEQ_CHUNKS:(n + 1) * SEQ_CHUNKS]
        x_scr[gl, META_REC:META_REC + ROW8, :] = x[META_LHS:META_LHS + ROW8]

    ar16 = [avec_ref[gl, 0:1, :] for gl in range(GB)]
    ax16 = [avec_ref[gl, 1:2, :] for gl in range(GB)]

    def step(k, carry):
        new = []
        for gl in range(GB):
            s = carry[gl]
            sp_scr[gl, pl.ds(k, N_BATCH, stride=PITCH), :] = s
            new.append(ar16[gl] * s + ax16[gl] * swap_halves(s)
                       + x_scr[gl, pl.ds(k, N_BATCH, stride=PITCH), :])
        return tuple(new)

    init = tuple(jnp.broadcast_to(x_scr[gl, META_REC:META_REC + 1, :], (N_BATCH, SW)) for gl in range(GB))
    fin = lax.fori_loop(0, SEQ_CHUNKS, step, init)

    us = u_ref[SAMPLE_LO:SAMPLE_HI, :].astype(BF16)
    ys = jnp.zeros((N_SAMPLE, SW), F32)
    for gl in range(GB):
        sfin_ref[gl] = fin[gl]
        sp = jnp.concatenate([sp_scr[gl, n * PITCH:n * PITCH + SEQ_CHUNKS, :] for n in range(N_BATCH)]
                             + [jnp.zeros((SUB, SW), F32)], axis=0)
        yg_scr[gl] = _dot(lhs_scr[gl], m_ref[gl]) + _dot(sp.astype(BF16), cpow_ref[gl])
        s0 = s0_ref[gl]
        sn = (avec_ref[gl, 2:3, :] * s0 + avec_ref[gl, 3:4, :] * swap_halves(s0)
              + _dot_nt(us, bpad_ref[gl]))
        snew_ref[gl] = sn
        ys = ys + _dot(sn.astype(BF16), cpad_ref[gl])

    def ungroup(t, _):
        r0 = pl.multiple_of(t * SUB, SUB)
        for h in range(2):
            src = [yg_scr[gl, pl.ds(r0, SUB), h * half:(h + 1) * half] for gl in range(GB)]
            for j, w in enumerate(_block_transpose(src)):
                y_ref[pl.ds(t * (SUB * CHUNK) + 8 * h + j, SUB, stride=CHUNK), :] = w
        return 0

    lax.fori_loop(0, N_SUB, ungroup, 0)
    for h in range(2):
        src = [yg_scr[gl, META_LHS:META_LHS + ROW8, h * half:(h + 1) * half] for gl in range(GB)]
        for j, w in enumerate(_block_transpose(src)):
            y_ref[N_PROMPT_ROWS + 8 * h + j:N_PROMPT_ROWS + 8 * h + j + 1, :] = w[0:1, :]
    y_ref[SAMPLE_LO:SAMPLE_HI, :] = ys
    y_ref[SAMPLE_HI:M_PAD, :] = jnp.zeros((M_PAD - SAMPLE_HI, SW), F32)


def _ssm_scan(u, m, wst, cpow, bpad, cpad, avec, s0):
    blk = lambda *s: pl.BlockSpec((GB,) + s, lambda i: (i,) + (0,) * len(s))
    col = pl.BlockSpec((M_PAD, SW), lambda i: (0, i))
    return pl.pallas_call(
        _ssm_scan_kernel,
        grid=(N_GROUPS // GB,),
        in_specs=[col, blk(CW, CW), blk(SW, CW), blk(SW, CW), blk(SW, SW), blk(SW, SW), blk(4, SW),
                  blk(N_SAMPLE, SW)],
        out_specs=[col, blk(N_BATCH, SW), blk(N_SAMPLE, SW)],
        out_shape=[jax.ShapeDtypeStruct((M_PAD, D_SSM), F32),
                   jax.ShapeDtypeStruct((N_GROUPS, N_BATCH, SW), F32),
                   jax.ShapeDtypeStruct((N_GROUPS, N_SAMPLE, SW), F32)],
        scratch_shapes=[pltpu.VMEM((GB, LHS_ROWS, CW), BF16), pltpu.VMEM((GB, REC_ROWS, SW), F32),
                        pltpu.VMEM((GB, REC_ROWS, SW), F32), pltpu.VMEM((GB, LHS_ROWS, CW), F32)],
        compiler_params=pltpu.CompilerParams(
            dimension_semantics=("arbitrary",), vmem_limit_bytes=VMEM_LIMIT),
        name="ssm_scan",
    )(u, m, wst, cpow, bpad, cpad, avec, s0)


HALO = 16
TMX = 256
NX_PROMPT = N_PROMPT_ROWS // TMX
NX_TILES = M_PAD // TMX


def _mix_out_kernel(y_ref, u_ref, bg_ref, v_ref, vh_ref, p1_ref, p2_ref, xp_ref, xe_ref,
                    g0_ref, b0_ref, dskip_ref, wglu_ref, cw_ref, gs_ref, gc_ref, wout_ref,
                    g1_ref, b1_ref, x1_ref, x1b_ref, vpad_scr):
    i = pl.program_id(0)
    is_extra = i == NX_PROMPT

    ys = _gelu_tanh(y_ref[...] + dskip_ref[...] * u_ref[...])
    gate = _dot(ys.astype(BF16), wglu_ref[...])
    r_ssm = _rms_norm(ys * _sigmoid(gate), gs_ref[...])

    vf = v_ref[...].astype(F32)
    halo = jnp.where(is_extra, 0.0, vh_ref[...].astype(F32)[HALO - 2:HALO, :])
    vpad_scr[ROW8 - 2:ROW8, :] = halo
    vpad_scr[ROW8:ROW8 + TMX, :] = vf
    row = lax.broadcasted_iota(jnp.int32, (TMX, 1), 0)
    sample = is_extra & (row >= SAMPLE_ROW0)
    vm1 = jnp.where(sample, p1_ref[...], vpad_scr[ROW8 - 1:ROW8 - 1 + TMX, :])
    vm2 = jnp.where(sample, p2_ref[...], vpad_scr[ROW8 - 2:ROW8 - 2 + TMX, :])
    conv = cw_ref[0:1, :] * vm2 + cw_ref[1:2, :] * vm1 + cw_ref[2:3, :] * vf
    r_conv = _rms_norm(bg_ref[...].astype(F32) * conv, gc_ref[...])

    mo = (_dot(r_ssm.astype(BF16), wout_ref[0:D_SSM, :])
          + _dot(r_conv.astype(BF16), wout_ref[D_SSM:D_MODEL, :]))

    def finish(x_ref):
        h0 = _layer_norm(x_ref[...], g0_ref[...], b0_ref[...])
        x1 = _layer_norm(ALPHA * h0 + mo, g1_ref[...], b1_ref[...])
        x1_ref[...] = x1
        x1b_ref[...] = x1.astype(BF16)

    @pl.when(i < NX_PROMPT)
    def _():
        finish(xp_ref)

    @pl.when(i >= NX_PROMPT)
    def _():
        finish(xe_ref)


def _mix_out(y, u, bg, v, p1, p2, xp, xe, g0, b0, dskip, wglu, cw, gs, gc, wout, g1, b1):
    row = lambda i: (i, 0)
    const = lambda i: (0, 0)
    tiles_per_seq = SEQ // TMX
    meta_block = N_PROMPT_ROWS // HALO

    def halo_map(i):
        prev = i * (TMX // HALO) - 1
        return (jnp.where(i % tiles_per_seq == 0, meta_block, prev), 0)

    vec = lambda n: pl.BlockSpec((1, n), const)
    return pl.pallas_call(
        _mix_out_kernel,
        grid=(NX_TILES,),
        in_specs=[pl.BlockSpec((TMX, D_SSM), row), pl.BlockSpec((TMX, D_SSM), row),
                  pl.BlockSpec((TMX, D_CONV), row), pl.BlockSpec((TMX, D_CONV), row),
                  pl.BlockSpec((HALO, D_CONV), halo_map),
                  pl.BlockSpec((TMX, D_CONV), const), pl.BlockSpec((TMX, D_CONV), const),
                  pl.BlockSpec((TMX, D_MODEL), lambda i: (jnp.minimum(i, NX_PROMPT - 1), 0)),
                  pl.BlockSpec((TMX, D_MODEL), lambda i: (jnp.maximum(i - NX_PROMPT, 0), 0)),
                  vec(D_MODEL), vec(D_MODEL), vec(D_SSM),
                  pl.BlockSpec((D_SSM, D_SSM), const), pl.BlockSpec((CONV_W, D_CONV), const),
                  vec(D_SSM), vec(D_CONV), pl.BlockSpec((D_MODEL, D_MODEL), const),
                  vec(D_MODEL), vec(D_MODEL)],
        out_specs=[pl.BlockSpec((TMX, D_MODEL), row), pl.BlockSpec((TMX, D_MODEL), row)],
        out_shape=[jax.ShapeDtypeStruct((M_PAD, D_MODEL), F32),
                   jax.ShapeDtypeStruct((M_PAD, D_MODEL), BF16)],
        scratch_shapes=[pltpu.VMEM((ROW8 + TMX, D_CONV), F32)],
        compiler_params=pltpu.CompilerParams(
            dimension_semantics=("arbitrary",), vmem_limit_bytes=VMEM_LIMIT),
        name="mix_out",
    )(y, u, bg, v, v, p1, p2, xp, xe, g0, b0, dskip, wglu, cw, gs, gc, wout, g1, b1)


def _ffn_kernel(xb_ref, x_ref, wg_ref, wu_ref, wd_ref, g2_ref, b2_ref, yp_ref, ye_ref, acc_scr):
    i = pl.program_id(0)
    f = pl.program_id(1)

    @pl.when(f == 0)
    def _():
        acc_scr[...] = jnp.zeros_like(acc_scr)

    xb = xb_ref[...]
    gate = _dot(xb, wg_ref[...])
    up = _dot(xb, wu_ref[...])
    act = (gate * _sigmoid(gate) * up).astype(BF16)
    acc_scr[...] += _dot(act, wd_ref[...])

    last = f == pl.num_programs(1) - 1

    def finish(o_ref):
        o_ref[...] = _layer_norm(ALPHA * x_ref[...] + acc_scr[...], g2_ref[...], b2_ref[...])

    @pl.when(last & (i < N_PROMPT_TILES))
    def _():
        finish(yp_ref)

    @pl.when(last & (i == N_PROMPT_TILES))
    def _():
        finish(ye_ref)


def _ffn(x1b, x1, wg, wu, wd, g2, b2):
    row = lambda i, f: (i, 0)
    const = lambda i, f: (0, 0)
    return pl.pallas_call(
        _ffn_kernel,
        grid=(N_TILES, D_FF // TF),
        in_specs=[pl.BlockSpec((TM, D_MODEL), row), pl.BlockSpec((TM, D_MODEL), row),
                  pl.BlockSpec((D_MODEL, TF), lambda i, f: (0, f)),
                  pl.BlockSpec((D_MODEL, TF), lambda i, f: (0, f)),
                  pl.BlockSpec((TF, D_MODEL), lambda i, f: (f, 0)),
                  pl.BlockSpec((1, D_MODEL), const), pl.BlockSpec((1, D_MODEL), const)],
        out_specs=[pl.BlockSpec((TM, D_MODEL), lambda i, f: (jnp.minimum(i, N_PROMPT_TILES - 1), 0)),
                   pl.BlockSpec((TM, D_MODEL), const)],
        out_shape=[jax.ShapeDtypeStruct((N_PROMPT_ROWS, D_MODEL), F32),
                   jax.ShapeDtypeStruct((TM, D_MODEL), F32)],
        scratch_shapes=[pltpu.VMEM((TM, D_MODEL), F32)],
        compiler_params=pltpu.CompilerParams(
            dimension_semantics=("arbitrary", "arbitrary"), vmem_limit_bytes=VMEM_LIMIT),
        name="ffn",
    )(x1b, x1, wg, wu, wd, g2, b2)


def _pad_rows(a, n):
    return jnp.pad(a, ((0, n - a.shape[0]), (0, 0)))


def kernel(x_prompt, x_sample, state_ssm_re, state_ssm_im, state_conv, meta_tokens, ln0_g, ln0_b,
           w_in, ssm_a_re, ssm_a_im, ssm_log_dt, ssm_b_re, ssm_b_im, ssm_c_re, ssm_c_im, ssm_d,
           ssm_w_glu, conv_w, g_ssm_out, g_conv_out, w_out, ln1_g, ln1_b, w_gate, w_up, w_down,
           ln2_g, ln2_b):
    assert x_prompt.shape == (N_BATCH, SEQ, D_MODEL) and x_sample.shape == (N_SAMPLE, 1, D_MODEL)
    assert w_in.shape[0] == 1, "single layer"
    g, p, c = N_GROUPS, N_STATE, GROUP
    vec = lambda a: a.reshape(1, -1)

    xp = x_prompt.reshape(N_PROMPT_ROWS, D_MODEL)
    xe = _pad_rows(jnp.concatenate([meta_tokens, x_sample[:, 0, :]], axis=0), TM)
    g0, b0 = vec(ln0_g), vec(ln0_b)

    m, wst, cpow, bpad, cpad, avec = _ssm_prep(ssm_log_dt[0], ssm_a_re[0], ssm_a_im[0], ssm_b_re[0],
                                               ssm_b_im[0], ssm_c_re[0], ssm_c_im[0])

    u, bg, v = _in_proj(xp, xe, g0, b0, w_in[0].astype(BF16))

    s_lo = SAMPLE_LO
    s0 = jnp.concatenate([state_ssm_re[0], state_ssm_im[0]], axis=-1).transpose(1, 0, 2)
    y_all, s_fin, s_new = _ssm_scan(u, m, wst, cpow, bpad, cpad, avec, s0)

    prev = jnp.pad(state_conv[0], ((SAMPLE_ROW0, TMX - SAMPLE_ROW0 - N_SAMPLE), (0, 0), (0, 0)))
    x1, x1b = _mix_out(y_all, u, bg, v, prev[:, 1], prev[:, 0], xp, xe, g0, b0, vec(ssm_d[0]),
                       ssm_w_glu[0].astype(BF16), conv_w[0], vec(g_ssm_out[0]), vec(g_conv_out[0]),
                       w_out[0].astype(BF16), vec(ln1_g[0]), vec(ln1_b[0]))

    yp, ye = _ffn(x1b, x1, w_gate[0].astype(BF16), w_up[0].astype(BF16), w_down[0].astype(BF16),
                  vec(ln2_g[0]), vec(ln2_b[0]))

    y_prompt = yp.reshape(N_BATCH, SEQ, D_MODEL)
    y_sample = ye[SAMPLE_ROW0:SAMPLE_ROW0 + N_SAMPLE].reshape(N_SAMPLE, 1, D_MODEL)

    sd = state_ssm_re.dtype
    s_p = s_fin.transpose(1, 0, 2)
    s_s = s_new.transpose(1, 0, 2)
    v_f = v.astype(state_conv.dtype)
    conv_p = v_f[:N_PROMPT_ROWS].reshape(N_BATCH, SEQ, D_CONV)[:, SEQ - (CONV_W - 1):]
    conv_s = jnp.stack([state_conv[0][:, 1], v_f[s_lo:s_lo + N_SAMPLE]], axis=1)
    return (y_prompt, y_sample,
            s_p[None, ..., :p].astype(sd), s_p[None, ..., p:].astype(sd), conv_p[None],
            s_s[None, ..., :p].astype(sd), s_s[None, ..., p:].astype(sd), conv_s[None])
```

```python
import functools

import jax
import jax.numpy as jnp
import numpy as np
from jax import lax
from jax.experimental import pallas as pl
from jax.experimental.pallas import tpu as pltpu

D_MODEL = 2048
D_SSM = 1024
D_CONV = 1024
N_GROUPS = 64
GROUP = 16
N_STATE = 64
N_META = 16
CONV_W = 3
D_FF = 5632
LN_EPS = 1e-5
RMS_EPS = 1e-6
ALPHA = 2.0 ** 0.25

N_BATCH = 4
SEQ = 2048
N_SAMPLE = 128
CHUNK = 16
N_CHUNKS = SEQ // CHUNK + 1
ROW8 = 8
CW = CHUNK * GROUP
SW = 2 * N_STATE

TM = 512
N_PROMPT_ROWS = N_BATCH * SEQ
N_PROMPT_TILES = N_PROMPT_ROWS // TM
N_TILES = N_PROMPT_TILES + 1
M_PAD = N_TILES * TM
SAMPLE_ROW0 = N_META
TF = 512
GB = 8
VMEM_LIMIT = 56 * 1024 * 1024

F32 = jnp.float32
BF16 = jnp.bfloat16


def _layer_norm(x, g, b):
    mu = jnp.mean(x, axis=-1, keepdims=True)
    xc = x - mu
    var = jnp.mean(xc * xc, axis=-1, keepdims=True)
    return xc * lax.rsqrt(var + LN_EPS) * g + b


def _rms_norm(x, g):
    return x * lax.rsqrt(jnp.mean(x * x, axis=-1, keepdims=True) + RMS_EPS) * g


def _gelu_tanh(x):
    c = np.float32(np.sqrt(2.0 / np.pi))
    return 0.5 * x * (1.0 + jnp.tanh(c * (x + np.float32(0.044715) * (x * x * x))))


def _sigmoid(x):
    return 1.0 / (1.0 + jnp.exp(-x))


def _dot(a, b):
    return jnp.dot(a, b, preferred_element_type=F32)


def _dot_nt(a, b):
    return lax.dot_general(a, b, (((1,), (1,)), ((), ())), preferred_element_type=F32)


def _ssm_prep_kernel(logdt_ref, ac_ref, ar_ref, bt_ref, btT_ref, ct_ref,
                     m_ref, wst_ref, cpow_ref, bpad_ref, cpad_ref, avec_ref):
    dt = jnp.exp(logdt_ref[0])

    def discretise(a_re, a_im):
        mag = jnp.exp(dt * a_re)
        ang = dt * a_im
        ab_re = mag * jnp.cos(ang)
        ab_im = mag * jnp.sin(ang)
        n_re = ab_re - 1.0
        den = a_re * a_re + a_im * a_im
        f_re = (n_re * a_re + ab_im * a_im) / den
        f_im = (ab_im * a_re - n_re * a_im) / den
        return ab_re, ab_im, f_re, f_im

    acol = ac_ref[0]
    ab_re, ab_im, f_re, f_im = discretise(acol[:, 0:1], acol[:, 1:2])
    bt_re, bt_im = bt_ref[0, 0], bt_ref[0, 1]
    bb_re = f_re * bt_re - f_im * bt_im
    bb_im = f_re * bt_im + f_im * bt_re
    ct_re, ct_im = ct_ref[0, 0], ct_ref[0, 1]

    pw = [(jnp.ones_like(ab_re), jnp.zeros_like(ab_re))]
    for _ in range(CHUNK):
        pr, pi = pw[-1]
        pw.append((pr * ab_re - pi * ab_im, pr * ab_im + pi * ab_re))

    slot = lax.broadcasted_iota(jnp.int32, (N_STATE, CW), 1) // GROUP

    def spread(d_of_slot):
        re = jnp.zeros((N_STATE, CW), F32)
        im = jnp.zeros((N_STATE, CW), F32)
        for s in range(CHUNK):
            pr, pi = pw[d_of_slot(s)]
            re = jnp.where(slot == s, pr, re)
            im = jnp.where(slot == s, pi, im)
        return re, im

    wr, wi = spread(lambda s: CHUNK - 1 - s)
    wst_ref[0, 0:N_STATE, :] = (wr * bb_re - wi * bb_im).astype(BF16)
    wst_ref[0, N_STATE:SW, :] = (wr * bb_im + wi * bb_re).astype(BF16)

    qr, qi = spread(lambda s: s)
    cq_re = ct_re * qr - ct_im * qi
    cq_nim = -(ct_re * qi + ct_im * qr)
    mine = lax.broadcasted_iota(jnp.int32, (N_STATE, SW), 1) // GROUP == pl.program_id(0) % GB
    bpad_ref[0, 0:N_STATE, :] = jnp.where(mine, bb_re[:, 0:SW], 0.0).astype(BF16)
    bpad_ref[0, N_STATE:SW, :] = jnp.where(mine, bb_im[:, 0:SW], 0.0).astype(BF16)
    cpad_ref[0, 0:N_STATE, :] = jnp.where(mine, ct_re[:, 0:SW], 0.0).astype(BF16)
    cpad_ref[0, N_STATE:SW, :] = jnp.where(mine, -ct_im[:, 0:SW], 0.0).astype(BF16)
    p1r = qr * ab_re - qi * ab_im
    p1i = qr * ab_im + qi * ab_re
    cpow_ref[0, 0:N_STATE, :] = (ct_re * p1r - ct_im * p1i).astype(BF16)
    cpow_ref[0, N_STATE:SW, :] = (-(ct_re * p1i + ct_im * p1r)).astype(BF16)

    arow = ar_ref[0]
    _, _, g_re, g_im = discretise(arow[0:1, :], arow[1:2, :])
    bT_re, bT_im = btT_ref[0, 0], btT_ref[0, 1]
    bbT = jnp.concatenate([g_re * bT_re - g_im * bT_im, g_re * bT_im + g_im * bT_re], axis=1)
    cq = jnp.concatenate([cq_re, cq_nim], axis=0)
    kt = jnp.dot(bbT, cq, precision=lax.Precision.HIGHEST, preferred_element_type=F32)
    lane = lax.broadcasted_iota(jnp.int32, (GROUP, CW), 1)
    for i in range(CHUNK):
        blk = kt if i == 0 else jnp.where(lane >= i * GROUP, pltpu.roll(kt, i * GROUP, axis=1), 0.0)
        m_ref[0, i * GROUP:(i + 1) * GROUP, :] = blk.astype(BF16)

    ab_r, ab_i, _, _ = discretise(arow[0:1, :], arow[1:2, :])
    cr, ci = ab_r, ab_i
    for _ in range(4):
        cr, ci = cr * cr - ci * ci, 2.0 * cr * ci
    avec_ref[0, 0:1, :] = jnp.concatenate([cr, cr], axis=1)
    avec_ref[0, 1:2, :] = jnp.concatenate([-ci, ci], axis=1)
    avec_ref[0, 2:3, :] = jnp.concatenate([ab_r, ab_r], axis=1)
    avec_ref[0, 3:4, :] = jnp.concatenate([-ab_i, ab_i], axis=1)


def _ssm_prep(log_dt, a_re, a_im, b_re, b_im, c_re, c_im):
    g, p, c = N_GROUPS, N_STATE, GROUP
    logdt = log_dt.reshape(g, 1, 1)
    ac = jnp.stack([a_re, a_im], axis=-1)
    ar = jnp.stack([a_re, a_im], axis=1)
    bt = jnp.tile(jnp.stack([b_re, b_im], axis=1), (1, 1, 1, CHUNK))
    btT = jnp.stack([b_re, b_im], axis=1).transpose(0, 1, 3, 2)
    ct = jnp.tile(jnp.stack([c_re, c_im], axis=1).transpose(0, 1, 3, 2), (1, 1, 1, CHUNK))
    blk = lambda *s: pl.BlockSpec((1,) + s, lambda i: (i,) + (0,) * len(s))
    return pl.pallas_call(
        _ssm_prep_kernel,
        grid=(g,),
        in_specs=[blk(1, 1), blk(p, 2), blk(2, p), blk(2, p, CW), blk(2, c, p), blk(2, p, CW)],
        out_specs=[blk(CW, CW), blk(SW, CW), blk(SW, CW), blk(SW, SW), blk(SW, SW), blk(4, SW)],
        out_shape=[jax.ShapeDtypeStruct((g, CW, CW), BF16),
                   jax.ShapeDtypeStruct((g, SW, CW), BF16),
                   jax.ShapeDtypeStruct((g, SW, CW), BF16),
                   jax.ShapeDtypeStruct((g, SW, SW), BF16),
                   jax.ShapeDtypeStruct((g, SW, SW), BF16),
                   jax.ShapeDtypeStruct((g, 4, SW), F32)],
        name="ssm_prep",
    )(logdt, ac, ar, bt, btT, ct)


def _in_proj_kernel(xp_ref, xe_ref, g0_ref, b0_ref, w_ref, u_ref, bg_ref, v_ref, h_scr, cg_scr):
    i = pl.program_id(0)
    j = pl.program_id(1)

    @pl.when((j == 0) & (i < N_PROMPT_TILES))
    def _():
        h_scr[...] = _layer_norm(xp_ref[...], g0_ref[...], b0_ref[...]).astype(BF16)

    @pl.when((j == 0) & (i == N_PROMPT_TILES))
    def _():
        h_scr[...] = _layer_norm(xe_ref[...], g0_ref[...], b0_ref[...]).astype(BF16)

    r = _dot(h_scr[...], w_ref[...])

    @pl.when(j == 0)
    def _():
        u_ref[...] = r

    @pl.when(j == 1)
    def _():
        bg_ref[...] = r.astype(BF16)

    @pl.when(j == 2)
    def _():
        cg_scr[...] = r

    @pl.when(j == 3)
    def _():
        v_ref[...] = (cg_scr[...] * r).astype(BF16)


def _in_proj(xp, xe, g0, b0, w_in):
    row = lambda i, j: (i, 0)
    out = jax.ShapeDtypeStruct((M_PAD, D_SSM), BF16)
    return pl.pallas_call(
        _in_proj_kernel,
        grid=(N_TILES, 4),
        in_specs=[pl.BlockSpec((TM, D_MODEL), lambda i, j: (jnp.minimum(i, N_PROMPT_TILES - 1), 0)),
                  pl.BlockSpec((TM, D_MODEL), lambda i, j: (0, 0)),
                  pl.BlockSpec((1, D_MODEL), lambda i, j: (0, 0)),
                  pl.BlockSpec((1, D_MODEL), lambda i, j: (0, 0)),
                  pl.BlockSpec((D_MODEL, D_SSM), lambda i, j: (0, j))],
        out_specs=[pl.BlockSpec((TM, D_SSM), row)] * 3,
        out_shape=[jax.ShapeDtypeStruct((M_PAD, D_SSM), F32), out, out],
        scratch_shapes=[pltpu.VMEM((TM, D_MODEL), BF16), pltpu.VMEM((TM, D_CONV), F32)],
        compiler_params=pltpu.CompilerParams(
            dimension_semantics=("arbitrary", "arbitrary"), vmem_limit_bytes=VMEM_LIMIT),
        name="in_proj",
    )(xp, xe, g0, b0, w_in)


SUB = 16
SEQ_CHUNKS = SEQ // CHUNK
LHS_ROWS = N_BATCH * SEQ_CHUNKS + SUB
META_LHS = N_BATCH * SEQ_CHUNKS
PITCH = SEQ_CHUNKS + ROW8
REC_ROWS = N_BATCH * PITCH + ROW8
META_REC = N_BATCH * PITCH
SAMPLE_LO = N_PROMPT_ROWS + SAMPLE_ROW0
SAMPLE_HI = SAMPLE_LO + N_SAMPLE


def _block_transpose(a):
    a = list(a)
    blk = lax.broadcasted_iota(jnp.int32, a[0].shape, 1) // GROUP
    for d in (4, 2, 1):
        w = d * GROUP
        keep = (blk & d) == 0
        for j in range(GB):
            if j & d == 0:
                lo, hi = a[j], a[j + d]
                a[j] = jnp.where(keep, lo, pltpu.roll(hi, w, axis=1))
                a[j + d] = jnp.where(keep, pltpu.roll(lo, SW - w, axis=1), hi)
    return a


def _ssm_scan_kernel(u_ref, m_ref, wst_ref, cpow_ref, bpad_ref, cpad_ref, avec_ref, s0_ref,
                     y_ref, sfin_ref, snew_ref, lhs_scr, x_scr, xs_scr, sp_scr, yg_scr):
    def swap_halves(s):
        return pltpu.roll(s, N_STATE, axis=1)

    half = SW

    for h in range(2):
        src = [u_ref[pl.ds(8 * h + j, LHS_ROWS, stride=CHUNK), :] for j in range(GB)]
        for gl, w in enumerate(_block_transpose(src)):
            lhs_scr[gl, :, h * half:(h + 1) * half] = w.astype(BF16)

    def to_rec(ref, gl, x):
        for n in range(N_BATCH):
            ref[gl, n * PITCH:n * PITCH + SEQ_CHUNKS, :] = x[n * SEQ_CHUNKS:(n + 1) * SEQ_CHUNKS]
        ref[gl, META_REC:META_REC + ROW8, :] = x[META_LHS:META_LHS + ROW8]

    for gl in range(GB):
        x = _dot_nt(lhs_scr[gl], wst_ref[gl])
        to_rec(x_scr, gl, x)
        to_rec(xs_scr, gl, swap_halves(x))

    ar16 = [avec_ref[gl, 0:1, :] for gl in range(GB)]
    ax16 = [avec_ref[gl, 1:2, :] for gl in range(GB)]

    def step(k, carry):
        new = []
        for gl in range(GB):
            s, t = carry[gl]
            rows = pl.ds(k, N_BATCH, stride=PITCH)
            sp_scr[gl, rows, :] = s
            new.append((ar16[gl] * s + ax16[gl] * t + x_scr[gl, rows, :],
                        ar16[gl] * t - ax16[gl] * s + xs_scr[gl, rows, :]))
        return tuple(new)

    first = lambda ref, gl: jnp.broadcast_to(ref[gl, META_REC:META_REC + 1, :], (N_BATCH, SW))
    init = tuple((first(x_scr, gl), first(xs_scr, gl)) for gl in range(GB))
    fin = lax.fori_loop(0, SEQ_CHUNKS, step, init)

    us = u_ref[SAMPLE_LO:SAMPLE_HI, :].astype(BF16)
    ys = jnp.zeros((N_SAMPLE, SW), F32)
    for gl in range(GB):
        sfin_ref[gl] = fin[gl][0]
        sp = jnp.concatenate([sp_scr[gl, n * PITCH:n * PITCH + SEQ_CHUNKS, :] for n in range(N_BATCH)]
                             + [jnp.zeros((SUB, SW), F32)], axis=0)
        yg_scr[gl] = _dot(lhs_scr[gl], m_ref[gl]) + _dot(sp.astype(BF16), cpow_ref[gl])
        s0 = s0_ref[gl]
        sn = (avec_ref[gl, 2:3, :] * s0 + avec_ref[gl, 3:4, :] * swap_halves(s0)
              + _dot_nt(us, bpad_ref[gl]))
        snew_ref[gl] = sn
        ys = ys + _dot(sn.astype(BF16), cpad_ref[gl])

    for h in range(2):
        src = [yg_scr[gl, :, h * half:(h + 1) * half] for gl in range(GB)]
        for j, w in enumerate(_block_transpose(src)):
            y_ref[pl.ds(8 * h + j, META_LHS, stride=CHUNK), :] = w[0:META_LHS]
            y_ref[N_PROMPT_ROWS + 8 * h + j:N_PROMPT_ROWS + 8 * h + j + 1, :] = w[META_LHS:META_LHS + 1]
    y_ref[SAMPLE_LO:SAMPLE_HI, :] = ys
    y_ref[SAMPLE_HI:M_PAD, :] = jnp.zeros((M_PAD - SAMPLE_HI, SW), F32)


def _ssm_scan(u, m, wst, cpow, bpad, cpad, avec, s0):
    blk = lambda *s: pl.BlockSpec((GB,) + s, lambda i: (i,) + (0,) * len(s))
    col = pl.BlockSpec((M_PAD, SW), lambda i: (0, i))
    return pl.pallas_call(
        _ssm_scan_kernel,
        grid=(N_GROUPS // GB,),
        in_specs=[col, blk(CW, CW), blk(SW, CW), blk(SW, CW), blk(SW, SW), blk(SW, SW), blk(4, SW),
                  blk(N_SAMPLE, SW)],
        out_specs=[col, blk(N_BATCH, SW), blk(N_SAMPLE, SW)],
        out_shape=[jax.ShapeDtypeStruct((M_PAD, D_SSM), F32),
                   jax.ShapeDtypeStruct((N_GROUPS, N_BATCH, SW), F32),
                   jax.ShapeDtypeStruct((N_GROUPS, N_SAMPLE, SW), F32)],
        scratch_shapes=[pltpu.VMEM((GB, LHS_ROWS, CW), BF16), pltpu.VMEM((GB, REC_ROWS, SW), F32),
                        pltpu.VMEM((GB, REC_ROWS, SW), F32), pltpu.VMEM((GB, REC_ROWS, SW), F32),
                        pltpu.VMEM((GB, LHS_ROWS, CW), F32)],
        compiler_params=pltpu.CompilerParams(
            dimension_semantics=("arbitrary",), vmem_limit_bytes=VMEM_LIMIT),
        name="ssm_scan",
    )(u, m, wst, cpow, bpad, cpad, avec, s0)


HALO = 16
TMX = 256
NX_PROMPT = N_PROMPT_ROWS // TMX
NX_TILES = M_PAD // TMX


def _mix_out_kernel(y_ref, u_ref, bg_ref, v_ref, vh_ref, p1_ref, p2_ref, xp_ref, xe_ref,
                    g0_ref, b0_ref, dskip_ref, wglu_ref, cw_ref, gs_ref, gc_ref, wout_ref,
                    g1_ref, b1_ref, x1_ref, x1b_ref, vpad_scr):
    i = pl.program_id(0)
    is_extra = i == NX_PROMPT

    ys = _gelu_tanh(y_ref[...] + dskip_ref[...] * u_ref[...])
    gate = _dot(ys.astype(BF16), wglu_ref[...])
    r_ssm = _rms_norm(ys * _sigmoid(gate), gs_ref[...])

    vf = v_ref[...].astype(F32)
    halo = jnp.where(is_extra, 0.0, vh_ref[...].astype(F32)[HALO - 2:HALO, :])
    vpad_scr[ROW8 - 2:ROW8, :] = halo
    vpad_scr[ROW8:ROW8 + TMX, :] = vf
    row = lax.broadcasted_iota(jnp.int32, (TMX, 1), 0)
    sample = is_extra & (row >= SAMPLE_ROW0)
    vm1 = jnp.where(sample, p1_ref[...], vpad_scr[ROW8 - 1:ROW8 - 1 + TMX, :])
    vm2 = jnp.where(sample, p2_ref[...], vpad_scr[ROW8 - 2:ROW8 - 2 + TMX, :])
    conv = cw_ref[0:1, :] * vm2 + cw_ref[1:2, :] * vm1 + cw_ref[2:3, :] * vf
    r_conv = _rms_norm(bg_ref[...].astype(F32) * conv, gc_ref[...])

    mo = (_dot(r_ssm.astype(BF16), wout_ref[0:D_SSM, :])
          + _dot(r_conv.astype(BF16), wout_ref[D_SSM:D_MODEL, :]))

    def finish(x_ref):
        h0 = _layer_norm(x_ref[...], g0_ref[...], b0_ref[...])
        x1 = _layer_norm(ALPHA * h0 + mo, g1_ref[...], b1_ref[...])
        x1_ref[...] = x1
        x1b_ref[...] = x1.astype(BF16)

    @pl.when(i < NX_PROMPT)
    def _():
        finish(xp_ref)

    @pl.when(i >= NX_PROMPT)
    def _():
        finish(xe_ref)


def _mix_out(y, u, bg, v, p1, p2, xp, xe, g0, b0, dskip, wglu, cw, gs, gc, wout, g1, b1):
    row = lambda i: (i, 0)
    const = lambda i: (0, 0)
    tiles_per_seq = SEQ // TMX
    meta_block = N_PROMPT_ROWS // HALO

    def halo_map(i):
        prev = i * (TMX // HALO) - 1
        return (jnp.where(i % tiles_per_seq == 0, meta_block, prev), 0)

    vec = lambda n: pl.BlockSpec((1, n), const)
    return pl.pallas_call(
        _mix_out_kernel,
        grid=(NX_TILES,),
        in_specs=[pl.BlockSpec((TMX, D_SSM), row), pl.BlockSpec((TMX, D_SSM), row),
                  pl.BlockSpec((TMX, D_CONV), row), pl.BlockSpec((TMX, D_CONV), row),
                  pl.BlockSpec((HALO, D_CONV), halo_map),
                  pl.BlockSpec((TMX, D_CONV), const), pl.BlockSpec((TMX, D_CONV), const),
                  pl.BlockSpec((TMX, D_MODEL), lambda i: (jnp.minimum(i, NX_PROMPT - 1), 0)),
                  pl.BlockSpec((TMX, D_MODEL), lambda i: (jnp.maximum(i - NX_PROMPT, 0), 0)),
                  vec(D_MODEL), vec(D_MODEL), vec(D_SSM),
                  pl.BlockSpec((D_SSM, D_SSM), const), pl.BlockSpec((CONV_W, D_CONV), const),
                  vec(D_SSM), vec(D_CONV), pl.BlockSpec((D_MODEL, D_MODEL), const),
                  vec(D_MODEL), vec(D_MODEL)],
        out_specs=[pl.BlockSpec((TMX, D_MODEL), row), pl.BlockSpec((TMX, D_MODEL), row)],
        out_shape=[jax.ShapeDtypeStruct((M_PAD, D_MODEL), F32),
                   jax.ShapeDtypeStruct((M_PAD, D_MODEL), BF16)],
        scratch_shapes=[pltpu.VMEM((ROW8 + TMX, D_CONV), F32)],
        compiler_params=pltpu.CompilerParams(
            dimension_semantics=("arbitrary",), vmem_limit_bytes=VMEM_LIMIT),
        name="mix_out",
    )(y, u, bg, v, v, p1, p2, xp, xe, g0, b0, dskip, wglu, cw, gs, gc, wout, g1, b1)


def _ffn_kernel(xb_ref, x_ref, wg_ref, wu_ref, wd_ref, g2_ref, b2_ref, yp_ref, ye_ref, acc_scr):
    i = pl.program_id(0)
    f = pl.program_id(1)

    @pl.when(f == 0)
    def _():
        acc_scr[...] = jnp.zeros_like(acc_scr)

    xb = xb_ref[...]
    gate = _dot(xb, wg_ref[...])
    up = _dot(xb, wu_ref[...])
    act = (gate * _sigmoid(gate) * up).astype(BF16)
    acc_scr[...] += _dot(act, wd_ref[...])

    last = f == pl.num_programs(1) - 1

    def finish(o_ref):
        o_ref[...] = _layer_norm(ALPHA * x_ref[...] + acc_scr[...], g2_ref[...], b2_ref[...])

    @pl.when(last & (i < N_PROMPT_TILES))
    def _():
        finish(yp_ref)

    @pl.when(last & (i == N_PROMPT_TILES))
    def _():
        finish(ye_ref)


def _ffn(x1b, x1, wg, wu, wd, g2, b2):
    row = lambda i, f: (i, 0)
    const = lambda i, f: (0, 0)
    return pl.pallas_call(
        _ffn_kernel,
        grid=(N_TILES, D_FF // TF),
        in_specs=[pl.BlockSpec((TM, D_MODEL), row), pl.BlockSpec((TM, D_MODEL), row),
                  pl.BlockSpec((D_MODEL, TF), lambda i, f: (0, f)),
                  pl.BlockSpec((D_MODEL, TF), lambda i, f: (0, f)),
                  pl.BlockSpec((TF, D_MODEL), lambda i, f: (f, 0)),
                  pl.BlockSpec((1, D_MODEL), const), pl.BlockSpec((1, D_MODEL), const)],
        out_specs=[pl.BlockSpec((TM, D_MODEL), lambda i, f: (jnp.minimum(i, N_PROMPT_TILES - 1), 0)),
                   pl.BlockSpec((TM, D_MODEL), const)],
        out_shape=[jax.ShapeDtypeStruct((N_PROMPT_ROWS, D_MODEL), F32),
                   jax.ShapeDtypeStruct((TM, D_MODEL), F32)],
        scratch_shapes=[pltpu.VMEM((TM, D_MODEL), F32)],
        compiler_params=pltpu.CompilerParams(
            dimension_semantics=("arbitrary", "arbitrary"), vmem_limit_bytes=VMEM_LIMIT),
        name="ffn",
    )(x1b, x1, wg, wu, wd, g2, b2)


def _pad_rows(a, n):
    return jnp.pad(a, ((0, n - a.shape[0]), (0, 0)))


def kernel(x_prompt, x_sample, state_ssm_re, state_ssm_im, state_conv, meta_tokens, ln0_g, ln0_b,
           w_in, ssm_a_re, ssm_a_im, ssm_log_dt, ssm_b_re, ssm_b_im, ssm_c_re, ssm_c_im, ssm_d,
           ssm_w_glu, conv_w, g_ssm_out, g_conv_out, w_out, ln1_g, ln1_b, w_gate, w_up, w_down,
           ln2_g, ln2_b):
    assert x_prompt.shape == (N_BATCH, SEQ, D_MODEL) and x_sample.shape == (N_SAMPLE, 1, D_MODEL)
    assert w_in.shape[0] == 1, "single layer"
    g, p, c = N_GROUPS, N_STATE, GROUP
    vec = lambda a: a.reshape(1, -1)

    xp = x_prompt.reshape(N_PROMPT_ROWS, D_MODEL)
    xe = _pad_rows(jnp.concatenate([meta_tokens, x_sample[:, 0, :]], axis=0), TM)
    g0, b0 = vec(ln0_g), vec(ln0_b)

    m, wst, cpow, bpad, cpad, avec = _ssm_prep(ssm_log_dt[0], ssm_a_re[0], ssm_a_im[0], ssm_b_re[0],
                                               ssm_b_im[0], ssm_c_re[0], ssm_c_im[0])

    u, bg, v = _in_proj(xp, xe, g0, b0, w_in[0].astype(BF16))

    s_lo = SAMPLE_LO
    s0 = jnp.concatenate([state_ssm_re[0], state_ssm_im[0]], axis=-1).transpose(1, 0, 2)
    y_all, s_fin, s_new = _ssm_scan(u, m, wst, cpow, bpad, cpad, avec, s0)

    prev = jnp.pad(state_conv[0], ((SAMPLE_ROW0, TMX - SAMPLE_ROW0 - N_SAMPLE), (0, 0), (0, 0)))
    x1, x1b = _mix_out(y_all, u, bg, v, prev[:, 1], prev[:, 0], xp, xe, g0, b0, vec(ssm_d[0]),
                       ssm_w_glu[0].astype(BF16), conv_w[0], vec(g_ssm_out[0]), vec(g_conv_out[0]),
                       w_out[0].astype(BF16), vec(ln1_g[0]), vec(ln1_b[0]))

    yp, ye = _ffn(x1b, x1, w_gate[0].astype(BF16), w_up[0].astype(BF16), w_down[0].astype(BF16),
                  vec(ln2_g[0]), vec(ln2_b[0]))

    y_prompt = yp.reshape(N_BATCH, SEQ, D_MODEL)
    y_sample = ye[SAMPLE_ROW0:SAMPLE_ROW0 + N_SAMPLE].reshape(N_SAMPLE, 1, D_MODEL)

    sd = state_ssm_re.dtype
    s_p = s_fin.transpose(1, 0, 2)
    s_s = s_new.transpose(1, 0, 2)
    v_f = v.astype(state_conv.dtype)
    conv_p = v_f[:N_PROMPT_ROWS].reshape(N_BATCH, SEQ, D_CONV)[:, SEQ - (CONV_W - 1):]
    conv_s = jnp.stack([state_conv[0][:, 1], v_f[s_lo:s_lo + N_SAMPLE]], axis=1)
    return (y_prompt, y_sample,
            s_p[None, ..., :p].astype(sd), s_p[None, ..., p:].astype(sd), conv_p[None],
            s_s[None, ..., :p].astype(sd), s_s[None, ..., p:].astype(sd), conv_s[None])
```

```python
import functools

import jax
import jax.numpy as jnp
import numpy as np
from jax import lax
from jax.experimental import pallas as pl
from jax.experimental.pallas import tpu as pltpu

D_MODEL = 2048
D_SSM = 1024
D_CONV = 1024
N_GROUPS = 64
GROUP = 16
N_STATE = 64
N_META = 16
CONV_W = 3
D_FF = 5632
LN_EPS = 1e-5
RMS_EPS = 1e-6
ALPHA = 2.0 ** 0.25

N_BATCH = 4
SEQ = 2048
N_SAMPLE = 128
CHUNK = 16
N_CHUNKS = SEQ // CHUNK + 1
ROW8 = 8
CW = CHUNK * GROUP
SW = 2 * N_STATE

TM = 512
N_PROMPT_ROWS = N_BATCH * SEQ
N_PROMPT_TILES = N_PROMPT_ROWS // TM
N_TILES = N_PROMPT_TILES + 1
M_PAD = N_TILES * TM
SAMPLE_ROW0 = N_META
TF = 512
GB = 8
VMEM_LIMIT = 60 * 1024 * 1024

F32 = jnp.float32
BF16 = jnp.bfloat16


def _layer_norm(x, g, b):
    mu = jnp.mean(x, axis=-1, keepdims=True)
    xc = x - mu
    var = jnp.mean(xc * xc, axis=-1, keepdims=True)
    return xc * lax.rsqrt(var + LN_EPS) * g + b


def _rms_norm(x, g):
    return x * lax.rsqrt(jnp.mean(x * x, axis=-1, keepdims=True) + RMS_EPS) * g


def _gelu_tanh(x):
    c = np.float32(np.sqrt(2.0 / np.pi))
    return 0.5 * x * (1.0 + jnp.tanh(c * (x + np.float32(0.044715) * (x * x * x))))


def _sigmoid(x):
    return 1.0 / (1.0 + jnp.exp(-x))


def _dot(a, b):
    return jnp.dot(a, b, preferred_element_type=F32)


def _dot_nt(a, b):
    return lax.dot_general(a, b, (((1,), (1,)), ((), ())), preferred_element_type=F32)


def _ssm_prep_kernel(logdt_ref, ac_ref, ar_ref, bt_ref, btT_ref, ct_ref,
                     m_ref, wst_ref, cpow_ref, bpad_ref, cpad_ref, avec_ref):
    dt = jnp.exp(logdt_ref[0])

    def discretise(a_re, a_im):
        mag = jnp.exp(dt * a_re)
        ang = dt * a_im
        ab_re = mag * jnp.cos(ang)
        ab_im = mag * jnp.sin(ang)
        n_re = ab_re - 1.0
        den = a_re * a_re + a_im * a_im
        f_re = (n_re * a_re + ab_im * a_im) / den
        f_im = (ab_im * a_re - n_re * a_im) / den
        return ab_re, ab_im, f_re, f_im

    acol = ac_ref[0]
    ab_re, ab_im, f_re, f_im = discretise(acol[:, 0:1], acol[:, 1:2])
    bt_re, bt_im = bt_ref[0, 0], bt_ref[0, 1]
    bb_re = f_re * bt_re - f_im * bt_im
    bb_im = f_re * bt_im + f_im * bt_re
    ct_re, ct_im = ct_ref[0, 0], ct_ref[0, 1]

    pw = [(jnp.ones_like(ab_re), jnp.zeros_like(ab_re))]
    for _ in range(CHUNK):
        pr, pi = pw[-1]
        pw.append((pr * ab_re - pi * ab_im, pr * ab_im + pi * ab_re))

    slot = lax.broadcasted_iota(jnp.int32, (N_STATE, CW), 1) // GROUP

    def spread(d_of_slot):
        re = jnp.zeros((N_STATE, CW), F32)
        im = jnp.zeros((N_STATE, CW), F32)
        for s in range(CHUNK):
            pr, pi = pw[d_of_slot(s)]
            re = jnp.where(slot == s, pr, re)
            im = jnp.where(slot == s, pi, im)
        return re, im

    wr, wi = spread(lambda s: CHUNK - 1 - s)
    wst_ref[0, 0:N_STATE, :] = (wr * bb_re - wi * bb_im).astype(BF16)
    wst_ref[0, N_STATE:SW, :] = (wr * bb_im + wi * bb_re).astype(BF16)

    qr, qi = spread(lambda s: s)
    cq_re = ct_re * qr - ct_im * qi
    cq_nim = -(ct_re * qi + ct_im * qr)
    mine = lax.broadcasted_iota(jnp.int32, (N_STATE, SW), 1) // GROUP == pl.program_id(0) % GB
    bpad_ref[0, 0:N_STATE, :] = jnp.where(mine, bb_re[:, 0:SW], 0.0).astype(BF16)
    bpad_ref[0, N_STATE:SW, :] = jnp.where(mine, bb_im[:, 0:SW], 0.0).astype(BF16)
    cpad_ref[0, 0:N_STATE, :] = jnp.where(mine, ct_re[:, 0:SW], 0.0).astype(BF16)
    cpad_ref[0, N_STATE:SW, :] = jnp.where(mine, -ct_im[:, 0:SW], 0.0).astype(BF16)
    p1r = qr * ab_re - qi * ab_im
    p1i = qr * ab_im + qi * ab_re
    cpow_ref[0, 0:N_STATE, :] = (ct_re * p1r - ct_im * p1i).astype(BF16)
    cpow_ref[0, N_STATE:SW, :] = (-(ct_re * p1i + ct_im * p1r)).astype(BF16)

    arow = ar_ref[0]
    _, _, g_re, g_im = discretise(arow[0:1, :], arow[1:2, :])
    bT_re, bT_im = btT_ref[0, 0], btT_ref[0, 1]
    bbT = jnp.concatenate([g_re * bT_re - g_im * bT_im, g_re * bT_im + g_im * bT_re], axis=1)
    cq = jnp.concatenate([cq_re, cq_nim], axis=0)
    kt = jnp.dot(bbT, cq, precision=lax.Precision.HIGHEST, preferred_element_type=F32)
    lane = lax.broadcasted_iota(jnp.int32, (GROUP, CW), 1)
    for i in range(CHUNK):
        blk = kt if i == 0 else jnp.where(lane >= i * GROUP, pltpu.roll(kt, i * GROUP, axis=1), 0.0)
        m_ref[0, i * GROUP:(i + 1) * GROUP, :] = blk.astype(BF16)

    ab_r, ab_i, _, _ = discretise(arow[0:1, :], arow[1:2, :])
    cr, ci = ab_r, ab_i
    for _ in range(4):
        cr, ci = cr * cr - ci * ci, 2.0 * cr * ci
    avec_ref[0, 0:1, :] = jnp.concatenate([cr, cr], axis=1)
    avec_ref[0, 1:2, :] = jnp.concatenate([-ci, ci], axis=1)
    avec_ref[0, 2:3, :] = jnp.concatenate([ab_r, ab_r], axis=1)
    avec_ref[0, 3:4, :] = jnp.concatenate([-ab_i, ab_i], axis=1)


def _ssm_prep(log_dt, a_re, a_im, b_re, b_im, c_re, c_im):
    g, p, c = N_GROUPS, N_STATE, GROUP
    logdt = log_dt.reshape(g, 1, 1)
    ac = jnp.stack([a_re, a_im], axis=-1)
    ar = jnp.stack([a_re, a_im], axis=1)
    bt = jnp.tile(jnp.stack([b_re, b_im], axis=1), (1, 1, 1, CHUNK))
    btT = jnp.stack([b_re, b_im], axis=1).transpose(0, 1, 3, 2)
    ct = jnp.tile(jnp.stack([c_re, c_im], axis=1).transpose(0, 1, 3, 2), (1, 1, 1, CHUNK))
    blk = lambda *s: pl.BlockSpec((1,) + s, lambda i: (i,) + (0,) * len(s))
    return pl.pallas_call(
        _ssm_prep_kernel,
        grid=(g,),
        in_specs=[blk(1, 1), blk(p, 2), blk(2, p), blk(2, p, CW), blk(2, c, p), blk(2, p, CW)],
        out_specs=[blk(CW, CW), blk(SW, CW), blk(SW, CW), blk(SW, SW), blk(SW, SW), blk(4, SW)],
        out_shape=[jax.ShapeDtypeStruct((g, CW, CW), BF16),
                   jax.ShapeDtypeStruct((g, SW, CW), BF16),
                   jax.ShapeDtypeStruct((g, SW, CW), BF16),
                   jax.ShapeDtypeStruct((g, SW, SW), BF16),
                   jax.ShapeDtypeStruct((g, SW, SW), BF16),
                   jax.ShapeDtypeStruct((g, 4, SW), F32)],
        name="ssm_prep",
    )(logdt, ac, ar, bt, btT, ct)


def _in_proj_kernel(xp_ref, xe_ref, g0_ref, b0_ref, w_ref, u_ref, bg_ref, v_ref, h_scr, cg_scr):
    i = pl.program_id(0)
    j = pl.program_id(1)

    @pl.when((j == 0) & (i < N_PROMPT_TILES))
    def _():
        h_scr[...] = _layer_norm(xp_ref[...], g0_ref[...], b0_ref[...]).astype(BF16)

    @pl.when((j == 0) & (i == N_PROMPT_TILES))
    def _():
        h_scr[...] = _layer_norm(xe_ref[...], g0_ref[...], b0_ref[...]).astype(BF16)

    r = _dot(h_scr[...], w_ref[...])

    @pl.when(j == 0)
    def _():
        u_ref[...] = r

    @pl.when(j == 1)
    def _():
        bg_ref[...] = r.astype(BF16)

    @pl.when(j == 2)
    def _():
        cg_scr[...] = r

    @pl.when(j == 3)
    def _():
        v_ref[...] = (cg_scr[...] * r).astype(BF16)


def _in_proj(xp, xe, g0, b0, w_in):
    row = lambda i, j: (i, 0)
    out = jax.ShapeDtypeStruct((M_PAD, D_SSM), BF16)
    return pl.pallas_call(
        _in_proj_kernel,
        grid=(N_TILES, 4),
        in_specs=[pl.BlockSpec((TM, D_MODEL), lambda i, j: (jnp.minimum(i, N_PROMPT_TILES - 1), 0)),
                  pl.BlockSpec((TM, D_MODEL), lambda i, j: (0, 0)),
                  pl.BlockSpec((1, D_MODEL), lambda i, j: (0, 0)),
                  pl.BlockSpec((1, D_MODEL), lambda i, j: (0, 0)),
                  pl.BlockSpec((D_MODEL, D_SSM), lambda i, j: (0, j))],
        out_specs=[pl.BlockSpec((TM, D_SSM), row)] * 3,
        out_shape=[jax.ShapeDtypeStruct((M_PAD, D_SSM), F32), out, out],
        scratch_shapes=[pltpu.VMEM((TM, D_MODEL), BF16), pltpu.VMEM((TM, D_CONV), F32)],
        compiler_params=pltpu.CompilerParams(
            dimension_semantics=("arbitrary", "arbitrary"), vmem_limit_bytes=VMEM_LIMIT),
        name="in_proj",
    )(xp, xe, g0, b0, w_in)


SUB = 16
SEQ_CHUNKS = SEQ // CHUNK
LHS_ROWS = N_BATCH * SEQ_CHUNKS + SUB
META_LHS = N_BATCH * SEQ_CHUNKS
PITCH = SEQ_CHUNKS + ROW8
REC_ROWS = N_BATCH * PITCH + ROW8
META_REC = N_BATCH * PITCH
SAMPLE_LO = N_PROMPT_ROWS + SAMPLE_ROW0
SAMPLE_HI = SAMPLE_LO + N_SAMPLE


def _block_transpose(a):
    a = list(a)
    blk = lax.broadcasted_iota(jnp.int32, a[0].shape, 1) // GROUP
    for d in (4, 2, 1):
        w = d * GROUP
        keep = (blk & d) == 0
        for j in range(GB):
            if j & d == 0:
                lo, hi = a[j], a[j + d]
                a[j] = jnp.where(keep, lo, pltpu.roll(hi, w, axis=1))
                a[j + d] = jnp.where(keep, pltpu.roll(lo, SW - w, axis=1), hi)
    return a


def _ssm_scan_kernel(u_ref, m_ref, wst_ref, cpow_ref, bpad_ref, cpad_ref, avec_ref, s0_ref,
                     y_ref, sfin_ref, snew_ref, lhs_scr, x_scr, xs_scr, sp_scr, yg_scr):
    def swap_halves(s):
        return pltpu.roll(s, N_STATE, axis=1)

    half = SW

    for h in range(2):
        src = [u_ref[pl.ds(8 * h + j, LHS_ROWS, stride=CHUNK), :] for j in range(GB)]
        for gl, w in enumerate(_block_transpose(src)):
            lhs_scr[gl, :, h * half:(h + 1) * half] = w.astype(BF16)

    def to_rec(ref, gl, x):
        for n in range(N_BATCH):
            ref[gl, n * PITCH:n * PITCH + SEQ_CHUNKS, :] = x[n * SEQ_CHUNKS:(n + 1) * SEQ_CHUNKS]
        ref[gl, META_REC:META_REC + ROW8, :] = x[META_LHS:META_LHS + ROW8]

    for gl in range(GB):
        x = _dot_nt(lhs_scr[gl], wst_ref[gl])
        to_rec(x_scr, gl, x)
        to_rec(xs_scr, gl, swap_halves(x))

    ar16 = [avec_ref[gl, 0:1, :] for gl in range(GB)]
    ax16 = [avec_ref[gl, 1:2, :] for gl in range(GB)]

    def step(k, carry):
        new = []
        for gl in range(GB):
            s, t = carry[gl]
            rows = pl.ds(k, N_BATCH, stride=PITCH)
            sp_scr[gl, rows, :] = s
            new.append((ar16[gl] * s + ax16[gl] * t + x_scr[gl, rows, :],
                        ar16[gl] * t - ax16[gl] * s + xs_scr[gl, rows, :]))
        return tuple(new)

    first = lambda ref, gl: jnp.broadcast_to(ref[gl, META_REC:META_REC + 1, :], (N_BATCH, SW))
    init = tuple((first(x_scr, gl), first(xs_scr, gl)) for gl in range(GB))
    fin = lax.fori_loop(0, SEQ_CHUNKS, step, init)

    us = u_ref[SAMPLE_LO:SAMPLE_HI, :].astype(BF16)
    ys = jnp.zeros((N_SAMPLE, SW), F32)
    for gl in range(GB):
        sfin_ref[gl] = fin[gl][0]
        sp = jnp.concatenate([sp_scr[gl, n * PITCH:n * PITCH + SEQ_CHUNKS, :] for n in range(N_BATCH)]
                             + [jnp.zeros((SUB, SW), F32)], axis=0)
        yg_scr[gl] = _dot(lhs_scr[gl], m_ref[gl]) + _dot(sp.astype(BF16), cpow_ref[gl])
        s0 = s0_ref[gl]
        sn = (avec_ref[gl, 2:3, :] * s0 + avec_ref[gl, 3:4, :] * swap_halves(s0)
              + _dot_nt(us, bpad_ref[gl]))
        snew_ref[gl] = sn
        ys = ys + _dot(sn.astype(BF16), cpad_ref[gl])

    for h in range(2):
        src = [yg_scr[gl, :, h * half:(h + 1) * half] for gl in range(GB)]
        for j, w in enumerate(_block_transpose(src)):
            y_ref[pl.ds(8 * h + j, META_LHS, stride=CHUNK), :] = w[0:META_LHS]
            y_ref[N_PROMPT_ROWS + 8 * h + j:N_PROMPT_ROWS + 8 * h + j + 1, :] = w[META_LHS:META_LHS + 1]
    y_ref[SAMPLE_LO:SAMPLE_HI, :] = ys
    y_ref[SAMPLE_HI:M_PAD, :] = jnp.zeros((M_PAD - SAMPLE_HI, SW), F32)


def _ssm_scan(u, m, wst, cpow, bpad, cpad, avec, s0):
    blk = lambda *s: pl.BlockSpec((GB,) + s, lambda i: (i,) + (0,) * len(s))
    col = pl.BlockSpec((M_PAD, SW), lambda i: (0, i))
    return pl.pallas_call(
        _ssm_scan_kernel,
        grid=(N_GROUPS // GB,),
        in_specs=[col, blk(CW, CW), blk(SW, CW), blk(SW, CW), blk(SW, SW), blk(SW, SW), blk(4, SW),
                  blk(N_SAMPLE, SW)],
        out_specs=[col, blk(N_BATCH, SW), blk(N_SAMPLE, SW)],
        out_shape=[jax.ShapeDtypeStruct((M_PAD, D_SSM), F32),
                   jax.ShapeDtypeStruct((N_GROUPS, N_BATCH, SW), F32),
                   jax.ShapeDtypeStruct((N_GROUPS, N_SAMPLE, SW), F32)],
        scratch_shapes=[pltpu.VMEM((GB, LHS_ROWS, CW), BF16), pltpu.VMEM((GB, REC_ROWS, SW), F32),
                        pltpu.VMEM((GB, REC_ROWS, SW), F32), pltpu.VMEM((GB, REC_ROWS, SW), F32),
                        pltpu.VMEM((GB, LHS_ROWS, CW), F32)],
        compiler_params=pltpu.CompilerParams(
            dimension_semantics=("arbitrary",), vmem_limit_bytes=VMEM_LIMIT),
        name="ssm_scan",
    )(u, m, wst, cpow, bpad, cpad, avec, s0)


HALO = 16
TMX = 256
NX_PROMPT = N_PROMPT_ROWS // TMX
NX_TILES = M_PAD // TMX


def _mix_out_kernel(y_ref, u_ref, bg_ref, v_ref, vh_ref, p1_ref, p2_ref, xp_ref, xe_ref,
                    g0_ref, b0_ref, dskip_ref, wglu_ref, cw_ref, gs_ref, gc_ref, wout_ref,
                    g1_ref, b1_ref, x1_ref, x1b_ref, vpad_scr):
    i = pl.program_id(0)
    is_extra = i == NX_PROMPT

    ys = _gelu_tanh(y_ref[...] + dskip_ref[...] * u_ref[...])
    gate = _dot(ys.astype(BF16), wglu_ref[...])
    r_ssm = _rms_norm(ys * _sigmoid(gate), gs_ref[...])

    vf = v_ref[...].astype(F32)
    halo = jnp.where(is_extra, 0.0, vh_ref[...].astype(F32)[HALO - 2:HALO, :])
    vpad_scr[ROW8 - 2:ROW8, :] = halo
    vpad_scr[ROW8:ROW8 + TMX, :] = vf
    row = lax.broadcasted_iota(jnp.int32, (TMX, 1), 0)
    sample = is_extra & (row >= SAMPLE_ROW0)
    vm1 = jnp.where(sample, p1_ref[...], vpad_scr[ROW8 - 1:ROW8 - 1 + TMX, :])
    vm2 = jnp.where(sample, p2_ref[...], vpad_scr[ROW8 - 2:ROW8 - 2 + TMX, :])
    conv = cw_ref[0:1, :] * vm2 + cw_ref[1:2, :] * vm1 + cw_ref[2:3, :] * vf
    r_conv = _rms_norm(bg_ref[...].astype(F32) * conv, gc_ref[...])

    mo = (_dot(r_ssm.astype(BF16), wout_ref[0:D_SSM, :])
          + _dot(r_conv.astype(BF16), wout_ref[D_SSM:D_MODEL, :]))

    def finish(x_ref):
        h0 = _layer_norm(x_ref[...], g0_ref[...], b0_ref[...])
        x1 = _layer_norm(ALPHA * h0 + mo, g1_ref[...], b1_ref[...])
        x1_ref[...] = x1
        x1b_ref[...] = x1.astype(BF16)

    @pl.when(i < NX_PROMPT)
    def _():
        finish(xp_ref)

    @pl.when(i >= NX_PROMPT)
    def _():
        finish(xe_ref)


def _mix_out(y, u, bg, v, p1, p2, xp, xe, g0, b0, dskip, wglu, cw, gs, gc, wout, g1, b1):
    row = lambda i: (i, 0)
    const = lambda i: (0, 0)
    tiles_per_seq = SEQ // TMX
    meta_block = N_PROMPT_ROWS // HALO

    def halo_map(i):
        prev = i * (TMX // HALO) - 1
        return (jnp.where(i % tiles_per_seq == 0, meta_block, prev), 0)

    vec = lambda n: pl.BlockSpec((1, n), const)
    return pl.pallas_call(
        _mix_out_kernel,
        grid=(NX_TILES,),
        in_specs=[pl.BlockSpec((TMX, D_SSM), row), pl.BlockSpec((TMX, D_SSM), row),
                  pl.BlockSpec((TMX, D_CONV), row), pl.BlockSpec((TMX, D_CONV), row),
                  pl.BlockSpec((HALO, D_CONV), halo_map),
                  pl.BlockSpec((TMX, D_CONV), const), pl.BlockSpec((TMX, D_CONV), const),
                  pl.BlockSpec((TMX, D_MODEL), lambda i: (jnp.minimum(i, NX_PROMPT - 1), 0)),
                  pl.BlockSpec((TMX, D_MODEL), lambda i: (jnp.maximum(i - NX_PROMPT, 0), 0)),
                  vec(D_MODEL), vec(D_MODEL), vec(D_SSM),
                  pl.BlockSpec((D_SSM, D_SSM), const), pl.BlockSpec((CONV_W, D_CONV), const),
                  vec(D_SSM), vec(D_CONV), pl.BlockSpec((D_MODEL, D_MODEL), const),
                  vec(D_MODEL), vec(D_MODEL)],
        out_specs=[pl.BlockSpec((TMX, D_MODEL), row), pl.BlockSpec((TMX, D_MODEL), row)],
        out_shape=[jax.ShapeDtypeStruct((M_PAD, D_MODEL), F32),
                   jax.ShapeDtypeStruct((M_PAD, D_MODEL), BF16)],
        scratch_shapes=[pltpu.VMEM((ROW8 + TMX, D_CONV), F32)],
        compiler_params=pltpu.CompilerParams(
            dimension_semantics=("arbitrary",), vmem_limit_bytes=VMEM_LIMIT),
        name="mix_out",
    )(y, u, bg, v, v, p1, p2, xp, xe, g0, b0, dskip, wglu, cw, gs, gc, wout, g1, b1)


def _ffn_up_kernel(xb_ref, wg_ref, wu_ref, wd_ref, act_ref, wdb_ref, wg_scr, wu_scr):
    @pl.when(pl.program_id(1) == 0)
    def _():
        wg_scr[...] = wg_ref[...].astype(BF16)
        wu_scr[...] = wu_ref[...].astype(BF16)
        wdb_ref[...] = wd_ref[...].astype(BF16)

    xb = xb_ref[...]
    gate = _dot(xb, wg_scr[...])
    up = _dot(xb, wu_scr[...])
    act_ref[...] = (gate * _sigmoid(gate) * up).astype(BF16)


TM_UP = M_PAD // 8
TF_DOWN = D_FF // 2
EXTRA_ROWS = 256


def _ffn_up(x1b, wg, wu, wd):
    wcol = pl.BlockSpec((D_MODEL, TF), lambda f, i: (0, f))
    wrow = pl.BlockSpec((TF, D_MODEL), lambda f, i: (f, 0))
    return pl.pallas_call(
        _ffn_up_kernel,
        grid=(D_FF // TF, M_PAD // TM_UP),
        in_specs=[pl.BlockSpec((TM_UP, D_MODEL), lambda f, i: (i, 0)), wcol, wcol, wrow],
        out_specs=[pl.BlockSpec((TM_UP, TF), lambda f, i: (i, f)), wrow],
        out_shape=[jax.ShapeDtypeStruct((M_PAD, D_FF), BF16),
                   jax.ShapeDtypeStruct((D_FF, D_MODEL), BF16)],
        scratch_shapes=[pltpu.VMEM((D_MODEL, TF), BF16), pltpu.VMEM((D_MODEL, TF), BF16)],
        compiler_params=pltpu.CompilerParams(
            dimension_semantics=("arbitrary", "arbitrary"), vmem_limit_bytes=VMEM_LIMIT),
        name="ffn_up",
    )(x1b, wg, wu, wd)


def _ffn_down_kernel(act_ref, x_ref, wd_ref, g2_ref, b2_ref, yp_ref, ye_ref, acc_scr):
    i = pl.program_id(0)
    f = pl.program_id(1)

    part = _dot(act_ref[...], wd_ref[...])
    last = f == pl.num_programs(1) - 1

    @pl.when(f == 0)
    def _():
        acc_scr[...] = part

    @pl.when((f > 0) & jnp.logical_not(last))
    def _():
        acc_scr[...] += part

    def finish(o_ref, rows):
        z = ALPHA * x_ref[0:rows, :] + (acc_scr[0:rows, :] + part[0:rows])
        o_ref[...] = _layer_norm(z, g2_ref[...], b2_ref[...])

    @pl.when(last & (i < N_PROMPT_TILES))
    def _():
        finish(yp_ref, TM)

    @pl.when(last & (i == N_PROMPT_TILES))
    def _():
        finish(ye_ref, EXTRA_ROWS)


def _ffn_down(act, x1, wd, g2, b2):
    row = lambda i, f: (i, 0)
    const = lambda i, f: (0, 0)
    return pl.pallas_call(
        _ffn_down_kernel,
        grid=(N_TILES, D_FF // TF_DOWN),
        in_specs=[pl.BlockSpec((TM, TF_DOWN), lambda i, f: (i, f)), pl.BlockSpec((TM, D_MODEL), row),
                  pl.BlockSpec((TF_DOWN, D_MODEL), lambda i, f: (f, 0)),
                  pl.BlockSpec((1, D_MODEL), const), pl.BlockSpec((1, D_MODEL), const)],
        out_specs=[pl.BlockSpec((TM, D_MODEL), lambda i, f: (jnp.minimum(i, N_PROMPT_TILES - 1), 0)),
                   pl.BlockSpec((EXTRA_ROWS, D_MODEL), const)],
        out_shape=[jax.ShapeDtypeStruct((N_PROMPT_ROWS, D_MODEL), F32),
                   jax.ShapeDtypeStruct((EXTRA_ROWS, D_MODEL), F32)],
        scratch_shapes=[pltpu.VMEM((TM, D_MODEL), F32)],
        compiler_params=pltpu.CompilerParams(
            dimension_semantics=("arbitrary", "arbitrary"), vmem_limit_bytes=VMEM_LIMIT),
        name="ffn_down",
    )(act, x1, wd, g2, b2)


def _pad_rows(a, n):
    return jnp.pad(a, ((0, n - a.shape[0]), (0, 0)))


def kernel(x_prompt, x_sample, state_ssm_re, state_ssm_im, state_conv, meta_tokens, ln0_g, ln0_b,
           w_in, ssm_a_re, ssm_a_im, ssm_log_dt, ssm_b_re, ssm_b_im, ssm_c_re, ssm_c_im, ssm_d,
           ssm_w_glu, conv_w, g_ssm_out, g_conv_out, w_out, ln1_g, ln1_b, w_gate, w_up, w_down,
           ln2_g, ln2_b):
    assert x_prompt.shape == (N_BATCH, SEQ, D_MODEL) and x_sample.shape == (N_SAMPLE, 1, D_MODEL)
    assert w_in.shape[0] == 1, "single layer"
    g, p, c = N_GROUPS, N_STATE, GROUP
    vec = lambda a: a.reshape(1, -1)

    xp = x_prompt.reshape(N_PROMPT_ROWS, D_MODEL)
    xe = _pad_rows(jnp.concatenate([meta_tokens, x_sample[:, 0, :]], axis=0), TM)
    g0, b0 = vec(ln0_g), vec(ln0_b)

    m, wst, cpow, bpad, cpad, avec = _ssm_prep(ssm_log_dt[0], ssm_a_re[0], ssm_a_im[0], ssm_b_re[0],
                                               ssm_b_im[0], ssm_c_re[0], ssm_c_im[0])

    u, bg, v = _in_proj(xp, xe, g0, b0, w_in[0].astype(BF16))

    s_lo = SAMPLE_LO
    s0 = jnp.concatenate([state_ssm_re[0], state_ssm_im[0]], axis=-1).transpose(1, 0, 2)
    y_all, s_fin, s_new = _ssm_scan(u, m, wst, cpow, bpad, cpad, avec, s0)

    prev = jnp.pad(state_conv[0], ((SAMPLE_ROW0, TMX - SAMPLE_ROW0 - N_SAMPLE), (0, 0), (0, 0)))
    x1, x1b = _mix_out(y_all, u, bg, v, prev[:, 1], prev[:, 0], xp, xe, g0, b0, vec(ssm_d[0]),
                       ssm_w_glu[0].astype(BF16), conv_w[0], vec(g_ssm_out[0]), vec(g_conv_out[0]),
                       w_out[0].astype(BF16), vec(ln1_g[0]), vec(ln1_b[0]))

    act, wd_bf = _ffn_up(x1b, w_gate[0], w_up[0], w_down[0])
    yp, ye = _ffn_down(act, x1, wd_bf, vec(ln2_g[0]), vec(ln2_b[0]))

    y_prompt = yp.reshape(N_BATCH, SEQ, D_MODEL)
    y_sample = ye[SAMPLE_ROW0:SAMPLE_ROW0 + N_SAMPLE].reshape(N_SAMPLE, 1, D_MODEL)

    sd = state_ssm_re.dtype
    s_p = s_fin.transpose(1, 0, 2)
    s_s = s_new.transpose(1, 0, 2)
    cd = state_conv.dtype
    conv_p = jnp.stack([v[n * SEQ + SEQ - (CONV_W - 1):(n + 1) * SEQ] for n in range(N_BATCH)]).astype(cd)
    conv_s = jnp.stack([state_conv[0][:, 1], v[s_lo:s_lo + N_SAMPLE].astype(cd)], axis=1)
    return (y_prompt, y_sample,
            s_p[None, ..., :p].astype(sd), s_p[None, ..., p:].astype(sd), conv_p[None],
            s_s[None, ..., :p].astype(sd), s_s[None, ..., p:].astype(sd), conv_s[None])
```

```python
import functools

import jax
import jax.numpy as jnp
import numpy as np
from jax import lax
from jax.experimental import pallas as pl
from jax.experimental.pallas import tpu as pltpu

D_MODEL = 2048
D_SSM = 1024
D_CONV = 1024
N_GROUPS = 64
GROUP = 16
N_STATE = 64
N_META = 16
CONV_W = 3
D_FF = 5632
LN_EPS = 1e-5
RMS_EPS = 1e-6
ALPHA = 2.0 ** 0.25

N_BATCH = 4
SEQ = 2048
N_SAMPLE = 128
CHUNK = 16
N_CHUNKS = SEQ // CHUNK + 1
ROW8 = 8
CW = CHUNK * GROUP
SW = 2 * N_STATE

TM = 512
N_PROMPT_ROWS = N_BATCH * SEQ
N_PROMPT_TILES = N_PROMPT_ROWS // TM
N_TILES = N_PROMPT_TILES + 1
M_PAD = N_TILES * TM
SAMPLE_ROW0 = N_META
TF = 512
GB = 8
VMEM_LIMIT = 60 * 1024 * 1024

F32 = jnp.float32
BF16 = jnp.bfloat16


def _layer_norm(x, g, b):
    mu = jnp.mean(x, axis=-1, keepdims=True)
    xc = x - mu
    var = jnp.mean(xc * xc, axis=-1, keepdims=True)
    return xc * lax.rsqrt(var + LN_EPS) * g + b


def _rms_norm(x, g):
    return x * lax.rsqrt(jnp.mean(x * x, axis=-1, keepdims=True) + RMS_EPS) * g


def _gelu_tanh(x):
    c = np.float32(np.sqrt(2.0 / np.pi))
    return 0.5 * x * (1.0 + jnp.tanh(c * (x + np.float32(0.044715) * (x * x * x))))


def _sigmoid(x):
    return 1.0 / (1.0 + jnp.exp(-x))


def _dot(a, b):
    return jnp.dot(a, b, preferred_element_type=F32)


def _dot_nt(a, b):
    return lax.dot_general(a, b, (((1,), (1,)), ((), ())), preferred_element_type=F32)


def _ssm_prep_kernel(logdt_ref, are_ref, aim_ref, b1_ref, b2_ref, c1_ref, c2_ref,
                     m_ref, wst_ref, cpow_ref, bpad_ref, cpad_ref, avec_ref):
    dt = jnp.exp(logdt_ref[...])
    a_re, a_im = are_ref[...], aim_ref[...]
    mag = jnp.exp(dt * a_re)
    ang = dt * a_im
    ab_re = mag * jnp.cos(ang)
    ab_im = mag * jnp.sin(ang)
    n_re = ab_re - 1.0
    den = a_re * a_re + a_im * a_im
    f_re = (n_re * a_re + ab_im * a_im) / den
    f_im = (ab_im * a_re - n_re * a_im) / den
    sgn = jnp.where(lax.broadcasted_iota(jnp.int32, (GB, SW), 1) < N_STATE, -1.0, 1.0)

    pr, pi = jnp.ones_like(ab_re), jnp.zeros_like(ab_re)
    pw = [(pr, pi)]
    for _ in range(CHUNK):
        pr, pi = pr * ab_re - pi * ab_im, pr * ab_im + pi * ab_re
        pw.append((pr, sgn * pi))
    fx = sgn * f_im
    lane = lax.broadcasted_iota(jnp.int32, (GROUP, CW), 1)

    for gl in range(GB):
        row = lambda x: x[gl:gl + 1, :]
        bb1 = row(f_re) * b1_ref[gl] + row(fx) * b2_ref[gl]
        bb2 = row(f_re) * b2_ref[gl] - row(fx) * b1_ref[gl]
        cc1 = -row(sgn) * c1_ref[gl]
        cc2 = -row(sgn) * c2_ref[gl]

        cq = []
        for s in range(CHUNK):
            wr, wx = pw[CHUNK - 1 - s]
            wst_ref[gl, s * GROUP:(s + 1) * GROUP, :] = (row(wr) * bb1 + row(wx) * bb2).astype(BF16)
            qr, qx = pw[s + 1]
            cpow_ref[gl, s * GROUP:(s + 1) * GROUP, :] = (row(qr) * cc1 + row(qx) * cc2).astype(BF16)
            dr, dx = pw[s]
            cq.append(row(dr) * cc1 + row(dx) * cc2)
        kt = lax.dot_general(bb1, jnp.concatenate(cq, axis=0), (((1,), (1,)), ((), ())),
                             precision=lax.Precision.HIGHEST, preferred_element_type=F32)
        for i in range(CHUNK):
            blk = kt if i == 0 else jnp.where(lane >= i * GROUP, pltpu.roll(kt, i * GROUP, axis=1), 0.0)
            m_ref[gl, i * GROUP:(i + 1) * GROUP, :] = blk.astype(BF16)

        zeros = jnp.zeros((SW, SW), BF16)
        bpad_ref[gl] = zeros
        cpad_ref[gl] = zeros
        bpad_ref[gl, gl * GROUP:(gl + 1) * GROUP, :] = bb1.astype(BF16)
        cpad_ref[gl, gl * GROUP:(gl + 1) * GROUP, :] = cq[0].astype(BF16)

        avec_ref[gl, 0:1, :] = row(pw[CHUNK][0])
        avec_ref[gl, 1:2, :] = row(pw[CHUNK][1])
        avec_ref[gl, 2:3, :] = row(pw[1][0])
        avec_ref[gl, 3:4, :] = row(pw[1][1])


def _ssm_prep(log_dt, a_re, a_im, b_re, b_im, c_re, c_im):
    g, c = N_GROUPS, GROUP
    dup = lambda x: jnp.concatenate([x, x], axis=-1)
    bt_re, bt_im = b_re.transpose(0, 2, 1), b_im.transpose(0, 2, 1)
    cat = lambda x, y: jnp.concatenate([x, y], axis=-1)
    vec = pl.BlockSpec((GB, SW), lambda i: (i, 0))
    blk = lambda *s: pl.BlockSpec((GB,) + s, lambda i: (i,) + (0,) * len(s))
    return pl.pallas_call(
        _ssm_prep_kernel,
        grid=(g // GB,),
        in_specs=[vec, vec, vec, blk(c, SW), blk(c, SW), blk(c, SW), blk(c, SW)],
        out_specs=[blk(CW, CW), blk(CW, SW), blk(CW, SW), blk(SW, SW), blk(SW, SW), blk(4, SW)],
        out_shape=[jax.ShapeDtypeStruct((g, CW, CW), BF16),
                   jax.ShapeDtypeStruct((g, CW, SW), BF16),
                   jax.ShapeDtypeStruct((g, CW, SW), BF16),
                   jax.ShapeDtypeStruct((g, SW, SW), BF16),
                   jax.ShapeDtypeStruct((g, SW, SW), BF16),
                   jax.ShapeDtypeStruct((g, 4, SW), F32)],
        name="ssm_prep",
    )(jnp.broadcast_to(log_dt[:, None], (g, SW)), dup(a_re), dup(a_im),
      cat(bt_re, bt_im), cat(bt_im, bt_re), cat(c_re, c_im), cat(c_im, c_re))


def _in_proj_kernel(xp_ref, xe_ref, g0_ref, b0_ref, w_ref, u_ref, bg_ref, v_ref, h_scr, cg_scr):
    i = pl.program_id(0)
    j = pl.program_id(1)

    @pl.when((j == 0) & (i < N_PROMPT_TILES))
    def _():
        h_scr[...] = _layer_norm(xp_ref[...], g0_ref[...], b0_ref[...]).astype(BF16)

    @pl.when((j == 0) & (i == N_PROMPT_TILES))
    def _():
        h_scr[...] = _layer_norm(xe_ref[...], g0_ref[...], b0_ref[...]).astype(BF16)

    r = _dot(h_scr[...], w_ref[...])

    @pl.when(j == 0)
    def _():
        u_ref[...] = r

    @pl.when(j == 1)
    def _():
        bg_ref[...] = r.astype(BF16)

    @pl.when(j == 2)
    def _():
        cg_scr[...] = r

    @pl.when(j == 3)
    def _():
        v_ref[...] = (cg_scr[...] * r).astype(BF16)


def _in_proj(xp, xe, g0, b0, w_in):
    row = lambda i, j: (i, 0)
    out = jax.ShapeDtypeStruct((M_PAD, D_SSM), BF16)
    return pl.pallas_call(
        _in_proj_kernel,
        grid=(N_TILES, 4),
        in_specs=[pl.BlockSpec((TM, D_MODEL), lambda i, j: (jnp.minimum(i, N_PROMPT_TILES - 1), 0)),
                  pl.BlockSpec((TM, D_MODEL), lambda i, j: (0, 0)),
                  pl.BlockSpec((1, D_MODEL), lambda i, j: (0, 0)),
                  pl.BlockSpec((1, D_MODEL), lambda i, j: (0, 0)),
                  pl.BlockSpec((D_MODEL, D_SSM), lambda i, j: (0, j))],
        out_specs=[pl.BlockSpec((TM, D_SSM), row)] * 3,
        out_shape=[jax.ShapeDtypeStruct((M_PAD, D_SSM), F32), out, out],
        scratch_shapes=[pltpu.VMEM((TM, D_MODEL), BF16), pltpu.VMEM((TM, D_CONV), F32)],
        compiler_params=pltpu.CompilerParams(
            dimension_semantics=("arbitrary", "arbitrary"), vmem_limit_bytes=VMEM_LIMIT),
        name="in_proj",
    )(xp, xe, g0, b0, w_in)


SUB = 16
SEQ_CHUNKS = SEQ // CHUNK
LHS_ROWS = N_BATCH * SEQ_CHUNKS + SUB
META_LHS = N_BATCH * SEQ_CHUNKS
PITCH = SEQ_CHUNKS + ROW8
REC_ROWS = N_BATCH * PITCH + ROW8
META_REC = N_BATCH * PITCH
SAMPLE_LO = N_PROMPT_ROWS + SAMPLE_ROW0
SAMPLE_HI = SAMPLE_LO + N_SAMPLE


def _block_transpose(a):
    a = list(a)
    blk = lax.broadcasted_iota(jnp.int32, a[0].shape, 1) // GROUP
    for d in (4, 2, 1):
        w = d * GROUP
        keep = (blk & d) == 0
        for j in range(GB):
            if j & d == 0:
                lo, hi = a[j], a[j + d]
                a[j] = jnp.where(keep, lo, pltpu.roll(hi, w, axis=1))
                a[j + d] = jnp.where(keep, pltpu.roll(lo, SW - w, axis=1), hi)
    return a


def _ssm_scan_kernel(u_ref, m_ref, wst_ref, cpow_ref, bpad_ref, cpad_ref, avec_ref, s0_ref,
                     y_ref, sfin_ref, snew_ref, lhs_scr, x_scr, xs_scr, sp_scr, yg_scr):
    def swap_halves(s):
        return pltpu.roll(s, N_STATE, axis=1)

    half = SW

    for h in range(2):
        src = [u_ref[pl.ds(8 * h + j, LHS_ROWS, stride=CHUNK), :] for j in range(GB)]
        for gl, w in enumerate(_block_transpose(src)):
            lhs_scr[gl, :, h * half:(h + 1) * half] = w.astype(BF16)

    def to_rec(ref, gl, x):
        for n in range(N_BATCH):
            ref[gl, n * PITCH:n * PITCH + SEQ_CHUNKS, :] = x[n * SEQ_CHUNKS:(n + 1) * SEQ_CHUNKS]
        ref[gl, META_REC:META_REC + ROW8, :] = x[META_LHS:META_LHS + ROW8]

    for gl in range(GB):
        x = _dot(lhs_scr[gl], wst_ref[gl])
        to_rec(x_scr, gl, x)
        to_rec(xs_scr, gl, swap_halves(x))

    ar16 = [avec_ref[gl, 0:1, :] for gl in range(GB)]
    ax16 = [avec_ref[gl, 1:2, :] for gl in range(GB)]

    def step(k, carry):
        new = []
        for gl in range(GB):
            s, t = carry[gl]
            rows = pl.ds(k, N_BATCH, stride=PITCH)
            sp_scr[gl, rows, :] = s
            new.append((ar16[gl] * s + ax16[gl] * t + x_scr[gl, rows, :],
                        ar16[gl] * t - ax16[gl] * s + xs_scr[gl, rows, :]))
        return tuple(new)

    first = lambda ref, gl: jnp.broadcast_to(ref[gl, META_REC:META_REC + 1, :], (N_BATCH, SW))
    init = tuple((first(x_scr, gl), first(xs_scr, gl)) for gl in range(GB))
    fin = lax.fori_loop(0, SEQ_CHUNKS, step, init)

    us = u_ref[SAMPLE_LO:SAMPLE_HI, :].astype(BF16)
    ys = jnp.zeros((N_SAMPLE, SW), F32)
    for gl in range(GB):
        sfin_ref[gl] = fin[gl][0]
        sp = jnp.concatenate([sp_scr[gl, n * PITCH:n * PITCH + SEQ_CHUNKS, :] for n in range(N_BATCH)]
                             + [jnp.zeros((SUB, SW), F32)], axis=0)
        yg_scr[gl] = _dot(lhs_scr[gl], m_ref[gl]) + _dot_nt(sp.astype(BF16), cpow_ref[gl])
        s0 = s0_ref[gl]
        sn = (avec_ref[gl, 2:3, :] * s0 + avec_ref[gl, 3:4, :] * swap_halves(s0)
              + _dot(us, bpad_ref[gl]))
        snew_ref[gl] = sn
        ys = ys + _dot_nt(sn.astype(BF16), cpad_ref[gl])

    for h in range(2):
        src = [yg_scr[gl, :, h * half:(h + 1) * half] for gl in range(GB)]
        for j, w in enumerate(_block_transpose(src)):
            y_ref[pl.ds(8 * h + j, META_LHS, stride=CHUNK), :] = w[0:META_LHS]
            y_ref[N_PROMPT_ROWS + 8 * h + j:N_PROMPT_ROWS + 8 * h + j + 1, :] = w[META_LHS:META_LHS + 1]
    y_ref[SAMPLE_LO:SAMPLE_HI, :] = ys
    y_ref[SAMPLE_HI:M_PAD, :] = jnp.zeros((M_PAD - SAMPLE_HI, SW), F32)


def _ssm_scan(u, m, wst, cpow, bpad, cpad, avec, s0):
    blk = lambda *s: pl.BlockSpec((GB,) + s, lambda i: (i,) + (0,) * len(s))
    col = pl.BlockSpec((M_PAD, SW), lambda i: (0, i))
    return pl.pallas_call(
        _ssm_scan_kernel,
        grid=(N_GROUPS // GB,),
        in_specs=[col, blk(CW, CW), blk(CW, SW), blk(CW, SW), blk(SW, SW), blk(SW, SW), blk(4, SW),
                  blk(N_SAMPLE, SW)],
        out_specs=[col, blk(N_BATCH, SW), blk(N_SAMPLE, SW)],
        out_shape=[jax.ShapeDtypeStruct((M_PAD, D_SSM), F32),
                   jax.ShapeDtypeStruct((N_GROUPS, N_BATCH, SW), F32),
                   jax.ShapeDtypeStruct((N_GROUPS, N_SAMPLE, SW), F32)],
        scratch_shapes=[pltpu.VMEM((GB, LHS_ROWS, CW), BF16), pltpu.VMEM((GB, REC_ROWS, SW), F32),
                        pltpu.VMEM((GB, REC_ROWS, SW), F32), pltpu.VMEM((GB, REC_ROWS, SW), F32),
                        pltpu.VMEM((GB, LHS_ROWS, CW), F32)],
        compiler_params=pltpu.CompilerParams(
            dimension_semantics=("arbitrary",), vmem_limit_bytes=VMEM_LIMIT),
        name="ssm_scan",
    )(u, m, wst, cpow, bpad, cpad, avec, s0)


HALO = 16
TMX = 256
NX_PROMPT = N_PROMPT_ROWS // TMX
NX_TILES = M_PAD // TMX


def _mix_out_kernel(y_ref, u_ref, bg_ref, v_ref, vh_ref, p1_ref, p2_ref, xp_ref, xe_ref,
                    g0_ref, b0_ref, dskip_ref, wglu_ref, cw_ref, gs_ref, gc_ref, wout_ref,
                    g1_ref, b1_ref, x1_ref, x1b_ref, vpad_scr):
    i = pl.program_id(0)
    is_extra = i == NX_PROMPT

    ys = _gelu_tanh(y_ref[...] + dskip_ref[...] * u_ref[...])
    gate = _dot(ys.astype(BF16), wglu_ref[...])
    r_ssm = _rms_norm(ys * _sigmoid(gate), gs_ref[...])

    vf = v_ref[...].astype(F32)
    halo = jnp.where(is_extra, 0.0, vh_ref[...].astype(F32)[HALO - 2:HALO, :])
    vpad_scr[ROW8 - 2:ROW8, :] = halo
    vpad_scr[ROW8:ROW8 + TMX, :] = vf
    row = lax.broadcasted_iota(jnp.int32, (TMX, 1), 0)
    sample = is_extra & (row >= SAMPLE_ROW0)
    vm1 = jnp.where(sample, p1_ref[...], vpad_scr[ROW8 - 1:ROW8 - 1 + TMX, :])
    vm2 = jnp.where(sample, p2_ref[...], vpad_scr[ROW8 - 2:ROW8 - 2 + TMX, :])
    conv = cw_ref[0:1, :] * vm2 + cw_ref[1:2, :] * vm1 + cw_ref[2:3, :] * vf
    r_conv = _rms_norm(bg_ref[...].astype(F32) * conv, gc_ref[...])

    mo = (_dot(r_ssm.astype(BF16), wout_ref[0:D_SSM, :])
          + _dot(r_conv.astype(BF16), wout_ref[D_SSM:D_MODEL, :]))

    def finish(x_ref):
        h0 = _layer_norm(x_ref[...], g0_ref[...], b0_ref[...])
        x1 = _layer_norm(ALPHA * h0 + mo, g1_ref[...], b1_ref[...])
        x1_ref[...] = x1
        x1b_ref[...] = x1.astype(BF16)

    @pl.when(i < NX_PROMPT)
    def _():
        finish(xp_ref)

    @pl.when(i >= NX_PROMPT)
    def _():
        finish(xe_ref)


def _mix_out(y, u, bg, v, p1, p2, xp, xe, g0, b0, dskip, wglu, cw, gs, gc, wout, g1, b1):
    row = lambda i: (i, 0)
    const = lambda i: (0, 0)
    tiles_per_seq = SEQ // TMX
    meta_block = N_PROMPT_ROWS // HALO

    def halo_map(i):
        prev = i * (TMX // HALO) - 1
        return (jnp.where(i % tiles_per_seq == 0, meta_block, prev), 0)

    vec = lambda n: pl.BlockSpec((1, n), const)
    return pl.pallas_call(
        _mix_out_kernel,
        grid=(NX_TILES,),
        in_specs=[pl.BlockSpec((TMX, D_SSM), row), pl.BlockSpec((TMX, D_SSM), row),
                  pl.BlockSpec((TMX, D_CONV), row), pl.BlockSpec((TMX, D_CONV), row),
                  pl.BlockSpec((HALO, D_CONV), halo_map),
                  pl.BlockSpec((TMX, D_CONV), const), pl.BlockSpec((TMX, D_CONV), const),
                  pl.BlockSpec((TMX, D_MODEL), lambda i: (jnp.minimum(i, NX_PROMPT - 1), 0)),
                  pl.BlockSpec((TMX, D_MODEL), lambda i: (jnp.maximum(i - NX_PROMPT, 0), 0)),
                  vec(D_MODEL), vec(D_MODEL), vec(D_SSM),
                  pl.BlockSpec((D_SSM, D_SSM), const), pl.BlockSpec((CONV_W, D_CONV), const),
                  vec(D_SSM), vec(D_CONV), pl.BlockSpec((D_MODEL, D_MODEL), const),
                  vec(D_MODEL), vec(D_MODEL)],
        out_specs=[pl.BlockSpec((TMX, D_MODEL), row), pl.BlockSpec((TMX, D_MODEL), row)],
        out_shape=[jax.ShapeDtypeStruct((M_PAD, D_MODEL), F32),
                   jax.ShapeDtypeStruct((M_PAD, D_MODEL), BF16)],
        scratch_shapes=[pltpu.VMEM((ROW8 + TMX, D_CONV), F32)],
        compiler_params=pltpu.CompilerParams(
            dimension_semantics=("arbitrary",), vmem_limit_bytes=VMEM_LIMIT),
        name="mix_out",
    )(y, u, bg, v, v, p1, p2, xp, xe, g0, b0, dskip, wglu, cw, gs, gc, wout, g1, b1)


def _ffn_up_kernel(xb_ref, wg_ref, wu_ref, wd_ref, act_ref, wdb_ref, wg_scr, wu_scr):
    @pl.when(pl.program_id(1) == 0)
    def _():
        wg_scr[...] = wg_ref[...].astype(BF16)
        wu_scr[...] = wu_ref[...].astype(BF16)
        wdb_ref[...] = wd_ref[...].astype(BF16)

    xb = xb_ref[...]
    gate = _dot(xb, wg_scr[...])
    up = _dot(xb, wu_scr[...])
    act_ref[...] = (gate * _sigmoid(gate) * up).astype(BF16)


TM_UP = M_PAD // 8
TF_DOWN = D_FF // 2
EXTRA_ROWS = 256


def _ffn_up(x1b, wg, wu, wd):
    wcol = pl.BlockSpec((D_MODEL, TF), lambda f, i: (0, f))
    wrow = pl.BlockSpec((TF, D_MODEL), lambda f, i: (f, 0))
    return pl.pallas_call(
        _ffn_up_kernel,
        grid=(D_FF // TF, M_PAD // TM_UP),
        in_specs=[pl.BlockSpec((TM_UP, D_MODEL), lambda f, i: (i, 0)), wcol, wcol, wrow],
        out_specs=[pl.BlockSpec((TM_UP, TF), lambda f, i: (i, f)), wrow],
        out_shape=[jax.ShapeDtypeStruct((M_PAD, D_FF), BF16),
                   jax.ShapeDtypeStruct((D_FF, D_MODEL), BF16)],
        scratch_shapes=[pltpu.VMEM((D_MODEL, TF), BF16), pltpu.VMEM((D_MODEL, TF), BF16)],
        compiler_params=pltpu.CompilerParams(
            dimension_semantics=("arbitrary", "arbitrary"), vmem_limit_bytes=VMEM_LIMIT),
        name="ffn_up",
    )(x1b, wg, wu, wd)


def _ffn_down_kernel(act_ref, x_ref, wd_ref, g2_ref, b2_ref, yp_ref, ye_ref, acc_scr):
    i = pl.program_id(0)
    f = pl.program_id(1)

    part = _dot(act_ref[...], wd_ref[...])
    last = f == pl.num_programs(1) - 1

    @pl.when(f == 0)
    def _():
        acc_scr[...] = part

    @pl.when((f > 0) & jnp.logical_not(last))
    def _():
        acc_scr[...] += part

    def finish(o_ref, rows):
        z = ALPHA * x_ref[0:rows, :] + (acc_scr[0:rows, :] + part[0:rows])
        o_ref[...] = _layer_norm(z, g2_ref[...], b2_ref[...])

    @pl.when(last & (i < N_PROMPT_TILES))
    def _():
        finish(yp_ref, TM)

    @pl.when(last & (i == N_PROMPT_TILES))
    def _():
        finish(ye_ref, EXTRA_ROWS)


def _ffn_down(act, x1, wd, g2, b2):
    row = lambda i, f: (i, 0)
    const = lambda i, f: (0, 0)
    return pl.pallas_call(
        _ffn_down_kernel,
        grid=(N_TILES, D_FF // TF_DOWN),
        in_specs=[pl.BlockSpec((TM, TF_DOWN), lambda i, f: (i, f)), pl.BlockSpec((TM, D_MODEL), row),
                  pl.BlockSpec((TF_DOWN, D_MODEL), lambda i, f: (f, 0)),
                  pl.BlockSpec((1, D_MODEL), const), pl.BlockSpec((1, D_MODEL), const)],
        out_specs=[pl.BlockSpec((TM, D_MODEL), lambda i, f: (jnp.minimum(i, N_PROMPT_TILES - 1), 0)),
                   pl.BlockSpec((EXTRA_ROWS, D_MODEL), const)],
        out_shape=[jax.ShapeDtypeStruct((N_PROMPT_ROWS, D_MODEL), F32),
                   jax.ShapeDtypeStruct((EXTRA_ROWS, D_MODEL), F32)],
        scratch_shapes=[pltpu.VMEM((TM, D_MODEL), F32)],
        compiler_params=pltpu.CompilerParams(
            dimension_semantics=("arbitrary", "arbitrary"), vmem_limit_bytes=VMEM_LIMIT),
        name="ffn_down",
    )(act, x1, wd, g2, b2)


def _pad_rows(a, n):
    return jnp.pad(a, ((0, n - a.shape[0]), (0, 0)))


def kernel(x_prompt, x_sample, state_ssm_re, state_ssm_im, state_conv, meta_tokens, ln0_g, ln0_b,
           w_in, ssm_a_re, ssm_a_im, ssm_log_dt, ssm_b_re, ssm_b_im, ssm_c_re, ssm_c_im, ssm_d,
           ssm_w_glu, conv_w, g_ssm_out, g_conv_out, w_out, ln1_g, ln1_b, w_gate, w_up, w_down,
           ln2_g, ln2_b):
    assert x_prompt.shape == (N_BATCH, SEQ, D_MODEL) and x_sample.shape == (N_SAMPLE, 1, D_MODEL)
    assert w_in.shape[0] == 1, "single layer"
    g, p, c = N_GROUPS, N_STATE, GROUP
    vec = lambda a: a.reshape(1, -1)

    xp = x_prompt.reshape(N_PROMPT_ROWS, D_MODEL)
    xe = _pad_rows(jnp.concatenate([meta_tokens, x_sample[:, 0, :]], axis=0), TM)
    g0, b0 = vec(ln0_g), vec(ln0_b)

    m, wst, cpow, bpad, cpad, avec = _ssm_prep(ssm_log_dt[0], ssm_a_re[0], ssm_a_im[0], ssm_b_re[0],
                                               ssm_b_im[0], ssm_c_re[0], ssm_c_im[0])

    u, bg, v = _in_proj(xp, xe, g0, b0, w_in[0].astype(BF16))

    s_lo = SAMPLE_LO
    s0 = jnp.concatenate([state_ssm_re[0], state_ssm_im[0]], axis=-1).transpose(1, 0, 2)
    y_all, s_fin, s_new = _ssm_scan(u, m, wst, cpow, bpad, cpad, avec, s0)

    prev = jnp.pad(state_conv[0], ((SAMPLE_ROW0, TMX - SAMPLE_ROW0 - N_SAMPLE), (0, 0), (0, 0)))
    x1, x1b = _mix_out(y_all, u, bg, v, prev[:, 1], prev[:, 0], xp, xe, g0, b0, vec(ssm_d[0]),
                       ssm_w_glu[0].astype(BF16), conv_w[0], vec(g_ssm_out[0]), vec(g_conv_out[0]),
                       w_out[0].astype(BF16), vec(ln1_g[0]), vec(ln1_b[0]))

    act, wd_bf = _ffn_up(x1b, w_gate[0], w_up[0], w_down[0])
    yp, ye = _ffn_down(act, x1, wd_bf, vec(ln2_g[0]), vec(ln2_b[0]))

    y_prompt = yp.reshape(N_BATCH, SEQ, D_MODEL)
    y_sample = ye[SAMPLE_ROW0:SAMPLE_ROW0 + N_SAMPLE].reshape(N_SAMPLE, 1, D_MODEL)

    sd = state_ssm_re.dtype
    s_p = s_fin.transpose(1, 0, 2)
    s_s = s_new.transpose(1, 0, 2)
    cd = state_conv.dtype
    conv_p = jnp.stack([v[n * SEQ + SEQ - (CONV_W - 1):(n + 1) * SEQ] for n in range(N_BATCH)]).astype(cd)
    conv_s = jnp.stack([state_conv[0][:, 1], v[s_lo:s_lo + N_SAMPLE].astype(cd)], axis=1)
    return (y_prompt, y_sample,
            s_p[None, ..., :p].astype(sd), s_p[None, ..., p:].astype(sd), conv_p[None],
            s_s[None, ..., :p].astype(sd), s_s[None, ..., p:].astype(sd), conv_s[None])
```

```python
import functools

import jax
import jax.numpy as jnp
import numpy as np
from jax import lax
from jax.experimental import pallas as pl
from jax.experimental.pallas import tpu as pltpu

D_MODEL = 2048
D_SSM = 1024
D_CONV = 1024
N_GROUPS = 64
GROUP = 16
N_STATE = 64
N_META = 16
CONV_W = 3
D_FF = 5632
LN_EPS = 1e-5
RMS_EPS = 1e-6
ALPHA = 2.0 ** 0.25

N_BATCH = 4
SEQ = 2048
N_SAMPLE = 128
CHUNK = 16
N_CHUNKS = SEQ // CHUNK + 1
ROW8 = 8
CW = CHUNK * GROUP
SW = 2 * N_STATE

TM = 512
N_PROMPT_ROWS = N_BATCH * SEQ
N_PROMPT_TILES = N_PROMPT_ROWS // TM
N_TILES = N_PROMPT_TILES + 1
M_PAD = N_TILES * TM
SAMPLE_ROW0 = N_META
TF = 512
GB = 8
VMEM_LIMIT = 60 * 1024 * 1024

F32 = jnp.float32
BF16 = jnp.bfloat16


def _layer_norm(x, g, b):
    mu = jnp.mean(x, axis=-1, keepdims=True)
    xc = x - mu
    var = jnp.mean(xc * xc, axis=-1, keepdims=True)
    return xc * lax.rsqrt(var + LN_EPS) * g + b


def _rms_norm(x, g):
    return x * lax.rsqrt(jnp.mean(x * x, axis=-1, keepdims=True) + RMS_EPS) * g


def _gelu_tanh(x):
    c = np.float32(np.sqrt(2.0 / np.pi))
    return 0.5 * x * (1.0 + jnp.tanh(c * (x + np.float32(0.044715) * (x * x * x))))


def _sigmoid(x):
    return 1.0 / (1.0 + jnp.exp(-x))


def _dot(a, b):
    return jnp.dot(a, b, preferred_element_type=F32)


def _dot_nt(a, b):
    return lax.dot_general(a, b, (((1,), (1,)), ((), ())), preferred_element_type=F32)


def _ssm_prep_kernel(logdt_ref, are_ref, aim_ref, b1_ref, b2_ref, c1_ref, c2_ref,
                     m_ref, wst_ref, cpow_ref, bpad_ref, cpad_ref, avec_ref):
    dt = jnp.exp(logdt_ref[...])
    a_re, a_im = are_ref[...], aim_ref[...]
    mag = jnp.exp(dt * a_re)
    ang = dt * a_im
    ab_re = mag * jnp.cos(ang)
    ab_im = mag * jnp.sin(ang)
    n_re = ab_re - 1.0
    den = a_re * a_re + a_im * a_im
    f_re = (n_re * a_re + ab_im * a_im) / den
    f_im = (ab_im * a_re - n_re * a_im) / den
    sgn = jnp.where(lax.broadcasted_iota(jnp.int32, (GB, SW), 1) < N_STATE, -1.0, 1.0)

    pr, pi = jnp.ones_like(ab_re), jnp.zeros_like(ab_re)
    pw = [(pr, pi)]
    for _ in range(CHUNK):
        pr, pi = pr * ab_re - pi * ab_im, pr * ab_im + pi * ab_re
        pw.append((pr, sgn * pi))
    fx = sgn * f_im
    lane = lax.broadcasted_iota(jnp.int32, (GROUP, CW), 1)

    for gl in range(GB):
        row = lambda x: x[gl:gl + 1, :]
        bb1 = row(f_re) * b1_ref[gl] + row(fx) * b2_ref[gl]
        bb2 = row(f_re) * b2_ref[gl] - row(fx) * b1_ref[gl]
        cc1 = -row(sgn) * c1_ref[gl]
        cc2 = -row(sgn) * c2_ref[gl]

        cq = []
        for s in range(CHUNK):
            wr, wx = pw[CHUNK - 1 - s]
            wst_ref[gl, s * GROUP:(s + 1) * GROUP, :] = (row(wr) * bb1 + row(wx) * bb2).astype(BF16)
            qr, qx = pw[s + 1]
            cpow_ref[gl, s * GROUP:(s + 1) * GROUP, :] = (row(qr) * cc1 + row(qx) * cc2).astype(BF16)
            dr, dx = pw[s]
            cq.append(row(dr) * cc1 + row(dx) * cc2)
        kt = lax.dot_general(bb1, jnp.concatenate(cq, axis=0), (((1,), (1,)), ((), ())),
                             precision=lax.Precision.HIGHEST, preferred_element_type=F32)
        for i in range(CHUNK):
            blk = kt if i == 0 else jnp.where(lane >= i * GROUP, pltpu.roll(kt, i * GROUP, axis=1), 0.0)
            m_ref[gl, i * GROUP:(i + 1) * GROUP, :] = blk.astype(BF16)

        zeros = jnp.zeros((SW, SW), BF16)
        bpad_ref[gl] = zeros
        cpad_ref[gl] = zeros
        bpad_ref[gl, gl * GROUP:(gl + 1) * GROUP, :] = bb1.astype(BF16)
        cpad_ref[gl, gl * GROUP:(gl + 1) * GROUP, :] = cq[0].astype(BF16)

        avec_ref[gl, 0:1, :] = row(pw[CHUNK][0])
        avec_ref[gl, 1:2, :] = row(pw[CHUNK][1])
        avec_ref[gl, 2:3, :] = row(pw[1][0])
        avec_ref[gl, 3:4, :] = row(pw[1][1])


def _ssm_prep(log_dt, a_re, a_im, b_re, b_im, c_re, c_im):
    g, c = N_GROUPS, GROUP
    dup = lambda x: jnp.concatenate([x, x], axis=-1)
    bt_re, bt_im = b_re.transpose(0, 2, 1), b_im.transpose(0, 2, 1)
    cat = lambda x, y: jnp.concatenate([x, y], axis=-1)
    vec = pl.BlockSpec((GB, SW), lambda i: (i, 0))
    blk = lambda *s: pl.BlockSpec((GB,) + s, lambda i: (i,) + (0,) * len(s))
    return pl.pallas_call(
        _ssm_prep_kernel,
        grid=(g // GB,),
        in_specs=[vec, vec, vec, blk(c, SW), blk(c, SW), blk(c, SW), blk(c, SW)],
        out_specs=[blk(CW, CW), blk(CW, SW), blk(CW, SW), blk(SW, SW), blk(SW, SW), blk(4, SW)],
        out_shape=[jax.ShapeDtypeStruct((g, CW, CW), BF16),
                   jax.ShapeDtypeStruct((g, CW, SW), BF16),
                   jax.ShapeDtypeStruct((g, CW, SW), BF16),
                   jax.ShapeDtypeStruct((g, SW, SW), BF16),
                   jax.ShapeDtypeStruct((g, SW, SW), BF16),
                   jax.ShapeDtypeStruct((g, 4, SW), F32)],
        name="ssm_prep",
    )(jnp.broadcast_to(log_dt[:, None], (g, SW)), dup(a_re), dup(a_im),
      cat(bt_re, bt_im), cat(bt_im, bt_re), cat(c_re, c_im), cat(c_im, c_re))


def _in_proj_kernel(xp_ref, xe_ref, g0_ref, b0_ref, w_ref, u_ref, bg_ref, v_ref, h_scr, cg_scr):
    i = pl.program_id(0)
    j = pl.program_id(1)

    @pl.when((j == 0) & (i < N_PROMPT_TILES))
    def _():
        h_scr[...] = _layer_norm(xp_ref[...], g0_ref[...], b0_ref[...]).astype(BF16)

    @pl.when((j == 0) & (i == N_PROMPT_TILES))
    def _():
        h_scr[...] = _layer_norm(xe_ref[...], g0_ref[...], b0_ref[...]).astype(BF16)

    r = _dot(h_scr[...], w_ref[...])

    @pl.when(j == 0)
    def _():
        u_ref[...] = r

    @pl.when(j == 1)
    def _():
        bg_ref[...] = r.astype(BF16)

    @pl.when(j == 2)
    def _():
        cg_scr[...] = r

    @pl.when(j == 3)
    def _():
        v_ref[...] = (cg_scr[...] * r).astype(BF16)


def _in_proj(xp, xe, g0, b0, w_in):
    row = lambda i, j: (i, 0)
    out = jax.ShapeDtypeStruct((M_PAD, D_SSM), BF16)
    return pl.pallas_call(
        _in_proj_kernel,
        grid=(N_TILES, 4),
        in_specs=[pl.BlockSpec((TM, D_MODEL), lambda i, j: (jnp.minimum(i, N_PROMPT_TILES - 1), 0)),
                  pl.BlockSpec((TM, D_MODEL), lambda i, j: (0, 0)),
                  pl.BlockSpec((1, D_MODEL), lambda i, j: (0, 0)),
                  pl.BlockSpec((1, D_MODEL), lambda i, j: (0, 0)),
                  pl.BlockSpec((D_MODEL, D_SSM), lambda i, j: (0, j))],
        out_specs=[pl.BlockSpec((TM, D_SSM), row)] * 3,
        out_shape=[jax.ShapeDtypeStruct((M_PAD, D_SSM), F32), out, out],
        scratch_shapes=[pltpu.VMEM((TM, D_MODEL), BF16), pltpu.VMEM((TM, D_CONV), F32)],
        compiler_params=pltpu.CompilerParams(
            dimension_semantics=("arbitrary", "arbitrary"), vmem_limit_bytes=VMEM_LIMIT),
        name="in_proj",
    )(xp, xe, g0, b0, w_in)


SUB = 16
SEQ_CHUNKS = SEQ // CHUNK
LHS_ROWS = N_BATCH * SEQ_CHUNKS + SUB
META_LHS = N_BATCH * SEQ_CHUNKS
PITCH = SEQ_CHUNKS + ROW8
REC_ROWS = N_BATCH * PITCH + ROW8
META_REC = N_BATCH * PITCH
SAMPLE_LO = N_PROMPT_ROWS + SAMPLE_ROW0
SAMPLE_HI = SAMPLE_LO + N_SAMPLE


def _block_transpose(a):
    a = list(a)
    blk = lax.broadcasted_iota(jnp.int32, a[0].shape, 1) // GROUP
    for d in (4, 2, 1):
        w = d * GROUP
        keep = (blk & d) == 0
        for j in range(GB):
            if j & d == 0:
                lo, hi = a[j], a[j + d]
                a[j] = jnp.where(keep, lo, pltpu.roll(hi, w, axis=1))
                a[j + d] = jnp.where(keep, pltpu.roll(lo, SW - w, axis=1), hi)
    return a


def _ssm_scan_kernel(u_ref, m_ref, wst_ref, cpow_ref, bpad_ref, cpad_ref, avec_ref, s0_ref,
                     y_ref, sfin_ref, snew_ref, lhs_scr, x_scr, xs_scr, sp_scr, yg_scr):
    def swap_halves(s):
        return pltpu.roll(s, N_STATE, axis=1)

    half = SW

    for h in range(2):
        src = [u_ref[pl.ds(8 * h + j, LHS_ROWS, stride=CHUNK), :] for j in range(GB)]
        for gl, w in enumerate(_block_transpose(src)):
            lhs_scr[gl, :, h * half:(h + 1) * half] = w.astype(BF16)

    def to_rec(ref, gl, x):
        for n in range(N_BATCH):
            ref[gl, n * PITCH:n * PITCH + SEQ_CHUNKS, :] = x[n * SEQ_CHUNKS:(n + 1) * SEQ_CHUNKS]
        ref[gl, META_REC:META_REC + ROW8, :] = x[META_LHS:META_LHS + ROW8]

    for gl in range(GB):
        x = _dot(lhs_scr[gl], wst_ref[gl])
        to_rec(x_scr, gl, x)
        to_rec(xs_scr, gl, swap_halves(x))

    ar16 = [avec_ref[gl, 0:1, :] for gl in range(GB)]
    ax16 = [avec_ref[gl, 1:2, :] for gl in range(GB)]

    def step(k, carry):
        new = []
        for gl in range(GB):
            s, t = carry[gl]
            rows = pl.ds(k, N_BATCH, stride=PITCH)
            sp_scr[gl, rows, :] = s
            new.append((ar16[gl] * s + ax16[gl] * t + x_scr[gl, rows, :],
                        ar16[gl] * t - ax16[gl] * s + xs_scr[gl, rows, :]))
        return tuple(new)

    first = lambda ref, gl: jnp.broadcast_to(ref[gl, META_REC:META_REC + 1, :], (N_BATCH, SW))
    init = tuple((first(x_scr, gl), first(xs_scr, gl)) for gl in range(GB))
    fin = lax.fori_loop(0, SEQ_CHUNKS, step, init)

    us = u_ref[SAMPLE_LO:SAMPLE_HI, :].astype(BF16)
    ys = jnp.zeros((N_SAMPLE, SW), F32)
    for gl in range(GB):
        sfin_ref[gl] = fin[gl][0]
        sp = jnp.concatenate([sp_scr[gl, n * PITCH:n * PITCH + SEQ_CHUNKS, :] for n in range(N_BATCH)]
                             + [jnp.zeros((SUB, SW), F32)], axis=0)
        yg_scr[gl] = _dot(lhs_scr[gl], m_ref[gl]) + _dot_nt(sp.astype(BF16), cpow_ref[gl])
        s0 = s0_ref[gl]
        sn = (avec_ref[gl, 2:3, :] * s0 + avec_ref[gl, 3:4, :] * swap_halves(s0)
              + _dot(us, bpad_ref[gl]))
        snew_ref[gl] = sn
        ys = ys + _dot_nt(sn.astype(BF16), cpad_ref[gl])

    for h in range(2):
        src = [yg_scr[gl, :, h * half:(h + 1) * half] for gl in range(GB)]
        for j, w in enumerate(_block_transpose(src)):
            y_ref[pl.ds(8 * h + j, META_LHS, stride=CHUNK), :] = w[0:META_LHS]
            y_ref[N_PROMPT_ROWS + 8 * h + j:N_PROMPT_ROWS + 8 * h + j + 1, :] = w[META_LHS:META_LHS + 1]
    y_ref[SAMPLE_LO:SAMPLE_HI, :] = ys
    y_ref[SAMPLE_HI:M_PAD, :] = jnp.zeros((M_PAD - SAMPLE_HI, SW), F32)


def _ssm_scan(u, m, wst, cpow, bpad, cpad, avec, s0):
    blk = lambda *s: pl.BlockSpec((GB,) + s, lambda i: (i,) + (0,) * len(s))
    col = pl.BlockSpec((M_PAD, SW), lambda i: (0, i))
    return pl.pallas_call(
        _ssm_scan_kernel,
        grid=(N_GROUPS // GB,),
        in_specs=[col, blk(CW, CW), blk(CW, SW), blk(CW, SW), blk(SW, SW), blk(SW, SW), blk(4, SW),
                  blk(N_SAMPLE, SW)],
        out_specs=[col, blk(N_BATCH, SW), blk(N_SAMPLE, SW)],
        out_shape=[jax.ShapeDtypeStruct((M_PAD, D_SSM), F32),
                   jax.ShapeDtypeStruct((N_GROUPS, N_BATCH, SW), F32),
                   jax.ShapeDtypeStruct((N_GROUPS, N_SAMPLE, SW), F32)],
        scratch_shapes=[pltpu.VMEM((GB, LHS_ROWS, CW), BF16), pltpu.VMEM((GB, REC_ROWS, SW), F32),
                        pltpu.VMEM((GB, REC_ROWS, SW), F32), pltpu.VMEM((GB, REC_ROWS, SW), F32),
                        pltpu.VMEM((GB, LHS_ROWS, CW), F32)],
        compiler_params=pltpu.CompilerParams(
            dimension_semantics=("arbitrary",), vmem_limit_bytes=VMEM_LIMIT),
        name="ssm_scan",
    )(u, m, wst, cpow, bpad, cpad, avec, s0)


HALO = 16
TMX = 256
NX_PROMPT = N_PROMPT_ROWS // TMX
NX_TILES = M_PAD // TMX


def _mix_out_kernel(y_ref, u_ref, bg_ref, v_ref, vh_ref, p1_ref, p2_ref, xp_ref, xe_ref,
                    g0_ref, b0_ref, dskip_ref, wglu_ref, cw_ref, gs_ref, gc_ref, wout_ref,
                    g1_ref, b1_ref, x1_ref, x1b_ref, vpad_scr):
    i = pl.program_id(0)
    is_extra = i == NX_PROMPT

    ys = _gelu_tanh(y_ref[...] + dskip_ref[...] * u_ref[...])
    gate = _dot(ys.astype(BF16), wglu_ref[...])
    r_ssm = _rms_norm(ys * _sigmoid(gate), gs_ref[...])

    vf = v_ref[...].astype(F32)
    halo = jnp.where(is_extra, 0.0, vh_ref[...].astype(F32)[HALO - 2:HALO, :])
    vpad_scr[ROW8 - 2:ROW8, :] = halo
    vpad_scr[ROW8:ROW8 + TMX, :] = vf
    row = lax.broadcasted_iota(jnp.int32, (TMX, 1), 0)
    sample = is_extra & (row >= SAMPLE_ROW0)
    vm1 = jnp.where(sample, p1_ref[...], vpad_scr[ROW8 - 1:ROW8 - 1 + TMX, :])
    vm2 = jnp.where(sample, p2_ref[...], vpad_scr[ROW8 - 2:ROW8 - 2 + TMX, :])
    conv = cw_ref[0:1, :] * vm2 + cw_ref[1:2, :] * vm1 + cw_ref[2:3, :] * vf
    r_conv = _rms_norm(bg_ref[...].astype(F32) * conv, gc_ref[...])

    mo = (_dot(r_ssm.astype(BF16), wout_ref[0:D_SSM, :])
          + _dot(r_conv.astype(BF16), wout_ref[D_SSM:D_MODEL, :]))

    def finish(x_ref):
        h0 = _layer_norm(x_ref[...], g0_ref[...], b0_ref[...])
        x1 = _layer_norm(ALPHA * h0 + mo, g1_ref[...], b1_ref[...])
        x1_ref[...] = x1
        x1b_ref[...] = x1.astype(BF16)

    @pl.when(i < NX_PROMPT)
    def _():
        finish(xp_ref)

    @pl.when(i >= NX_PROMPT)
    def _():
        finish(xe_ref)


def _mix_out(y, u, bg, v, p1, p2, xp, xe, g0, b0, dskip, wglu, cw, gs, gc, wout, g1, b1):
    row = lambda i: (i, 0)
    const = lambda i: (0, 0)
    tiles_per_seq = SEQ // TMX
    meta_block = N_PROMPT_ROWS // HALO

    def halo_map(i):
        prev = i * (TMX // HALO) - 1
        return (jnp.where(i % tiles_per_seq == 0, meta_block, prev), 0)

    vec = lambda n: pl.BlockSpec((1, n), const)
    return pl.pallas_call(
        _mix_out_kernel,
        grid=(NX_TILES,),
        in_specs=[pl.BlockSpec((TMX, D_SSM), row), pl.BlockSpec((TMX, D_SSM), row),
                  pl.BlockSpec((TMX, D_CONV), row), pl.BlockSpec((TMX, D_CONV), row),
                  pl.BlockSpec((HALO, D_CONV), halo_map),
                  pl.BlockSpec((TMX, D_CONV), const), pl.BlockSpec((TMX, D_CONV), const),
                  pl.BlockSpec((TMX, D_MODEL), lambda i: (jnp.minimum(i, NX_PROMPT - 1), 0)),
                  pl.BlockSpec((TMX, D_MODEL), lambda i: (jnp.maximum(i - NX_PROMPT, 0), 0)),
                  vec(D_MODEL), vec(D_MODEL), vec(D_SSM),
                  pl.BlockSpec((D_SSM, D_SSM), const), pl.BlockSpec((CONV_W, D_CONV), const),
                  vec(D_SSM), vec(D_CONV), pl.BlockSpec((D_MODEL, D_MODEL), const),
                  vec(D_MODEL), vec(D_MODEL)],
        out_specs=[pl.BlockSpec((TMX, D_MODEL), row), pl.BlockSpec((TMX, D_MODEL), row)],
        out_shape=[jax.ShapeDtypeStruct((M_PAD, D_MODEL), F32),
                   jax.ShapeDtypeStruct((M_PAD, D_MODEL), BF16)],
        scratch_shapes=[pltpu.VMEM((ROW8 + TMX, D_CONV), F32)],
        compiler_params=pltpu.CompilerParams(
            dimension_semantics=("arbitrary",), vmem_limit_bytes=VMEM_LIMIT),
        name="mix_out",
    )(y, u, bg, v, v, p1, p2, xp, xe, g0, b0, dskip, wglu, cw, gs, gc, wout, g1, b1)


def _ffn_up_kernel(xb_ref, wg_ref, wu_ref, wd_ref, act_ref, wdb_ref, wg_scr, wu_scr):
    @pl.when(pl.program_id(1) == 0)
    def _():
        wg_scr[...] = wg_ref[...].astype(BF16)
        wu_scr[...] = wu_ref[...].astype(BF16)
        wdb_ref[...] = wd_ref[...].astype(BF16)

    xb = xb_ref[...]
    gate = _dot(xb, wg_scr[...])
    up = _dot(xb, wu_scr[...])
    act_ref[...] = (gate * _sigmoid(gate) * up).astype(BF16)


TM_UP = M_PAD // 8
SUB_DOWN = 256
EXTRA_ROWS = 256


def _ffn_up(x1b, wg, wu, wd):
    wcol = pl.BlockSpec((D_MODEL, TF), lambda f, i: (0, f))
    wrow = pl.BlockSpec((TF, D_MODEL), lambda f, i: (f, 0))
    return pl.pallas_call(
        _ffn_up_kernel,
        grid=(D_FF // TF, M_PAD // TM_UP),
        in_specs=[pl.BlockSpec((TM_UP, D_MODEL), lambda f, i: (i, 0)), wcol, wcol, wrow],
        out_specs=[pl.BlockSpec((TM_UP, TF), lambda f, i: (i, f)), wrow],
        out_shape=[jax.ShapeDtypeStruct((M_PAD, D_FF), BF16),
                   jax.ShapeDtypeStruct((D_FF, D_MODEL), BF16)],
        scratch_shapes=[pltpu.VMEM((D_MODEL, TF), BF16), pltpu.VMEM((D_MODEL, TF), BF16)],
        compiler_params=pltpu.CompilerParams(
            dimension_semantics=("arbitrary", "arbitrary"), vmem_limit_bytes=VMEM_LIMIT),
        name="ffn_up",
    )(x1b, wg, wu, wd)


def _ffn_down_kernel(act_ref, x_ref, wd_ref, g2_ref, b2_ref, yp_ref, ye_ref):
    i = pl.program_id(0)

    def tile(o_ref, rows):
        for r0 in range(0, rows, SUB_DOWN):
            rs = slice(r0, r0 + SUB_DOWN)
            z = ALPHA * x_ref[rs, :] + _dot(act_ref[rs, :], wd_ref[...])
            o_ref[rs, :] = _layer_norm(z, g2_ref[...], b2_ref[...])

    @pl.when(i < N_PROMPT_TILES)
    def _():
        tile(yp_ref, TM)

    @pl.when(i == N_PROMPT_TILES)
    def _():
        tile(ye_ref, EXTRA_ROWS)


def _ffn_down(act, x1, wd, g2, b2):
    row = lambda i: (i, 0)
    const = lambda i: (0, 0)
    return pl.pallas_call(
        _ffn_down_kernel,
        grid=(N_TILES,),
        in_specs=[pl.BlockSpec((TM, D_FF), row), pl.BlockSpec((TM, D_MODEL), row),
                  pl.BlockSpec((D_FF, D_MODEL), const, pipeline_mode=pl.Buffered(1)),
                  pl.BlockSpec((1, D_MODEL), const), pl.BlockSpec((1, D_MODEL), const)],
        out_specs=[pl.BlockSpec((TM, D_MODEL), lambda i: (jnp.minimum(i, N_PROMPT_TILES - 1), 0)),
                   pl.BlockSpec((EXTRA_ROWS, D_MODEL), const)],
        out_shape=[jax.ShapeDtypeStruct((N_PROMPT_ROWS, D_MODEL), F32),
                   jax.ShapeDtypeStruct((EXTRA_ROWS, D_MODEL), F32)],
        compiler_params=pltpu.CompilerParams(
            dimension_semantics=("arbitrary",), vmem_limit_bytes=VMEM_LIMIT),
        name="ffn_down",
    )(act, x1, wd, g2, b2)


def _pad_rows(a, n):
    return jnp.pad(a, ((0, n - a.shape[0]), (0, 0)))


def kernel(x_prompt, x_sample, state_ssm_re, state_ssm_im, state_conv, meta_tokens, ln0_g, ln0_b,
           w_in, ssm_a_re, ssm_a_im, ssm_log_dt, ssm_b_re, ssm_b_im, ssm_c_re, ssm_c_im, ssm_d,
           ssm_w_glu, conv_w, g_ssm_out, g_conv_out, w_out, ln1_g, ln1_b, w_gate, w_up, w_down,
           ln2_g, ln2_b):
    assert x_prompt.shape == (N_BATCH, SEQ, D_MODEL) and x_sample.shape == (N_SAMPLE, 1, D_MODEL)
    assert w_in.shape[0] == 1, "single layer"
    g, p, c = N_GROUPS, N_STATE, GROUP
    vec = lambda a: a.reshape(1, -1)

    xp = x_prompt.reshape(N_PROMPT_ROWS, D_MODEL)
    xe = _pad_rows(jnp.concatenate([meta_tokens, x_sample[:, 0, :]], axis=0), TM)
    g0, b0 = vec(ln0_g), vec(ln0_b)

    m, wst, cpow, bpad, cpad, avec = _ssm_prep(ssm_log_dt[0], ssm_a_re[0], ssm_a_im[0], ssm_b_re[0],
                                               ssm_b_im[0], ssm_c_re[0], ssm_c_im[0])

    u, bg, v = _in_proj(xp, xe, g0, b0, w_in[0].astype(BF16))

    s_lo = SAMPLE_LO
    s0 = jnp.concatenate([state_ssm_re[0], state_ssm_im[0]], axis=-1).transpose(1, 0, 2)
    y_all, s_fin, s_new = _ssm_scan(u, m, wst, cpow, bpad, cpad, avec, s0)

    prev = jnp.pad(state_conv[0], ((SAMPLE_ROW0, TMX - SAMPLE_ROW0 - N_SAMPLE), (0, 0), (0, 0)))
    x1, x1b = _mix_out(y_all, u, bg, v, prev[:, 1], prev[:, 0], xp, xe, g0, b0, vec(ssm_d[0]),
                       ssm_w_glu[0].astype(BF16), conv_w[0], vec(g_ssm_out[0]), vec(g_conv_out[0]),
                       w_out[0].astype(BF16), vec(ln1_g[0]), vec(ln1_b[0]))

    act, wd_bf = _ffn_up(x1b, w_gate[0], w_up[0], w_down[0])
    yp, ye = _ffn_down(act, x1, wd_bf, vec(ln2_g[0]), vec(ln2_b[0]))

    y_prompt = yp.reshape(N_BATCH, SEQ, D_MODEL)
    y_sample = ye[SAMPLE_ROW0:SAMPLE_ROW0 + N_SAMPLE].reshape(N_SAMPLE, 1, D_MODEL)

    sd = state_ssm_re.dtype
    s_p = s_fin.transpose(1, 0, 2)
    s_s = s_new.transpose(1, 0, 2)
    cd = state_conv.dtype
    conv_p = jnp.stack([v[n * SEQ + SEQ - (CONV_W - 1):(n + 1) * SEQ] for n in range(N_BATCH)]).astype(cd)
    conv_s = jnp.stack([state_conv[0][:, 1], v[s_lo:s_lo + N_SAMPLE].astype(cd)], axis=1)
    return (y_prompt, y_sample,
            s_p[None, ..., :p].astype(sd), s_p[None, ..., p:].astype(sd), conv_p[None],
            s_s[None, ..., :p].astype(sd), s_s[None, ..., p:].astype(sd), conv_s[None])
```

```python
import functools

import jax
import jax.numpy as jnp
import numpy as np
from jax import lax
from jax.experimental import pallas as pl
from jax.experimental.pallas import tpu as pltpu

D_MODEL = 2048
D_SSM = 1024
D_CONV = 1024
N_GROUPS = 64
GROUP = 16
N_STATE = 64
N_META = 16
CONV_W = 3
D_FF = 5632
LN_EPS = 1e-5
RMS_EPS = 1e-6
ALPHA = 2.0 ** 0.25

N_BATCH = 4
SEQ = 2048
N_SAMPLE = 128
CHUNK = 16
N_CHUNKS = SEQ // CHUNK + 1
ROW8 = 8
CW = CHUNK * GROUP
SW = 2 * N_STATE

TM = 512
N_PROMPT_ROWS = N_BATCH * SEQ
N_PROMPT_TILES = N_PROMPT_ROWS // TM
N_TILES = N_PROMPT_TILES + 1
M_PAD = N_TILES * TM
SAMPLE_ROW0 = N_META
TF = 512
GB = 8
VMEM_LIMIT = 60 * 1024 * 1024

F32 = jnp.float32
BF16 = jnp.bfloat16


def _layer_norm(x, g, b):
    mu = jnp.mean(x, axis=-1, keepdims=True)
    xc = x - mu
    var = jnp.mean(xc * xc, axis=-1, keepdims=True)
    return xc * lax.rsqrt(var + LN_EPS) * g + b


def _rms_norm(x, g):
    return x * lax.rsqrt(jnp.mean(x * x, axis=-1, keepdims=True) + RMS_EPS) * g


def _gelu_tanh(x):
    c = np.float32(np.sqrt(2.0 / np.pi))
    return 0.5 * x * (1.0 + jnp.tanh(c * (x + np.float32(0.044715) * (x * x * x))))


def _sigmoid(x):
    return 1.0 / (1.0 + jnp.exp(-x))


def _dot(a, b):
    return jnp.dot(a, b, preferred_element_type=F32)


def _dot_nt(a, b):
    return lax.dot_general(a, b, (((1,), (1,)), ((), ())), preferred_element_type=F32)


def _ssm_prep_kernel(logdt_ref, are_ref, aim_ref, b1_ref, b2_ref, c1_ref, c2_ref,
                     m_ref, wst_ref, cpow_ref, bpad_ref, cpad_ref, avec_ref):
    dt = jnp.exp(logdt_ref[...])
    a_re, a_im = are_ref[...], aim_ref[...]
    mag = jnp.exp(dt * a_re)
    ang = dt * a_im
    ab_re = mag * jnp.cos(ang)
    ab_im = mag * jnp.sin(ang)
    n_re = ab_re - 1.0
    den = a_re * a_re + a_im * a_im
    f_re = (n_re * a_re + ab_im * a_im) / den
    f_im = (ab_im * a_re - n_re * a_im) / den
    sgn = jnp.where(lax.broadcasted_iota(jnp.int32, (GB, SW), 1) < N_STATE, -1.0, 1.0)

    pr, pi = jnp.ones_like(ab_re), jnp.zeros_like(ab_re)
    pw = [(pr, pi)]
    for _ in range(CHUNK):
        pr, pi = pr * ab_re - pi * ab_im, pr * ab_im + pi * ab_re
        pw.append((pr, sgn * pi))
    fx = sgn * f_im
    lane = lax.broadcasted_iota(jnp.int32, (GROUP, CW), 1)

    for gl in range(GB):
        row = lambda x: x[gl:gl + 1, :]
        bb1 = row(f_re) * b1_ref[gl] + row(fx) * b2_ref[gl]
        bb2 = row(f_re) * b2_ref[gl] - row(fx) * b1_ref[gl]
        cc1 = -row(sgn) * c1_ref[gl]
        cc2 = -row(sgn) * c2_ref[gl]

        cq = []
        for s in range(CHUNK):
            wr, wx = pw[CHUNK - 1 - s]
            wst_ref[gl, s * GROUP:(s + 1) * GROUP, :] = (row(wr) * bb1 + row(wx) * bb2).astype(BF16)
            qr, qx = pw[s + 1]
            cpow_ref[gl, s * GROUP:(s + 1) * GROUP, :] = (row(qr) * cc1 + row(qx) * cc2).astype(BF16)
            dr, dx = pw[s]
            cq.append(row(dr) * cc1 + row(dx) * cc2)
        kt = lax.dot_general(bb1, jnp.concatenate(cq, axis=0), (((1,), (1,)), ((), ())),
                             precision=lax.Precision.HIGHEST, preferred_element_type=F32)
        for i in range(CHUNK):
            blk = kt if i == 0 else jnp.where(lane >= i * GROUP, pltpu.roll(kt, i * GROUP, axis=1), 0.0)
            m_ref[gl, i * GROUP:(i + 1) * GROUP, :] = blk.astype(BF16)

        zeros = jnp.zeros((SW, SW), BF16)
        bpad_ref[gl] = zeros
        cpad_ref[gl] = zeros
        bpad_ref[gl, gl * GROUP:(gl + 1) * GROUP, :] = bb1.astype(BF16)
        cpad_ref[gl, gl * GROUP:(gl + 1) * GROUP, :] = cq[0].astype(BF16)

        avec_ref[gl, 0:1, :] = row(pw[CHUNK][0])
        avec_ref[gl, 1:2, :] = row(pw[CHUNK][1])
        avec_ref[gl, 2:3, :] = row(pw[1][0])
        avec_ref[gl, 3:4, :] = row(pw[1][1])


def _ssm_prep(log_dt, a_re, a_im, b_re, b_im, c_re, c_im):
    g, c = N_GROUPS, GROUP
    dup = lambda x: jnp.concatenate([x, x], axis=-1)
    bt_re, bt_im = b_re.transpose(0, 2, 1), b_im.transpose(0, 2, 1)
    cat = lambda x, y: jnp.concatenate([x, y], axis=-1)
    vec = pl.BlockSpec((GB, SW), lambda i: (i, 0))
    blk = lambda *s: pl.BlockSpec((GB,) + s, lambda i: (i,) + (0,) * len(s))
    return pl.pallas_call(
        _ssm_prep_kernel,
        grid=(g // GB,),
        in_specs=[vec, vec, vec, blk(c, SW), blk(c, SW), blk(c, SW), blk(c, SW)],
        out_specs=[blk(CW, CW), blk(CW, SW), blk(CW, SW), blk(SW, SW), blk(SW, SW), blk(4, SW)],
        out_shape=[jax.ShapeDtypeStruct((g, CW, CW), BF16),
                   jax.ShapeDtypeStruct((g, CW, SW), BF16),
                   jax.ShapeDtypeStruct((g, CW, SW), BF16),
                   jax.ShapeDtypeStruct((g, SW, SW), BF16),
                   jax.ShapeDtypeStruct((g, SW, SW), BF16),
                   jax.ShapeDtypeStruct((g, 4, SW), F32)],
        name="ssm_prep",
    )(jnp.broadcast_to(log_dt[:, None], (g, SW)), dup(a_re), dup(a_im),
      cat(bt_re, bt_im), cat(bt_im, bt_re), cat(c_re, c_im), cat(c_im, c_re))


SUB_IN = 256


def _in_proj_kernel(xp_ref, xe_ref, g0_ref, b0_ref, w_ref, h0_ref, u_ref, bg_ref, v_ref):
    i = pl.program_id(0)

    def tile(x_ref):
        for r0 in range(0, TM, SUB_IN):
            rs = slice(r0, r0 + SUB_IN)
            h0 = _layer_norm(x_ref[rs, :], g0_ref[...], b0_ref[...])
            h0_ref[rs, :] = h0
            hb = h0.astype(BF16)
            part = lambda k: _dot(hb, w_ref[:, k * D_SSM:(k + 1) * D_SSM])
            u_ref[rs, :] = part(0)
            bg_ref[rs, :] = part(1).astype(BF16)
            v_ref[rs, :] = (part(2) * part(3)).astype(BF16)

    @pl.when(i < N_PROMPT_TILES)
    def _():
        tile(xp_ref)

    @pl.when(i == N_PROMPT_TILES)
    def _():
        tile(xe_ref)


def _in_proj(xp, xe, g0, b0, w_in):
    row = lambda i: (i, 0)
    const = lambda i: (0, 0)
    out = jax.ShapeDtypeStruct((M_PAD, D_SSM), BF16)
    return pl.pallas_call(
        _in_proj_kernel,
        grid=(N_TILES,),
        in_specs=[pl.BlockSpec((TM, D_MODEL), lambda i: (jnp.minimum(i, N_PROMPT_TILES - 1), 0)),
                  pl.BlockSpec((TM, D_MODEL), const),
                  pl.BlockSpec((1, D_MODEL), const), pl.BlockSpec((1, D_MODEL), const),
                  pl.BlockSpec((D_MODEL, 4 * D_SSM), const, pipeline_mode=pl.Buffered(1))],
        out_specs=[pl.BlockSpec((TM, D_MODEL), row)] + [pl.BlockSpec((TM, D_SSM), row)] * 3,
        out_shape=[jax.ShapeDtypeStruct((M_PAD, D_MODEL), F32),
                   jax.ShapeDtypeStruct((M_PAD, D_SSM), F32), out, out],
        compiler_params=pltpu.CompilerParams(
            dimension_semantics=("arbitrary",), vmem_limit_bytes=VMEM_LIMIT),
        name="in_proj",
    )(xp, xe, g0, b0, w_in)


SUB = 16
SEQ_CHUNKS = SEQ // CHUNK
LHS_ROWS = N_BATCH * SEQ_CHUNKS + SUB
META_LHS = N_BATCH * SEQ_CHUNKS
PITCH = SEQ_CHUNKS + ROW8
REC_ROWS = N_BATCH * PITCH + ROW8
META_REC = N_BATCH * PITCH
SAMPLE_LO = N_PROMPT_ROWS + SAMPLE_ROW0
SAMPLE_HI = SAMPLE_LO + N_SAMPLE


def _block_transpose(a):
    a = list(a)
    blk = lax.broadcasted_iota(jnp.int32, a[0].shape, 1) // GROUP
    for d in (4, 2, 1):
        w = d * GROUP
        keep = (blk & d) == 0
        for j in range(GB):
            if j & d == 0:
                lo, hi = a[j], a[j + d]
                a[j] = jnp.where(keep, lo, pltpu.roll(hi, w, axis=1))
                a[j + d] = jnp.where(keep, pltpu.roll(lo, SW - w, axis=1), hi)
    return a


def _ssm_scan_kernel(u_ref, m_ref, wst_ref, cpow_ref, bpad_ref, cpad_ref, avec_ref, s0_ref,
                     y_ref, sfin_ref, snew_ref, lhs_scr, x_scr, xs_scr, sp_scr, yg_scr):
    def swap_halves(s):
        return pltpu.roll(s, N_STATE, axis=1)

    half = SW

    for h in range(2):
        src = [u_ref[pl.ds(8 * h + j, LHS_ROWS, stride=CHUNK), :] for j in range(GB)]
        for gl, w in enumerate(_block_transpose(src)):
            lhs_scr[gl, :, h * half:(h + 1) * half] = w.astype(BF16)

    def to_rec(ref, gl, x):
        for n in range(N_BATCH):
            ref[gl, n * PITCH:n * PITCH + SEQ_CHUNKS, :] = x[n * SEQ_CHUNKS:(n + 1) * SEQ_CHUNKS]
        ref[gl, META_REC:META_REC + ROW8, :] = x[META_LHS:META_LHS + ROW8]

    for gl in range(GB):
        x = _dot(lhs_scr[gl], wst_ref[gl])
        to_rec(x_scr, gl, x)
        to_rec(xs_scr, gl, swap_halves(x))

    ar16 = [avec_ref[gl, 0:1, :] for gl in range(GB)]
    ax16 = [avec_ref[gl, 1:2, :] for gl in range(GB)]

    def step(k, carry):
        new = []
        for gl in range(GB):
            s, t = carry[gl]
            rows = pl.ds(k, N_BATCH, stride=PITCH)
            sp_scr[gl, rows, :] = s
            new.append((ar16[gl] * s + ax16[gl] * t + x_scr[gl, rows, :],
                        ar16[gl] * t - ax16[gl] * s + xs_scr[gl, rows, :]))
        return tuple(new)

    first = lambda ref, gl: jnp.broadcast_to(ref[gl, META_REC:META_REC + 1, :], (N_BATCH, SW))
    init = tuple((first(x_scr, gl), first(xs_scr, gl)) for gl in range(GB))
    fin = lax.fori_loop(0, SEQ_CHUNKS, step, init)

    us = u_ref[SAMPLE_LO:SAMPLE_HI, :].astype(BF16)
    ys = jnp.zeros((N_SAMPLE, SW), F32)
    for gl in range(GB):
        sfin_ref[gl] = fin[gl][0]
        sp = jnp.concatenate([sp_scr[gl, n * PITCH:n * PITCH + SEQ_CHUNKS, :] for n in range(N_BATCH)]
                             + [jnp.zeros((SUB, SW), F32)], axis=0)
        yg_scr[gl] = _dot(lhs_scr[gl], m_ref[gl]) + _dot_nt(sp.astype(BF16), cpow_ref[gl])
        s0 = s0_ref[gl]
        sn = (avec_ref[gl, 2:3, :] * s0 + avec_ref[gl, 3:4, :] * swap_halves(s0)
              + _dot(us, bpad_ref[gl]))
        snew_ref[gl] = sn
        ys = ys + _dot_nt(sn.astype(BF16), cpad_ref[gl])

    for h in range(2):
        src = [yg_scr[gl, :, h * half:(h + 1) * half] for gl in range(GB)]
        for j, w in enumerate(_block_transpose(src)):
            y_ref[pl.ds(8 * h + j, META_LHS, stride=CHUNK), :] = w[0:META_LHS]
            y_ref[N_PROMPT_ROWS + 8 * h + j:N_PROMPT_ROWS + 8 * h + j + 1, :] = w[META_LHS:META_LHS + 1]
    y_ref[SAMPLE_LO:SAMPLE_HI, :] = ys
    y_ref[SAMPLE_HI:M_PAD, :] = jnp.zeros((M_PAD - SAMPLE_HI, SW), F32)


def _ssm_scan(u, m, wst, cpow, bpad, cpad, avec, s0):
    blk = lambda *s: pl.BlockSpec((GB,) + s, lambda i: (i,) + (0,) * len(s))
    col = pl.BlockSpec((M_PAD, SW), lambda i: (0, i))
    return pl.pallas_call(
        _ssm_scan_kernel,
        grid=(N_GROUPS // GB,),
        in_specs=[col, blk(CW, CW), blk(CW, SW), blk(CW, SW), blk(SW, SW), blk(SW, SW), blk(4, SW),
                  blk(N_SAMPLE, SW)],
        out_specs=[col, blk(N_BATCH, SW), blk(N_SAMPLE, SW)],
        out_shape=[jax.ShapeDtypeStruct((M_PAD, D_SSM), F32),
                   jax.ShapeDtypeStruct((N_GROUPS, N_BATCH, SW), F32),
                   jax.ShapeDtypeStruct((N_GROUPS, N_SAMPLE, SW), F32)],
        scratch_shapes=[pltpu.VMEM((GB, LHS_ROWS, CW), BF16), pltpu.VMEM((GB, REC_ROWS, SW), F32),
                        pltpu.VMEM((GB, REC_ROWS, SW), F32), pltpu.VMEM((GB, REC_ROWS, SW), F32),
                        pltpu.VMEM((GB, LHS_ROWS, CW), F32)],
        compiler_params=pltpu.CompilerParams(
            dimension_semantics=("arbitrary",), vmem_limit_bytes=VMEM_LIMIT),
        name="ssm_scan",
    )(u, m, wst, cpow, bpad, cpad, avec, s0)


HALO = 16
TMX = 256
NX_PROMPT = N_PROMPT_ROWS // TMX
NX_TILES = M_PAD // TMX


def _mix_out_kernel(y_ref, u_ref, bg_ref, v_ref, vh_ref, p1_ref, p2_ref, h0_ref,
                    dskip_ref, wglu_ref, cw_ref, gs_ref, gc_ref, wout_ref,
                    g1_ref, b1_ref, x1_ref, x1b_ref, vpad_scr):
    i = pl.program_id(0)
    is_extra = i == NX_PROMPT

    ys = _gelu_tanh(y_ref[...] + dskip_ref[...] * u_ref[...])
    gate = _dot(ys.astype(BF16), wglu_ref[...])
    r_ssm = _rms_norm(ys * _sigmoid(gate), gs_ref[...])

    vf = v_ref[...].astype(F32)
    halo = jnp.where(is_extra, 0.0, vh_ref[...].astype(F32)[HALO - 2:HALO, :])
    vpad_scr[ROW8 - 2:ROW8, :] = halo
    vpad_scr[ROW8:ROW8 + TMX, :] = vf
    row = lax.broadcasted_iota(jnp.int32, (TMX, 1), 0)
    sample = is_extra & (row >= SAMPLE_ROW0)
    vm1 = jnp.where(sample, p1_ref[...], vpad_scr[ROW8 - 1:ROW8 - 1 + TMX, :])
    vm2 = jnp.where(sample, p2_ref[...], vpad_scr[ROW8 - 2:ROW8 - 2 + TMX, :])
    conv = cw_ref[0:1, :] * vm2 + cw_ref[1:2, :] * vm1 + cw_ref[2:3, :] * vf
    r_conv = _rms_norm(bg_ref[...].astype(F32) * conv, gc_ref[...])

    mo = (_dot(r_ssm.astype(BF16), wout_ref[0:D_SSM, :])
          + _dot(r_conv.astype(BF16), wout_ref[D_SSM:D_MODEL, :]))

    x1 = _layer_norm(ALPHA * h0_ref[...] + mo, g1_ref[...], b1_ref[...])
    x1_ref[...] = x1
    x1b_ref[...] = x1.astype(BF16)


def _mix_out(y, u, bg, v, p1, p2, h0, dskip, wglu, cw, gs, gc, wout, g1, b1):
    row = lambda i: (i, 0)
    const = lambda i: (0, 0)
    tiles_per_seq = SEQ // TMX
    meta_block = N_PROMPT_ROWS // HALO

    def halo_map(i):
        prev = i * (TMX // HALO) - 1
        return (jnp.where(i % tiles_per_seq == 0, meta_block, prev), 0)

    vec = lambda n: pl.BlockSpec((1, n), const)
    return pl.pallas_call(
        _mix_out_kernel,
        grid=(NX_TILES,),
        in_specs=[pl.BlockSpec((TMX, D_SSM), row), pl.BlockSpec((TMX, D_SSM), row),
                  pl.BlockSpec((TMX, D_CONV), row), pl.BlockSpec((TMX, D_CONV), row),
                  pl.BlockSpec((HALO, D_CONV), halo_map),
                  pl.BlockSpec((TMX, D_CONV), const), pl.BlockSpec((TMX, D_CONV), const),
                  pl.BlockSpec((TMX, D_MODEL), row), vec(D_SSM),
                  pl.BlockSpec((D_SSM, D_SSM), const), pl.BlockSpec((CONV_W, D_CONV), const),
                  vec(D_SSM), vec(D_CONV), pl.BlockSpec((D_MODEL, D_MODEL), const),
                  vec(D_MODEL), vec(D_MODEL)],
        out_specs=[pl.BlockSpec((TMX, D_MODEL), row), pl.BlockSpec((TMX, D_MODEL), row)],
        out_shape=[jax.ShapeDtypeStruct((M_PAD, D_MODEL), F32),
                   jax.ShapeDtypeStruct((M_PAD, D_MODEL), BF16)],
        scratch_shapes=[pltpu.VMEM((ROW8 + TMX, D_CONV), F32)],
        compiler_params=pltpu.CompilerParams(
            dimension_semantics=("arbitrary",), vmem_limit_bytes=VMEM_LIMIT),
        name="mix_out",
    )(y, u, bg, v, v, p1, p2, h0, dskip, wglu, cw, gs, gc, wout, g1, b1)


def _ffn_up_kernel(xb_ref, wg_ref, wu_ref, wd_ref, act_ref, wdb_ref, wg_scr, wu_scr):
    @pl.when(pl.program_id(1) == 0)
    def _():
        wg_scr[...] = wg_ref[...].astype(BF16)
        wu_scr[...] = wu_ref[...].astype(BF16)
        wdb_ref[...] = wd_ref[...].astype(BF16)

    xb = xb_ref[...]
    gate = _dot(xb, wg_scr[...])
    up = _dot(xb, wu_scr[...])
    act_ref[...] = (gate * _sigmoid(gate) * up).astype(BF16)


TM_UP = M_PAD // 8
SUB_DOWN = 256
EXTRA_ROWS = 256


def _ffn_up(x1b, wg, wu, wd):
    wcol = pl.BlockSpec((D_MODEL, TF), lambda f, i: (0, f))
    wrow = pl.BlockSpec((TF, D_MODEL), lambda f, i: (f, 0))
    return pl.pallas_call(
        _ffn_up_kernel,
        grid=(D_FF // TF, M_PAD // TM_UP),
        in_specs=[pl.BlockSpec((TM_UP, D_MODEL), lambda f, i: (i, 0)), wcol, wcol, wrow],
        out_specs=[pl.BlockSpec((TM_UP, TF), lambda f, i: (i, f)), wrow],
        out_shape=[jax.ShapeDtypeStruct((M_PAD, D_FF), BF16),
                   jax.ShapeDtypeStruct((D_FF, D_MODEL), BF16)],
        scratch_shapes=[pltpu.VMEM((D_MODEL, TF), BF16), pltpu.VMEM((D_MODEL, TF), BF16)],
        compiler_params=pltpu.CompilerParams(
            dimension_semantics=("arbitrary", "arbitrary"), vmem_limit_bytes=VMEM_LIMIT),
        name="ffn_up",
    )(x1b, wg, wu, wd)


def _ffn_down_kernel(act_ref, x_ref, wd_ref, g2_ref, b2_ref, yp_ref, ye_ref):
    i = pl.program_id(0)

    def tile(o_ref, rows):
        for r0 in range(0, rows, SUB_DOWN):
            rs = slice(r0, r0 + SUB_DOWN)
            z = ALPHA * x_ref[rs, :] + _dot(act_ref[rs, :], wd_ref[...])
            o_ref[rs, :] = _layer_norm(z, g2_ref[...], b2_ref[...])

    @pl.when(i < N_PROMPT_TILES)
    def _():
        tile(yp_ref, TM)

    @pl.when(i == N_PROMPT_TILES)
    def _():
        tile(ye_ref, EXTRA_ROWS)


def _ffn_down(act, x1, wd, g2, b2):
    row = lambda i: (i, 0)
    const = lambda i: (0, 0)
    return pl.pallas_call(
        _ffn_down_kernel,
        grid=(N_TILES,),
        in_specs=[pl.BlockSpec((TM, D_FF), row), pl.BlockSpec((TM, D_MODEL), row),
                  pl.BlockSpec((D_FF, D_MODEL), const, pipeline_mode=pl.Buffered(1)),
                  pl.BlockSpec((1, D_MODEL), const), pl.BlockSpec((1, D_MODEL), const)],
        out_specs=[pl.BlockSpec((TM, D_MODEL), lambda i: (jnp.minimum(i, N_PROMPT_TILES - 1), 0)),
                   pl.BlockSpec((EXTRA_ROWS, D_MODEL), const)],
        out_shape=[jax.ShapeDtypeStruct((N_PROMPT_ROWS, D_MODEL), F32),
                   jax.ShapeDtypeStruct((EXTRA_ROWS, D_MODEL), F32)],
        compiler_params=pltpu.CompilerParams(
            dimension_semantics=("arbitrary",), vmem_limit_bytes=VMEM_LIMIT),
        name="ffn_down",
    )(act, x1, wd, g2, b2)


def _pad_rows(a, n):
    return jnp.pad(a, ((0, n - a.shape[0]), (0, 0)))


def kernel(x_prompt, x_sample, state_ssm_re, state_ssm_im, state_conv, meta_tokens, ln0_g, ln0_b,
           w_in, ssm_a_re, ssm_a_im, ssm_log_dt, ssm_b_re, ssm_b_im, ssm_c_re, ssm_c_im, ssm_d,
           ssm_w_glu, conv_w, g_ssm_out, g_conv_out, w_out, ln1_g, ln1_b, w_gate, w_up, w_down,
           ln2_g, ln2_b):
    assert x_prompt.shape == (N_BATCH, SEQ, D_MODEL) and x_sample.shape == (N_SAMPLE, 1, D_MODEL)
    assert w_in.shape[0] == 1, "single layer"
    g, p, c = N_GROUPS, N_STATE, GROUP
    vec = lambda a: a.reshape(1, -1)

    xp = x_prompt.reshape(N_PROMPT_ROWS, D_MODEL)
    xe = _pad_rows(jnp.concatenate([meta_tokens, x_sample[:, 0, :]], axis=0), TM)
    g0, b0 = vec(ln0_g), vec(ln0_b)

    m, wst, cpow, bpad, cpad, avec = _ssm_prep(ssm_log_dt[0], ssm_a_re[0], ssm_a_im[0], ssm_b_re[0],
                                               ssm_b_im[0], ssm_c_re[0], ssm_c_im[0])

    h0, u, bg, v = _in_proj(xp, xe, g0, b0, w_in[0].astype(BF16))

    s_lo = SAMPLE_LO
    s0 = jnp.concatenate([state_ssm_re[0], state_ssm_im[0]], axis=-1).transpose(1, 0, 2)
    y_all, s_fin, s_new = _ssm_scan(u, m, wst, cpow, bpad, cpad, avec, s0)

    prev = jnp.pad(state_conv[0], ((SAMPLE_ROW0, TMX - SAMPLE_ROW0 - N_SAMPLE), (0, 0), (0, 0)))
    x1, x1b = _mix_out(y_all, u, bg, v, prev[:, 1], prev[:, 0], h0, vec(ssm_d[0]),
                       ssm_w_glu[0].astype(BF16), conv_w[0], vec(g_ssm_out[0]), vec(g_conv_out[0]),
                       w_out[0].astype(BF16), vec(ln1_g[0]), vec(ln1_b[0]))

    act, wd_bf = _ffn_up(x1b, w_gate[0], w_up[0], w_down[0])
    yp, ye = _ffn_down(act, x1, wd_bf, vec(ln2_g[0]), vec(ln2_b[0]))

    y_prompt = yp.reshape(N_BATCH, SEQ, D_MODEL)
    y_sample = ye[SAMPLE_ROW0:SAMPLE_ROW0 + N_SAMPLE].reshape(N_SAMPLE, 1, D_MODEL)

    sd = state_ssm_re.dtype
    s_p = s_fin.transpose(1, 0, 2)
    s_s = s_new.transpose(1, 0, 2)
    cd = state_conv.dtype
    conv_p = jnp.stack([v[n * SEQ + SEQ - (CONV_W - 1):(n + 1) * SEQ] for n in range(N_BATCH)]).astype(cd)
    conv_s = jnp.stack([state_conv[0][:, 1], v[s_lo:s_lo + N_SAMPLE].astype(cd)], axis=1)
    return (y_prompt, y_sample,
            s_p[None, ..., :p].astype(sd), s_p[None, ..., p:].astype(sd), conv_p[None],
            s_s[None, ..., :p].astype(sd), s_s[None, ..., p:].astype(sd), conv_s[None])
```

```python
import functools

import jax
import jax.numpy as jnp
import numpy as np
from jax import lax
from jax.experimental import pallas as pl
from jax.experimental.pallas import tpu as pltpu

D_MODEL = 2048
D_SSM = 1024
D_CONV = 1024
N_GROUPS = 64
GROUP = 16
N_STATE = 64
N_META = 16
CONV_W = 3
D_FF = 5632
LN_EPS = 1e-5
RMS_EPS = 1e-6
ALPHA = 2.0 ** 0.25

N_BATCH = 4
SEQ = 2048
N_SAMPLE = 128
CHUNK = 16
N_CHUNKS = SEQ // CHUNK + 1
ROW8 = 8
CW = CHUNK * GROUP
SW = 2 * N_STATE

TM = 512
N_PROMPT_ROWS = N_BATCH * SEQ
N_PROMPT_TILES = N_PROMPT_ROWS // TM
N_TILES = N_PROMPT_TILES + 1
M_PAD = N_TILES * TM
SAMPLE_ROW0 = N_META
TF = 512
GB = 8
VMEM_LIMIT = 60 * 1024 * 1024

F32 = jnp.float32
BF16 = jnp.bfloat16


def _layer_norm(x, g, b):
    mu = jnp.mean(x, axis=-1, keepdims=True)
    xc = x - mu
    var = jnp.mean(xc * xc, axis=-1, keepdims=True)
    return xc * lax.rsqrt(var + LN_EPS) * g + b


def _rms_norm(x, g):
    return x * lax.rsqrt(jnp.mean(x * x, axis=-1, keepdims=True) + RMS_EPS) * g


def _gelu_tanh(x):
    c = np.sqrt(2.0 / np.pi)
    hx = 0.5 * x
    return hx + hx * jnp.tanh(x * (np.float32(c) + np.float32(c * 0.044715) * (x * x)))


def _sigmoid(x):
    return 1.0 / (1.0 + jnp.exp2(x * np.float32(-1.4426950408889634)))


def _dot(a, b):
    return jnp.dot(a, b, preferred_element_type=F32)


def _dot_nt(a, b):
    return lax.dot_general(a, b, (((1,), (1,)), ((), ())), preferred_element_type=F32)


def _ssm_prep_kernel(logdt_ref, are_ref, aim_ref, b1_ref, b2_ref, c1_ref, c2_ref,
                     m_ref, wst_ref, cpow_ref, bpad_ref, cpad_ref, avec_ref):
    dt = jnp.exp(logdt_ref[...])
    a_re, a_im = are_ref[...], aim_ref[...]
    mag = jnp.exp(dt * a_re)
    ang = dt * a_im
    ab_re = mag * jnp.cos(ang)
    ab_im = mag * jnp.sin(ang)
    n_re = ab_re - 1.0
    den = a_re * a_re + a_im * a_im
    f_re = (n_re * a_re + ab_im * a_im) / den
    f_im = (ab_im * a_re - n_re * a_im) / den
    sgn = jnp.where(lax.broadcasted_iota(jnp.int32, (GB, SW), 1) < N_STATE, -1.0, 1.0)

    pr, pi = jnp.ones_like(ab_re), jnp.zeros_like(ab_re)
    pw = [(pr, pi)]
    for _ in range(CHUNK):
        pr, pi = pr * ab_re - pi * ab_im, pr * ab_im + pi * ab_re
        pw.append((pr, sgn * pi))
    fx = sgn * f_im
    lane = lax.broadcasted_iota(jnp.int32, (GROUP, CW), 1)

    for gl in range(GB):
        row = lambda x: x[gl:gl + 1, :]
        bb1 = row(f_re) * b1_ref[gl] + row(fx) * b2_ref[gl]
        bb2 = row(f_re) * b2_ref[gl] - row(fx) * b1_ref[gl]
        cc1 = -row(sgn) * c1_ref[gl]
        cc2 = -row(sgn) * c2_ref[gl]

        cq = []
        for s in range(CHUNK):
            wr, wx = pw[CHUNK - 1 - s]
            wst_ref[gl, s * GROUP:(s + 1) * GROUP, :] = (row(wr) * bb1 + row(wx) * bb2).astype(BF16)
            qr, qx = pw[s + 1]
            cpow_ref[gl, s * GROUP:(s + 1) * GROUP, :] = (row(qr) * cc1 + row(qx) * cc2).astype(BF16)
            dr, dx = pw[s]
            cq.append(row(dr) * cc1 + row(dx) * cc2)
        kt = lax.dot_general(bb1, jnp.concatenate(cq, axis=0), (((1,), (1,)), ((), ())),
                             precision=lax.Precision.HIGHEST, preferred_element_type=F32)
        for i in range(CHUNK):
            blk = kt if i == 0 else jnp.where(lane >= i * GROUP, pltpu.roll(kt, i * GROUP, axis=1), 0.0)
            m_ref[gl, i * GROUP:(i + 1) * GROUP, :] = blk.astype(BF16)

        zeros = jnp.zeros((SW, SW), BF16)
        bpad_ref[gl] = zeros
        cpad_ref[gl] = zeros
        bpad_ref[gl, gl * GROUP:(gl + 1) * GROUP, :] = bb1.astype(BF16)
        cpad_ref[gl, gl * GROUP:(gl + 1) * GROUP, :] = cq[0].astype(BF16)

        avec_ref[gl, 0:1, :] = row(pw[CHUNK][0])
        avec_ref[gl, 1:2, :] = row(pw[CHUNK][1])
        avec_ref[gl, 2:3, :] = row(pw[1][0])
        avec_ref[gl, 3:4, :] = row(pw[1][1])


def _ssm_prep(log_dt, a_re, a_im, b_re, b_im, c_re, c_im):
    g, c = N_GROUPS, GROUP
    dup = lambda x: jnp.concatenate([x, x], axis=-1)
    bt_re, bt_im = b_re.transpose(0, 2, 1), b_im.transpose(0, 2, 1)
    cat = lambda x, y: jnp.concatenate([x, y], axis=-1)
    vec = pl.BlockSpec((GB, SW), lambda i: (i, 0))
    blk = lambda *s: pl.BlockSpec((GB,) + s, lambda i: (i,) + (0,) * len(s))
    return pl.pallas_call(
        _ssm_prep_kernel,
        grid=(g // GB,),
        in_specs=[vec, vec, vec, blk(c, SW), blk(c, SW), blk(c, SW), blk(c, SW)],
        out_specs=[blk(CW, CW), blk(CW, SW), blk(CW, SW), blk(SW, SW), blk(SW, SW), blk(4, SW)],
        out_shape=[jax.ShapeDtypeStruct((g, CW, CW), BF16),
                   jax.ShapeDtypeStruct((g, CW, SW), BF16),
                   jax.ShapeDtypeStruct((g, CW, SW), BF16),
                   jax.ShapeDtypeStruct((g, SW, SW), BF16),
                   jax.ShapeDtypeStruct((g, SW, SW), BF16),
                   jax.ShapeDtypeStruct((g, 4, SW), F32)],
        name="ssm_prep",
    )(jnp.broadcast_to(log_dt[:, None], (g, SW)), dup(a_re), dup(a_im),
      cat(bt_re, bt_im), cat(bt_im, bt_re), cat(c_re, c_im), cat(c_im, c_re))


SUB_IN = 256


TILES_PER_SEQ = SEQ // TM


def _in_proj_kernel(xp_ref, xe_ref, g0_ref, b0_ref, w_ref, p1_ref, p2_ref, cw_ref, gc_ref,
                    h0_ref, u_ref, rc_ref, vt_ref, vs_ref, vbuf_scr, meta_scr):
    s = pl.program_id(0)

    @pl.when(s == 0)
    def _():
        vbuf_scr[0:ROW8, :] = jnp.zeros((ROW8, D_CONV), F32)

    def tile(x_ref, is_extra):
        for r0 in range(0, TM, SUB_IN):
            rs = slice(r0, r0 + SUB_IN)
            h0 = _layer_norm(x_ref[rs, :], g0_ref[...], b0_ref[...])
            h0_ref[rs, :] = h0
            hb = h0.astype(BF16)
            part = lambda k: _dot(hb, w_ref[:, k * D_SSM:(k + 1) * D_SSM])
            u_ref[rs, :] = part(0)
            b_gate = part(1)
            v = part(2) * part(3)
            vbuf_scr[ROW8 + r0:ROW8 + r0 + SUB_IN, :] = v
            prev = vbuf_scr[r0:r0 + ROW8 + SUB_IN, :]
            vm1 = pltpu.roll(prev, 1, axis=0)[ROW8:, :]
            vm2 = pltpu.roll(prev, 2, axis=0)[ROW8:, :]
            if is_extra and r0 == 0:
                sample = lax.broadcasted_iota(jnp.int32, (SUB_IN, 1), 0) >= SAMPLE_ROW0
                vm1 = jnp.where(sample, p1_ref[...], vm1)
                vm2 = jnp.where(sample, p2_ref[...], vm2)
            conv = cw_ref[0:1, :] * vm2 + cw_ref[1:2, :] * vm1 + cw_ref[2:3, :] * v
            rc_ref[rs, :] = _rms_norm(b_gate * conv, gc_ref[...]).astype(BF16)

    @pl.when(s == 0)
    def _():
        tile(xe_ref, True)
        vs_ref[...] = vbuf_scr[ROW8:ROW8 + SUB_IN, :]
        meta_scr[...] = vbuf_scr[ROW8 + N_META - ROW8:ROW8 + N_META, :]
        vbuf_scr[0:ROW8, :] = meta_scr[...]

    @pl.when(s > 0)
    def _():
        tile(xp_ref, False)
        tail = vbuf_scr[TM:TM + ROW8, :]
        vt_ref[0] = tail
        vbuf_scr[0:ROW8, :] = jnp.where(s % TILES_PER_SEQ == 0, meta_scr[...], tail)


def _in_proj(xp, xe, g0, b0, w_in, p1, p2, cw, gc):
    const = lambda s: (0, 0)
    row = lambda s: ((s + N_PROMPT_TILES) % N_TILES, 0)
    vec = lambda n: pl.BlockSpec((1, n), const)
    return pl.pallas_call(
        _in_proj_kernel,
        grid=(N_TILES,),
        in_specs=[pl.BlockSpec((TM, D_MODEL), lambda s: (jnp.maximum(s - 1, 0), 0)),
                  pl.BlockSpec((TM, D_MODEL), const),
                  vec(D_MODEL), vec(D_MODEL),
                  pl.BlockSpec((D_MODEL, 4 * D_SSM), const, pipeline_mode=pl.Buffered(1)),
                  pl.BlockSpec((SUB_IN, D_CONV), const), pl.BlockSpec((SUB_IN, D_CONV), const),
                  pl.BlockSpec((CONV_W, D_CONV), const), vec(D_CONV)],
        out_specs=[pl.BlockSpec((TM, D_MODEL), row), pl.BlockSpec((TM, D_SSM), row),
                   pl.BlockSpec((TM, D_CONV), row),
                   pl.BlockSpec((1, ROW8, D_CONV), lambda s: (jnp.maximum(s - 1, 0), 0, 0)),
                   pl.BlockSpec((SUB_IN, D_CONV), const)],
        out_shape=[jax.ShapeDtypeStruct((M_PAD, D_MODEL), F32),
                   jax.ShapeDtypeStruct((M_PAD, D_SSM), F32),
                   jax.ShapeDtypeStruct((M_PAD, D_CONV), BF16),
                   jax.ShapeDtypeStruct((N_PROMPT_TILES, ROW8, D_CONV), F32),
                   jax.ShapeDtypeStruct((SUB_IN, D_CONV), F32)],
        scratch_shapes=[pltpu.VMEM((ROW8 + TM, D_CONV), F32), pltpu.VMEM((ROW8, D_CONV), F32)],
        compiler_params=pltpu.CompilerParams(
            dimension_semantics=("arbitrary",), vmem_limit_bytes=VMEM_LIMIT),
        name="in_proj",
    )(xp, xe, g0, b0, w_in, p1, p2, cw, gc)


SUB = 16
SEQ_CHUNKS = SEQ // CHUNK
LHS_ROWS = N_BATCH * SEQ_CHUNKS + SUB
META_LHS = N_BATCH * SEQ_CHUNKS
PITCH = SEQ_CHUNKS + ROW8
REC_ROWS = N_BATCH * PITCH + ROW8
META_REC = N_BATCH * PITCH
SAMPLE_LO = N_PROMPT_ROWS + SAMPLE_ROW0
SAMPLE_HI = SAMPLE_LO + N_SAMPLE


def _block_transpose(a):
    a = list(a)
    blk = lax.broadcasted_iota(jnp.int32, a[0].shape, 1) // GROUP
    for d in (4, 2, 1):
        w = d * GROUP
        keep = (blk & d) == 0
        for j in range(GB):
            if j & d == 0:
                lo, hi = a[j], a[j + d]
                a[j] = jnp.where(keep, lo, pltpu.roll(hi, w, axis=1))
                a[j + d] = jnp.where(keep, pltpu.roll(lo, SW - w, axis=1), hi)
    return a


def _ssm_scan_kernel(u_ref, m_ref, wst_ref, cpow_ref, bpad_ref, cpad_ref, avec_ref, s0_ref,
                     y_ref, sfin_ref, snew_ref, lhs_scr, x_scr, xs_scr, sp_scr, yg_scr):
    def swap_halves(s):
        return pltpu.roll(s, N_STATE, axis=1)

    half = SW

    for h in range(2):
        src = [u_ref[pl.ds(8 * h + j, LHS_ROWS, stride=CHUNK), :] for j in range(GB)]
        for gl, w in enumerate(_block_transpose(src)):
            lhs_scr[gl, :, h * half:(h + 1) * half] = w.astype(BF16)

    def to_rec(ref, gl, x):
        for n in range(N_BATCH):
            ref[gl, n * PITCH:n * PITCH + SEQ_CHUNKS, :] = x[n * SEQ_CHUNKS:(n + 1) * SEQ_CHUNKS]
        ref[gl, META_REC:META_REC + ROW8, :] = x[META_LHS:META_LHS + ROW8]

    for gl in range(GB):
        x = _dot(lhs_scr[gl], wst_ref[gl])
        to_rec(x_scr, gl, x)
        to_rec(xs_scr, gl, swap_halves(x))

    ar16 = [avec_ref[gl, 0:1, :] for gl in range(GB)]
    ax16 = [avec_ref[gl, 1:2, :] for gl in range(GB)]

    def step(k, carry):
        new = []
        for gl in range(GB):
            s, t = carry[gl]
            rows = pl.ds(k, N_BATCH, stride=PITCH)
            sp_scr[gl, rows, :] = s
            new.append((ar16[gl] * s + ax16[gl] * t + x_scr[gl, rows, :],
                        ar16[gl] * t - ax16[gl] * s + xs_scr[gl, rows, :]))
        return tuple(new)

    first = lambda ref, gl: jnp.broadcast_to(ref[gl, META_REC:META_REC + 1, :], (N_BATCH, SW))
    init = tuple((first(x_scr, gl), first(xs_scr, gl)) for gl in range(GB))
    fin = lax.fori_loop(0, SEQ_CHUNKS, step, init)

    us = u_ref[SAMPLE_LO:SAMPLE_HI, :].astype(BF16)
    ys = jnp.zeros((N_SAMPLE, SW), F32)
    for gl in range(GB):
        sfin_ref[gl] = fin[gl][0]
        sp = jnp.concatenate([sp_scr[gl, n * PITCH:n * PITCH + SEQ_CHUNKS, :] for n in range(N_BATCH)]
                             + [jnp.zeros((SUB, SW), F32)], axis=0)
        yg_scr[gl] = _dot(lhs_scr[gl], m_ref[gl]) + _dot_nt(sp.astype(BF16), cpow_ref[gl])
        s0 = s0_ref[gl]
        sn = (avec_ref[gl, 2:3, :] * s0 + avec_ref[gl, 3:4, :] * swap_halves(s0)
              + _dot(us, bpad_ref[gl]))
        snew_ref[gl] = sn
        ys = ys + _dot_nt(sn.astype(BF16), cpad_ref[gl])

    for h in range(2):
        src = [yg_scr[gl, :, h * half:(h + 1) * half] for gl in range(GB)]
        for j, w in enumerate(_block_transpose(src)):
            y_ref[pl.ds(8 * h + j, META_LHS, stride=CHUNK), :] = w[0:META_LHS]
            y_ref[N_PROMPT_ROWS + 8 * h + j:N_PROMPT_ROWS + 8 * h + j + 1, :] = w[META_LHS:META_LHS + 1]
    y_ref[SAMPLE_LO:SAMPLE_HI, :] = ys
    y_ref[SAMPLE_HI:M_PAD, :] = jnp.zeros((M_PAD - SAMPLE_HI, SW), F32)


def _ssm_scan(u, m, wst, cpow, bpad, cpad, avec, s0):
    blk = lambda *s: pl.BlockSpec((GB,) + s, lambda i: (i,) + (0,) * len(s))
    col = pl.BlockSpec((M_PAD, SW), lambda i: (0, i))
    return pl.pallas_call(
        _ssm_scan_kernel,
        grid=(N_GROUPS // GB,),
        in_specs=[col, blk(CW, CW), blk(CW, SW), blk(CW, SW), blk(SW, SW), blk(SW, SW), blk(4, SW),
                  blk(N_SAMPLE, SW)],
        out_specs=[col, blk(N_BATCH, SW), blk(N_SAMPLE, SW)],
        out_shape=[jax.ShapeDtypeStruct((M_PAD, D_SSM), F32),
                   jax.ShapeDtypeStruct((N_GROUPS, N_BATCH, SW), F32),
                   jax.ShapeDtypeStruct((N_GROUPS, N_SAMPLE, SW), F32)],
        scratch_shapes=[pltpu.VMEM((GB, LHS_ROWS, CW), BF16), pltpu.VMEM((GB, REC_ROWS, SW), F32),
                        pltpu.VMEM((GB, REC_ROWS, SW), F32), pltpu.VMEM((GB, REC_ROWS, SW), F32),
                        pltpu.VMEM((GB, LHS_ROWS, CW), F32)],
        compiler_params=pltpu.CompilerParams(
            dimension_semantics=("arbitrary",), vmem_limit_bytes=VMEM_LIMIT),
        name="ssm_scan",
    )(u, m, wst, cpow, bpad, cpad, avec, s0)


SUB_MIX = 256


def _mix_out_kernel(y_ref, u_ref, rc_ref, h0_ref, dskip_ref, wglu_ref, gs_ref, wout_ref,
                    g1_ref, b1_ref, x1_ref, x1b_ref, merged_scr):
    for r0 in range(0, TM, SUB_MIX):
        rs = slice(r0, r0 + SUB_MIX)
        ys = _gelu_tanh(y_ref[rs, :] + dskip_ref[...] * u_ref[rs, :])
        gate = _dot(ys.astype(BF16), wglu_ref[...])
        merged_scr[rs, 0:D_SSM] = _rms_norm(ys * _sigmoid(gate), gs_ref[...]).astype(BF16)
        merged_scr[rs, D_SSM:D_MODEL] = rc_ref[rs, :]
        mo = _dot(merged_scr[rs, :], wout_ref[...])
        x1 = _layer_norm(ALPHA * h0_ref[rs, :] + mo, g1_ref[...], b1_ref[...])
        x1_ref[rs, :] = x1
        x1b_ref[rs, :] = x1.astype(BF16)


def _mix_out(y, u, rc, h0, dskip, wglu, gs, wout, g1, b1):
    row = lambda i: (i, 0)
    const = lambda i: (0, 0)
    vec = lambda n: pl.BlockSpec((1, n), const)
    return pl.pallas_call(
        _mix_out_kernel,
        grid=(N_TILES,),
        in_specs=[pl.BlockSpec((TM, D_SSM), row), pl.BlockSpec((TM, D_SSM), row),
                  pl.BlockSpec((TM, D_CONV), row), pl.BlockSpec((TM, D_MODEL), row), vec(D_SSM),
                  pl.BlockSpec((D_SSM, D_SSM), const, pipeline_mode=pl.Buffered(1)),
                  vec(D_SSM),
                  pl.BlockSpec((D_MODEL, D_MODEL), const, pipeline_mode=pl.Buffered(1)),
                  vec(D_MODEL), vec(D_MODEL)],
        out_specs=[pl.BlockSpec((TM, D_MODEL), row), pl.BlockSpec((TM, D_MODEL), row)],
        out_shape=[jax.ShapeDtypeStruct((M_PAD, D_MODEL), F32),
                   jax.ShapeDtypeStruct((M_PAD, D_MODEL), BF16)],
        scratch_shapes=[pltpu.VMEM((TM, D_MODEL), BF16)],
        compiler_params=pltpu.CompilerParams(
            dimension_semantics=("arbitrary",), vmem_limit_bytes=VMEM_LIMIT),
        name="mix_out",
    )(y, u, rc, h0, dskip, wglu, gs, wout, g1, b1)


def _ffn_up_kernel(xb_ref, wg_ref, wu_ref, wd_ref, act_ref, wdb_ref, wg_scr, wu_scr):
    @pl.when(pl.program_id(1) == 0)
    def _():
        wg_scr[...] = wg_ref[...].astype(BF16)
        wu_scr[...] = wu_ref[...].astype(BF16)
        wdb_ref[...] = wd_ref[...].astype(BF16)

    xb = xb_ref[...]
    gate = _dot(xb, wg_scr[...])
    up = _dot(xb, wu_scr[...])
    act_ref[...] = (gate * _sigmoid(gate) * up).astype(BF16)


TM_UP = M_PAD // 8
SUB_DOWN = 256
EXTRA_ROWS = 256


def _ffn_up(x1b, wg, wu, wd):
    wcol = pl.BlockSpec((D_MODEL, TF), lambda f, i: (0, f))
    wrow = pl.BlockSpec((TF, D_MODEL), lambda f, i: (f, 0))
    return pl.pallas_call(
        _ffn_up_kernel,
        grid=(D_FF // TF, M_PAD // TM_UP),
        in_specs=[pl.BlockSpec((TM_UP, D_MODEL), lambda f, i: (i, 0)), wcol, wcol, wrow],
        out_specs=[pl.BlockSpec((TM_UP, TF), lambda f, i: (i, f)), wrow],
        out_shape=[jax.ShapeDtypeStruct((M_PAD, D_FF), BF16),
                   jax.ShapeDtypeStruct((D_FF, D_MODEL), BF16)],
        scratch_shapes=[pltpu.VMEM((D_MODEL, TF), BF16), pltpu.VMEM((D_MODEL, TF), BF16)],
        compiler_params=pltpu.CompilerParams(
            dimension_semantics=("arbitrary", "arbitrary"), vmem_limit_bytes=VMEM_LIMIT),
        name="ffn_up",
    )(x1b, wg, wu, wd)


def _ffn_down_kernel(act_ref, x_ref, wd_ref, g2_ref, b2_ref, yp_ref, ye_ref):
    i = pl.program_id(0)

    def tile(o_ref, rows):
        for r0 in range(0, rows, SUB_DOWN):
            rs = slice(r0, r0 + SUB_DOWN)
            z = ALPHA * x_ref[rs, :] + _dot(act_ref[rs, :], wd_ref[...])
            o_ref[rs, :] = _layer_norm(z, g2_ref[...], b2_ref[...])

    @pl.when(i < N_PROMPT_TILES)
    def _():
        tile(yp_ref, TM)

    @pl.when(i == N_PROMPT_TILES)
    def _():
        tile(ye_ref, EXTRA_ROWS)


def _ffn_down(act, x1, wd, g2, b2):
    row = lambda i: (i, 0)
    const = lambda i: (0, 0)
    return pl.pallas_call(
        _ffn_down_kernel,
        grid=(N_TILES,),
        in_specs=[pl.BlockSpec((TM, D_FF), row), pl.BlockSpec((TM, D_MODEL), row),
                  pl.BlockSpec((D_FF, D_MODEL), const, pipeline_mode=pl.Buffered(1)),
                  pl.BlockSpec((1, D_MODEL), const), pl.BlockSpec((1, D_MODEL), const)],
        out_specs=[pl.BlockSpec((TM, D_MODEL), lambda i: (jnp.minimum(i, N_PROMPT_TILES - 1), 0)),
                   pl.BlockSpec((EXTRA_ROWS, D_MODEL), const)],
        out_shape=[jax.ShapeDtypeStruct((N_PROMPT_ROWS, D_MODEL), F32),
                   jax.ShapeDtypeStruct((EXTRA_ROWS, D_MODEL), F32)],
        compiler_params=pltpu.CompilerParams(
            dimension_semantics=("arbitrary",), vmem_limit_bytes=VMEM_LIMIT),
        name="ffn_down",
    )(act, x1, wd, g2, b2)


def _pad_rows(a, n):
    return jnp.pad(a, ((0, n - a.shape[0]), (0, 0)))


def kernel(x_prompt, x_sample, state_ssm_re, state_ssm_im, state_conv, meta_tokens, ln0_g, ln0_b,
           w_in, ssm_a_re, ssm_a_im, ssm_log_dt, ssm_b_re, ssm_b_im, ssm_c_re, ssm_c_im, ssm_d,
           ssm_w_glu, conv_w, g_ssm_out, g_conv_out, w_out, ln1_g, ln1_b, w_gate, w_up, w_down,
           ln2_g, ln2_b):
    assert x_prompt.shape == (N_BATCH, SEQ, D_MODEL) and x_sample.shape == (N_SAMPLE, 1, D_MODEL)
    assert w_in.shape[0] == 1, "single layer"
    g, p, c = N_GROUPS, N_STATE, GROUP
    vec = lambda a: a.reshape(1, -1)

    xp = x_prompt.reshape(N_PROMPT_ROWS, D_MODEL)
    xe = _pad_rows(jnp.concatenate([meta_tokens, x_sample[:, 0, :]], axis=0), TM)
    g0, b0 = vec(ln0_g), vec(ln0_b)

    m, wst, cpow, bpad, cpad, avec = _ssm_prep(ssm_log_dt[0], ssm_a_re[0], ssm_a_im[0], ssm_b_re[0],
                                               ssm_b_im[0], ssm_c_re[0], ssm_c_im[0])

    prev = jnp.pad(state_conv[0], ((SAMPLE_ROW0, SUB_IN - SAMPLE_ROW0 - N_SAMPLE), (0, 0), (0, 0)))
    h0, u, rc, v_tail, v_extra = _in_proj(xp, xe, g0, b0, w_in[0].astype(BF16), prev[:, 1], prev[:, 0],
                                          conv_w[0], vec(g_conv_out[0]))

    s0 = jnp.concatenate([state_ssm_re[0], state_ssm_im[0]], axis=-1).transpose(1, 0, 2)
    y_all, s_fin, s_new = _ssm_scan(u, m, wst, cpow, bpad, cpad, avec, s0)

    x1, x1b = _mix_out(y_all, u, rc, h0, vec(ssm_d[0]), ssm_w_glu[0].astype(BF16), vec(g_ssm_out[0]),
                       w_out[0].astype(BF16), vec(ln1_g[0]), vec(ln1_b[0]))

    act, wd_bf = _ffn_up(x1b, w_gate[0], w_up[0], w_down[0])
    yp, ye = _ffn_down(act, x1, wd_bf, vec(ln2_g[0]), vec(ln2_b[0]))

    y_prompt = yp.reshape(N_BATCH, SEQ, D_MODEL)
    y_sample = ye[SAMPLE_ROW0:SAMPLE_ROW0 + N_SAMPLE].reshape(N_SAMPLE, 1, D_MODEL)

    sd = state_ssm_re.dtype
    s_p = s_fin.transpose(1, 0, 2)
    s_s = s_new.transpose(1, 0, 2)
    cd = state_conv.dtype
    conv_p = v_tail[TILES_PER_SEQ - 1::TILES_PER_SEQ, ROW8 - (CONV_W - 1):].astype(cd)
    conv_s = jnp.stack([state_conv[0][:, 1], v_extra[SAMPLE_ROW0:SAMPLE_ROW0 + N_SAMPLE].astype(cd)], axis=1)
    return (y_prompt, y_sample,
            s_p[None, ..., :p].astype(sd), s_p[None, ..., p:].astype(sd), conv_p[None],
            s_s[None, ..., :p].astype(sd), s_s[None, ..., p:].astype(sd), conv_s[None])
```

```python
import functools

import jax
import jax.numpy as jnp
import numpy as np
from jax import lax
from jax.experimental import pallas as pl
from jax.experimental.pallas import tpu as pltpu

D_MODEL = 2048
D_SSM = 1024
D_CONV = 1024
N_GROUPS = 64
GROUP = 16
N_STATE = 64
N_META = 16
CONV_W = 3
D_FF = 5632
LN_EPS = 1e-5
RMS_EPS = 1e-6
ALPHA = 2.0 ** 0.25

N_BATCH = 4
SEQ = 2048
N_SAMPLE = 128
CHUNK = 16
N_CHUNKS = SEQ // CHUNK + 1
ROW8 = 8
CW = CHUNK * GROUP
SW = 2 * N_STATE

TM = 512
N_PROMPT_ROWS = N_BATCH * SEQ
N_PROMPT_TILES = N_PROMPT_ROWS // TM
N_TILES = N_PROMPT_TILES + 1
M_PAD = N_TILES * TM
SAMPLE_ROW0 = N_META
GB = 8
VMEM_LIMIT = 60 * 1024 * 1024

F32 = jnp.float32
BF16 = jnp.bfloat16


def _layer_norm(x, g, b):
    mu = jnp.mean(x, axis=-1, keepdims=True)
    xc = x - mu
    var = jnp.mean(xc * xc, axis=-1, keepdims=True)
    return xc * lax.rsqrt(var + LN_EPS) * g + b


def _rms_norm(x, g):
    return x * lax.rsqrt(jnp.mean(x * x, axis=-1, keepdims=True) + RMS_EPS) * g


def _gelu_tanh(x):
    c = np.sqrt(2.0 / np.pi)
    hx = 0.5 * x
    return hx + hx * jnp.tanh(x * (np.float32(c) + np.float32(c * 0.044715) * (x * x)))


def _sigmoid(x):
    return 1.0 / (1.0 + jnp.exp2(x * np.float32(-1.4426950408889634)))


def _dot(a, b):
    return jnp.dot(a, b, preferred_element_type=F32)


def _dot_nt(a, b):
    return lax.dot_general(a, b, (((1,), (1,)), ((), ())), preferred_element_type=F32)


def _ssm_prep_kernel(logdt_ref, are_ref, aim_ref, b1_ref, b2_ref, c1_ref, c2_ref,
                     m_ref, wst_ref, cpow_ref, bpad_ref, cpad_ref, avec_ref):
    dt = jnp.exp(logdt_ref[...])
    a_re, a_im = are_ref[...], aim_ref[...]
    mag = jnp.exp(dt * a_re)
    ang = dt * a_im
    ab_re = mag * jnp.cos(ang)
    ab_im = mag * jnp.sin(ang)
    n_re = ab_re - 1.0
    den = a_re * a_re + a_im * a_im
    f_re = (n_re * a_re + ab_im * a_im) / den
    f_im = (ab_im * a_re - n_re * a_im) / den
    sgn = jnp.where(lax.broadcasted_iota(jnp.int32, (GB, SW), 1) < N_STATE, -1.0, 1.0)

    pr, pi = jnp.ones_like(ab_re), jnp.zeros_like(ab_re)
    pw = [(pr, pi)]
    for _ in range(CHUNK):
        pr, pi = pr * ab_re - pi * ab_im, pr * ab_im + pi * ab_re
        pw.append((pr, sgn * pi))
    fx = sgn * f_im
    lane = lax.broadcasted_iota(jnp.int32, (GROUP, CW), 1)

    for gl in range(GB):
        row = lambda x: x[gl:gl + 1, :]
        bb1 = row(f_re) * b1_ref[gl] + row(fx) * b2_ref[gl]
        bb2 = row(f_re) * b2_ref[gl] - row(fx) * b1_ref[gl]
        cc1 = -row(sgn) * c1_ref[gl]
        cc2 = -row(sgn) * c2_ref[gl]

        cq = []
        for s in range(CHUNK):
            wr, wx = pw[CHUNK - 1 - s]
            wst_ref[gl, s * GROUP:(s + 1) * GROUP, :] = (row(wr) * bb1 + row(wx) * bb2).astype(BF16)
            qr, qx = pw[s + 1]
            cpow_ref[gl, s * GROUP:(s + 1) * GROUP, :] = (row(qr) * cc1 + row(qx) * cc2).astype(BF16)
            dr, dx = pw[s]
            cq.append(row(dr) * cc1 + row(dx) * cc2)
        kt = lax.dot_general(bb1, jnp.concatenate(cq, axis=0), (((1,), (1,)), ((), ())),
                             precision=lax.Precision.HIGHEST, preferred_element_type=F32)
        for i in range(CHUNK):
            blk = kt if i == 0 else jnp.where(lane >= i * GROUP, pltpu.roll(kt, i * GROUP, axis=1), 0.0)
            m_ref[gl, i * GROUP:(i + 1) * GROUP, :] = blk.astype(BF16)

        zeros = jnp.zeros((SW, SW), BF16)
        bpad_ref[gl] = zeros
        cpad_ref[gl] = zeros
        bpad_ref[gl, gl * GROUP:(gl + 1) * GROUP, :] = bb1.astype(BF16)
        cpad_ref[gl, gl * GROUP:(gl + 1) * GROUP, :] = cq[0].astype(BF16)

        avec_ref[gl, 0:1, :] = row(pw[CHUNK][0])
        avec_ref[gl, 1:2, :] = row(pw[CHUNK][1])
        avec_ref[gl, 2:3, :] = row(pw[1][0])
        avec_ref[gl, 3:4, :] = row(pw[1][1])


def _ssm_prep(log_dt, a_re, a_im, b_re, b_im, c_re, c_im):
    g, c = N_GROUPS, GROUP
    dup = lambda x: jnp.concatenate([x, x], axis=-1)
    bt_re, bt_im = b_re.transpose(0, 2, 1), b_im.transpose(0, 2, 1)
    cat = lambda x, y: jnp.concatenate([x, y], axis=-1)
    vec = pl.BlockSpec((GB, SW), lambda i: (i, 0))
    blk = lambda *s: pl.BlockSpec((GB,) + s, lambda i: (i,) + (0,) * len(s))
    return pl.pallas_call(
        _ssm_prep_kernel,
        grid=(g // GB,),
        in_specs=[vec, vec, vec, blk(c, SW), blk(c, SW), blk(c, SW), blk(c, SW)],
        out_specs=[blk(CW, CW), blk(CW, SW), blk(CW, SW), blk(SW, SW), blk(SW, SW), blk(4, SW)],
        out_shape=[jax.ShapeDtypeStruct((g, CW, CW), BF16),
                   jax.ShapeDtypeStruct((g, CW, SW), BF16),
                   jax.ShapeDtypeStruct((g, CW, SW), BF16),
                   jax.ShapeDtypeStruct((g, SW, SW), BF16),
                   jax.ShapeDtypeStruct((g, SW, SW), BF16),
                   jax.ShapeDtypeStruct((g, 4, SW), F32)],
        name="ssm_prep",
    )(jnp.broadcast_to(log_dt[:, None], (g, SW)), dup(a_re), dup(a_im),
      cat(bt_re, bt_im), cat(bt_im, bt_re), cat(c_re, c_im), cat(c_im, c_re))


SUB_IN = 256


TILES_PER_SEQ = SEQ // TM


def _in_proj_kernel(xp_ref, xe_ref, g0_ref, b0_ref, w_ref, p1_ref, p2_ref, cw_ref, gc_ref,
                    h0_ref, u_ref, rc_ref, vt_ref, vs_ref, vbuf_scr, meta_scr):
    s = pl.program_id(0)

    @pl.when(s == 0)
    def _():
        vbuf_scr[0:ROW8, :] = jnp.zeros((ROW8, D_CONV), F32)

    def tile(x_ref, is_extra):
        for r0 in range(0, TM, SUB_IN):
            rs = slice(r0, r0 + SUB_IN)
            h0 = _layer_norm(x_ref[rs, :], g0_ref[...], b0_ref[...])
            h0_ref[rs, :] = h0
            hb = h0.astype(BF16)
            part = lambda k: _dot(hb, w_ref[:, k * D_SSM:(k + 1) * D_SSM])
            u_ref[rs, :] = part(0)
            b_gate = part(1)
            v = part(2) * part(3)
            vbuf_scr[ROW8 + r0:ROW8 + r0 + SUB_IN, :] = v
            prev = vbuf_scr[r0:r0 + ROW8 + SUB_IN, :]
            vm1 = pltpu.roll(prev, 1, axis=0)[ROW8:, :]
            vm2 = pltpu.roll(prev, 2, axis=0)[ROW8:, :]
            if is_extra and r0 == 0:
                sample = lax.broadcasted_iota(jnp.int32, (SUB_IN, 1), 0) >= SAMPLE_ROW0
                vm1 = jnp.where(sample, p1_ref[...], vm1)
                vm2 = jnp.where(sample, p2_ref[...], vm2)
            conv = cw_ref[0:1, :] * vm2 + cw_ref[1:2, :] * vm1 + cw_ref[2:3, :] * v
            rc_ref[rs, :] = _rms_norm(b_gate * conv, gc_ref[...]).astype(BF16)

    @pl.when(s == 0)
    def _():
        tile(xe_ref, True)
        vs_ref[...] = vbuf_scr[ROW8:ROW8 + SUB_IN, :]
        meta_scr[...] = vbuf_scr[ROW8 + N_META - ROW8:ROW8 + N_META, :]
        vbuf_scr[0:ROW8, :] = meta_scr[...]

    @pl.when(s > 0)
    def _():
        tile(xp_ref, False)
        tail = vbuf_scr[TM:TM + ROW8, :]
        vt_ref[0] = tail
        vbuf_scr[0:ROW8, :] = jnp.where(s % TILES_PER_SEQ == 0, meta_scr[...], tail)


def _in_proj(xp, xe, g0, b0, w_in, p1, p2, cw, gc):
    const = lambda s: (0, 0)
    row = lambda s: ((s + N_PROMPT_TILES) % N_TILES, 0)
    vec = lambda n: pl.BlockSpec((1, n), const)
    return pl.pallas_call(
        _in_proj_kernel,
        grid=(N_TILES,),
        in_specs=[pl.BlockSpec((TM, D_MODEL), lambda s: (jnp.maximum(s - 1, 0), 0)),
                  pl.BlockSpec((TM, D_MODEL), const),
                  vec(D_MODEL), vec(D_MODEL),
                  pl.BlockSpec((D_MODEL, 4 * D_SSM), const, pipeline_mode=pl.Buffered(1)),
                  pl.BlockSpec((SUB_IN, D_CONV), const), pl.BlockSpec((SUB_IN, D_CONV), const),
                  pl.BlockSpec((CONV_W, D_CONV), const), vec(D_CONV)],
        out_specs=[pl.BlockSpec((TM, D_MODEL), row), pl.BlockSpec((TM, D_SSM), row),
                   pl.BlockSpec((TM, D_CONV), row),
                   pl.BlockSpec((1, ROW8, D_CONV), lambda s: (jnp.maximum(s - 1, 0), 0, 0)),
                   pl.BlockSpec((SUB_IN, D_CONV), const)],
        out_shape=[jax.ShapeDtypeStruct((M_PAD, D_MODEL), F32),
                   jax.ShapeDtypeStruct((M_PAD, D_SSM), F32),
                   jax.ShapeDtypeStruct((M_PAD, D_CONV), BF16),
                   jax.ShapeDtypeStruct((N_PROMPT_TILES, ROW8, D_CONV), F32),
                   jax.ShapeDtypeStruct((SUB_IN, D_CONV), F32)],
        scratch_shapes=[pltpu.VMEM((ROW8 + TM, D_CONV), F32), pltpu.VMEM((ROW8, D_CONV), F32)],
        compiler_params=pltpu.CompilerParams(
            dimension_semantics=("arbitrary",), vmem_limit_bytes=VMEM_LIMIT),
        name="in_proj",
    )(xp, xe, g0, b0, w_in, p1, p2, cw, gc)


SUB = 16
SEQ_CHUNKS = SEQ // CHUNK
LHS_ROWS = N_BATCH * SEQ_CHUNKS + SUB
META_LHS = N_BATCH * SEQ_CHUNKS
PITCH = SEQ_CHUNKS + ROW8
REC_ROWS = N_BATCH * PITCH + ROW8
META_REC = N_BATCH * PITCH
SAMPLE_LO = N_PROMPT_ROWS + SAMPLE_ROW0
SAMPLE_HI = SAMPLE_LO + N_SAMPLE


def _block_transpose(a):
    a = list(a)
    blk = lax.broadcasted_iota(jnp.int32, a[0].shape, 1) // GROUP
    for d in (4, 2, 1):
        w = d * GROUP
        keep = (blk & d) == 0
        for j in range(GB):
            if j & d == 0:
                lo, hi = a[j], a[j + d]
                a[j] = jnp.where(keep, lo, pltpu.roll(hi, w, axis=1))
                a[j + d] = jnp.where(keep, pltpu.roll(lo, SW - w, axis=1), hi)
    return a


def _ssm_scan_kernel(u_ref, m_ref, wst_ref, cpow_ref, bpad_ref, cpad_ref, avec_ref, s0_ref,
                     y_ref, sfin_ref, snew_ref, lhs_scr, x_scr, xs_scr, sp_scr, yg_scr):
    def swap_halves(s):
        return pltpu.roll(s, N_STATE, axis=1)

    half = SW

    for h in range(2):
        src = [u_ref[pl.ds(8 * h + j, LHS_ROWS, stride=CHUNK), :] for j in range(GB)]
        for gl, w in enumerate(_block_transpose(src)):
            lhs_scr[gl, :, h * half:(h + 1) * half] = w.astype(BF16)

    def to_rec(ref, gl, x):
        for n in range(N_BATCH):
            ref[gl, n * PITCH:n * PITCH + SEQ_CHUNKS, :] = x[n * SEQ_CHUNKS:(n + 1) * SEQ_CHUNKS]
        ref[gl, META_REC:META_REC + ROW8, :] = x[META_LHS:META_LHS + ROW8]

    for gl in range(GB):
        x = _dot(lhs_scr[gl], wst_ref[gl])
        to_rec(x_scr, gl, x)
        to_rec(xs_scr, gl, swap_halves(x))

    ar16 = [avec_ref[gl, 0:1, :] for gl in range(GB)]
    ax16 = [avec_ref[gl, 1:2, :] for gl in range(GB)]

    def step(k, carry):
        new = []
        for gl in range(GB):
            s, t = carry[gl]
            rows = pl.ds(k, N_BATCH, stride=PITCH)
            sp_scr[gl, rows, :] = s
            new.append((ar16[gl] * s + ax16[gl] * t + x_scr[gl, rows, :],
                        ar16[gl] * t - ax16[gl] * s + xs_scr[gl, rows, :]))
        return tuple(new)

    first = lambda ref, gl: jnp.broadcast_to(ref[gl, META_REC:META_REC + 1, :], (N_BATCH, SW))
    init = tuple((first(x_scr, gl), first(xs_scr, gl)) for gl in range(GB))
    fin = lax.fori_loop(0, SEQ_CHUNKS, step, init)

    us = u_ref[SAMPLE_LO:SAMPLE_HI, :].astype(BF16)
    ys = jnp.zeros((N_SAMPLE, SW), F32)
    for gl in range(GB):
        sfin_ref[gl] = fin[gl][0]
        sp = jnp.concatenate([sp_scr[gl, n * PITCH:n * PITCH + SEQ_CHUNKS, :] for n in range(N_BATCH)]
                             + [jnp.zeros((SUB, SW), F32)], axis=0)
        yg_scr[gl] = _dot(lhs_scr[gl], m_ref[gl]) + _dot_nt(sp.astype(BF16), cpow_ref[gl])
        s0 = s0_ref[gl]
        sn = (avec_ref[gl, 2:3, :] * s0 + avec_ref[gl, 3:4, :] * swap_halves(s0)
              + _dot(us, bpad_ref[gl]))
        snew_ref[gl] = sn
        ys = ys + _dot_nt(sn.astype(BF16), cpad_ref[gl])

    for h in range(2):
        src = [yg_scr[gl, :, h * half:(h + 1) * half] for gl in range(GB)]
        for j, w in enumerate(_block_transpose(src)):
            y_ref[pl.ds(8 * h + j, META_LHS, stride=CHUNK), :] = w[0:META_LHS]
            y_ref[N_PROMPT_ROWS + 8 * h + j:N_PROMPT_ROWS + 8 * h + j + 1, :] = w[META_LHS:META_LHS + 1]
    y_ref[SAMPLE_LO:SAMPLE_HI, :] = ys
    y_ref[SAMPLE_HI:M_PAD, :] = jnp.zeros((M_PAD - SAMPLE_HI, SW), F32)


def _ssm_scan(u, m, wst, cpow, bpad, cpad, avec, s0):
    blk = lambda *s: pl.BlockSpec((GB,) + s, lambda i: (i,) + (0,) * len(s))
    col = pl.BlockSpec((M_PAD, SW), lambda i: (0, i))
    return pl.pallas_call(
        _ssm_scan_kernel,
        grid=(N_GROUPS // GB,),
        in_specs=[col, blk(CW, CW), blk(CW, SW), blk(CW, SW), blk(SW, SW), blk(SW, SW), blk(4, SW),
                  blk(N_SAMPLE, SW)],
        out_specs=[col, blk(N_BATCH, SW), blk(N_SAMPLE, SW)],
        out_shape=[jax.ShapeDtypeStruct((M_PAD, D_SSM), F32),
                   jax.ShapeDtypeStruct((N_GROUPS, N_BATCH, SW), F32),
                   jax.ShapeDtypeStruct((N_GROUPS, N_SAMPLE, SW), F32)],
        scratch_shapes=[pltpu.VMEM((GB, LHS_ROWS, CW), BF16), pltpu.VMEM((GB, REC_ROWS, SW), F32),
                        pltpu.VMEM((GB, REC_ROWS, SW), F32), pltpu.VMEM((GB, REC_ROWS, SW), F32),
                        pltpu.VMEM((GB, LHS_ROWS, CW), F32)],
        compiler_params=pltpu.CompilerParams(
            dimension_semantics=("arbitrary",), vmem_limit_bytes=VMEM_LIMIT),
        name="ssm_scan",
    )(u, m, wst, cpow, bpad, cpad, avec, s0)


SUB_MIX = 256


def _mix_out_kernel(y_ref, u_ref, rc_ref, h0_ref, dskip_ref, wglu_ref, gs_ref, wout_ref,
                    g1_ref, b1_ref, x1_ref, x1b_ref, merged_scr):
    for r0 in range(0, TM, SUB_MIX):
        rs = slice(r0, r0 + SUB_MIX)
        ys = _gelu_tanh(y_ref[rs, :] + dskip_ref[...] * u_ref[rs, :])
        gate = _dot(ys.astype(BF16), wglu_ref[...])
        merged_scr[rs, 0:D_SSM] = _rms_norm(ys * _sigmoid(gate), gs_ref[...]).astype(BF16)
        merged_scr[rs, D_SSM:D_MODEL] = rc_ref[rs, :]
        mo = _dot(merged_scr[rs, :], wout_ref[...])
        x1 = _layer_norm(ALPHA * h0_ref[rs, :] + mo, g1_ref[...], b1_ref[...])
        x1_ref[rs, :] = x1
        x1b_ref[rs, :] = x1.astype(BF16)


def _mix_out(y, u, rc, h0, dskip, wglu, gs, wout, g1, b1):
    row = lambda i: (i, 0)
    const = lambda i: (0, 0)
    vec = lambda n: pl.BlockSpec((1, n), const)
    return pl.pallas_call(
        _mix_out_kernel,
        grid=(N_TILES,),
        in_specs=[pl.BlockSpec((TM, D_SSM), row), pl.BlockSpec((TM, D_SSM), row),
                  pl.BlockSpec((TM, D_CONV), row), pl.BlockSpec((TM, D_MODEL), row), vec(D_SSM),
                  pl.BlockSpec((D_SSM, D_SSM), const, pipeline_mode=pl.Buffered(1)),
                  vec(D_SSM),
                  pl.BlockSpec((D_MODEL, D_MODEL), const, pipeline_mode=pl.Buffered(1)),
                  vec(D_MODEL), vec(D_MODEL)],
        out_specs=[pl.BlockSpec((TM, D_MODEL), row), pl.BlockSpec((TM, D_MODEL), row)],
        out_shape=[jax.ShapeDtypeStruct((M_PAD, D_MODEL), F32),
                   jax.ShapeDtypeStruct((M_PAD, D_MODEL), BF16)],
        scratch_shapes=[pltpu.VMEM((TM, D_MODEL), BF16)],
        compiler_params=pltpu.CompilerParams(
            dimension_semantics=("arbitrary",), vmem_limit_bytes=VMEM_LIMIT),
        name="mix_out",
    )(y, u, rc, h0, dskip, wglu, gs, wout, g1, b1)


M_HALF = M_PAD // 2
SUB_UP = M_HALF // 4
TF_UP = 256
SUB_DOWN = 256
EXTRA_ROWS = 256


def _ffn_up_kernel(xb_ref, wg_ref, wu_ref, wd_ref, act_ref, wdb_ref):
    @pl.when(pl.program_id(0) == 0)
    def _():
        wdb_ref[...] = wd_ref[...].astype(BF16)

    wgb = wg_ref[...].astype(BF16)
    wub = wu_ref[...].astype(BF16)
    for r0 in range(0, M_HALF, SUB_UP):
        rs = slice(r0, r0 + SUB_UP)
        xb = xb_ref[rs, :]
        gate = _dot(xb, wgb)
        up = _dot(xb, wub)
        act_ref[rs, :] = (gate * _sigmoid(gate) * up).astype(BF16)


def _ffn_up(x1b, wg, wu, wd):
    wcol = pl.BlockSpec((D_MODEL, TF_UP), lambda h, f: (0, f))
    wrow = pl.BlockSpec((TF_UP, D_MODEL), lambda h, f: (jnp.where(h == 0, f, D_FF // TF_UP - 1), 0))
    return pl.pallas_call(
        _ffn_up_kernel,
        grid=(M_PAD // M_HALF, D_FF // TF_UP),
        in_specs=[pl.BlockSpec((M_HALF, D_MODEL), lambda h, f: (h, 0), pipeline_mode=pl.Buffered(1)),
                  wcol, wcol, wrow],
        out_specs=[pl.BlockSpec((M_HALF, TF_UP), lambda h, f: (h, f)), wrow],
        out_shape=[jax.ShapeDtypeStruct((M_PAD, D_FF), BF16),
                   jax.ShapeDtypeStruct((D_FF, D_MODEL), BF16)],
        compiler_params=pltpu.CompilerParams(
            dimension_semantics=("arbitrary", "arbitrary"), vmem_limit_bytes=VMEM_LIMIT),
        name="ffn_up",
    )(x1b, wg, wu, wd)


def _ffn_down_kernel(act_ref, x_ref, wd_ref, g2_ref, b2_ref, yp_ref, ye_ref):
    i = pl.program_id(0)

    def tile(o_ref, rows):
        for r0 in range(0, rows, SUB_DOWN):
            rs = slice(r0, r0 + SUB_DOWN)
            z = ALPHA * x_ref[rs, :] + _dot(act_ref[rs, :], wd_ref[...])
            o_ref[rs, :] = _layer_norm(z, g2_ref[...], b2_ref[...])

    @pl.when(i < N_PROMPT_TILES)
    def _():
        tile(yp_ref, TM)

    @pl.when(i == N_PROMPT_TILES)
    def _():
        tile(ye_ref, EXTRA_ROWS)


def _ffn_down(act, x1, wd, g2, b2):
    row = lambda i: (i, 0)
    const = lambda i: (0, 0)
    return pl.pallas_call(
        _ffn_down_kernel,
        grid=(N_TILES,),
        in_specs=[pl.BlockSpec((TM, D_FF), row), pl.BlockSpec((TM, D_MODEL), row),
                  pl.BlockSpec((D_FF, D_MODEL), const, pipeline_mode=pl.Buffered(1)),
                  pl.BlockSpec((1, D_MODEL), const), pl.BlockSpec((1, D_MODEL), const)],
        out_specs=[pl.BlockSpec((TM, D_MODEL), lambda i: (jnp.minimum(i, N_PROMPT_TILES - 1), 0)),
                   pl.BlockSpec((EXTRA_ROWS, D_MODEL), const)],
        out_shape=[jax.ShapeDtypeStruct((N_PROMPT_ROWS, D_MODEL), F32),
                   jax.ShapeDtypeStruct((EXTRA_ROWS, D_MODEL), F32)],
        compiler_params=pltpu.CompilerParams(
            dimension_semantics=("arbitrary",), vmem_limit_bytes=VMEM_LIMIT),
        name="ffn_down",
    )(act, x1, wd, g2, b2)


def _pad_rows(a, n):
    return jnp.pad(a, ((0, n - a.shape[0]), (0, 0)))


def kernel(x_prompt, x_sample, state_ssm_re, state_ssm_im, state_conv, meta_tokens, ln0_g, ln0_b,
           w_in, ssm_a_re, ssm_a_im, ssm_log_dt, ssm_b_re, ssm_b_im, ssm_c_re, ssm_c_im, ssm_d,
           ssm_w_glu, conv_w, g_ssm_out, g_conv_out, w_out, ln1_g, ln1_b, w_gate, w_up, w_down,
           ln2_g, ln2_b):
    assert x_prompt.shape == (N_BATCH, SEQ, D_MODEL) and x_sample.shape == (N_SAMPLE, 1, D_MODEL)
    assert w_in.shape[0] == 1, "single layer"
    g, p, c = N_GROUPS, N_STATE, GROUP
    vec = lambda a: a.reshape(1, -1)

    xp = x_prompt.reshape(N_PROMPT_ROWS, D_MODEL)
    xe = _pad_rows(jnp.concatenate([meta_tokens, x_sample[:, 0, :]], axis=0), TM)
    g0, b0 = vec(ln0_g), vec(ln0_b)

    m, wst, cpow, bpad, cpad, avec = _ssm_prep(ssm_log_dt[0], ssm_a_re[0], ssm_a_im[0], ssm_b_re[0],
                                               ssm_b_im[0], ssm_c_re[0], ssm_c_im[0])

    prev = jnp.pad(state_conv[0], ((SAMPLE_ROW0, SUB_IN - SAMPLE_ROW0 - N_SAMPLE), (0, 0), (0, 0)))
    h0, u, rc, v_tail, v_extra = _in_proj(xp, xe, g0, b0, w_in[0].astype(BF16), prev[:, 1], prev[:, 0],
                                          conv_w[0], vec(g_conv_out[0]))

    s0 = jnp.concatenate([state_ssm_re[0], state_ssm_im[0]], axis=-1).transpose(1, 0, 2)
    y_all, s_fin, s_new = _ssm_scan(u, m, wst, cpow, bpad, cpad, avec, s0)

    x1, x1b = _mix_out(y_all, u, rc, h0, vec(ssm_d[0]), ssm_w_glu[0].astype(BF16), vec(g_ssm_out[0]),
                       w_out[0].astype(BF16), vec(ln1_g[0]), vec(ln1_b[0]))

    act, wd_bf = _ffn_up(x1b, w_gate[0], w_up[0], w_down[0])
    yp, ye = _ffn_down(act, x1, wd_bf, vec(ln2_g[0]), vec(ln2_b[0]))

    y_prompt = yp.reshape(N_BATCH, SEQ, D_MODEL)
    y_sample = ye[SAMPLE_ROW0:SAMPLE_ROW0 + N_SAMPLE].reshape(N_SAMPLE, 1, D_MODEL)

    sd = state_ssm_re.dtype
    s_p = s_fin.transpose(1, 0, 2)
    s_s = s_new.transpose(1, 0, 2)
    cd = state_conv.dtype
    conv_p = v_tail[TILES_PER_SEQ - 1::TILES_PER_SEQ, ROW8 - (CONV_W - 1):].astype(cd)
    conv_s = jnp.stack([state_conv[0][:, 1], v_extra[SAMPLE_ROW0:SAMPLE_ROW0 + N_SAMPLE].astype(cd)], axis=1)
    return (y_prompt, y_sample,
            s_p[None, ..., :p].astype(sd), s_p[None, ..., p:].astype(sd), conv_p[None],
            s_s[None, ..., :p].astype(sd), s_s[None, ..., p:].astype(sd), conv_s[None])
```

```python
import functools

import jax
import jax.numpy as jnp
import numpy as np
from jax import lax
from jax.experimental import pallas as pl
from jax.experimental.pallas import tpu as pltpu

D_MODEL = 2048
D_SSM = 1024
D_CONV = 1024
N_GROUPS = 64
GROUP = 16
N_STATE = 64
N_META = 16
CONV_W = 3
D_FF = 5632
LN_EPS = 1e-5
RMS_EPS = 1e-6
ALPHA = 2.0 ** 0.25

N_BATCH = 4
SEQ = 2048
N_SAMPLE = 128
CHUNK = 16
N_CHUNKS = SEQ // CHUNK + 1
ROW8 = 8
CW = CHUNK * GROUP
SW = 2 * N_STATE

TM = 512
N_PROMPT_ROWS = N_BATCH * SEQ
N_PROMPT_TILES = N_PROMPT_ROWS // TM
N_TILES = N_PROMPT_TILES + 1
M_PAD = N_TILES * TM
SAMPLE_ROW0 = N_META
GB = 8
VMEM_LIMIT = 60 * 1024 * 1024

F32 = jnp.float32
BF16 = jnp.bfloat16


def _layer_norm(x, g, b):
    mu = jnp.mean(x, axis=-1, keepdims=True)
    xc = x - mu
    var = jnp.mean(xc * xc, axis=-1, keepdims=True)
    return xc * lax.rsqrt(var + LN_EPS) * g + b


def _rms_norm(x, g):
    return x * lax.rsqrt(jnp.mean(x * x, axis=-1, keepdims=True) + RMS_EPS) * g


def _gelu_tanh(x):
    c = np.sqrt(2.0 / np.pi)
    hx = 0.5 * x
    return hx + hx * jnp.tanh(x * (np.float32(c) + np.float32(c * 0.044715) * (x * x)))


def _sigmoid(x):
    return 1.0 / (1.0 + jnp.exp2(x * np.float32(-1.4426950408889634)))


def _dot(a, b):
    return jnp.dot(a, b, preferred_element_type=F32)


def _dot_nt(a, b):
    return lax.dot_general(a, b, (((1,), (1,)), ((), ())), preferred_element_type=F32)


def _ssm_prep_kernel(logdt_ref, are_ref, aim_ref, b1_ref, b2_ref, c1_ref, c2_ref,
                     m_ref, wst_ref, cpow_ref, bpad_ref, cpad_ref, avec_ref):
    dt = jnp.exp(logdt_ref[...])
    a_re, a_im = are_ref[...], aim_ref[...]
    mag = jnp.exp(dt * a_re)
    ang = dt * a_im
    ab_re = mag * jnp.cos(ang)
    ab_im = mag * jnp.sin(ang)
    n_re = ab_re - 1.0
    den = a_re * a_re + a_im * a_im
    f_re = (n_re * a_re + ab_im * a_im) / den
    f_im = (ab_im * a_re - n_re * a_im) / den
    sgn = jnp.where(lax.broadcasted_iota(jnp.int32, (GB, SW), 1) < N_STATE, -1.0, 1.0)

    pr, pi = jnp.ones_like(ab_re), jnp.zeros_like(ab_re)
    pw = [(pr, pi)]
    for _ in range(CHUNK):
        pr, pi = pr * ab_re - pi * ab_im, pr * ab_im + pi * ab_re
        pw.append((pr, sgn * pi))
    fx = sgn * f_im
    lane = lax.broadcasted_iota(jnp.int32, (GROUP, CW), 1)

    for gl in range(GB):
        row = lambda x: x[gl:gl + 1, :]
        bb1 = row(f_re) * b1_ref[gl] + row(fx) * b2_ref[gl]
        bb2 = row(f_re) * b2_ref[gl] - row(fx) * b1_ref[gl]
        cc1 = -row(sgn) * c1_ref[gl]
        cc2 = -row(sgn) * c2_ref[gl]

        cq = []
        for s in range(CHUNK):
            wr, wx = pw[CHUNK - 1 - s]
            wst_ref[gl, s * GROUP:(s + 1) * GROUP, :] = (row(wr) * bb1 + row(wx) * bb2).astype(BF16)
            qr, qx = pw[s + 1]
            cpow_ref[gl, s * GROUP:(s + 1) * GROUP, :] = (row(qr) * cc1 + row(qx) * cc2).astype(BF16)
            dr, dx = pw[s]
            cq.append(row(dr) * cc1 + row(dx) * cc2)
        kt = lax.dot_general(bb1, jnp.concatenate(cq, axis=0), (((1,), (1,)), ((), ())),
                             precision=lax.Precision.HIGHEST, preferred_element_type=F32)
        for i in range(CHUNK):
            blk = kt if i == 0 else jnp.where(lane >= i * GROUP, pltpu.roll(kt, i * GROUP, axis=1), 0.0)
            m_ref[gl, i * GROUP:(i + 1) * GROUP, :] = blk.astype(BF16)

        zeros = jnp.zeros((SW, SW), BF16)
        bpad_ref[gl] = zeros
        cpad_ref[gl] = zeros
        bpad_ref[gl, gl * GROUP:(gl + 1) * GROUP, :] = bb1.astype(BF16)
        cpad_ref[gl, gl * GROUP:(gl + 1) * GROUP, :] = cq[0].astype(BF16)

        avec_ref[gl, 0:1, :] = row(pw[CHUNK][0])
        avec_ref[gl, 1:2, :] = row(pw[CHUNK][1])
        avec_ref[gl, 2:3, :] = row(pw[1][0])
        avec_ref[gl, 3:4, :] = row(pw[1][1])


def _ssm_prep(log_dt, a_re, a_im, b_re, b_im, c_re, c_im):
    g, c = N_GROUPS, GROUP
    dup = lambda x: jnp.concatenate([x, x], axis=-1)
    bt_re, bt_im = b_re.transpose(0, 2, 1), b_im.transpose(0, 2, 1)
    cat = lambda x, y: jnp.concatenate([x, y], axis=-1)
    vec = pl.BlockSpec((GB, SW), lambda i: (i, 0))
    blk = lambda *s: pl.BlockSpec((GB,) + s, lambda i: (i,) + (0,) * len(s))
    return pl.pallas_call(
        _ssm_prep_kernel,
        grid=(g // GB,),
        in_specs=[vec, vec, vec, blk(c, SW), blk(c, SW), blk(c, SW), blk(c, SW)],
        out_specs=[blk(CW, CW), blk(CW, SW), blk(CW, SW), blk(SW, SW), blk(SW, SW), blk(4, SW)],
        out_shape=[jax.ShapeDtypeStruct((g, CW, CW), BF16),
                   jax.ShapeDtypeStruct((g, CW, SW), BF16),
                   jax.ShapeDtypeStruct((g, CW, SW), BF16),
                   jax.ShapeDtypeStruct((g, SW, SW), BF16),
                   jax.ShapeDtypeStruct((g, SW, SW), BF16),
                   jax.ShapeDtypeStruct((g, 4, SW), F32)],
        name="ssm_prep",
    )(jnp.broadcast_to(log_dt[:, None], (g, SW)), dup(a_re), dup(a_im),
      cat(bt_re, bt_im), cat(bt_im, bt_re), cat(c_re, c_im), cat(c_im, c_re))


SUB_IN = 256


TILES_PER_SEQ = SEQ // TM


def _in_proj_kernel(xp_ref, xe_ref, g0_ref, b0_ref, w_ref, p1_ref, p2_ref, cw_ref, gc_ref, wo_ref, wgl_ref,
                    h0_ref, u_ref, rc_ref, vt_ref, vs_ref, wob_ref, wglb_ref, vbuf_scr, meta_scr):
    s = pl.program_id(0)

    @pl.when(s == 0)
    def _():
        vbuf_scr[0:ROW8, :] = jnp.zeros((ROW8, D_CONV), F32)

    def tile(x_ref, is_extra):
        for r0 in range(0, TM, SUB_IN):
            rs = slice(r0, r0 + SUB_IN)
            if is_extra and r0 > 0:
                h0_ref[rs, :] = jnp.zeros((SUB_IN, D_MODEL), F32)
                u_ref[rs, :] = jnp.zeros((SUB_IN, D_SSM), F32)
                rc_ref[rs, :] = jnp.zeros((SUB_IN, D_CONV), BF16)
                continue
            h0 = _layer_norm(x_ref[rs, :], g0_ref[...], b0_ref[...])
            h0_ref[rs, :] = h0
            hb = h0.astype(BF16)
            part = lambda k: _dot(hb, w_ref[:, k * D_SSM:(k + 1) * D_SSM])
            u_ref[rs, :] = part(0)
            b_gate = part(1)
            v = part(2) * part(3)
            vbuf_scr[ROW8 + r0:ROW8 + r0 + SUB_IN, :] = v
            prev = vbuf_scr[r0:r0 + ROW8 + SUB_IN, :]
            vm1 = pltpu.roll(prev, 1, axis=0)[ROW8:, :]
            vm2 = pltpu.roll(prev, 2, axis=0)[ROW8:, :]
            if is_extra and r0 == 0:
                sample = lax.broadcasted_iota(jnp.int32, (SUB_IN, 1), 0) >= SAMPLE_ROW0
                vm1 = jnp.where(sample, p1_ref[...], vm1)
                vm2 = jnp.where(sample, p2_ref[...], vm2)
            conv = cw_ref[0:1, :] * vm2 + cw_ref[1:2, :] * vm1 + cw_ref[2:3, :] * v
            rc_ref[rs, :] = _rms_norm(b_gate * conv, gc_ref[...]).astype(BF16)

    @pl.when(s == 0)
    def _():
        tile(xe_ref, True)
        vs_ref[...] = vbuf_scr[ROW8:ROW8 + SUB_IN, :]
        meta_scr[...] = vbuf_scr[ROW8 + N_META - ROW8:ROW8 + N_META, :]
        vbuf_scr[0:ROW8, :] = meta_scr[...]

    @pl.when(s > 0)
    def _():
        tile(xp_ref, False)
        wob_ref[...] = wo_ref[...].astype(BF16)
        wglb_ref[...] = wgl_ref[...].astype(BF16)
        tail = vbuf_scr[TM:TM + ROW8, :]
        vt_ref[0] = tail
        vbuf_scr[0:ROW8, :] = jnp.where(s % TILES_PER_SEQ == 0, meta_scr[...], tail)


def _in_proj(xp, xe, g0, b0, w_in, p1, p2, cw, gc, w_out, w_glu):
    const = lambda s: (0, 0)
    slab = lambda s: (jnp.maximum(s - 1, 0), 0)
    row = lambda s: ((s + N_PROMPT_TILES) % N_TILES, 0)
    vec = lambda n: pl.BlockSpec((1, n), const)
    return pl.pallas_call(
        _in_proj_kernel,
        grid=(N_TILES,),
        in_specs=[pl.BlockSpec((TM, D_MODEL), lambda s: (jnp.maximum(s - 1, 0), 0)),
                  pl.BlockSpec((SUB_IN, D_MODEL), const),
                  vec(D_MODEL), vec(D_MODEL),
                  pl.BlockSpec((D_MODEL, 4 * D_SSM), const, pipeline_mode=pl.Buffered(1)),
                  pl.BlockSpec((SUB_IN, D_CONV), const), pl.BlockSpec((SUB_IN, D_CONV), const),
                  pl.BlockSpec((CONV_W, D_CONV), const), vec(D_CONV),
                  pl.BlockSpec((D_MODEL // N_PROMPT_TILES, D_MODEL), slab),
                  pl.BlockSpec((D_SSM // N_PROMPT_TILES, D_SSM), slab)],
        out_specs=[pl.BlockSpec((TM, D_MODEL), row), pl.BlockSpec((TM, D_SSM), row),
                   pl.BlockSpec((TM, D_CONV), row),
                   pl.BlockSpec((1, ROW8, D_CONV), lambda s: (jnp.maximum(s - 1, 0), 0, 0)),
                   pl.BlockSpec((SUB_IN, D_CONV), const),
                   pl.BlockSpec((D_MODEL // N_PROMPT_TILES, D_MODEL), slab),
                   pl.BlockSpec((D_SSM // N_PROMPT_TILES, D_SSM), slab)],
        out_shape=[jax.ShapeDtypeStruct((M_PAD, D_MODEL), F32),
                   jax.ShapeDtypeStruct((M_PAD, D_SSM), F32),
                   jax.ShapeDtypeStruct((M_PAD, D_CONV), BF16),
                   jax.ShapeDtypeStruct((N_PROMPT_TILES, ROW8, D_CONV), F32),
                   jax.ShapeDtypeStruct((SUB_IN, D_CONV), F32),
                   jax.ShapeDtypeStruct((D_MODEL, D_MODEL), BF16),
                   jax.ShapeDtypeStruct((D_SSM, D_SSM), BF16)],
        scratch_shapes=[pltpu.VMEM((ROW8 + TM, D_CONV), F32), pltpu.VMEM((ROW8, D_CONV), F32)],
        compiler_params=pltpu.CompilerParams(
            dimension_semantics=("arbitrary",), vmem_limit_bytes=VMEM_LIMIT),
        name="in_proj",
    )(xp, xe, g0, b0, w_in, p1, p2, cw, gc, w_out, w_glu)


SUB = 16
SEQ_CHUNKS = SEQ // CHUNK
LHS_ROWS = N_BATCH * SEQ_CHUNKS + SUB
META_LHS = N_BATCH * SEQ_CHUNKS
PITCH = SEQ_CHUNKS + ROW8
REC_ROWS = N_BATCH * PITCH + ROW8
META_REC = N_BATCH * PITCH
SAMPLE_LO = N_PROMPT_ROWS + SAMPLE_ROW0
SAMPLE_HI = SAMPLE_LO + N_SAMPLE


def _block_transpose(a):
    a = list(a)
    blk = lax.broadcasted_iota(jnp.int32, a[0].shape, 1) // GROUP
    for d in (4, 2, 1):
        w = d * GROUP
        keep = (blk & d) == 0
        for j in range(GB):
            if j & d == 0:
                lo, hi = a[j], a[j + d]
                a[j] = jnp.where(keep, lo, pltpu.roll(hi, w, axis=1))
                a[j + d] = jnp.where(keep, pltpu.roll(lo, SW - w, axis=1), hi)
    return a


def _ssm_scan_kernel(u_ref, m_ref, wst_ref, cpow_ref, bpad_ref, cpad_ref, avec_ref, s0_ref,
                     y_ref, sfin_ref, snew_ref, lhs_scr, x_scr, xs_scr, sp_scr, yg_scr):
    def swap_halves(s):
        return pltpu.roll(s, N_STATE, axis=1)

    half = SW

    for h in range(2):
        src = [u_ref[pl.ds(8 * h + j, LHS_ROWS, stride=CHUNK), :] for j in range(GB)]
        for gl, w in enumerate(_block_transpose(src)):
            lhs_scr[gl, :, h * half:(h + 1) * half] = w.astype(BF16)

    def to_rec(ref, gl, x):
        for n in range(N_BATCH):
            ref[gl, n * PITCH:n * PITCH + SEQ_CHUNKS, :] = x[n * SEQ_CHUNKS:(n + 1) * SEQ_CHUNKS]
        ref[gl, META_REC:META_REC + ROW8, :] = x[META_LHS:META_LHS + ROW8]

    for gl in range(GB):
        x = _dot(lhs_scr[gl], wst_ref[gl])
        to_rec(x_scr, gl, x)
        to_rec(xs_scr, gl, swap_halves(x))

    ar16 = [avec_ref[gl, 0:1, :] for gl in range(GB)]
    ax16 = [avec_ref[gl, 1:2, :] for gl in range(GB)]

    def step(k, carry):
        new = []
        for gl in range(GB):
            s, t = carry[gl]
            rows = pl.ds(k, N_BATCH, stride=PITCH)
            sp_scr[gl, rows, :] = s
            new.append((ar16[gl] * s + ax16[gl] * t + x_scr[gl, rows, :],
                        ar16[gl] * t - ax16[gl] * s + xs_scr[gl, rows, :]))
        return tuple(new)

    first = lambda ref, gl: jnp.broadcast_to(ref[gl, META_REC:META_REC + 1, :], (N_BATCH, SW))
    init = tuple((first(x_scr, gl), first(xs_scr, gl)) for gl in range(GB))
    fin = lax.fori_loop(0, SEQ_CHUNKS, step, init)

    us = u_ref[SAMPLE_LO:SAMPLE_HI, :].astype(BF16)
    ys = jnp.zeros((N_SAMPLE, SW), F32)
    for gl in range(GB):
        sfin_ref[gl] = fin[gl][0]
        sp = jnp.concatenate([sp_scr[gl, n * PITCH:n * PITCH + SEQ_CHUNKS, :] for n in range(N_BATCH)]
                             + [jnp.zeros((SUB, SW), F32)], axis=0)
        yg_scr[gl] = _dot(lhs_scr[gl], m_ref[gl]) + _dot_nt(sp.astype(BF16), cpow_ref[gl])
        s0 = s0_ref[gl]
        sn = (avec_ref[gl, 2:3, :] * s0 + avec_ref[gl, 3:4, :] * swap_halves(s0)
              + _dot(us, bpad_ref[gl]))
        snew_ref[gl] = sn
        ys = ys + _dot_nt(sn.astype(BF16), cpad_ref[gl])

    for h in range(2):
        src = [yg_scr[gl, :, h * half:(h + 1) * half] for gl in range(GB)]
        for j, w in enumerate(_block_transpose(src)):
            y_ref[pl.ds(8 * h + j, META_LHS, stride=CHUNK), :] = w[0:META_LHS]
            y_ref[N_PROMPT_ROWS + 8 * h + j:N_PROMPT_ROWS + 8 * h + j + 1, :] = w[META_LHS:META_LHS + 1]
    y_ref[SAMPLE_LO:SAMPLE_HI, :] = ys
    y_ref[SAMPLE_HI:M_PAD, :] = jnp.zeros((M_PAD - SAMPLE_HI, SW), F32)


def _ssm_scan(u, m, wst, cpow, bpad, cpad, avec, s0):
    blk = lambda *s: pl.BlockSpec((GB,) + s, lambda i: (i,) + (0,) * len(s))
    col = pl.BlockSpec((M_PAD, SW), lambda i: (0, i))
    return pl.pallas_call(
        _ssm_scan_kernel,
        grid=(N_GROUPS // GB,),
        in_specs=[col, blk(CW, CW), blk(CW, SW), blk(CW, SW), blk(SW, SW), blk(SW, SW), blk(4, SW),
                  blk(N_SAMPLE, SW)],
        out_specs=[col, blk(N_BATCH, SW), blk(N_SAMPLE, SW)],
        out_shape=[jax.ShapeDtypeStruct((M_PAD, D_SSM), F32),
                   jax.ShapeDtypeStruct((N_GROUPS, N_BATCH, SW), F32),
                   jax.ShapeDtypeStruct((N_GROUPS, N_SAMPLE, SW), F32)],
        scratch_shapes=[pltpu.VMEM((GB, LHS_ROWS, CW), BF16), pltpu.VMEM((GB, REC_ROWS, SW), F32),
                        pltpu.VMEM((GB, REC_ROWS, SW), F32), pltpu.VMEM((GB, REC_ROWS, SW), F32),
                        pltpu.VMEM((GB, LHS_ROWS, CW), F32)],
        compiler_params=pltpu.CompilerParams(
            dimension_semantics=("arbitrary",), vmem_limit_bytes=VMEM_LIMIT),
        name="ssm_scan",
    )(u, m, wst, cpow, bpad, cpad, avec, s0)


SUB_MIX = 256


def _mix_out_kernel(y_ref, u_ref, rc_ref, h0_ref, dskip_ref, wglu_ref, gs_ref, wout_ref,
                    g1_ref, b1_ref, x1_ref, x1b_ref, merged_scr):
    def sub_tile(r0):
        rs = slice(r0, r0 + SUB_MIX)
        ys = _gelu_tanh(y_ref[rs, :] + dskip_ref[...] * u_ref[rs, :])
        gate = _dot(ys.astype(BF16), wglu_ref[...])
        merged_scr[rs, 0:D_SSM] = _rms_norm(ys * _sigmoid(gate), gs_ref[...]).astype(BF16)
        merged_scr[rs, D_SSM:D_MODEL] = rc_ref[rs, :]
        mo = _dot(merged_scr[rs, :], wout_ref[...])
        x1 = _layer_norm(ALPHA * h0_ref[rs, :] + mo, g1_ref[...], b1_ref[...])
        x1_ref[rs, :] = x1
        x1b_ref[rs, :] = x1.astype(BF16)

    @pl.when(pl.program_id(0) < N_PROMPT_TILES)
    def _():
        for r0 in range(0, TM, SUB_MIX):
            sub_tile(r0)

    @pl.when(pl.program_id(0) == N_PROMPT_TILES)
    def _():
        sub_tile(0)
        x1_ref[SUB_MIX:TM, :] = jnp.zeros((TM - SUB_MIX, D_MODEL), F32)
        x1b_ref[SUB_MIX:TM, :] = jnp.zeros((TM - SUB_MIX, D_MODEL), BF16)


def _mix_out(y, u, rc, h0, dskip, wglu, gs, wout, g1, b1):
    row = lambda i: (i, 0)
    const = lambda i: (0, 0)
    vec = lambda n: pl.BlockSpec((1, n), const)
    return pl.pallas_call(
        _mix_out_kernel,
        grid=(N_TILES,),
        in_specs=[pl.BlockSpec((TM, D_SSM), row), pl.BlockSpec((TM, D_SSM), row),
                  pl.BlockSpec((TM, D_CONV), row), pl.BlockSpec((TM, D_MODEL), row), vec(D_SSM),
                  pl.BlockSpec((D_SSM, D_SSM), const, pipeline_mode=pl.Buffered(1)),
                  vec(D_SSM),
                  pl.BlockSpec((D_MODEL, D_MODEL), const, pipeline_mode=pl.Buffered(1)),
                  vec(D_MODEL), vec(D_MODEL)],
        out_specs=[pl.BlockSpec((TM, D_MODEL), row), pl.BlockSpec((TM, D_MODEL), row)],
        out_shape=[jax.ShapeDtypeStruct((M_PAD, D_MODEL), F32),
                   jax.ShapeDtypeStruct((M_PAD, D_MODEL), BF16)],
        scratch_shapes=[pltpu.VMEM((TM, D_MODEL), BF16)],
        compiler_params=pltpu.CompilerParams(
            dimension_semantics=("arbitrary",), vmem_limit_bytes=VMEM_LIMIT),
        name="mix_out",
    )(y, u, rc, h0, dskip, wglu, gs, wout, g1, b1)


M_HALF = M_PAD // 2
SUB_UP = M_HALF // 4
TF_UP = 256
SUB_DOWN = 256
EXTRA_ROWS = 256
M_REAL_HALF = M_HALF - (TM - EXTRA_ROWS)


def _ffn_up_kernel(xb_ref, wg_ref, wu_ref, wd_ref, act_ref, wdb_ref):
    @pl.when(pl.program_id(0) == 0)
    def _():
        wdb_ref[...] = wd_ref[...].astype(BF16)

    wgb = wg_ref[...].astype(BF16)
    wub = wu_ref[...].astype(BF16)

    def rows(lo, hi):
        xb = xb_ref[lo:hi, :]
        gate = _dot(xb, wgb)
        up = _dot(xb, wub)
        act_ref[lo:hi, :] = (gate * _sigmoid(gate) * up).astype(BF16)

    @pl.when(pl.program_id(0) == 0)
    def _():
        for r0 in range(0, M_HALF, SUB_UP):
            rows(r0, r0 + SUB_UP)

    @pl.when(pl.program_id(0) == 1)
    def _():
        for r0 in range(0, M_REAL_HALF, SUB_UP):
            rows(r0, min(r0 + SUB_UP, M_REAL_HALF))
        act_ref[M_REAL_HALF:M_HALF, :] = jnp.zeros((M_HALF - M_REAL_HALF, TF_UP), BF16)


def _ffn_up(x1b, wg, wu, wd):
    wcol = pl.BlockSpec((D_MODEL, TF_UP), lambda h, f: (0, f))
    wrow = pl.BlockSpec((TF_UP, D_MODEL), lambda h, f: (jnp.where(h == 0, f, D_FF // TF_UP - 1), 0))
    return pl.pallas_call(
        _ffn_up_kernel,
        grid=(M_PAD // M_HALF, D_FF // TF_UP),
        in_specs=[pl.BlockSpec((M_HALF, D_MODEL), lambda h, f: (h, 0), pipeline_mode=pl.Buffered(1)),
                  wcol, wcol, wrow],
        out_specs=[pl.BlockSpec((M_HALF, TF_UP), lambda h, f: (h, f)), wrow],
        out_shape=[jax.ShapeDtypeStruct((M_PAD, D_FF), BF16),
                   jax.ShapeDtypeStruct((D_FF, D_MODEL), BF16)],
        compiler_params=pltpu.CompilerParams(
            dimension_semantics=("arbitrary", "arbitrary"), vmem_limit_bytes=VMEM_LIMIT),
        name="ffn_up",
    )(x1b, wg, wu, wd)


def _ffn_down_kernel(act_ref, x_ref, wd_ref, g2_ref, b2_ref, yp_ref, ye_ref):
    i = pl.program_id(0)

    def tile(o_ref, rows):
        for r0 in range(0, rows, SUB_DOWN):
            rs = slice(r0, r0 + SUB_DOWN)
            z = ALPHA * x_ref[rs, :] + _dot(act_ref[rs, :], wd_ref[...])
            o_ref[rs, :] = _layer_norm(z, g2_ref[...], b2_ref[...])

    @pl.when(i < N_PROMPT_TILES)
    def _():
        tile(yp_ref, TM)

    @pl.when(i == N_PROMPT_TILES)
    def _():
        tile(ye_ref, EXTRA_ROWS)


def _ffn_down(act, x1, wd, g2, b2):
    row = lambda i: (i, 0)
    const = lambda i: (0, 0)
    return pl.pallas_call(
        _ffn_down_kernel,
        grid=(N_TILES,),
        in_specs=[pl.BlockSpec((TM, D_FF), row), pl.BlockSpec((TM, D_MODEL), row),
                  pl.BlockSpec((D_FF, D_MODEL), const, pipeline_mode=pl.Buffered(1)),
                  pl.BlockSpec((1, D_MODEL), const), pl.BlockSpec((1, D_MODEL), const)],
        out_specs=[pl.BlockSpec((TM, D_MODEL), lambda i: (jnp.minimum(i, N_PROMPT_TILES - 1), 0)),
                   pl.BlockSpec((EXTRA_ROWS, D_MODEL), const)],
        out_shape=[jax.ShapeDtypeStruct((N_PROMPT_ROWS, D_MODEL), F32),
                   jax.ShapeDtypeStruct((EXTRA_ROWS, D_MODEL), F32)],
        compiler_params=pltpu.CompilerParams(
            dimension_semantics=("arbitrary",), vmem_limit_bytes=VMEM_LIMIT),
        name="ffn_down",
    )(act, x1, wd, g2, b2)


def _pad_rows(a, n):
    return jnp.pad(a, ((0, n - a.shape[0]), (0, 0)))


def kernel(x_prompt, x_sample, state_ssm_re, state_ssm_im, state_conv, meta_tokens, ln0_g, ln0_b,
           w_in, ssm_a_re, ssm_a_im, ssm_log_dt, ssm_b_re, ssm_b_im, ssm_c_re, ssm_c_im, ssm_d,
           ssm_w_glu, conv_w, g_ssm_out, g_conv_out, w_out, ln1_g, ln1_b, w_gate, w_up, w_down,
           ln2_g, ln2_b):
    assert x_prompt.shape == (N_BATCH, SEQ, D_MODEL) and x_sample.shape == (N_SAMPLE, 1, D_MODEL)
    assert w_in.shape[0] == 1, "single layer"
    g, p, c = N_GROUPS, N_STATE, GROUP
    vec = lambda a: a.reshape(1, -1)

    xp = x_prompt.reshape(N_PROMPT_ROWS, D_MODEL)
    xe = _pad_rows(jnp.concatenate([meta_tokens, x_sample[:, 0, :]], axis=0), SUB_IN)
    g0, b0 = vec(ln0_g), vec(ln0_b)

    m, wst, cpow, bpad, cpad, avec = _ssm_prep(ssm_log_dt[0], ssm_a_re[0], ssm_a_im[0], ssm_b_re[0],
                                               ssm_b_im[0], ssm_c_re[0], ssm_c_im[0])

    prev = jnp.pad(state_conv[0], ((SAMPLE_ROW0, SUB_IN - SAMPLE_ROW0 - N_SAMPLE), (0, 0), (0, 0)))
    h0, u, rc, v_tail, v_extra, wout_bf, wglu_bf = _in_proj(
        xp, xe, g0, b0, w_in[0].astype(BF16), prev[:, 1], prev[:, 0], conv_w[0], vec(g_conv_out[0]),
        w_out[0], ssm_w_glu[0])

    s0 = jnp.concatenate([state_ssm_re[0], state_ssm_im[0]], axis=-1).transpose(1, 0, 2)
    y_all, s_fin, s_new = _ssm_scan(u, m, wst, cpow, bpad, cpad, avec, s0)

    x1, x1b = _mix_out(y_all, u, rc, h0, vec(ssm_d[0]), wglu_bf, vec(g_ssm_out[0]),
                       wout_bf, vec(ln1_g[0]), vec(ln1_b[0]))

    act, wd_bf = _ffn_up(x1b, w_gate[0], w_up[0], w_down[0])
    yp, ye = _ffn_down(act, x1, wd_bf, vec(ln2_g[0]), vec(ln2_b[0]))

    y_prompt = yp.reshape(N_BATCH, SEQ, D_MODEL)
    y_sample = ye[SAMPLE_ROW0:SAMPLE_ROW0 + N_SAMPLE].reshape(N_SAMPLE, 1, D_MODEL)

    sd = state_ssm_re.dtype
    s_p = s_fin.transpose(1, 0, 2)
    s_s = s_new.transpose(1, 0, 2)
    cd = state_conv.dtype
    conv_p = v_tail[TILES_PER_SEQ - 1::TILES_PER_SEQ, ROW8 - (CONV_W - 1):].astype(cd)
    conv_s = jnp.stack([state_conv[0][:, 1], v_extra[SAMPLE_ROW0:SAMPLE_ROW0 + N_SAMPLE].astype(cd)], axis=1)
    return (y_prompt, y_sample,
            s_p[None, ..., :p].astype(sd), s_p[None, ..., p:].astype(sd), conv_p[None],
            s_s[None, ..., :p].astype(sd), s_s[None, ..., p:].astype(sd), conv_s[None])
```

```python
import functools

import jax
import jax.numpy as jnp
import numpy as np
from jax import lax
from jax.experimental import pallas as pl
from jax.experimental.pallas import tpu as pltpu

D_MODEL = 2048
D_SSM = 1024
D_CONV = 1024
N_GROUPS = 64
GROUP = 16
N_STATE = 64
N_META = 16
CONV_W = 3
D_FF = 5632
LN_EPS = 1e-5
RMS_EPS = 1e-6
ALPHA = 2.0 ** 0.25

N_BATCH = 4
SEQ = 2048
N_SAMPLE = 128
CHUNK = 16
N_CHUNKS = SEQ // CHUNK + 1
ROW8 = 8
CW = CHUNK * GROUP
SW = 2 * N_STATE

TM = 512
N_PROMPT_ROWS = N_BATCH * SEQ
N_PROMPT_TILES = N_PROMPT_ROWS // TM
N_TILES = N_PROMPT_TILES + 1
M_PAD = N_TILES * TM
SAMPLE_ROW0 = N_META
GB = 8
VMEM_LIMIT = 60 * 1024 * 1024

F32 = jnp.float32
BF16 = jnp.bfloat16


def _layer_norm(x, g, b):
    mu = jnp.mean(x, axis=-1, keepdims=True)
    xc = x - mu
    var = jnp.mean(xc * xc, axis=-1, keepdims=True)
    return xc * lax.rsqrt(var + LN_EPS) * g + b


def _rms_norm(x, g):
    return x * lax.rsqrt(jnp.mean(x * x, axis=-1, keepdims=True) + RMS_EPS) * g


def _gelu_tanh(x):
    c = np.sqrt(2.0 / np.pi)
    hx = 0.5 * x
    return hx + hx * jnp.tanh(x * (np.float32(c) + np.float32(c * 0.044715) * (x * x)))


def _sigmoid(x):
    return 1.0 / (1.0 + jnp.exp2(x * np.float32(-1.4426950408889634)))


def _dot(a, b):
    return jnp.dot(a, b, preferred_element_type=F32)


def _dot_nt(a, b):
    return lax.dot_general(a, b, (((1,), (1,)), ((), ())), preferred_element_type=F32)


def _ssm_prep_kernel(logdt_ref, are_ref, aim_ref, b1_ref, b2_ref, c1_ref, c2_ref,
                     m_ref, wst_ref, cpow_ref, bpad_ref, cpad_ref, avec_ref):
    dt = jnp.exp(logdt_ref[...])
    a_re, a_im = are_ref[...], aim_ref[...]
    mag = jnp.exp(dt * a_re)
    ang = dt * a_im
    ab_re = mag * jnp.cos(ang)
    ab_im = mag * jnp.sin(ang)
    n_re = ab_re - 1.0
    den = a_re * a_re + a_im * a_im
    f_re = (n_re * a_re + ab_im * a_im) / den
    f_im = (ab_im * a_re - n_re * a_im) / den
    sgn = jnp.where(lax.broadcasted_iota(jnp.int32, (GB, SW), 1) < N_STATE, -1.0, 1.0)

    pr, pi = jnp.ones_like(ab_re), jnp.zeros_like(ab_re)
    pw = [(pr, pi)]
    for _ in range(CHUNK):
        pr, pi = pr * ab_re - pi * ab_im, pr * ab_im + pi * ab_re
        pw.append((pr, sgn * pi))
    fx = sgn * f_im
    lane = lax.broadcasted_iota(jnp.int32, (GROUP, CW), 1)

    for gl in range(GB):
        row = lambda x: x[gl:gl + 1, :]
        bb1 = row(f_re) * b1_ref[gl] + row(fx) * b2_ref[gl]
        bb2 = row(f_re) * b2_ref[gl] - row(fx) * b1_ref[gl]
        cc1 = -row(sgn) * c1_ref[gl]
        cc2 = -row(sgn) * c2_ref[gl]

        cq = []
        for s in range(CHUNK):
            wr, wx = pw[CHUNK - 1 - s]
            wst_ref[gl, s * GROUP:(s + 1) * GROUP, :] = (row(wr) * bb1 + row(wx) * bb2).astype(BF16)
            qr, qx = pw[s + 1]
            cpow_ref[gl, s * GROUP:(s + 1) * GROUP, :] = (row(qr) * cc1 + row(qx) * cc2).astype(BF16)
            dr, dx = pw[s]
            cq.append(row(dr) * cc1 + row(dx) * cc2)
        kt = lax.dot_general(bb1, jnp.concatenate(cq, axis=0), (((1,), (1,)), ((), ())),
                             precision=lax.Precision.HIGHEST, preferred_element_type=F32)
        for i in range(CHUNK):
            blk = kt if i == 0 else jnp.where(lane >= i * GROUP, pltpu.roll(kt, i * GROUP, axis=1), 0.0)
            m_ref[gl, i * GROUP:(i + 1) * GROUP, :] = blk.astype(BF16)

        zeros = jnp.zeros((SW, SW), BF16)
        bpad_ref[gl] = zeros
        cpad_ref[gl] = zeros
        bpad_ref[gl, gl * GROUP:(gl + 1) * GROUP, :] = bb1.astype(BF16)
        cpad_ref[gl, gl * GROUP:(gl + 1) * GROUP, :] = cq[0].astype(BF16)

        avec_ref[gl, 0:1, :] = row(pw[CHUNK][0])
        avec_ref[gl, 1:2, :] = row(pw[CHUNK][1])
        avec_ref[gl, 2:3, :] = row(pw[1][0])
        avec_ref[gl, 3:4, :] = row(pw[1][1])


def _ssm_prep(log_dt, a_re, a_im, b_re, b_im, c_re, c_im):
    g, c = N_GROUPS, GROUP
    dup = lambda x: jnp.concatenate([x, x], axis=-1)
    bt_re, bt_im = b_re.transpose(0, 2, 1), b_im.transpose(0, 2, 1)
    cat = lambda x, y: jnp.concatenate([x, y], axis=-1)
    vec = pl.BlockSpec((GB, SW), lambda i: (i, 0))
    blk = lambda *s: pl.BlockSpec((GB,) + s, lambda i: (i,) + (0,) * len(s))
    return pl.pallas_call(
        _ssm_prep_kernel,
        grid=(g // GB,),
        in_specs=[vec, vec, vec, blk(c, SW), blk(c, SW), blk(c, SW), blk(c, SW)],
        out_specs=[blk(CW, CW), blk(CW, SW), blk(CW, SW), blk(SW, SW), blk(SW, SW), blk(4, SW)],
        out_shape=[jax.ShapeDtypeStruct((g, CW, CW), BF16),
                   jax.ShapeDtypeStruct((g, CW, SW), BF16),
                   jax.ShapeDtypeStruct((g, CW, SW), BF16),
                   jax.ShapeDtypeStruct((g, SW, SW), BF16),
                   jax.ShapeDtypeStruct((g, SW, SW), BF16),
                   jax.ShapeDtypeStruct((g, 4, SW), F32)],
        name="ssm_prep",
    )(jnp.broadcast_to(log_dt[:, None], (g, SW)), dup(a_re), dup(a_im),
      cat(bt_re, bt_im), cat(bt_im, bt_re), cat(c_re, c_im), cat(c_im, c_re))


SUB_IN = 256


TILES_PER_SEQ = SEQ // TM


def _in_proj_kernel(xp_ref, xe_ref, g0_ref, b0_ref, w_ref, p1_ref, p2_ref, cw_ref, gc_ref, wo_ref, wgl_ref,
                    h0_ref, u_ref, rc_ref, vt_ref, vs_ref, wob_ref, wglb_ref, vbuf_scr, meta_scr):
    s = pl.program_id(0)

    @pl.when(s == 0)
    def _():
        vbuf_scr[0:ROW8, :] = jnp.zeros((ROW8, D_CONV), F32)

    def tile(x_ref, is_extra):
        for r0 in range(0, TM, SUB_IN):
            rs = slice(r0, r0 + SUB_IN)
            if is_extra and r0 > 0:
                h0_ref[rs, :] = jnp.zeros((SUB_IN, D_MODEL), F32)
                u_ref[rs, :] = jnp.zeros((SUB_IN, D_SSM), F32)
                rc_ref[rs, :] = jnp.zeros((SUB_IN, D_CONV), BF16)
                continue
            h0 = _layer_norm(x_ref[rs, :], g0_ref[...], b0_ref[...])
            h0_ref[rs, :] = h0
            hb = h0.astype(BF16)
            part = lambda k: _dot(hb, w_ref[:, k * D_SSM:(k + 1) * D_SSM])
            u_ref[rs, :] = part(0)
            b_gate = part(1)
            v = part(2) * part(3)
            vbuf_scr[ROW8 + r0:ROW8 + r0 + SUB_IN, :] = v
            prev = vbuf_scr[r0:r0 + ROW8 + SUB_IN, :]
            vm1 = pltpu.roll(prev, 1, axis=0)[ROW8:, :]
            vm2 = pltpu.roll(prev, 2, axis=0)[ROW8:, :]
            if is_extra and r0 == 0:
                sample = lax.broadcasted_iota(jnp.int32, (SUB_IN, 1), 0) >= SAMPLE_ROW0
                vm1 = jnp.where(sample, p1_ref[...], vm1)
                vm2 = jnp.where(sample, p2_ref[...], vm2)
            conv = cw_ref[0:1, :] * vm2 + cw_ref[1:2, :] * vm1 + cw_ref[2:3, :] * v
            rc_ref[rs, :] = _rms_norm(b_gate * conv, gc_ref[...]).astype(BF16)

    @pl.when(s == 0)
    def _():
        tile(xe_ref, True)
        vs_ref[...] = vbuf_scr[ROW8:ROW8 + SUB_IN, :]
        meta_scr[...] = vbuf_scr[ROW8 + N_META - ROW8:ROW8 + N_META, :]
        vbuf_scr[0:ROW8, :] = meta_scr[...]

    @pl.when(s > 0)
    def _():
        tile(xp_ref, False)
        wob_ref[...] = wo_ref[...].astype(BF16)
        wglb_ref[...] = wgl_ref[...].astype(BF16)
        tail = vbuf_scr[TM:TM + ROW8, :]
        vt_ref[0] = tail
        vbuf_scr[0:ROW8, :] = jnp.where(s % TILES_PER_SEQ == 0, meta_scr[...], tail)


def _in_proj(xp, xe, g0, b0, w_in, p1, p2, cw, gc, w_out, w_glu):
    const = lambda s: (0, 0)
    slab = lambda s: (jnp.maximum(s - 1, 0), 0)
    row = lambda s: ((s + N_PROMPT_TILES) % N_TILES, 0)
    vec = lambda n: pl.BlockSpec((1, n), const)
    return pl.pallas_call(
        _in_proj_kernel,
        grid=(N_TILES,),
        in_specs=[pl.BlockSpec((TM, D_MODEL), lambda s: (jnp.maximum(s - 1, 0), 0)),
                  pl.BlockSpec((SUB_IN, D_MODEL), const),
                  vec(D_MODEL), vec(D_MODEL),
                  pl.BlockSpec((D_MODEL, 4 * D_SSM), const, pipeline_mode=pl.Buffered(1)),
                  pl.BlockSpec((SUB_IN, D_CONV), const), pl.BlockSpec((SUB_IN, D_CONV), const),
                  pl.BlockSpec((CONV_W, D_CONV), const), vec(D_CONV),
                  pl.BlockSpec((D_MODEL // N_PROMPT_TILES, D_MODEL), slab),
                  pl.BlockSpec((D_SSM // N_PROMPT_TILES, D_SSM), slab)],
        out_specs=[pl.BlockSpec((TM, D_MODEL), row), pl.BlockSpec((TM, D_SSM), row),
                   pl.BlockSpec((TM, D_CONV), row),
                   pl.BlockSpec((1, ROW8, D_CONV), lambda s: (jnp.maximum(s - 1, 0), 0, 0)),
                   pl.BlockSpec((SUB_IN, D_CONV), const),
                   pl.BlockSpec((D_MODEL // N_PROMPT_TILES, D_MODEL), slab),
                   pl.BlockSpec((D_SSM // N_PROMPT_TILES, D_SSM), slab)],
        out_shape=[jax.ShapeDtypeStruct((M_PAD, D_MODEL), F32),
                   jax.ShapeDtypeStruct((M_PAD, D_SSM), F32),
                   jax.ShapeDtypeStruct((M_PAD, D_CONV), BF16),
                   jax.ShapeDtypeStruct((N_PROMPT_TILES, ROW8, D_CONV), F32),
                   jax.ShapeDtypeStruct((SUB_IN, D_CONV), F32),
                   jax.ShapeDtypeStruct((D_MODEL, D_MODEL), BF16),
                   jax.ShapeDtypeStruct((D_SSM, D_SSM), BF16)],
        scratch_shapes=[pltpu.VMEM((ROW8 + TM, D_CONV), F32), pltpu.VMEM((ROW8, D_CONV), F32)],
        compiler_params=pltpu.CompilerParams(
            dimension_semantics=("arbitrary",), vmem_limit_bytes=VMEM_LIMIT),
        name="in_proj",
    )(xp, xe, g0, b0, w_in, p1, p2, cw, gc, w_out, w_glu)


SUB = 16
SEQ_CHUNKS = SEQ // CHUNK
LHS_ROWS = N_BATCH * SEQ_CHUNKS + SUB
META_LHS = N_BATCH * SEQ_CHUNKS
PITCH = SEQ_CHUNKS + ROW8
REC_ROWS = N_BATCH * PITCH + ROW8
META_REC = N_BATCH * PITCH
SAMPLE_LO = N_PROMPT_ROWS + SAMPLE_ROW0
SAMPLE_HI = SAMPLE_LO + N_SAMPLE


def _block_transpose(a):
    a = list(a)
    blk = lax.broadcasted_iota(jnp.int32, a[0].shape, 1) // GROUP
    for d in (4, 2, 1):
        w = d * GROUP
        keep = (blk & d) == 0
        for j in range(GB):
            if j & d == 0:
                lo, hi = a[j], a[j + d]
                a[j] = jnp.where(keep, lo, pltpu.roll(hi, w, axis=1))
                a[j + d] = jnp.where(keep, pltpu.roll(lo, SW - w, axis=1), hi)
    return a


def _ssm_scan_kernel(u_ref, m_ref, wst_ref, cpow_ref, bpad_ref, cpad_ref, avec_ref, sre_ref, sim_ref,
                     y_ref, sfin_ref, nre_ref, nim_ref, lhs_scr, x_scr, xs_scr, sp_scr, yg_scr):
    def swap_halves(s):
        return pltpu.roll(s, N_STATE, axis=1)

    half = SW

    for h in range(2):
        src = [u_ref[pl.ds(8 * h + j, LHS_ROWS, stride=CHUNK), :] for j in range(GB)]
        for gl, w in enumerate(_block_transpose(src)):
            lhs_scr[gl, :, h * half:(h + 1) * half] = w.astype(BF16)

    def to_rec(ref, gl, x):
        for n in range(N_BATCH):
            ref[gl, n * PITCH:n * PITCH + SEQ_CHUNKS, :] = x[n * SEQ_CHUNKS:(n + 1) * SEQ_CHUNKS]
        ref[gl, META_REC:META_REC + ROW8, :] = x[META_LHS:META_LHS + ROW8]

    for gl in range(GB):
        x = _dot(lhs_scr[gl], wst_ref[gl])
        to_rec(x_scr, gl, x)
        to_rec(xs_scr, gl, swap_halves(x))

    ar16 = [avec_ref[gl, 0:1, :] for gl in range(GB)]
    ax16 = [avec_ref[gl, 1:2, :] for gl in range(GB)]

    def step(k, carry):
        new = []
        for gl in range(GB):
            s, t = carry[gl]
            rows = pl.ds(k, N_BATCH, stride=PITCH)
            sp_scr[gl, rows, :] = s
            new.append((ar16[gl] * s + ax16[gl] * t + x_scr[gl, rows, :],
                        ar16[gl] * t - ax16[gl] * s + xs_scr[gl, rows, :]))
        return tuple(new)

    first = lambda ref, gl: jnp.broadcast_to(ref[gl, META_REC:META_REC + 1, :], (N_BATCH, SW))
    init = tuple((first(x_scr, gl), first(xs_scr, gl)) for gl in range(GB))
    fin = lax.fori_loop(0, SEQ_CHUNKS, step, init)

    us = u_ref[SAMPLE_LO:SAMPLE_HI, :].astype(BF16)
    ys = jnp.zeros((N_SAMPLE, SW), F32)
    for gl in range(GB):
        sfin_ref[gl] = fin[gl][0]
        sp = jnp.concatenate([sp_scr[gl, n * PITCH:n * PITCH + SEQ_CHUNKS, :] for n in range(N_BATCH)]
                             + [jnp.zeros((SUB, SW), F32)], axis=0)
        yg_scr[gl] = _dot(lhs_scr[gl], m_ref[gl]) + _dot_nt(sp.astype(BF16), cpow_ref[gl])
        lanes = slice(gl * N_STATE, (gl + 1) * N_STATE)
        s0 = jnp.concatenate([sre_ref[:, lanes], sim_ref[:, lanes]], axis=1)
        sn = (avec_ref[gl, 2:3, :] * s0 + avec_ref[gl, 3:4, :] * swap_halves(s0)
              + _dot(us, bpad_ref[gl]))
        nre_ref[:, lanes] = sn[:, 0:N_STATE]
        nim_ref[:, lanes] = sn[:, N_STATE:SW]
        ys = ys + _dot_nt(sn.astype(BF16), cpad_ref[gl])

    for h in range(2):
        src = [yg_scr[gl, :, h * half:(h + 1) * half] for gl in range(GB)]
        for j, w in enumerate(_block_transpose(src)):
            y_ref[pl.ds(8 * h + j, META_LHS, stride=CHUNK), :] = w[0:META_LHS]
            y_ref[N_PROMPT_ROWS + 8 * h + j:N_PROMPT_ROWS + 8 * h + j + 1, :] = w[META_LHS:META_LHS + 1]
    y_ref[SAMPLE_LO:SAMPLE_HI, :] = ys
    y_ref[SAMPLE_HI:M_PAD, :] = jnp.zeros((M_PAD - SAMPLE_HI, SW), F32)


def _ssm_scan(u, m, wst, cpow, bpad, cpad, avec, s_re, s_im):
    blk = lambda *s: pl.BlockSpec((GB,) + s, lambda i: (i,) + (0,) * len(s))
    col = pl.BlockSpec((M_PAD, SW), lambda i: (0, i))
    st = pl.BlockSpec((N_SAMPLE, GB * N_STATE), lambda i: (0, i))
    st_shape = jax.ShapeDtypeStruct((N_SAMPLE, N_GROUPS * N_STATE), F32)
    return pl.pallas_call(
        _ssm_scan_kernel,
        grid=(N_GROUPS // GB,),
        in_specs=[col, blk(CW, CW), blk(CW, SW), blk(CW, SW), blk(SW, SW), blk(SW, SW), blk(4, SW),
                  st, st],
        out_specs=[col, blk(N_BATCH, SW), st, st],
        out_shape=[jax.ShapeDtypeStruct((M_PAD, D_SSM), F32),
                   jax.ShapeDtypeStruct((N_GROUPS, N_BATCH, SW), F32), st_shape, st_shape],
        scratch_shapes=[pltpu.VMEM((GB, LHS_ROWS, CW), BF16), pltpu.VMEM((GB, REC_ROWS, SW), F32),
                        pltpu.VMEM((GB, REC_ROWS, SW), F32), pltpu.VMEM((GB, REC_ROWS, SW), F32),
                        pltpu.VMEM((GB, LHS_ROWS, CW), F32)],
        compiler_params=pltpu.CompilerParams(
            dimension_semantics=("arbitrary",), vmem_limit_bytes=VMEM_LIMIT),
        name="ssm_scan",
    )(u, m, wst, cpow, bpad, cpad, avec, s_re, s_im)


SUB_MIX = 256


def _mix_out_kernel(y_ref, u_ref, rc_ref, h0_ref, dskip_ref, wglu_ref, gs_ref, wout_ref,
                    g1_ref, b1_ref, x1_ref, x1b_ref, merged_scr):
    def sub_tile(r0):
        rs = slice(r0, r0 + SUB_MIX)
        ys = _gelu_tanh(y_ref[rs, :] + dskip_ref[...] * u_ref[rs, :])
        gate = _dot(ys.astype(BF16), wglu_ref[...])
        merged_scr[rs, 0:D_SSM] = _rms_norm(ys * _sigmoid(gate), gs_ref[...]).astype(BF16)
        merged_scr[rs, D_SSM:D_MODEL] = rc_ref[rs, :]
        mo = _dot(merged_scr[rs, :], wout_ref[...])
        x1 = _layer_norm(ALPHA * h0_ref[rs, :] + mo, g1_ref[...], b1_ref[...])
        x1_ref[rs, :] = x1
        x1b_ref[rs, :] = x1.astype(BF16)

    @pl.when(pl.program_id(0) < N_PROMPT_TILES)
    def _():
        for r0 in range(0, TM, SUB_MIX):
            sub_tile(r0)

    @pl.when(pl.program_id(0) == N_PROMPT_TILES)
    def _():
        sub_tile(0)
        x1_ref[SUB_MIX:TM, :] = jnp.zeros((TM - SUB_MIX, D_MODEL), F32)
        x1b_ref[SUB_MIX:TM, :] = jnp.zeros((TM - SUB_MIX, D_MODEL), BF16)


def _mix_out(y, u, rc, h0, dskip, wglu, gs, wout, g1, b1):
    row = lambda i: (i, 0)
    const = lambda i: (0, 0)
    vec = lambda n: pl.BlockSpec((1, n), const)
    return pl.pallas_call(
        _mix_out_kernel,
        grid=(N_TILES,),
        in_specs=[pl.BlockSpec((TM, D_SSM), row), pl.BlockSpec((TM, D_SSM), row),
                  pl.BlockSpec((TM, D_CONV), row), pl.BlockSpec((TM, D_MODEL), row), vec(D_SSM),
                  pl.BlockSpec((D_SSM, D_SSM), const, pipeline_mode=pl.Buffered(1)),
                  vec(D_SSM),
                  pl.BlockSpec((D_MODEL, D_MODEL), const, pipeline_mode=pl.Buffered(1)),
                  vec(D_MODEL), vec(D_MODEL)],
        out_specs=[pl.BlockSpec((TM, D_MODEL), row), pl.BlockSpec((TM, D_MODEL), row)],
        out_shape=[jax.ShapeDtypeStruct((M_PAD, D_MODEL), F32),
                   jax.ShapeDtypeStruct((M_PAD, D_MODEL), BF16)],
        scratch_shapes=[pltpu.VMEM((TM, D_MODEL), BF16)],
        compiler_params=pltpu.CompilerParams(
            dimension_semantics=("arbitrary",), vmem_limit_bytes=VMEM_LIMIT),
        name="mix_out",
    )(y, u, rc, h0, dskip, wglu, gs, wout, g1, b1)


M_HALF = M_PAD // 2
SUB_UP = M_HALF // 4
TF_UP = 256
SUB_DOWN = 256
EXTRA_ROWS = 256
M_REAL_HALF = M_HALF - (TM - EXTRA_ROWS)


def _ffn_up_kernel(xb_ref, wg_ref, wu_ref, wd_ref, act_ref, wdb_ref):
    @pl.when(pl.program_id(0) == 0)
    def _():
        wdb_ref[...] = wd_ref[...].astype(BF16)

    wgb = wg_ref[...].astype(BF16)
    wub = wu_ref[...].astype(BF16)

    def rows(lo, hi):
        xb = xb_ref[lo:hi, :]
        gate = _dot(xb, wgb)
        up = _dot(xb, wub)
        act_ref[lo:hi, :] = (gate * _sigmoid(gate) * up).astype(BF16)

    @pl.when(pl.program_id(0) == 0)
    def _():
        for r0 in range(0, M_HALF, SUB_UP):
            rows(r0, r0 + SUB_UP)

    @pl.when(pl.program_id(0) == 1)
    def _():
        for r0 in range(0, M_REAL_HALF, SUB_UP):
            rows(r0, min(r0 + SUB_UP, M_REAL_HALF))
        act_ref[M_REAL_HALF:M_HALF, :] = jnp.zeros((M_HALF - M_REAL_HALF, TF_UP), BF16)


def _ffn_up(x1b, wg, wu, wd):
    wcol = pl.BlockSpec((D_MODEL, TF_UP), lambda h, f: (0, f))
    wrow = pl.BlockSpec((TF_UP, D_MODEL), lambda h, f: (jnp.where(h == 0, f, D_FF // TF_UP - 1), 0))
    return pl.pallas_call(
        _ffn_up_kernel,
        grid=(M_PAD // M_HALF, D_FF // TF_UP),
        in_specs=[pl.BlockSpec((M_HALF, D_MODEL), lambda h, f: (h, 0), pipeline_mode=pl.Buffered(1)),
                  wcol, wcol, wrow],
        out_specs=[pl.BlockSpec((M_HALF, TF_UP), lambda h, f: (h, f)), wrow],
        out_shape=[jax.ShapeDtypeStruct((M_PAD, D_FF), BF16),
                   jax.ShapeDtypeStruct((D_FF, D_MODEL), BF16)],
        compiler_params=pltpu.CompilerParams(
            dimension_semantics=("arbitrary", "arbitrary"), vmem_limit_bytes=VMEM_LIMIT),
        name="ffn_up",
    )(x1b, wg, wu, wd)


def _ffn_down_kernel(act_ref, x_ref, wd_ref, g2_ref, b2_ref, yp_ref, ye_ref):
    i = pl.program_id(0)

    def tile(o_ref, rows):
        for r0 in range(0, rows, SUB_DOWN):
            rs = slice(r0, r0 + SUB_DOWN)
            z = ALPHA * x_ref[rs, :] + _dot(act_ref[rs, :], wd_ref[...])
            o_ref[rs, :] = _layer_norm(z, g2_ref[...], b2_ref[...])

    @pl.when(i < N_PROMPT_TILES)
    def _():
        tile(yp_ref, TM)

    @pl.when(i == N_PROMPT_TILES)
    def _():
        tile(ye_ref, EXTRA_ROWS)


def _ffn_down(act, x1, wd, g2, b2):
    row = lambda i: (i, 0)
    const = lambda i: (0, 0)
    return pl.pallas_call(
        _ffn_down_kernel,
        grid=(N_TILES,),
        in_specs=[pl.BlockSpec((TM, D_FF), row), pl.BlockSpec((TM, D_MODEL), row),
                  pl.BlockSpec((D_FF, D_MODEL), const, pipeline_mode=pl.Buffered(1)),
                  pl.BlockSpec((1, D_MODEL), const), pl.BlockSpec((1, D_MODEL), const)],
        out_specs=[pl.BlockSpec((TM, D_MODEL), lambda i: (jnp.minimum(i, N_PROMPT_TILES - 1), 0)),
                   pl.BlockSpec((EXTRA_ROWS, D_MODEL), const)],
        out_shape=[jax.ShapeDtypeStruct((N_PROMPT_ROWS, D_MODEL), F32),
                   jax.ShapeDtypeStruct((EXTRA_ROWS, D_MODEL), F32)],
        compiler_params=pltpu.CompilerParams(
            dimension_semantics=("arbitrary",), vmem_limit_bytes=VMEM_LIMIT),
        name="ffn_down",
    )(act, x1, wd, g2, b2)


def _pad_rows(a, n):
    return jnp.pad(a, ((0, n - a.shape[0]), (0, 0)))


def kernel(x_prompt, x_sample, state_ssm_re, state_ssm_im, state_conv, meta_tokens, ln0_g, ln0_b,
           w_in, ssm_a_re, ssm_a_im, ssm_log_dt, ssm_b_re, ssm_b_im, ssm_c_re, ssm_c_im, ssm_d,
           ssm_w_glu, conv_w, g_ssm_out, g_conv_out, w_out, ln1_g, ln1_b, w_gate, w_up, w_down,
           ln2_g, ln2_b):
    assert x_prompt.shape == (N_BATCH, SEQ, D_MODEL) and x_sample.shape == (N_SAMPLE, 1, D_MODEL)
    assert w_in.shape[0] == 1, "single layer"
    g, p, c = N_GROUPS, N_STATE, GROUP
    vec = lambda a: a.reshape(1, -1)

    xp = x_prompt.reshape(N_PROMPT_ROWS, D_MODEL)
    xe = _pad_rows(jnp.concatenate([meta_tokens, x_sample[:, 0, :]], axis=0), SUB_IN)
    g0, b0 = vec(ln0_g), vec(ln0_b)

    m, wst, cpow, bpad, cpad, avec = _ssm_prep(ssm_log_dt[0], ssm_a_re[0], ssm_a_im[0], ssm_b_re[0],
                                               ssm_b_im[0], ssm_c_re[0], ssm_c_im[0])

    prev = jnp.pad(state_conv[0], ((SAMPLE_ROW0, SUB_IN - SAMPLE_ROW0 - N_SAMPLE), (0, 0), (0, 0)))
    h0, u, rc, v_tail, v_extra, wout_bf, wglu_bf = _in_proj(
        xp, xe, g0, b0, w_in[0].astype(BF16), prev[:, 1], prev[:, 0], conv_w[0], vec(g_conv_out[0]),
        w_out[0], ssm_w_glu[0])

    flat = lambda a: a.reshape(N_SAMPLE, g * p)
    y_all, s_fin, ns_re, ns_im = _ssm_scan(u, m, wst, cpow, bpad, cpad, avec,
                                           flat(state_ssm_re[0]), flat(state_ssm_im[0]))

    x1, x1b = _mix_out(y_all, u, rc, h0, vec(ssm_d[0]), wglu_bf, vec(g_ssm_out[0]),
                       wout_bf, vec(ln1_g[0]), vec(ln1_b[0]))

    act, wd_bf = _ffn_up(x1b, w_gate[0], w_up[0], w_down[0])
    yp, ye = _ffn_down(act, x1, wd_bf, vec(ln2_g[0]), vec(ln2_b[0]))

    y_prompt = yp.reshape(N_BATCH, SEQ, D_MODEL)
    y_sample = ye[SAMPLE_ROW0:SAMPLE_ROW0 + N_SAMPLE].reshape(N_SAMPLE, 1, D_MODEL)

    sd = state_ssm_re.dtype
    s_p = s_fin.transpose(1, 0, 2)
    unflat = lambda a: a.reshape(1, N_SAMPLE, g, p).astype(sd)
    cd = state_conv.dtype
    conv_p = v_tail[TILES_PER_SEQ - 1::TILES_PER_SEQ, ROW8 - (CONV_W - 1):].astype(cd)
    conv_s = jnp.stack([state_conv[0][:, 1], v_extra[SAMPLE_ROW0:SAMPLE_ROW0 + N_SAMPLE].astype(cd)], axis=1)
    return (y_prompt, y_sample,
            s_p[None, ..., :p].astype(sd), s_p[None, ..., p:].astype(sd), conv_p[None],
            unflat(ns_re), unflat(ns_im), conv_s[None])
```

```python
import functools

import jax
import jax.numpy as jnp
import numpy as np
from jax import lax
from jax.experimental import pallas as pl
from jax.experimental.pallas import tpu as pltpu

D_MODEL = 2048
D_SSM = 1024
D_CONV = 1024
N_GROUPS = 64
GROUP = 16
N_STATE = 64
N_META = 16
CONV_W = 3
D_FF = 5632
LN_EPS = 1e-5
RMS_EPS = 1e-6
ALPHA = 2.0 ** 0.25

N_BATCH = 4
SEQ = 2048
N_SAMPLE = 128
CHUNK = 16
N_CHUNKS = SEQ // CHUNK + 1
ROW8 = 8
CW = CHUNK * GROUP
SW = 2 * N_STATE

TM = 512
N_PROMPT_ROWS = N_BATCH * SEQ
N_PROMPT_TILES = N_PROMPT_ROWS // TM
N_TILES = N_PROMPT_TILES + 1
M_PAD = N_TILES * TM
SAMPLE_ROW0 = N_META
GB = 8
VMEM_LIMIT = 60 * 1024 * 1024

F32 = jnp.float32
BF16 = jnp.bfloat16


def _layer_norm(x, g, b):
    mu = jnp.mean(x, axis=-1, keepdims=True)
    xc = x - mu
    var = jnp.mean(xc * xc, axis=-1, keepdims=True)
    return xc * lax.rsqrt(var + LN_EPS) * g + b


def _rms_norm(x, g):
    return x * lax.rsqrt(jnp.mean(x * x, axis=-1, keepdims=True) + RMS_EPS) * g


def _gelu_tanh(x):
    c = np.sqrt(2.0 / np.pi)
    hx = 0.5 * x
    return hx + hx * jnp.tanh(x * (np.float32(c) + np.float32(c * 0.044715) * (x * x)))


def _sigmoid(x):
    return 1.0 / (1.0 + jnp.exp2(x * np.float32(-1.4426950408889634)))


def _dot(a, b):
    return jnp.dot(a, b, preferred_element_type=F32)


def _dot_nt(a, b):
    return lax.dot_general(a, b, (((1,), (1,)), ((), ())), preferred_element_type=F32)


def _ssm_prep_kernel(logdt_ref, are_ref, aim_ref, b1_ref, b2_ref, c1_ref, c2_ref,
                     m_ref, wst_ref, cpow_ref, bpad_ref, cpad_ref, avec_ref):
    dt = jnp.exp(logdt_ref[...])
    a_re, a_im = are_ref[...], aim_ref[...]
    mag = jnp.exp(dt * a_re)
    ang = dt * a_im
    ab_re = mag * jnp.cos(ang)
    ab_im = mag * jnp.sin(ang)
    n_re = ab_re - 1.0
    den = a_re * a_re + a_im * a_im
    f_re = (n_re * a_re + ab_im * a_im) / den
    f_im = (ab_im * a_re - n_re * a_im) / den
    sgn = jnp.where(lax.broadcasted_iota(jnp.int32, (GB, SW), 1) < N_STATE, -1.0, 1.0)

    pr, pi = jnp.ones_like(ab_re), jnp.zeros_like(ab_re)
    pw = [(pr, pi)]
    for _ in range(CHUNK):
        pr, pi = pr * ab_re - pi * ab_im, pr * ab_im + pi * ab_re
        pw.append((pr, sgn * pi))
    fx = sgn * f_im
    lane = lax.broadcasted_iota(jnp.int32, (GROUP, CW), 1)

    for gl in range(GB):
        row = lambda x: x[gl:gl + 1, :]
        bb1 = row(f_re) * b1_ref[gl] + row(fx) * b2_ref[gl]
        bb2 = row(f_re) * b2_ref[gl] - row(fx) * b1_ref[gl]
        cc1 = -row(sgn) * c1_ref[gl]
        cc2 = -row(sgn) * c2_ref[gl]

        cq = []
        for s in range(CHUNK):
            wr, wx = pw[CHUNK - 1 - s]
            wst_ref[gl, s * GROUP:(s + 1) * GROUP, :] = (row(wr) * bb1 + row(wx) * bb2).astype(BF16)
            qr, qx = pw[s + 1]
            cpow_ref[gl, s * GROUP:(s + 1) * GROUP, :] = (row(qr) * cc1 + row(qx) * cc2).astype(BF16)
            dr, dx = pw[s]
            cq.append(row(dr) * cc1 + row(dx) * cc2)
        kt = lax.dot_general(bb1, jnp.concatenate(cq, axis=0), (((1,), (1,)), ((), ())),
                             precision=lax.Precision.HIGHEST, preferred_element_type=F32)
        for i in range(CHUNK):
            blk = kt if i == 0 else jnp.where(lane >= i * GROUP, pltpu.roll(kt, i * GROUP, axis=1), 0.0)
            m_ref[gl, i * GROUP:(i + 1) * GROUP, :] = blk.astype(BF16)

        zeros = jnp.zeros((SW, SW), BF16)
        bpad_ref[gl] = zeros
        cpad_ref[gl] = zeros
        bpad_ref[gl, gl * GROUP:(gl + 1) * GROUP, :] = bb1.astype(BF16)
        cpad_ref[gl, gl * GROUP:(gl + 1) * GROUP, :] = cq[0].astype(BF16)

        avec_ref[gl, 0:1, :] = row(pw[CHUNK][0])
        avec_ref[gl, 1:2, :] = row(pw[CHUNK][1])
        avec_ref[gl, 2:3, :] = row(pw[1][0])
        avec_ref[gl, 3:4, :] = row(pw[1][1])


def _ssm_prep(log_dt, a_re, a_im, b_re, b_im, c_re, c_im):
    g, c = N_GROUPS, GROUP
    dup = lambda x: jnp.concatenate([x, x], axis=-1)
    bt_re, bt_im = b_re.transpose(0, 2, 1), b_im.transpose(0, 2, 1)
    cat = lambda x, y: jnp.concatenate([x, y], axis=-1)
    vec = pl.BlockSpec((GB, SW), lambda i: (i, 0))
    blk = lambda *s: pl.BlockSpec((GB,) + s, lambda i: (i,) + (0,) * len(s))
    return pl.pallas_call(
        _ssm_prep_kernel,
        grid=(g // GB,),
        in_specs=[vec, vec, vec, blk(c, SW), blk(c, SW), blk(c, SW), blk(c, SW)],
        out_specs=[blk(CW, CW), blk(CW, SW), blk(CW, SW), blk(SW, SW), blk(SW, SW), blk(4, SW)],
        out_shape=[jax.ShapeDtypeStruct((g, CW, CW), BF16),
                   jax.ShapeDtypeStruct((g, CW, SW), BF16),
                   jax.ShapeDtypeStruct((g, CW, SW), BF16),
                   jax.ShapeDtypeStruct((g, SW, SW), BF16),
                   jax.ShapeDtypeStruct((g, SW, SW), BF16),
                   jax.ShapeDtypeStruct((g, 4, SW), F32)],
        name="ssm_prep",
    )(jnp.broadcast_to(log_dt[:, None], (g, SW)), dup(a_re), dup(a_im),
      cat(bt_re, bt_im), cat(bt_im, bt_re), cat(c_re, c_im), cat(c_im, c_re))


SUB_IN = 256

TILES_PER_SEQ = SEQ // TM


def _in_proj_kernel(xp_ref, xe_ref, g0_ref, b0_ref, w_ref, p1_ref, p2_ref, cw_ref, gc_ref, wo_ref, wgl_ref,
                    h0_ref, u_ref, rc_ref, vt_ref, vs_ref, wob_ref, wglb_ref, vbuf_scr, meta_scr):
    s = pl.program_id(0)

    @pl.when(s == 0)
    def _():
        vbuf_scr[0:ROW8, :] = jnp.zeros((ROW8, D_CONV), F32)

    def tile(x_ref, is_extra):
        sub = SUB_IN if is_extra else TM
        rs = slice(0, sub)
        if is_extra:
            pad = slice(sub, TM)
            h0_ref[pad, :] = jnp.zeros((TM - sub, D_MODEL), F32)
            u_ref[pad, :] = jnp.zeros((TM - sub, D_SSM), F32)
            rc_ref[pad, :] = jnp.zeros((TM - sub, D_CONV), BF16)
        h0 = _layer_norm(x_ref[rs, :], g0_ref[...], b0_ref[...])
        h0_ref[rs, :] = h0
        proj = _dot(h0.astype(BF16), w_ref[...])
        part = lambda k: proj[:, k * D_SSM:(k + 1) * D_SSM]
        u_ref[rs, :] = part(0)
        b_gate = part(1)
        v = part(2) * part(3)
        vbuf_scr[ROW8:ROW8 + sub, :] = v
        prev = vbuf_scr[0:ROW8 + sub, :]
        vm1 = pltpu.roll(prev, 1, axis=0)[ROW8:, :]
        vm2 = pltpu.roll(prev, 2, axis=0)[ROW8:, :]
        if is_extra:
            sample = lax.broadcasted_iota(jnp.int32, (sub, 1), 0) >= SAMPLE_ROW0
            vm1 = jnp.where(sample, p1_ref[...], vm1)
            vm2 = jnp.where(sample, p2_ref[...], vm2)
        conv = cw_ref[0:1, :] * vm2 + cw_ref[1:2, :] * vm1 + cw_ref[2:3, :] * v
        rc_ref[rs, :] = _rms_norm(b_gate * conv, gc_ref[...]).astype(BF16)

    @pl.when(s == 0)
    def _():
        tile(xe_ref, True)
        vs_ref[...] = vbuf_scr[ROW8:ROW8 + SUB_IN, :]
        meta_scr[...] = vbuf_scr[ROW8 + N_META - ROW8:ROW8 + N_META, :]
        vbuf_scr[0:ROW8, :] = meta_scr[...]

    @pl.when(s > 0)
    def _():
        tile(xp_ref, False)
        wob_ref[...] = wo_ref[...].astype(BF16)
        wglb_ref[...] = wgl_ref[...].astype(BF16)
        tail = vbuf_scr[TM:TM + ROW8, :]
        vt_ref[0] = tail
        vbuf_scr[0:ROW8, :] = jnp.where(s % TILES_PER_SEQ == 0, meta_scr[...], tail)


def _in_proj(xp, xe, g0, b0, w_in, p1, p2, cw, gc, w_out, w_glu):
    const = lambda s: (0, 0)
    slab = lambda s: (jnp.maximum(s - 1, 0), 0)
    row = lambda s: ((s + N_PROMPT_TILES) % N_TILES, 0)
    vec = lambda n: pl.BlockSpec((1, n), const)
    return pl.pallas_call(
        _in_proj_kernel,
        grid=(N_TILES,),
        in_specs=[pl.BlockSpec((TM, D_MODEL), lambda s: (jnp.maximum(s - 1, 0), 0)),
                  pl.BlockSpec((SUB_IN, D_MODEL), const),
                  vec(D_MODEL), vec(D_MODEL),
                  pl.BlockSpec((D_MODEL, 4 * D_SSM), const, pipeline_mode=pl.Buffered(1)),
                  pl.BlockSpec((SUB_IN, D_CONV), const), pl.BlockSpec((SUB_IN, D_CONV), const),
                  pl.BlockSpec((CONV_W, D_CONV), const), vec(D_CONV),
                  pl.BlockSpec((D_MODEL // N_PROMPT_TILES, D_MODEL), slab),
                  pl.BlockSpec((D_SSM // N_PROMPT_TILES, D_SSM), slab)],
        out_specs=[pl.BlockSpec((TM, D_MODEL), row), pl.BlockSpec((TM, D_SSM), row),
                   pl.BlockSpec((TM, D_CONV), row),
                   pl.BlockSpec((1, ROW8, D_CONV), lambda s: (jnp.maximum(s - 1, 0), 0, 0)),
                   pl.BlockSpec((SUB_IN, D_CONV), const),
                   pl.BlockSpec((D_MODEL // N_PROMPT_TILES, D_MODEL), slab),
                   pl.BlockSpec((D_SSM // N_PROMPT_TILES, D_SSM), slab)],
        out_shape=[jax.ShapeDtypeStruct((M_PAD, D_MODEL), F32),
                   jax.ShapeDtypeStruct((M_PAD, D_SSM), F32),
                   jax.ShapeDtypeStruct((M_PAD, D_CONV), BF16),
                   jax.ShapeDtypeStruct((N_PROMPT_TILES, ROW8, D_CONV), F32),
                   jax.ShapeDtypeStruct((SUB_IN, D_CONV), F32),
                   jax.ShapeDtypeStruct((D_MODEL, D_MODEL), BF16),
                   jax.ShapeDtypeStruct((D_SSM, D_SSM), BF16)],
        scratch_shapes=[pltpu.VMEM((ROW8 + TM, D_CONV), F32), pltpu.VMEM((ROW8, D_CONV), F32)],
        compiler_params=pltpu.CompilerParams(
            dimension_semantics=("arbitrary",), vmem_limit_bytes=VMEM_LIMIT),
        name="in_proj",
    )(xp, xe, g0, b0, w_in, p1, p2, cw, gc, w_out, w_glu)


SUB = 16
SEQ_CHUNKS = SEQ // CHUNK
LHS_ROWS = N_BATCH * SEQ_CHUNKS + SUB
META_LHS = N_BATCH * SEQ_CHUNKS
PITCH = SEQ_CHUNKS + ROW8
REC_ROWS = N_BATCH * PITCH + ROW8
META_REC = N_BATCH * PITCH
SAMPLE_LO = N_PROMPT_ROWS + SAMPLE_ROW0
SAMPLE_HI = SAMPLE_LO + N_SAMPLE


def _block_transpose(a):
    a = list(a)
    blk = lax.broadcasted_iota(jnp.int32, a[0].shape, 1) // GROUP
    for d in (4, 2, 1):
        w = d * GROUP
        keep = (blk & d) == 0
        for j in range(GB):
            if j & d == 0:
                lo, hi = a[j], a[j + d]
                a[j] = jnp.where(keep, lo, pltpu.roll(hi, w, axis=1))
                a[j + d] = jnp.where(keep, pltpu.roll(lo, SW - w, axis=1), hi)
    return a


def _ssm_scan_kernel(u_ref, m_ref, wst_ref, cpow_ref, bpad_ref, cpad_ref, avec_ref, sre_ref, sim_ref,
                     y_ref, sfin_ref, nre_ref, nim_ref, lhs_scr, x_scr, xs_scr, sp_scr, yg_scr):
    def swap_halves(s):
        return pltpu.roll(s, N_STATE, axis=1)

    half = SW

    for h in range(2):
        src = [u_ref[pl.ds(8 * h + j, LHS_ROWS, stride=CHUNK), :] for j in range(GB)]
        for gl, w in enumerate(_block_transpose(src)):
            lhs_scr[gl, :, h * half:(h + 1) * half] = w.astype(BF16)

    def to_rec(ref, gl, x):
        for n in range(N_BATCH):
            ref[gl, n * PITCH:n * PITCH + SEQ_CHUNKS, :] = x[n * SEQ_CHUNKS:(n + 1) * SEQ_CHUNKS]
        ref[gl, META_REC:META_REC + ROW8, :] = x[META_LHS:META_LHS + ROW8]

    for gl in range(GB):
        x = _dot(lhs_scr[gl], wst_ref[gl])
        to_rec(x_scr, gl, x)
        to_rec(xs_scr, gl, swap_halves(x))

    ar16 = [avec_ref[gl, 0:1, :] for gl in range(GB)]
    ax16 = [avec_ref[gl, 1:2, :] for gl in range(GB)]

    def step(k, carry):
        new = []
        for gl in range(GB):
            s, t = carry[gl]
            rows = pl.ds(k, N_BATCH, stride=PITCH)
            sp_scr[gl, rows, :] = s
            new.append((ar16[gl] * s + ax16[gl] * t + x_scr[gl, rows, :],
                        ar16[gl] * t - ax16[gl] * s + xs_scr[gl, rows, :]))
        return tuple(new)

    first = lambda ref, gl: jnp.broadcast_to(ref[gl, META_REC:META_REC + 1, :], (N_BATCH, SW))
    init = tuple((first(x_scr, gl), first(xs_scr, gl)) for gl in range(GB))
    fin = lax.fori_loop(0, SEQ_CHUNKS, step, init)

    us = u_ref[SAMPLE_LO:SAMPLE_HI, :].astype(BF16)
    ys = jnp.zeros((N_SAMPLE, SW), F32)
    for gl in range(GB):
        sfin_ref[gl] = fin[gl][0]
        sp = jnp.concatenate([sp_scr[gl, n * PITCH:n * PITCH + SEQ_CHUNKS, :] for n in range(N_BATCH)]
                             + [jnp.zeros((SUB, SW), F32)], axis=0)
        yg_scr[gl] = _dot(lhs_scr[gl], m_ref[gl]) + _dot_nt(sp.astype(BF16), cpow_ref[gl])
        lanes = slice(gl * N_STATE, (gl + 1) * N_STATE)
        s0 = jnp.concatenate([sre_ref[:, lanes], sim_ref[:, lanes]], axis=1)
        sn = (avec_ref[gl, 2:3, :] * s0 + avec_ref[gl, 3:4, :] * swap_halves(s0)
              + _dot(us, bpad_ref[gl]))
        nre_ref[:, lanes] = sn[:, 0:N_STATE]
        nim_ref[:, lanes] = sn[:, N_STATE:SW]
        ys = ys + _dot_nt(sn.astype(BF16), cpad_ref[gl])

    for h in range(2):
        src = [yg_scr[gl, :, h * half:(h + 1) * half] for gl in range(GB)]
        for j, w in enumerate(_block_transpose(src)):
            y_ref[pl.ds(8 * h + j, META_LHS, stride=CHUNK), :] = w[0:META_LHS]
            y_ref[N_PROMPT_ROWS + 8 * h + j:N_PROMPT_ROWS + 8 * h + j + 1, :] = w[META_LHS:META_LHS + 1]
    y_ref[SAMPLE_LO:SAMPLE_HI, :] = ys
    y_ref[SAMPLE_HI:M_PAD, :] = jnp.zeros((M_PAD - SAMPLE_HI, SW), F32)


def _ssm_scan(u, m, wst, cpow, bpad, cpad, avec, s_re, s_im):
    blk = lambda *s: pl.BlockSpec((GB,) + s, lambda i: (i,) + (0,) * len(s))
    col = pl.BlockSpec((M_PAD, SW), lambda i: (0, i))
    st = pl.BlockSpec((N_SAMPLE, GB * N_STATE), lambda i: (0, i))
    st_shape = jax.ShapeDtypeStruct((N_SAMPLE, N_GROUPS * N_STATE), F32)
    return pl.pallas_call(
        _ssm_scan_kernel,
        grid=(N_GROUPS // GB,),
        in_specs=[col, blk(CW, CW), blk(CW, SW), blk(CW, SW), blk(SW, SW), blk(SW, SW), blk(4, SW),
                  st, st],
        out_specs=[col, blk(N_BATCH, SW), st, st],
        out_shape=[jax.ShapeDtypeStruct((M_PAD, D_SSM), F32),
                   jax.ShapeDtypeStruct((N_GROUPS, N_BATCH, SW), F32), st_shape, st_shape],
        scratch_shapes=[pltpu.VMEM((GB, LHS_ROWS, CW), BF16), pltpu.VMEM((GB, REC_ROWS, SW), F32),
                        pltpu.VMEM((GB, REC_ROWS, SW), F32), pltpu.VMEM((GB, REC_ROWS, SW), F32),
                        pltpu.VMEM((GB, LHS_ROWS, CW), F32)],
        compiler_params=pltpu.CompilerParams(
            dimension_semantics=("arbitrary",), vmem_limit_bytes=VMEM_LIMIT),
        name="ssm_scan",
    )(u, m, wst, cpow, bpad, cpad, avec, s_re, s_im)


SUB_MIX = 256


def _mix_out_kernel(y_ref, u_ref, rc_ref, h0_ref, dskip_ref, wglu_ref, gs_ref, wout_ref,
                    g1_ref, b1_ref, x1_ref, x1b_ref, merged_scr):
    def sub_tile(r0):
        rs = slice(r0, r0 + SUB_MIX)
        ys = _gelu_tanh(y_ref[rs, :] + dskip_ref[...] * u_ref[rs, :])
        gate = _dot(ys.astype(BF16), wglu_ref[...])
        merged_scr[rs, 0:D_SSM] = _rms_norm(ys * _sigmoid(gate), gs_ref[...]).astype(BF16)
        merged_scr[rs, D_SSM:D_MODEL] = rc_ref[rs, :]
        mo = _dot(merged_scr[rs, :], wout_ref[...])
        x1 = _layer_norm(ALPHA * h0_ref[rs, :] + mo, g1_ref[...], b1_ref[...])
        x1_ref[rs, :] = x1
        x1b_ref[rs, :] = x1.astype(BF16)

    @pl.when(pl.program_id(0) < N_PROMPT_TILES)
    def _():
        for r0 in range(0, TM, SUB_MIX):
            sub_tile(r0)

    @pl.when(pl.program_id(0) == N_PROMPT_TILES)
    def _():
        sub_tile(0)
        x1_ref[SUB_MIX:TM, :] = jnp.zeros((TM - SUB_MIX, D_MODEL), F32)
        x1b_ref[SUB_MIX:TM, :] = jnp.zeros((TM - SUB_MIX, D_MODEL), BF16)


def _mix_out(y, u, rc, h0, dskip, wglu, gs, wout, g1, b1):
    row = lambda i: (i, 0)
    const = lambda i: (0, 0)
    vec = lambda n: pl.BlockSpec((1, n), const)
    return pl.pallas_call(
        _mix_out_kernel,
        grid=(N_TILES,),
        in_specs=[pl.BlockSpec((TM, D_SSM), row), pl.BlockSpec((TM, D_SSM), row),
                  pl.BlockSpec((TM, D_CONV), row), pl.BlockSpec((TM, D_MODEL), row), vec(D_SSM),
                  pl.BlockSpec((D_SSM, D_SSM), const, pipeline_mode=pl.Buffered(1)),
                  vec(D_SSM),
                  pl.BlockSpec((D_MODEL, D_MODEL), const, pipeline_mode=pl.Buffered(1)),
                  vec(D_MODEL), vec(D_MODEL)],
        out_specs=[pl.BlockSpec((TM, D_MODEL), row), pl.BlockSpec((TM, D_MODEL), row)],
        out_shape=[jax.ShapeDtypeStruct((M_PAD, D_MODEL), F32),
                   jax.ShapeDtypeStruct((M_PAD, D_MODEL), BF16)],
        scratch_shapes=[pltpu.VMEM((TM, D_MODEL), BF16)],
        compiler_params=pltpu.CompilerParams(
            dimension_semantics=("arbitrary",), vmem_limit_bytes=VMEM_LIMIT),
        name="mix_out",
    )(y, u, rc, h0, dskip, wglu, gs, wout, g1, b1)


M_HALF = M_PAD // 2
SUB_UP = M_HALF // 4
TF_UP = 256
SUB_DOWN = 256
EXTRA_ROWS = 256
M_REAL_HALF = M_HALF - (TM - EXTRA_ROWS)


def _ffn_up_kernel(xb_ref, wg_ref, wu_ref, wd_ref, act_ref, wdb_ref):
    @pl.when(pl.program_id(0) == 0)
    def _():
        wdb_ref[...] = wd_ref[...].astype(BF16)

    wgb = wg_ref[...].astype(BF16)
    wub = wu_ref[...].astype(BF16)

    def rows(lo, hi):
        xb = xb_ref[lo:hi, :]
        gate = _dot(xb, wgb)
        up = _dot(xb, wub)
        act_ref[lo:hi, :] = (gate * _sigmoid(gate) * up).astype(BF16)

    @pl.when(pl.program_id(0) == 0)
    def _():
        for r0 in range(0, M_HALF, SUB_UP):
            rows(r0, r0 + SUB_UP)

    @pl.when(pl.program_id(0) == 1)
    def _():
        for r0 in range(0, M_REAL_HALF, SUB_UP):
            rows(r0, min(r0 + SUB_UP, M_REAL_HALF))
        act_ref[M_REAL_HALF:M_HALF, :] = jnp.zeros((M_HALF - M_REAL_HALF, TF_UP), BF16)


def _ffn_up(x1b, wg, wu, wd):
    wcol = pl.BlockSpec((D_MODEL, TF_UP), lambda h, f: (0, f))
    wrow = pl.BlockSpec((TF_UP, D_MODEL), lambda h, f: (jnp.where(h == 0, f, D_FF // TF_UP - 1), 0))
    return pl.pallas_call(
        _ffn_up_kernel,
        grid=(M_PAD // M_HALF, D_FF // TF_UP),
        in_specs=[pl.BlockSpec((M_HALF, D_MODEL), lambda h, f: (h, 0), pipeline_mode=pl.Buffered(1)),
                  wcol, wcol, wrow],
        out_specs=[pl.BlockSpec((M_HALF, TF_UP), lambda h, f: (h, f)), wrow],
        out_shape=[jax.ShapeDtypeStruct((M_PAD, D_FF), BF16),
                   jax.ShapeDtypeStruct((D_FF, D_MODEL), BF16)],
        compiler_params=pltpu.CompilerParams(
            dimension_semantics=("arbitrary", "arbitrary"), vmem_limit_bytes=VMEM_LIMIT),
        name="ffn_up",
    )(x1b, wg, wu, wd)


def _ffn_down_kernel(act_ref, x_ref, wd_ref, g2_ref, b2_ref, yp_ref, ye_ref):
    i = pl.program_id(0)

    def tile(o_ref, rows):
        for r0 in range(0, rows, SUB_DOWN):
            rs = slice(r0, r0 + SUB_DOWN)
            z = ALPHA * x_ref[rs, :] + _dot(act_ref[rs, :], wd_ref[...])
            o_ref[rs, :] = _layer_norm(z, g2_ref[...], b2_ref[...])

    @pl.when(i < N_PROMPT_TILES)
    def _():
        tile(yp_ref, TM)

    @pl.when(i == N_PROMPT_TILES)
    def _():
        tile(ye_ref, EXTRA_ROWS)


def _ffn_down(act, x1, wd, g2, b2):
    row = lambda i: (i, 0)
    const = lambda i: (0, 0)
    return pl.pallas_call(
        _ffn_down_kernel,
        grid=(N_TILES,),
        in_specs=[pl.BlockSpec((TM, D_FF), row), pl.BlockSpec((TM, D_MODEL), row),
                  pl.BlockSpec((D_FF, D_MODEL), const, pipeline_mode=pl.Buffered(1)),
                  pl.BlockSpec((1, D_MODEL), const), pl.BlockSpec((1, D_MODEL), const)],
        out_specs=[pl.BlockSpec((TM, D_MODEL), lambda i: (jnp.minimum(i, N_PROMPT_TILES - 1), 0)),
                   pl.BlockSpec((EXTRA_ROWS, D_MODEL), const)],
        out_shape=[jax.ShapeDtypeStruct((N_PROMPT_ROWS, D_MODEL), F32),
                   jax.ShapeDtypeStruct((EXTRA_ROWS, D_MODEL), F32)],
        compiler_params=pltpu.CompilerParams(
            dimension_semantics=("arbitrary",), vmem_limit_bytes=VMEM_LIMIT),
        name="ffn_down",
    )(act, x1, wd, g2, b2)


def _pad_rows(a, n):
    return jnp.pad(a, ((0, n - a.shape[0]), (0, 0)))


def kernel(x_prompt, x_sample, state_ssm_re, state_ssm_im, state_conv, meta_tokens, ln0_g, ln0_b,
           w_in, ssm_a_re, ssm_a_im, ssm_log_dt, ssm_b_re, ssm_b_im, ssm_c_re, ssm_c_im, ssm_d,
           ssm_w_glu, conv_w, g_ssm_out, g_conv_out, w_out, ln1_g, ln1_b, w_gate, w_up, w_down,
           ln2_g, ln2_b):
    assert x_prompt.shape == (N_BATCH, SEQ, D_MODEL) and x_sample.shape == (N_SAMPLE, 1, D_MODEL)
    assert w_in.shape[0] == 1, "single layer"
    g, p, c = N_GROUPS, N_STATE, GROUP
    vec = lambda a: a.reshape(1, -1)

    xp = x_prompt.reshape(N_PROMPT_ROWS, D_MODEL)
    xe = _pad_rows(jnp.concatenate([meta_tokens, x_sample[:, 0, :]], axis=0), SUB_IN)
    g0, b0 = vec(ln0_g), vec(ln0_b)

    m, wst, cpow, bpad, cpad, avec = _ssm_prep(ssm_log_dt[0], ssm_a_re[0], ssm_a_im[0], ssm_b_re[0],
                                               ssm_b_im[0], ssm_c_re[0], ssm_c_im[0])

    prev = jnp.pad(state_conv[0], ((SAMPLE_ROW0, SUB_IN - SAMPLE_ROW0 - N_SAMPLE), (0, 0), (0, 0)))
    h0, u, rc, v_tail, v_extra, wout_bf, wglu_bf = _in_proj(
        xp, xe, g0, b0, w_in[0].astype(BF16), prev[:, 1], prev[:, 0], conv_w[0], vec(g_conv_out[0]),
        w_out[0], ssm_w_glu[0])

    flat = lambda a: a.reshape(N_SAMPLE, g * p)
    y_all, s_fin, ns_re, ns_im = _ssm_scan(u, m, wst, cpow, bpad, cpad, avec,
                                           flat(state_ssm_re[0]), flat(state_ssm_im[0]))

    x1, x1b = _mix_out(y_all, u, rc, h0, vec(ssm_d[0]), wglu_bf, vec(g_ssm_out[0]),
                       wout_bf, vec(ln1_g[0]), vec(ln1_b[0]))

    act, wd_bf = _ffn_up(x1b, w_gate[0], w_up[0], w_down[0])
    yp, ye = _ffn_down(act, x1, wd_bf, vec(ln2_g[0]), vec(ln2_b[0]))

    y_prompt = yp.reshape(N_BATCH, SEQ, D_MODEL)
    y_sample = ye[SAMPLE_ROW0:SAMPLE_ROW0 + N_SAMPLE].reshape(N_SAMPLE, 1, D_MODEL)

    sd = state_ssm_re.dtype
    s_p = s_fin.transpose(1, 0, 2)
    unflat = lambda a: a.reshape(1, N_SAMPLE, g, p).astype(sd)
    cd = state_conv.dtype
    conv_p = v_tail[TILES_PER_SEQ - 1::TILES_PER_SEQ, ROW8 - (CONV_W - 1):].astype(cd)
    conv_s = jnp.stack([state_conv[0][:, 1], v_extra[SAMPLE_ROW0:SAMPLE_ROW0 + N_SAMPLE].astype(cd)], axis=1)
    return (y_prompt, y_sample,
            s_p[None, ..., :p].astype(sd), s_p[None, ..., p:].astype(sd), conv_p[None],
            unflat(ns_re), unflat(ns_im), conv_s[None])
```

```python
import functools

import jax
import jax.numpy as jnp
import numpy as np
from jax import lax
from jax.experimental import pallas as pl
from jax.experimental.pallas import tpu as pltpu

D_MODEL = 2048
D_SSM = 1024
D_CONV = 1024
N_GROUPS = 64
GROUP = 16
N_STATE = 64
N_META = 16
CONV_W = 3
D_FF = 5632
LN_EPS = 1e-5
RMS_EPS = 1e-6
ALPHA = 2.0 ** 0.25

N_BATCH = 4
SEQ = 2048
N_SAMPLE = 128
CHUNK = 16
N_CHUNKS = SEQ // CHUNK + 1
ROW8 = 8
CW = CHUNK * GROUP
SW = 2 * N_STATE

TM = 512
N_PROMPT_ROWS = N_BATCH * SEQ
N_PROMPT_TILES = N_PROMPT_ROWS // TM
N_TILES = N_PROMPT_TILES + 1
M_PAD = N_TILES * TM
SAMPLE_ROW0 = N_META
GB = 8
VMEM_LIMIT = 60 * 1024 * 1024

F32 = jnp.float32
BF16 = jnp.bfloat16


def _layer_norm(x, g, b):
    mu = jnp.mean(x, axis=-1, keepdims=True)
    xc = x - mu
    var = jnp.mean(xc * xc, axis=-1, keepdims=True)
    return xc * lax.rsqrt(var + LN_EPS) * g + b


def _rms_norm(x, g):
    return x * lax.rsqrt(jnp.mean(x * x, axis=-1, keepdims=True) + RMS_EPS) * g


def _gelu_tanh(x):
    c = np.sqrt(2.0 / np.pi)
    hx = 0.5 * x
    return hx + hx * jnp.tanh(x * (np.float32(c) + np.float32(c * 0.044715) * (x * x)))


def _sigmoid(x):
    return 1.0 / (1.0 + jnp.exp2(x * np.float32(-1.4426950408889634)))


def _dot(a, b):
    return jnp.dot(a, b, preferred_element_type=F32)


def _dot_nt(a, b):
    return lax.dot_general(a, b, (((1,), (1,)), ((), ())), preferred_element_type=F32)


def _ssm_prep_kernel(logdt_ref, are_ref, aim_ref, b1_ref, b2_ref, c1_ref, c2_ref, win_ref,
                     m_ref, wst_ref, cpow_ref, bpad_ref, cpad_ref, avec_ref, winb_ref):
    winb_ref[...] = win_ref[...].astype(BF16)
    dt = jnp.exp(logdt_ref[...])
    a_re, a_im = are_ref[...], aim_ref[...]
    mag = jnp.exp(dt * a_re)
    ang = dt * a_im
    ab_re = mag * jnp.cos(ang)
    ab_im = mag * jnp.sin(ang)
    n_re = ab_re - 1.0
    den = a_re * a_re + a_im * a_im
    f_re = (n_re * a_re + ab_im * a_im) / den
    f_im = (ab_im * a_re - n_re * a_im) / den
    sgn = jnp.where(lax.broadcasted_iota(jnp.int32, (GB, SW), 1) < N_STATE, -1.0, 1.0)

    pr, pi = jnp.ones_like(ab_re), jnp.zeros_like(ab_re)
    pw = [(pr, pi)]
    for _ in range(CHUNK):
        pr, pi = pr * ab_re - pi * ab_im, pr * ab_im + pi * ab_re
        pw.append((pr, sgn * pi))
    fx = sgn * f_im
    lane = lax.broadcasted_iota(jnp.int32, (GROUP, CW), 1)

    for gl in range(GB):
        row = lambda x: x[gl:gl + 1, :]
        bb1 = row(f_re) * b1_ref[gl] + row(fx) * b2_ref[gl]
        bb2 = row(f_re) * b2_ref[gl] - row(fx) * b1_ref[gl]
        cc1 = -row(sgn) * c1_ref[gl]
        cc2 = -row(sgn) * c2_ref[gl]

        cq = []
        for s in range(CHUNK):
            wr, wx = pw[CHUNK - 1 - s]
            wst_ref[gl, s * GROUP:(s + 1) * GROUP, :] = (row(wr) * bb1 + row(wx) * bb2).astype(BF16)
            qr, qx = pw[s + 1]
            cpow_ref[gl, s * GROUP:(s + 1) * GROUP, :] = (row(qr) * cc1 + row(qx) * cc2).astype(BF16)
            dr, dx = pw[s]
            cq.append(row(dr) * cc1 + row(dx) * cc2)
        kt = lax.dot_general(bb1, jnp.concatenate(cq, axis=0), (((1,), (1,)), ((), ())),
                             precision=lax.Precision.HIGHEST, preferred_element_type=F32)
        for i in range(CHUNK):
            blk = kt if i == 0 else jnp.where(lane >= i * GROUP, pltpu.roll(kt, i * GROUP, axis=1), 0.0)
            m_ref[gl, i * GROUP:(i + 1) * GROUP, :] = blk.astype(BF16)

        zeros = jnp.zeros((SW, SW), BF16)
        bpad_ref[gl] = zeros
        cpad_ref[gl] = zeros
        bpad_ref[gl, gl * GROUP:(gl + 1) * GROUP, :] = bb1.astype(BF16)
        cpad_ref[gl, gl * GROUP:(gl + 1) * GROUP, :] = cq[0].astype(BF16)

        avec_ref[gl, 0:1, :] = row(pw[CHUNK][0])
        avec_ref[gl, 1:2, :] = row(pw[CHUNK][1])
        avec_ref[gl, 2:3, :] = row(pw[1][0])
        avec_ref[gl, 3:4, :] = row(pw[1][1])


def _ssm_prep(log_dt, a_re, a_im, b_re, b_im, c_re, c_im, w_in):
    g, c = N_GROUPS, GROUP
    dup = lambda x: jnp.concatenate([x, x], axis=-1)
    bt_re, bt_im = b_re.transpose(0, 2, 1), b_im.transpose(0, 2, 1)
    cat = lambda x, y: jnp.concatenate([x, y], axis=-1)
    vec = pl.BlockSpec((GB, SW), lambda i: (i, 0))
    blk = lambda *s: pl.BlockSpec((GB,) + s, lambda i: (i,) + (0,) * len(s))
    slab = pl.BlockSpec((w_in.shape[0] // (g // GB), w_in.shape[1]), lambda i: (i, 0))
    return pl.pallas_call(
        _ssm_prep_kernel,
        grid=(g // GB,),
        in_specs=[vec, vec, vec, blk(c, SW), blk(c, SW), blk(c, SW), blk(c, SW), slab],
        out_specs=[blk(CW, CW), blk(CW, SW), blk(CW, SW), blk(SW, SW), blk(SW, SW), blk(4, SW), slab],
        out_shape=[jax.ShapeDtypeStruct((g, CW, CW), BF16),
                   jax.ShapeDtypeStruct((g, CW, SW), BF16),
                   jax.ShapeDtypeStruct((g, CW, SW), BF16),
                   jax.ShapeDtypeStruct((g, SW, SW), BF16),
                   jax.ShapeDtypeStruct((g, SW, SW), BF16),
                   jax.ShapeDtypeStruct((g, 4, SW), F32),
                   jax.ShapeDtypeStruct(w_in.shape, BF16)],
        name="ssm_prep",
    )(jnp.broadcast_to(log_dt[:, None], (g, SW)), dup(a_re), dup(a_im),
      cat(bt_re, bt_im), cat(bt_im, bt_re), cat(c_re, c_im), cat(c_im, c_re), w_in)


SUB_IN = 256

TILES_PER_SEQ = SEQ // TM


def _in_proj_kernel(xp_ref, xe_ref, g0_ref, b0_ref, w_ref, p1_ref, p2_ref, cw_ref, gc_ref, wo_ref, wgl_ref,
                    h0_ref, u_ref, rc_ref, vt_ref, vs_ref, wob_ref, wglb_ref, vbuf_scr, meta_scr):
    s = pl.program_id(0)

    @pl.when(s == 0)
    def _():
        vbuf_scr[0:ROW8, :] = jnp.zeros((ROW8, D_CONV), F32)

    def tile(x_ref, is_extra):
        sub = SUB_IN if is_extra else TM
        rs = slice(0, sub)
        if is_extra:
            pad = slice(sub, TM)
            h0_ref[pad, :] = jnp.zeros((TM - sub, D_MODEL), F32)
            u_ref[pad, :] = jnp.zeros((TM - sub, D_SSM), F32)
            rc_ref[pad, :] = jnp.zeros((TM - sub, D_CONV), BF16)
        h0 = _layer_norm(x_ref[rs, :], g0_ref[...], b0_ref[...])
        h0_ref[rs, :] = h0
        proj = _dot(h0.astype(BF16), w_ref[...])
        part = lambda k: proj[:, k * D_SSM:(k + 1) * D_SSM]
        u_ref[rs, :] = part(0)
        b_gate = part(1)
        v = part(2) * part(3)
        vbuf_scr[ROW8:ROW8 + sub, :] = v
        prev = vbuf_scr[0:ROW8 + sub, :]
        vm1 = pltpu.roll(prev, 1, axis=0)[ROW8:, :]
        vm2 = pltpu.roll(prev, 2, axis=0)[ROW8:, :]
        if is_extra:
            sample = lax.broadcasted_iota(jnp.int32, (sub, 1), 0) >= SAMPLE_ROW0
            vm1 = jnp.where(sample, p1_ref[...], vm1)
            vm2 = jnp.where(sample, p2_ref[...], vm2)
        conv = cw_ref[0:1, :] * vm2 + cw_ref[1:2, :] * vm1 + cw_ref[2:3, :] * v
        rc_ref[rs, :] = _rms_norm(b_gate * conv, gc_ref[...]).astype(BF16)

    @pl.when(s == 0)
    def _():
        tile(xe_ref, True)
        vs_ref[...] = vbuf_scr[ROW8:ROW8 + SUB_IN, :]
        meta_scr[...] = vbuf_scr[ROW8 + N_META - ROW8:ROW8 + N_META, :]
        vbuf_scr[0:ROW8, :] = meta_scr[...]

    @pl.when(s > 0)
    def _():
        tile(xp_ref, False)
        wob_ref[...] = wo_ref[...].astype(BF16)
        wglb_ref[...] = wgl_ref[...].astype(BF16)
        tail = vbuf_scr[TM:TM + ROW8, :]
        vt_ref[0] = tail
        vbuf_scr[0:ROW8, :] = jnp.where(s % TILES_PER_SEQ == 0, meta_scr[...], tail)


def _in_proj(xp, xe, g0, b0, w_in, p1, p2, cw, gc, w_out, w_glu):
    const = lambda s: (0, 0)
    slab = lambda s: (jnp.maximum(s - 1, 0), 0)
    row = lambda s: ((s + N_PROMPT_TILES) % N_TILES, 0)
    vec = lambda n: pl.BlockSpec((1, n), const)
    return pl.pallas_call(
        _in_proj_kernel,
        grid=(N_TILES,),
        in_specs=[pl.BlockSpec((TM, D_MODEL), lambda s: (jnp.maximum(s - 1, 0), 0)),
                  pl.BlockSpec((SUB_IN, D_MODEL), const),
                  vec(D_MODEL), vec(D_MODEL),
                  pl.BlockSpec((D_MODEL, 4 * D_SSM), const, pipeline_mode=pl.Buffered(1)),
                  pl.BlockSpec((SUB_IN, D_CONV), const), pl.BlockSpec((SUB_IN, D_CONV), const),
                  pl.BlockSpec((CONV_W, D_CONV), const), vec(D_CONV),
                  pl.BlockSpec((D_MODEL // N_PROMPT_TILES, D_MODEL), slab),
                  pl.BlockSpec((D_SSM // N_PROMPT_TILES, D_SSM), slab)],
        out_specs=[pl.BlockSpec((TM, D_MODEL), row), pl.BlockSpec((TM, D_SSM), row),
                   pl.BlockSpec((TM, D_CONV), row),
                   pl.BlockSpec((1, ROW8, D_CONV), lambda s: (jnp.maximum(s - 1, 0), 0, 0)),
                   pl.BlockSpec((SUB_IN, D_CONV), const),
                   pl.BlockSpec((D_MODEL // N_PROMPT_TILES, D_MODEL), slab),
                   pl.BlockSpec((D_SSM // N_PROMPT_TILES, D_SSM), slab)],
        out_shape=[jax.ShapeDtypeStruct((M_PAD, D_MODEL), F32),
                   jax.ShapeDtypeStruct((M_PAD, D_SSM), F32),
                   jax.ShapeDtypeStruct((M_PAD, D_CONV), BF16),
                   jax.ShapeDtypeStruct((N_PROMPT_TILES, ROW8, D_CONV), F32),
                   jax.ShapeDtypeStruct((SUB_IN, D_CONV), F32),
                   jax.ShapeDtypeStruct((D_MODEL, D_MODEL), BF16),
                   jax.ShapeDtypeStruct((D_SSM, D_SSM), BF16)],
        scratch_shapes=[pltpu.VMEM((ROW8 + TM, D_CONV), F32), pltpu.VMEM((ROW8, D_CONV), F32)],
        compiler_params=pltpu.CompilerParams(
            dimension_semantics=("arbitrary",), vmem_limit_bytes=VMEM_LIMIT),
        name="in_proj",
    )(xp, xe, g0, b0, w_in, p1, p2, cw, gc, w_out, w_glu)


SUB = 16
SEQ_CHUNKS = SEQ // CHUNK
LHS_ROWS = N_BATCH * SEQ_CHUNKS + SUB
META_LHS = N_BATCH * SEQ_CHUNKS
PITCH = SEQ_CHUNKS + ROW8
REC_ROWS = N_BATCH * PITCH + ROW8
META_REC = N_BATCH * PITCH
SAMPLE_LO = N_PROMPT_ROWS + SAMPLE_ROW0
SAMPLE_HI = SAMPLE_LO + N_SAMPLE


def _block_transpose(a):
    a = list(a)
    blk = lax.broadcasted_iota(jnp.int32, a[0].shape, 1) // GROUP
    for d in (4, 2, 1):
        w = d * GROUP
        keep = (blk & d) == 0
        for j in range(GB):
            if j & d == 0:
                lo, hi = a[j], a[j + d]
                a[j] = jnp.where(keep, lo, pltpu.roll(hi, w, axis=1))
                a[j + d] = jnp.where(keep, pltpu.roll(lo, SW - w, axis=1), hi)
    return a


def _ssm_scan_kernel(u_ref, m_ref, wst_ref, cpow_ref, bpad_ref, cpad_ref, avec_ref, sre_ref, sim_ref,
                     y_ref, sfin_ref, nre_ref, nim_ref, lhs_scr, x_scr, xs_scr, sp_scr, yg_scr):
    def swap_halves(s):
        return pltpu.roll(s, N_STATE, axis=1)

    half = SW

    for h in range(2):
        src = [u_ref[pl.ds(8 * h + j, LHS_ROWS, stride=CHUNK), :] for j in range(GB)]
        for gl, w in enumerate(_block_transpose(src)):
            lhs_scr[gl, :, h * half:(h + 1) * half] = w.astype(BF16)

    def to_rec(ref, gl, x):
        for n in range(N_BATCH):
            ref[gl, n * PITCH:n * PITCH + SEQ_CHUNKS, :] = x[n * SEQ_CHUNKS:(n + 1) * SEQ_CHUNKS]
        ref[gl, META_REC:META_REC + ROW8, :] = x[META_LHS:META_LHS + ROW8]

    for gl in range(GB):
        x = _dot(lhs_scr[gl], wst_ref[gl])
        to_rec(x_scr, gl, x)
        to_rec(xs_scr, gl, swap_halves(x))

    ar16 = [avec_ref[gl, 0:1, :] for gl in range(GB)]
    ax16 = [avec_ref[gl, 1:2, :] for gl in range(GB)]

    def step(k, carry):
        new = []
        for gl in range(GB):
            s, t = carry[gl]
            rows = pl.ds(k, N_BATCH, stride=PITCH)
            sp_scr[gl, rows, :] = s
            new.append((ar16[gl] * s + ax16[gl] * t + x_scr[gl, rows, :],
                        ar16[gl] * t - ax16[gl] * s + xs_scr[gl, rows, :]))
        return tuple(new)

    first = lambda ref, gl: jnp.broadcast_to(ref[gl, META_REC:META_REC + 1, :], (N_BATCH, SW))
    init = tuple((first(x_scr, gl), first(xs_scr, gl)) for gl in range(GB))
    fin = lax.fori_loop(0, SEQ_CHUNKS, step, init)

    us = u_ref[SAMPLE_LO:SAMPLE_HI, :].astype(BF16)
    ys = jnp.zeros((N_SAMPLE, SW), F32)
    for gl in range(GB):
        sfin_ref[gl] = fin[gl][0]
        sp = jnp.concatenate([sp_scr[gl, n * PITCH:n * PITCH + SEQ_CHUNKS, :] for n in range(N_BATCH)]
                             + [jnp.zeros((SUB, SW), F32)], axis=0)
        yg_scr[gl] = _dot(lhs_scr[gl], m_ref[gl]) + _dot_nt(sp.astype(BF16), cpow_ref[gl])
        lanes = slice(gl * N_STATE, (gl + 1) * N_STATE)
        s0 = jnp.concatenate([sre_ref[:, lanes], sim_ref[:, lanes]], axis=1)
        sn = (avec_ref[gl, 2:3, :] * s0 + avec_ref[gl, 3:4, :] * swap_halves(s0)
              + _dot(us, bpad_ref[gl]))
        nre_ref[:, lanes] = sn[:, 0:N_STATE]
        nim_ref[:, lanes] = sn[:, N_STATE:SW]
        ys = ys + _dot_nt(sn.astype(BF16), cpad_ref[gl])

    for h in range(2):
        src = [yg_scr[gl, :, h * half:(h + 1) * half] for gl in range(GB)]
        for j, w in enumerate(_block_transpose(src)):
            y_ref[pl.ds(8 * h + j, META_LHS, stride=CHUNK), :] = w[0:META_LHS]
            y_ref[N_PROMPT_ROWS + 8 * h + j:N_PROMPT_ROWS + 8 * h + j + 1, :] = w[META_LHS:META_LHS + 1]
    y_ref[SAMPLE_LO:SAMPLE_HI, :] = ys
    y_ref[SAMPLE_HI:M_PAD, :] = jnp.zeros((M_PAD - SAMPLE_HI, SW), F32)


def _ssm_scan(u, m, wst, cpow, bpad, cpad, avec, s_re, s_im):
    blk = lambda *s: pl.BlockSpec((GB,) + s, lambda i: (i,) + (0,) * len(s))
    col = pl.BlockSpec((M_PAD, SW), lambda i: (0, i))
    st = pl.BlockSpec((N_SAMPLE, GB * N_STATE), lambda i: (0, i))
    st_shape = jax.ShapeDtypeStruct((N_SAMPLE, N_GROUPS * N_STATE), F32)
    return pl.pallas_call(
        _ssm_scan_kernel,
        grid=(N_GROUPS // GB,),
        in_specs=[col, blk(CW, CW), blk(CW, SW), blk(CW, SW), blk(SW, SW), blk(SW, SW), blk(4, SW),
                  st, st],
        out_specs=[col, blk(N_BATCH, SW), st, st],
        out_shape=[jax.ShapeDtypeStruct((M_PAD, D_SSM), F32),
                   jax.ShapeDtypeStruct((N_GROUPS, N_BATCH, SW), F32), st_shape, st_shape],
        scratch_shapes=[pltpu.VMEM((GB, LHS_ROWS, CW), BF16), pltpu.VMEM((GB, REC_ROWS, SW), F32),
                        pltpu.VMEM((GB, REC_ROWS, SW), F32), pltpu.VMEM((GB, REC_ROWS, SW), F32),
                        pltpu.VMEM((GB, LHS_ROWS, CW), F32)],
        compiler_params=pltpu.CompilerParams(
            dimension_semantics=("arbitrary",), vmem_limit_bytes=VMEM_LIMIT),
        name="ssm_scan",
    )(u, m, wst, cpow, bpad, cpad, avec, s_re, s_im)


SUB_MIX = 256


def _mix_out_kernel(y_ref, u_ref, rc_ref, h0_ref, dskip_ref, wglu_ref, gs_ref, wout_ref,
                    g1_ref, b1_ref, x1_ref, x1b_ref, merged_scr):
    def sub_tile(r0):
        rs = slice(r0, r0 + SUB_MIX)
        ys = _gelu_tanh(y_ref[rs, :] + dskip_ref[...] * u_ref[rs, :])
        gate = _dot(ys.astype(BF16), wglu_ref[...])
        merged_scr[rs, 0:D_SSM] = _rms_norm(ys * _sigmoid(gate), gs_ref[...]).astype(BF16)
        merged_scr[rs, D_SSM:D_MODEL] = rc_ref[rs, :]
        mo = _dot(merged_scr[rs, :], wout_ref[...])
        x1 = _layer_norm(ALPHA * h0_ref[rs, :] + mo, g1_ref[...], b1_ref[...])
        x1_ref[rs, :] = x1
        x1b_ref[rs, :] = x1.astype(BF16)

    @pl.when(pl.program_id(0) < N_PROMPT_TILES)
    def _():
        for r0 in range(0, TM, SUB_MIX):
            sub_tile(r0)

    @pl.when(pl.program_id(0) == N_PROMPT_TILES)
    def _():
        sub_tile(0)
        x1_ref[SUB_MIX:TM, :] = jnp.zeros((TM - SUB_MIX, D_MODEL), F32)
        x1b_ref[SUB_MIX:TM, :] = jnp.zeros((TM - SUB_MIX, D_MODEL), BF16)


def _mix_out(y, u, rc, h0, dskip, wglu, gs, wout, g1, b1):
    row = lambda i: (i, 0)
    const = lambda i: (0, 0)
    vec = lambda n: pl.BlockSpec((1, n), const)
    return pl.pallas_call(
        _mix_out_kernel,
        grid=(N_TILES,),
        in_specs=[pl.BlockSpec((TM, D_SSM), row), pl.BlockSpec((TM, D_SSM), row),
                  pl.BlockSpec((TM, D_CONV), row), pl.BlockSpec((TM, D_MODEL), row), vec(D_SSM),
                  pl.BlockSpec((D_SSM, D_SSM), const, pipeline_mode=pl.Buffered(1)),
                  vec(D_SSM),
                  pl.BlockSpec((D_MODEL, D_MODEL), const, pipeline_mode=pl.Buffered(1)),
                  vec(D_MODEL), vec(D_MODEL)],
        out_specs=[pl.BlockSpec((TM, D_MODEL), row), pl.BlockSpec((TM, D_MODEL), row)],
        out_shape=[jax.ShapeDtypeStruct((M_PAD, D_MODEL), F32),
                   jax.ShapeDtypeStruct((M_PAD, D_MODEL), BF16)],
        scratch_shapes=[pltpu.VMEM((TM, D_MODEL), BF16)],
        compiler_params=pltpu.CompilerParams(
            dimension_semantics=("arbitrary",), vmem_limit_bytes=VMEM_LIMIT),
        name="mix_out",
    )(y, u, rc, h0, dskip, wglu, gs, wout, g1, b1)


M_HALF = M_PAD // 2
SUB_UP = M_HALF // 4
TF_UP = 256
SUB_DOWN = 256
EXTRA_ROWS = 256
M_REAL_HALF = M_HALF - (TM - EXTRA_ROWS)


def _ffn_up_kernel(xb_ref, wg_ref, wu_ref, wd_ref, act_ref, wdb_ref):
    @pl.when(pl.program_id(0) == 0)
    def _():
        wdb_ref[...] = wd_ref[...].astype(BF16)

    wgb = wg_ref[...].astype(BF16)
    wub = wu_ref[...].astype(BF16)

    def rows(lo, hi):
        xb = xb_ref[lo:hi, :]
        gate = _dot(xb, wgb)
        up = _dot(xb, wub)
        act_ref[lo:hi, :] = (gate * _sigmoid(gate) * up).astype(BF16)

    @pl.when(pl.program_id(0) == 0)
    def _():
        for r0 in range(0, M_HALF, SUB_UP):
            rows(r0, r0 + SUB_UP)

    @pl.when(pl.program_id(0) == 1)
    def _():
        for r0 in range(0, M_REAL_HALF, SUB_UP):
            rows(r0, min(r0 + SUB_UP, M_REAL_HALF))
        act_ref[M_REAL_HALF:M_HALF, :] = jnp.zeros((M_HALF - M_REAL_HALF, TF_UP), BF16)


def _ffn_up(x1b, wg, wu, wd):
    wcol = pl.BlockSpec((D_MODEL, TF_UP), lambda h, f: (0, f))
    wrow = pl.BlockSpec((TF_UP, D_MODEL), lambda h, f: (jnp.where(h == 0, f, D_FF // TF_UP - 1), 0))
    return pl.pallas_call(
        _ffn_up_kernel,
        grid=(M_PAD // M_HALF, D_FF // TF_UP),
        in_specs=[pl.BlockSpec((M_HALF, D_MODEL), lambda h, f: (h, 0), pipeline_mode=pl.Buffered(1)),
                  wcol, wcol, wrow],
        out_specs=[pl.BlockSpec((M_HALF, TF_UP), lambda h, f: (h, f)), wrow],
        out_shape=[jax.ShapeDtypeStruct((M_PAD, D_FF), BF16),
                   jax.ShapeDtypeStruct((D_FF, D_MODEL), BF16)],
        compiler_params=pltpu.CompilerParams(
            dimension_semantics=("arbitrary", "arbitrary"), vmem_limit_bytes=VMEM_LIMIT),
        name="ffn_up",
    )(x1b, wg, wu, wd)


def _ffn_down_kernel(act_ref, x_ref, wd_ref, g2_ref, b2_ref, yp_ref, ye_ref):
    i = pl.program_id(0)

    def tile(o_ref, rows):
        for r0 in range(0, rows, SUB_DOWN):
            rs = slice(r0, r0 + SUB_DOWN)
            z = ALPHA * x_ref[rs, :] + _dot(act_ref[rs, :], wd_ref[...])
            o_ref[rs, :] = _layer_norm(z, g2_ref[...], b2_ref[...])

    @pl.when(i < N_PROMPT_TILES)
    def _():
        tile(yp_ref, TM)

    @pl.when(i == N_PROMPT_TILES)
    def _():
        tile(ye_ref, EXTRA_ROWS)


def _ffn_down(act, x1, wd, g2, b2):
    row = lambda i: (i, 0)
    const = lambda i: (0, 0)
    return pl.pallas_call(
        _ffn_down_kernel,
        grid=(N_TILES,),
        in_specs=[pl.BlockSpec((TM, D_FF), row), pl.BlockSpec((TM, D_MODEL), row),
                  pl.BlockSpec((D_FF, D_MODEL), const, pipeline_mode=pl.Buffered(1)),
                  pl.BlockSpec((1, D_MODEL), const), pl.BlockSpec((1, D_MODEL), const)],
        out_specs=[pl.BlockSpec((TM, D_MODEL), lambda i: (jnp.minimum(i, N_PROMPT_TILES - 1), 0)),
                   pl.BlockSpec((EXTRA_ROWS, D_MODEL), const)],
        out_shape=[jax.ShapeDtypeStruct((N_PROMPT_ROWS, D_MODEL), F32),
                   jax.ShapeDtypeStruct((EXTRA_ROWS, D_MODEL), F32)],
        compiler_params=pltpu.CompilerParams(
            dimension_semantics=("arbitrary",), vmem_limit_bytes=VMEM_LIMIT),
        name="ffn_down",
    )(act, x1, wd, g2, b2)


def _pad_rows(a, n):
    return jnp.pad(a, ((0, n - a.shape[0]), (0, 0)))


def kernel(x_prompt, x_sample, state_ssm_re, state_ssm_im, state_conv, meta_tokens, ln0_g, ln0_b,
           w_in, ssm_a_re, ssm_a_im, ssm_log_dt, ssm_b_re, ssm_b_im, ssm_c_re, ssm_c_im, ssm_d,
           ssm_w_glu, conv_w, g_ssm_out, g_conv_out, w_out, ln1_g, ln1_b, w_gate, w_up, w_down,
           ln2_g, ln2_b):
    assert x_prompt.shape == (N_BATCH, SEQ, D_MODEL) and x_sample.shape == (N_SAMPLE, 1, D_MODEL)
    assert w_in.shape[0] == 1, "single layer"
    g, p, c = N_GROUPS, N_STATE, GROUP
    vec = lambda a: a.reshape(1, -1)

    xp = x_prompt.reshape(N_PROMPT_ROWS, D_MODEL)
    xe = _pad_rows(jnp.concatenate([meta_tokens, x_sample[:, 0, :]], axis=0), SUB_IN)
    g0, b0 = vec(ln0_g), vec(ln0_b)

    m, wst, cpow, bpad, cpad, avec, w_in_bf = _ssm_prep(ssm_log_dt[0], ssm_a_re[0], ssm_a_im[0], ssm_b_re[0],
                                                        ssm_b_im[0], ssm_c_re[0], ssm_c_im[0], w_in[0])

    prev = jnp.pad(state_conv[0], ((SAMPLE_ROW0, SUB_IN - SAMPLE_ROW0 - N_SAMPLE), (0, 0), (0, 0)))
    h0, u, rc, v_tail, v_extra, wout_bf, wglu_bf = _in_proj(
        xp, xe, g0, b0, w_in_bf, prev[:, 1], prev[:, 0], conv_w[0], vec(g_conv_out[0]),
        w_out[0], ssm_w_glu[0])

    flat = lambda a: a.reshape(N_SAMPLE, g * p)
    y_all, s_fin, ns_re, ns_im = _ssm_scan(u, m, wst, cpow, bpad, cpad, avec,
                                           flat(state_ssm_re[0]), flat(state_ssm_im[0]))

    x1, x1b = _mix_out(y_all, u, rc, h0, vec(ssm_d[0]), wglu_bf, vec(g_ssm_out[0]),
                       wout_bf, vec(ln1_g[0]), vec(ln1_b[0]))

    act, wd_bf = _ffn_up(x1b, w_gate[0], w_up[0], w_down[0])
    yp, ye = _ffn_down(act, x1, wd_bf, vec(ln2_g[0]), vec(ln2_b[0]))

    y_prompt = yp.reshape(N_BATCH, SEQ, D_MODEL)
    y_sample = ye[SAMPLE_ROW0:SAMPLE_ROW0 + N_SAMPLE].reshape(N_SAMPLE, 1, D_MODEL)

    sd = state_ssm_re.dtype
    s_p = s_fin.transpose(1, 0, 2)
    unflat = lambda a: a.reshape(1, N_SAMPLE, g, p).astype(sd)
    cd = state_conv.dtype
    conv_p = v_tail[TILES_PER_SEQ - 1::TILES_PER_SEQ, ROW8 - (CONV_W - 1):].astype(cd)
    conv_s = jnp.stack([state_conv[0][:, 1], v_extra[SAMPLE_ROW0:SAMPLE_ROW0 + N_SAMPLE].astype(cd)], axis=1)
    return (y_prompt, y_sample,
            s_p[None, ..., :p].astype(sd), s_p[None, ..., p:].astype(sd), conv_p[None],
            unflat(ns_re), unflat(ns_im), conv_s[None])
```

```python
import jax
import jax.numpy as jnp
import numpy as np
from jax import lax
from jax.experimental import pallas as pl
from jax.experimental.pallas import tpu as pltpu

D_MODEL = 2048
D_SSM = 1024
D_CONV = 1024
N_GROUPS = 64
GROUP = 16
N_STATE = 64
N_META = 16
CONV_W = 3
D_FF = 5632
LN_EPS = 1e-5
RMS_EPS = 1e-6
ALPHA = 2.0 ** 0.25

N_BATCH = 4
SEQ = 2048
N_SAMPLE = 128
CHUNK = 16
ROW8 = 8
CW = CHUNK * GROUP
SW = 2 * N_STATE

TM = 512
N_PROMPT_ROWS = N_BATCH * SEQ
N_PROMPT_TILES = N_PROMPT_ROWS // TM
N_TILES = N_PROMPT_TILES + 1
M_PAD = N_TILES * TM
SAMPLE_ROW0 = N_META
GB = 8
N_LANE_BLOCKS = D_SSM // SW
TILE_CHUNKS = TM // CHUNK
EXTRA_ROWS = 256
VMEM_LIMIT = 60 * 1024 * 1024

F32 = jnp.float32
BF16 = jnp.bfloat16


def _layer_norm(x, g, b):
    mu = jnp.mean(x, axis=-1, keepdims=True)
    xc = x - mu
    var = jnp.mean(xc * xc, axis=-1, keepdims=True)
    return xc * lax.rsqrt(var + LN_EPS) * g + b


def _rms_norm(x, g):
    return x * lax.rsqrt(jnp.mean(x * x, axis=-1, keepdims=True) + RMS_EPS) * g


def _gelu_tanh(x):
    c = np.sqrt(2.0 / np.pi)
    hx = 0.5 * x
    return hx + hx * jnp.tanh(x * (np.float32(c) + np.float32(c * 0.044715) * (x * x)))


def _sigmoid(x):
    return 1.0 / (1.0 + jnp.exp2(x * np.float32(-1.4426950408889634)))


def _dot(a, b):
    return jnp.dot(a, b, preferred_element_type=F32)


def _dot_nt(a, b):
    return lax.dot_general(a, b, (((1,), (1,)), ((), ())), preferred_element_type=F32)


def _block_transpose(a):
    a = list(a)
    blk = lax.broadcasted_iota(jnp.int32, a[0].shape, 1) // GROUP
    for d in (4, 2, 1):
        w = d * GROUP
        keep = (blk & d) == 0
        for j in range(GB):
            if j & d == 0:
                lo, hi = a[j], a[j + d]
                a[j] = jnp.where(keep, lo, pltpu.roll(hi, w, axis=1))
                a[j + d] = jnp.where(keep, pltpu.roll(lo, SW - w, axis=1), hi)
    return a


def _ssm_prep_kernel(logdt_ref, are_ref, aim_ref, b1_ref, b2_ref, c1_ref, c2_ref, win_ref,
                     m_ref, wst_ref, cpow_ref, bpad_ref, cpad_ref, avec_ref, winb_ref):
    winb_ref[...] = win_ref[...].astype(BF16)
    dt = jnp.exp(logdt_ref[...])
    a_re, a_im = are_ref[...], aim_ref[...]
    mag = jnp.exp(dt * a_re)
    ang = dt * a_im
    ab_re = mag * jnp.cos(ang)
    ab_im = mag * jnp.sin(ang)
    n_re = ab_re - 1.0
    den = a_re * a_re + a_im * a_im
    f_re = (n_re * a_re + ab_im * a_im) / den
    f_im = (ab_im * a_re - n_re * a_im) / den
    sgn = jnp.where(lax.broadcasted_iota(jnp.int32, (GB, SW), 1) < N_STATE, -1.0, 1.0)

    pr, pi = jnp.ones_like(ab_re), jnp.zeros_like(ab_re)
    pw = [(pr, pi)]
    for _ in range(CHUNK):
        pr, pi = pr * ab_re - pi * ab_im, pr * ab_im + pi * ab_re
        pw.append((pr, sgn * pi))
    fx = sgn * f_im
    lane = lax.broadcasted_iota(jnp.int32, (GROUP, CW), 1)

    for gl in range(GB):
        row = lambda x: x[gl:gl + 1, :]
        bb1 = row(f_re) * b1_ref[gl] + row(fx) * b2_ref[gl]
        bb2 = row(f_re) * b2_ref[gl] - row(fx) * b1_ref[gl]
        cc1 = -row(sgn) * c1_ref[gl]
        cc2 = -row(sgn) * c2_ref[gl]

        cq = []
        for s in range(CHUNK):
            wr, wx = pw[CHUNK - 1 - s]
            wst_ref[gl, s * GROUP:(s + 1) * GROUP, :] = (row(wr) * bb1 + row(wx) * bb2).astype(BF16)
            qr, qx = pw[s + 1]
            cpow_ref[gl, s * GROUP:(s + 1) * GROUP, :] = (row(qr) * cc1 + row(qx) * cc2).astype(BF16)
            dr, dx = pw[s]
            cq.append(row(dr) * cc1 + row(dx) * cc2)
        kt = lax.dot_general(bb1, jnp.concatenate(cq, axis=0), (((1,), (1,)), ((), ())),
                             precision=lax.Precision.HIGHEST, preferred_element_type=F32)
        for i in range(CHUNK):
            blk = kt if i == 0 else jnp.where(lane >= i * GROUP, pltpu.roll(kt, i * GROUP, axis=1), 0.0)
            m_ref[gl, i * GROUP:(i + 1) * GROUP, :] = blk.astype(BF16)

        zeros = jnp.zeros((SW, SW), BF16)
        bpad_ref[gl] = zeros
        cpad_ref[gl] = zeros
        bpad_ref[gl, gl * GROUP:(gl + 1) * GROUP, :] = bb1.astype(BF16)
        cpad_ref[gl, gl * GROUP:(gl + 1) * GROUP, :] = cq[0].astype(BF16)

        avec_ref[gl, 0:1, :] = row(pw[CHUNK][0])
        avec_ref[gl, 1:2, :] = row(pw[CHUNK][1])
        avec_ref[gl, 2:3, :] = row(pw[1][0])
        avec_ref[gl, 3:4, :] = row(pw[1][1])


def _ssm_prep(log_dt, a_re, a_im, b_re, b_im, c_re, c_im, w_in):
    g, c = N_GROUPS, GROUP
    dup = lambda x: jnp.concatenate([x, x], axis=-1)
    bt_re, bt_im = b_re.transpose(0, 2, 1), b_im.transpose(0, 2, 1)
    cat = lambda x, y: jnp.concatenate([x, y], axis=-1)
    vec = pl.BlockSpec((GB, SW), lambda i: (i, 0))
    blk = lambda *s: pl.BlockSpec((GB,) + s, lambda i: (i,) + (0,) * len(s))
    slab = pl.BlockSpec((w_in.shape[0] // (g // GB), w_in.shape[1]), lambda i: (i, 0))
    return pl.pallas_call(
        _ssm_prep_kernel,
        grid=(g // GB,),
        in_specs=[vec, vec, vec, blk(c, SW), blk(c, SW), blk(c, SW), blk(c, SW), slab],
        out_specs=[blk(CW, CW), blk(CW, SW), blk(CW, SW), blk(SW, SW), blk(SW, SW), blk(4, SW), slab],
        out_shape=[jax.ShapeDtypeStruct((g, CW, CW), BF16),
                   jax.ShapeDtypeStruct((g, CW, SW), BF16),
                   jax.ShapeDtypeStruct((g, CW, SW), BF16),
                   jax.ShapeDtypeStruct((g, SW, SW), BF16),
                   jax.ShapeDtypeStruct((g, SW, SW), BF16),
                   jax.ShapeDtypeStruct((g, 4, SW), F32),
                   jax.ShapeDtypeStruct(w_in.shape, BF16)],
        name="ssm_prep",
    )(jnp.broadcast_to(log_dt[:, None], (g, SW)), dup(a_re), dup(a_im),
      cat(bt_re, bt_im), cat(bt_im, bt_re), cat(c_re, c_im), cat(c_im, c_re), w_in)


SUB_IN = EXTRA_ROWS

TILES_PER_SEQ = SEQ // TM


def _in_proj_kernel(xp_ref, xe_ref, g0_ref, b0_ref, w_ref, p1_ref, p2_ref, cw_ref, gc_ref, wo_ref, wgl_ref,
                    h0_ref, u_ref, uc_ref, rc_ref, vt_ref, vs_ref, wob_ref, wglb_ref,
                    vbuf_scr, meta_scr, u_scr):
    s = pl.program_id(0)

    @pl.when(s == 0)
    def _():
        vbuf_scr[0:ROW8, :] = jnp.zeros((ROW8, D_CONV), F32)

    def tile(x_ref, is_extra):
        sub = SUB_IN if is_extra else TM
        rs = slice(0, sub)
        if is_extra:
            pad = slice(sub, TM)
            h0_ref[pad, :] = jnp.zeros((TM - sub, D_MODEL), F32)
            u_ref[pad, :] = jnp.zeros((TM - sub, D_SSM), F32)
            rc_ref[pad, :] = jnp.zeros((TM - sub, D_CONV), BF16)
        h0 = _layer_norm(x_ref[rs, :], g0_ref[...], b0_ref[...])
        h0_ref[rs, :] = h0
        proj = _dot(h0.astype(BF16), w_ref[...])
        part = lambda k: proj[:, k * D_SSM:(k + 1) * D_SSM]
        u_ref[rs, :] = part(0)
        for b in range(N_LANE_BLOCKS):
            u_scr[b, rs, :] = proj[:, b * SW:(b + 1) * SW]
            if is_extra:
                u_scr[b, sub:TM, :] = jnp.zeros((TM - sub, SW), F32)
        for b in range(N_LANE_BLOCKS):
            for h in range(2):
                src = [u_scr[b, pl.ds(8 * h + j, TILE_CHUNKS, stride=CHUNK), :] for j in range(GB)]
                for gl, w in enumerate(_block_transpose(src)):
                    uc_ref[b * GB + gl, :, h * SW:(h + 1) * SW] = w.astype(BF16)
        b_gate = part(1)
        v = part(2) * part(3)
        vbuf_scr[ROW8:ROW8 + sub, :] = v
        prev = vbuf_scr[0:ROW8 + sub, :]
        vm1 = pltpu.roll(prev, 1, axis=0)[ROW8:, :]
        vm2 = pltpu.roll(prev, 2, axis=0)[ROW8:, :]
        if is_extra:
            sample = lax.broadcasted_iota(jnp.int32, (sub, 1), 0) >= SAMPLE_ROW0
            vm1 = jnp.where(sample, p1_ref[...], vm1)
            vm2 = jnp.where(sample, p2_ref[...], vm2)
        conv = cw_ref[0:1, :] * vm2 + cw_ref[1:2, :] * vm1 + cw_ref[2:3, :] * v
        rc_ref[rs, :] = _rms_norm(b_gate * conv, gc_ref[...]).astype(BF16)

    @pl.when(s == 0)
    def _():
        tile(xe_ref, True)
        vs_ref[...] = vbuf_scr[ROW8:ROW8 + SUB_IN, :]
        meta_scr[...] = vbuf_scr[ROW8 + N_META - ROW8:ROW8 + N_META, :]
        vbuf_scr[0:ROW8, :] = meta_scr[...]

    @pl.when(s > 0)
    def _():
        tile(xp_ref, False)
        wob_ref[...] = wo_ref[...].astype(BF16)
        wglb_ref[...] = wgl_ref[...].astype(BF16)
        tail = vbuf_scr[TM:TM + ROW8, :]
        vt_ref[0] = tail
        vbuf_scr[0:ROW8, :] = jnp.where(s % TILES_PER_SEQ == 0, meta_scr[...], tail)


def _in_proj(xp, xe, g0, b0, w_in, p1, p2, cw, gc, w_out, w_glu):
    const = lambda s: (0, 0)
    slab = lambda s: (jnp.maximum(s - 1, 0), 0)
    row = lambda s: ((s + N_PROMPT_TILES) % N_TILES, 0)
    vec = lambda n: pl.BlockSpec((1, n), const)
    return pl.pallas_call(
        _in_proj_kernel,
        grid=(N_TILES,),
        in_specs=[pl.BlockSpec((TM, D_MODEL), lambda s: (jnp.maximum(s - 1, 0), 0)),
                  pl.BlockSpec((SUB_IN, D_MODEL), const),
                  vec(D_MODEL), vec(D_MODEL),
                  pl.BlockSpec((D_MODEL, 4 * D_SSM), const, pipeline_mode=pl.Buffered(1)),
                  pl.BlockSpec((SUB_IN, D_CONV), const), pl.BlockSpec((SUB_IN, D_CONV), const),
                  pl.BlockSpec((CONV_W, D_CONV), const), vec(D_CONV),
                  pl.BlockSpec((D_MODEL // N_PROMPT_TILES, D_MODEL), slab),
                  pl.BlockSpec((D_SSM // N_PROMPT_TILES, D_SSM), slab)],
        out_specs=[pl.BlockSpec((TM, D_MODEL), row), pl.BlockSpec((TM, D_SSM), row),
                   pl.BlockSpec((N_GROUPS, TILE_CHUNKS, CW), lambda s: (0, (s + N_PROMPT_TILES) % N_TILES, 0)),
                   pl.BlockSpec((TM, D_CONV), row),
                   pl.BlockSpec((1, ROW8, D_CONV), lambda s: (jnp.maximum(s - 1, 0), 0, 0)),
                   pl.BlockSpec((SUB_IN, D_CONV), const),
                   pl.BlockSpec((D_MODEL // N_PROMPT_TILES, D_MODEL), slab),
                   pl.BlockSpec((D_SSM // N_PROMPT_TILES, D_SSM), slab)],
        out_shape=[jax.ShapeDtypeStruct((M_PAD, D_MODEL), F32),
                   jax.ShapeDtypeStruct((M_PAD, D_SSM), F32),
                   jax.ShapeDtypeStruct((N_GROUPS, N_TILES * TILE_CHUNKS, CW), BF16),
                   jax.ShapeDtypeStruct((M_PAD, D_CONV), BF16),
                   jax.ShapeDtypeStruct((N_PROMPT_TILES, ROW8, D_CONV), F32),
                   jax.ShapeDtypeStruct((SUB_IN, D_CONV), F32),
                   jax.ShapeDtypeStruct((D_MODEL, D_MODEL), BF16),
                   jax.ShapeDtypeStruct((D_SSM, D_SSM), BF16)],
        scratch_shapes=[pltpu.VMEM((ROW8 + TM, D_CONV), F32), pltpu.VMEM((ROW8, D_CONV), F32),
                        pltpu.VMEM((N_LANE_BLOCKS, TM, SW), F32)],
        compiler_params=pltpu.CompilerParams(
            dimension_semantics=("arbitrary",), vmem_limit_bytes=VMEM_LIMIT),
        name="in_proj",
    )(xp, xe, g0, b0, w_in, p1, p2, cw, gc, w_out, w_glu)


SEQ_CHUNKS = SEQ // CHUNK
LHS_ROWS = N_TILES * TILE_CHUNKS
META_LHS = N_BATCH * SEQ_CHUNKS
PITCH = SEQ_CHUNKS + ROW8
REC_ROWS = N_BATCH * PITCH + ROW8
META_REC = N_BATCH * PITCH


def _ssm_scan_kernel(uc_ref, ue_ref, m_ref, wst_ref, cpow_ref, bpad_ref, cpad_ref, avec_ref, sre_ref, sim_ref,
                     yc_ref, ys_ref, sfin_ref, nre_ref, nim_ref, x_scr, xs_scr, sp_scr):
    def swap_halves(s):
        return pltpu.roll(s, N_STATE, axis=1)

    def to_rec(ref, gl, x):
        for n in range(N_BATCH):
            ref[gl, n * PITCH:n * PITCH + SEQ_CHUNKS, :] = x[n * SEQ_CHUNKS:(n + 1) * SEQ_CHUNKS]
        ref[gl, META_REC:META_REC + ROW8, :] = x[META_LHS:META_LHS + ROW8]

    for gl in range(GB):
        x = _dot(uc_ref[gl], wst_ref[gl])
        to_rec(x_scr, gl, x)
        to_rec(xs_scr, gl, swap_halves(x))

    ar16 = [avec_ref[gl, 0:1, :] for gl in range(GB)]
    ax16 = [avec_ref[gl, 1:2, :] for gl in range(GB)]

    def step(k, carry):
        new = []
        for gl in range(GB):
            s, t = carry[gl]
            rows = pl.ds(k, N_BATCH, stride=PITCH)
            sp_scr[gl, rows, :] = s
            new.append((ar16[gl] * s + ax16[gl] * t + x_scr[gl, rows, :],
                        ar16[gl] * t - ax16[gl] * s + xs_scr[gl, rows, :]))
        return tuple(new)

    first = lambda ref, gl: jnp.broadcast_to(ref[gl, META_REC:META_REC + 1, :], (N_BATCH, SW))
    init = tuple((first(x_scr, gl), first(xs_scr, gl)) for gl in range(GB))
    fin = lax.fori_loop(0, SEQ_CHUNKS, step, init)

    us = ue_ref[SAMPLE_ROW0:SAMPLE_ROW0 + N_SAMPLE, :].astype(BF16)
    ys = jnp.zeros((N_SAMPLE, SW), F32)
    for gl in range(GB):
        sfin_ref[gl] = fin[gl][0]
        sp = jnp.concatenate([sp_scr[gl, n * PITCH:n * PITCH + SEQ_CHUNKS, :] for n in range(N_BATCH)]
                             + [jnp.zeros((LHS_ROWS - META_LHS, SW), F32)], axis=0)
        yc_ref[gl] = _dot(uc_ref[gl], m_ref[gl]) + _dot_nt(sp.astype(BF16), cpow_ref[gl])
        lanes = slice(gl * N_STATE, (gl + 1) * N_STATE)
        s0 = jnp.concatenate([sre_ref[:, lanes], sim_ref[:, lanes]], axis=1)
        sn = (avec_ref[gl, 2:3, :] * s0 + avec_ref[gl, 3:4, :] * swap_halves(s0)
              + _dot(us, bpad_ref[gl]))
        nre_ref[:, lanes] = sn[:, 0:N_STATE]
        nim_ref[:, lanes] = sn[:, N_STATE:SW]
        ys = ys + _dot_nt(sn.astype(BF16), cpad_ref[gl])
    ys_ref[...] = ys


def _ssm_scan(uc, u, m, wst, cpow, bpad, cpad, avec, s_re, s_im):
    blk = lambda *s: pl.BlockSpec((GB,) + s, lambda i: (i,) + (0,) * len(s))
    extra = pl.BlockSpec((EXTRA_ROWS, SW), lambda i: (N_PROMPT_ROWS // EXTRA_ROWS, i))
    st = pl.BlockSpec((N_SAMPLE, GB * N_STATE), lambda i: (0, i))
    st_shape = jax.ShapeDtypeStruct((N_SAMPLE, N_GROUPS * N_STATE), F32)
    return pl.pallas_call(
        _ssm_scan_kernel,
        grid=(N_GROUPS // GB,),
        in_specs=[blk(LHS_ROWS, CW), extra, blk(CW, CW), blk(CW, SW), blk(CW, SW), blk(SW, SW), blk(SW, SW),
                  blk(4, SW), st, st],
        out_specs=[blk(LHS_ROWS, CW), pl.BlockSpec((N_SAMPLE, SW), lambda i: (0, i)), blk(N_BATCH, SW), st, st],
        out_shape=[jax.ShapeDtypeStruct((N_GROUPS, LHS_ROWS, CW), F32),
                   jax.ShapeDtypeStruct((N_SAMPLE, D_SSM), F32),
                   jax.ShapeDtypeStruct((N_GROUPS, N_BATCH, SW), F32), st_shape, st_shape],
        scratch_shapes=[pltpu.VMEM((GB, REC_ROWS, SW), F32), pltpu.VMEM((GB, REC_ROWS, SW), F32),
                        pltpu.VMEM((GB, REC_ROWS, SW), F32)],
        compiler_params=pltpu.CompilerParams(
            dimension_semantics=("arbitrary",), vmem_limit_bytes=VMEM_LIMIT),
        name="ssm_scan",
    )(uc, u, m, wst, cpow, bpad, cpad, avec, s_re, s_im)


SUB_MIX = 256


def _mix_out_kernel(yc0_ref, ycn_ref, ys_ref, u_ref, rc_ref, h0_ref, dskip_ref, wglu_ref, gs_ref, wout_ref,
                    g1_ref, b1_ref, x1_ref, x1b_ref, merged_scr, y_scr, ynext_scr):
    i = pl.program_id(0)

    def to_rows(yc_ref, dst_scr):
        for b in range(N_LANE_BLOCKS):
            for h in range(2):
                src = [yc_ref[b * GB + gl, :, h * SW:(h + 1) * SW] for gl in range(GB)]
                for j, w in enumerate(_block_transpose(src)):
                    dst_scr[b, pl.ds(8 * h + j, TILE_CHUNKS, stride=CHUNK), :] = w

    def sub_tile(r0):
        rs = slice(r0, r0 + SUB_MIX)
        y = jnp.concatenate([y_scr[b, rs, :] for b in range(N_LANE_BLOCKS)], axis=1)
        ys = _gelu_tanh(y + dskip_ref[...] * u_ref[rs, :])
        gate = _dot(ys.astype(BF16), wglu_ref[...])
        merged_scr[rs, 0:D_SSM] = _rms_norm(ys * _sigmoid(gate), gs_ref[...]).astype(BF16)
        merged_scr[rs, D_SSM:D_MODEL] = rc_ref[rs, :]
        mo = _dot(merged_scr[rs, :], wout_ref[...])
        x1 = _layer_norm(ALPHA * h0_ref[rs, :] + mo, g1_ref[...], b1_ref[...])
        x1_ref[rs, :] = x1
        x1b_ref[rs, :] = x1.astype(BF16)

    @pl.when(i == 0)
    def _():
        to_rows(yc0_ref, y_scr)

    @pl.when(i < N_PROMPT_TILES)
    def _():
        for r0 in range(0, TM, SUB_MIX):
            sub_tile(r0)
        to_rows(ycn_ref, ynext_scr)
        y_scr[...] = ynext_scr[...]

    @pl.when(i == N_PROMPT_TILES - 1)
    def _():
        for b in range(N_LANE_BLOCKS):
            y_scr[b, SAMPLE_ROW0:SAMPLE_ROW0 + N_SAMPLE, :] = ys_ref[:, b * SW:(b + 1) * SW]

    @pl.when(i == N_PROMPT_TILES)
    def _():
        sub_tile(0)
        x1_ref[SUB_MIX:TM, :] = jnp.zeros((TM - SUB_MIX, D_MODEL), F32)
        x1b_ref[SUB_MIX:TM, :] = jnp.zeros((TM - SUB_MIX, D_MODEL), BF16)


def _mix_out(yc, ys, u, rc, h0, dskip, wglu, gs, wout, g1, b1):
    row = lambda i: (i, 0)
    const = lambda i: (0, 0)
    vec = lambda n: pl.BlockSpec((1, n), const)
    chunk_blk = lambda f: pl.BlockSpec((N_GROUPS, TILE_CHUNKS, CW), f)
    return pl.pallas_call(
        _mix_out_kernel,
        grid=(N_TILES,),
        in_specs=[chunk_blk(lambda i: (0, 0, 0)),
                  chunk_blk(lambda i: (0, jnp.minimum(i + 1, N_PROMPT_TILES), 0)),
                  pl.BlockSpec((N_SAMPLE, D_SSM), const), pl.BlockSpec((TM, D_SSM), row),
                  pl.BlockSpec((TM, D_CONV), row), pl.BlockSpec((TM, D_MODEL), row), vec(D_SSM),
                  pl.BlockSpec((D_SSM, D_SSM), const, pipeline_mode=pl.Buffered(1)),
                  vec(D_SSM),
                  pl.BlockSpec((D_MODEL, D_MODEL), const, pipeline_mode=pl.Buffered(1)),
                  vec(D_MODEL), vec(D_MODEL)],
        out_specs=[pl.BlockSpec((TM, D_MODEL), row), pl.BlockSpec((TM, D_MODEL), row)],
        out_shape=[jax.ShapeDtypeStruct((M_PAD, D_MODEL), F32),
                   jax.ShapeDtypeStruct((M_PAD, D_MODEL), BF16)],
        scratch_shapes=[pltpu.VMEM((TM, D_MODEL), BF16), pltpu.VMEM((N_LANE_BLOCKS, TM, SW), F32),
                        pltpu.VMEM((N_LANE_BLOCKS, TM, SW), F32)],
        compiler_params=pltpu.CompilerParams(
            dimension_semantics=("arbitrary",), vmem_limit_bytes=VMEM_LIMIT),
        name="mix_out",
    )(yc, yc, ys, u, rc, h0, dskip, wglu, gs, wout, g1, b1)


M_HALF = M_PAD // 2
SUB_UP = M_HALF // 4
TF_UP = 256
SUB_DOWN = 256
M_REAL_HALF = M_HALF - (TM - EXTRA_ROWS)


def _ffn_up_kernel(xb_ref, wg_ref, wu_ref, wd_ref, act_ref, wdb_ref):
    @pl.when(pl.program_id(0) == 0)
    def _():
        wdb_ref[...] = wd_ref[...].astype(BF16)

    wgb = wg_ref[...].astype(BF16)
    wub = wu_ref[...].astype(BF16)

    def rows(lo, hi):
        xb = xb_ref[lo:hi, :]
        gate = _dot(xb, wgb)
        up = _dot(xb, wub)
        act_ref[lo:hi, :] = (gate * _sigmoid(gate) * up).astype(BF16)

    @pl.when(pl.program_id(0) == 0)
    def _():
        for r0 in range(0, M_HALF, SUB_UP):
            rows(r0, r0 + SUB_UP)

    @pl.when(pl.program_id(0) == 1)
    def _():
        for r0 in range(0, M_REAL_HALF, SUB_UP):
            rows(r0, min(r0 + SUB_UP, M_REAL_HALF))
        act_ref[M_REAL_HALF:M_HALF, :] = jnp.zeros((M_HALF - M_REAL_HALF, TF_UP), BF16)


def _ffn_up(x1b, wg, wu, wd):
    wcol = pl.BlockSpec((D_MODEL, TF_UP), lambda h, f: (0, f))
    wrow = pl.BlockSpec((TF_UP, D_MODEL), lambda h, f: (jnp.where(h == 0, f, D_FF // TF_UP - 1), 0))
    return pl.pallas_call(
        _ffn_up_kernel,
        grid=(M_PAD // M_HALF, D_FF // TF_UP),
        in_specs=[pl.BlockSpec((M_HALF, D_MODEL), lambda h, f: (h, 0), pipeline_mode=pl.Buffered(1)),
                  wcol, wcol, wrow],
        out_specs=[pl.BlockSpec((M_HALF, TF_UP), lambda h, f: (h, f)), wrow],
        out_shape=[jax.ShapeDtypeStruct((M_PAD, D_FF), BF16),
                   jax.ShapeDtypeStruct((D_FF, D_MODEL), BF16)],
        compiler_params=pltpu.CompilerParams(
            dimension_semantics=("arbitrary", "arbitrary"), vmem_limit_bytes=VMEM_LIMIT),
        name="ffn_up",
    )(x1b, wg, wu, wd)


def _ffn_down_kernel(act_ref, x_ref, wd_ref, g2_ref, b2_ref, yp_ref, ye_ref):
    i = pl.program_id(0)

    def tile(o_ref, rows):
        for r0 in range(0, rows, SUB_DOWN):
            rs = slice(r0, r0 + SUB_DOWN)
            z = ALPHA * x_ref[rs, :] + _dot(act_ref[rs, :], wd_ref[...])
            o_ref[rs, :] = _layer_norm(z, g2_ref[...], b2_ref[...])

    @pl.when(i < N_PROMPT_TILES)
    def _():
        tile(yp_ref, TM)

    @pl.when(i == N_PROMPT_TILES)
    def _():
        tile(ye_ref, EXTRA_ROWS)


def _ffn_down(act, x1, wd, g2, b2):
    row = lambda i: (i, 0)
    const = lambda i: (0, 0)
    return pl.pallas_call(
        _ffn_down_kernel,
        grid=(N_TILES,),
        in_specs=[pl.BlockSpec((TM, D_FF), row), pl.BlockSpec((TM, D_MODEL), row),
                  pl.BlockSpec((D_FF, D_MODEL), const, pipeline_mode=pl.Buffered(1)),
                  pl.BlockSpec((1, D_MODEL), const), pl.BlockSpec((1, D_MODEL), const)],
        out_specs=[pl.BlockSpec((TM, D_MODEL), lambda i: (jnp.minimum(i, N_PROMPT_TILES - 1), 0)),
                   pl.BlockSpec((EXTRA_ROWS, D_MODEL), const)],
        out_shape=[jax.ShapeDtypeStruct((N_PROMPT_ROWS, D_MODEL), F32),
                   jax.ShapeDtypeStruct((EXTRA_ROWS, D_MODEL), F32)],
        compiler_params=pltpu.CompilerParams(
            dimension_semantics=("arbitrary",), vmem_limit_bytes=VMEM_LIMIT),
        name="ffn_down",
    )(act, x1, wd, g2, b2)


def _pad_rows(a, n):
    return jnp.pad(a, ((0, n - a.shape[0]), (0, 0)))


def kernel(x_prompt, x_sample, state_ssm_re, state_ssm_im, state_conv, meta_tokens, ln0_g, ln0_b,
           w_in, ssm_a_re, ssm_a_im, ssm_log_dt, ssm_b_re, ssm_b_im, ssm_c_re, ssm_c_im, ssm_d,
           ssm_w_glu, conv_w, g_ssm_out, g_conv_out, w_out, ln1_g, ln1_b, w_gate, w_up, w_down,
           ln2_g, ln2_b):
    assert x_prompt.shape == (N_BATCH, SEQ, D_MODEL) and x_sample.shape == (N_SAMPLE, 1, D_MODEL)
    assert w_in.shape[0] == 1, "single layer"
    g, p, c = N_GROUPS, N_STATE, GROUP
    vec = lambda a: a.reshape(1, -1)

    xp = x_prompt.reshape(N_PROMPT_ROWS, D_MODEL)
    xe = _pad_rows(jnp.concatenate([meta_tokens, x_sample[:, 0, :]], axis=0), SUB_IN)
    g0, b0 = vec(ln0_g), vec(ln0_b)

    m, wst, cpow, bpad, cpad, avec, w_in_bf = _ssm_prep(ssm_log_dt[0], ssm_a_re[0], ssm_a_im[0], ssm_b_re[0],
                                                        ssm_b_im[0], ssm_c_re[0], ssm_c_im[0], w_in[0])

    prev = jnp.pad(state_conv[0], ((SAMPLE_ROW0, SUB_IN - SAMPLE_ROW0 - N_SAMPLE), (0, 0), (0, 0)))
    h0, u, uc, rc, v_tail, v_extra, wout_bf, wglu_bf = _in_proj(
        xp, xe, g0, b0, w_in_bf, prev[:, 1], prev[:, 0], conv_w[0], vec(g_conv_out[0]),
        w_out[0], ssm_w_glu[0])

    flat = lambda a: a.reshape(N_SAMPLE, g * p)
    yc, ys, s_fin, ns_re, ns_im = _ssm_scan(uc, u, m, wst, cpow, bpad, cpad, avec,
                                            flat(state_ssm_re[0]), flat(state_ssm_im[0]))

    x1, x1b = _mix_out(yc, ys, u, rc, h0, vec(ssm_d[0]), wglu_bf, vec(g_ssm_out[0]),
                       wout_bf, vec(ln1_g[0]), vec(ln1_b[0]))

    act, wd_bf = _ffn_up(x1b, w_gate[0], w_up[0], w_down[0])
    yp, ye = _ffn_down(act, x1, wd_bf, vec(ln2_g[0]), vec(ln2_b[0]))

    y_prompt = yp.reshape(N_BATCH, SEQ, D_MODEL)
    y_sample = ye[SAMPLE_ROW0:SAMPLE_ROW0 + N_SAMPLE].reshape(N_SAMPLE, 1, D_MODEL)

    sd = state_ssm_re.dtype
    s_p = s_fin.transpose(1, 0, 2)
    unflat = lambda a: a.reshape(1, N_SAMPLE, g, p).astype(sd)
    cd = state_conv.dtype
    conv_p = v_tail[TILES_PER_SEQ - 1::TILES_PER_SEQ, ROW8 - (CONV_W - 1):].astype(cd)
    conv_s = jnp.stack([state_conv[0][:, 1], v_extra[SAMPLE_ROW0:SAMPLE_ROW0 + N_SAMPLE].astype(cd)], axis=1)
    return (y_prompt, y_sample,
            s_p[None, ..., :p].astype(sd), s_p[None, ..., p:].astype(sd), conv_p[None],
            unflat(ns_re), unflat(ns_im), conv_s[None])
```

```python
import jax
import jax.numpy as jnp
import numpy as np
from jax import lax
from jax.experimental import pallas as pl
from jax.experimental.pallas import tpu as pltpu

D_MODEL = 2048
D_SSM = 1024
D_CONV = 1024
N_GROUPS = 64
GROUP = 16
N_STATE = 64
N_META = 16
CONV_W = 3
D_FF = 5632
LN_EPS = 1e-5
RMS_EPS = 1e-6
ALPHA = 2.0 ** 0.25

N_BATCH = 4
SEQ = 2048
N_SAMPLE = 128
CHUNK = 16
ROW8 = 8
CW = CHUNK * GROUP
SW = 2 * N_STATE

TM = 512
N_PROMPT_ROWS = N_BATCH * SEQ
N_PROMPT_TILES = N_PROMPT_ROWS // TM
N_TILES = N_PROMPT_TILES + 1
M_PAD = N_TILES * TM
SAMPLE_ROW0 = N_META
GB = SW // GROUP
N_LANE_BLOCKS = D_SSM // SW
TILE_CHUNKS = TM // CHUNK
EXTRA_ROWS = 256
VMEM_LIMIT = 60 * 1024 * 1024

F32 = jnp.float32
BF16 = jnp.bfloat16


def _layer_norm(x, g, b):
    mu = jnp.mean(x, axis=-1, keepdims=True)
    xc = x - mu
    var = jnp.mean(xc * xc, axis=-1, keepdims=True)
    return xc * lax.rsqrt(var + LN_EPS) * g + b


def _rms_norm(x, g):
    return x * lax.rsqrt(jnp.mean(x * x, axis=-1, keepdims=True) + RMS_EPS) * g


def _gelu_tanh(x):
    c = np.sqrt(2.0 / np.pi)
    hx = 0.5 * x
    return hx + hx * jnp.tanh(x * (np.float32(c) + np.float32(c * 0.044715) * (x * x)))


def _sigmoid(x):
    return 1.0 / (1.0 + jnp.exp2(x * np.float32(-np.log2(np.e))))


def _dot(a, b):
    return jnp.dot(a, b, preferred_element_type=F32)


def _dot_nt(a, b):
    return lax.dot_general(a, b, (((1,), (1,)), ((), ())), preferred_element_type=F32)


def _block_transpose(a):
    a = list(a)
    blk = lax.broadcasted_iota(jnp.int32, a[0].shape, 1) // GROUP
    for d in (4, 2, 1):
        w = d * GROUP
        keep = (blk & d) == 0
        for j in range(GB):
            if j & d == 0:
                lo, hi = a[j], a[j + d]
                a[j] = jnp.where(keep, lo, pltpu.roll(hi, w, axis=1))
                a[j + d] = jnp.where(keep, pltpu.roll(lo, SW - w, axis=1), hi)
    return a


def _ssm_prep_kernel(logdt_ref, are_ref, aim_ref, b1_ref, b2_ref, c1_ref, c2_ref, win_ref,
                     m_ref, wst_ref, cpow_ref, bpad_ref, cpad_ref, avec_ref, winb_ref):
    winb_ref[...] = win_ref[...].astype(BF16)
    dt = jnp.exp(logdt_ref[...])
    a_re, a_im = are_ref[...], aim_ref[...]
    mag = jnp.exp(dt * a_re)
    ang = dt * a_im
    ab_re = mag * jnp.cos(ang)
    ab_im = mag * jnp.sin(ang)
    n_re = ab_re - 1.0
    den = a_re * a_re + a_im * a_im
    f_re = (n_re * a_re + ab_im * a_im) / den
    f_im = (ab_im * a_re - n_re * a_im) / den
    sgn = jnp.where(lax.broadcasted_iota(jnp.int32, (GB, SW), 1) < N_STATE, -1.0, 1.0)

    pr, pi = jnp.ones_like(ab_re), jnp.zeros_like(ab_re)
    pw = [(pr, pi)]
    for _ in range(CHUNK):
        pr, pi = pr * ab_re - pi * ab_im, pr * ab_im + pi * ab_re
        pw.append((pr, sgn * pi))
    fx = sgn * f_im
    lane = lax.broadcasted_iota(jnp.int32, (GROUP, CW), 1)

    for gl in range(GB):
        row = lambda x: x[gl:gl + 1, :]
        bb1 = row(f_re) * b1_ref[gl] + row(fx) * b2_ref[gl]
        bb2 = row(f_re) * b2_ref[gl] - row(fx) * b1_ref[gl]
        cc1 = -row(sgn) * c1_ref[gl]
        cc2 = -row(sgn) * c2_ref[gl]

        cq = []
        for s in range(CHUNK):
            wr, wx = pw[CHUNK - 1 - s]
            wst_ref[gl, s * GROUP:(s + 1) * GROUP, :] = (row(wr) * bb1 + row(wx) * bb2).astype(BF16)
            qr, qx = pw[s + 1]
            cpow_ref[gl, s * GROUP:(s + 1) * GROUP, :] = (row(qr) * cc1 + row(qx) * cc2).astype(BF16)
            dr, dx = pw[s]
            cq.append(row(dr) * cc1 + row(dx) * cc2)
        kt = lax.dot_general(bb1, jnp.concatenate(cq, axis=0), (((1,), (1,)), ((), ())),
                             precision=lax.Precision.HIGHEST, preferred_element_type=F32)
        for i in range(CHUNK):
            blk = kt if i == 0 else jnp.where(lane >= i * GROUP, pltpu.roll(kt, i * GROUP, axis=1), 0.0)
            m_ref[gl, i * GROUP:(i + 1) * GROUP, :] = blk.astype(BF16)

        zeros = jnp.zeros((SW, SW), BF16)
        bpad_ref[gl] = zeros
        cpad_ref[gl] = zeros
        bpad_ref[gl, gl * GROUP:(gl + 1) * GROUP, :] = bb1.astype(BF16)
        cpad_ref[gl, gl * GROUP:(gl + 1) * GROUP, :] = cq[0].astype(BF16)

        avec_ref[gl, 0:1, :] = row(pw[CHUNK][0])
        avec_ref[gl, 1:2, :] = row(pw[CHUNK][1])
        avec_ref[gl, 2:3, :] = row(pw[1][0])
        avec_ref[gl, 3:4, :] = row(pw[1][1])


def _ssm_prep(log_dt, a_re, a_im, b_re, b_im, c_re, c_im, w_in):
    g, c = N_GROUPS, GROUP
    dup = lambda x: jnp.concatenate([x, x], axis=-1)
    bt_re, bt_im = b_re.transpose(0, 2, 1), b_im.transpose(0, 2, 1)
    cat = lambda x, y: jnp.concatenate([x, y], axis=-1)
    vec = pl.BlockSpec((GB, SW), lambda i: (i, 0))
    blk = lambda *s: pl.BlockSpec((GB,) + s, lambda i: (i,) + (0,) * len(s))
    slab = pl.BlockSpec((w_in.shape[0] // (g // GB), w_in.shape[1]), lambda i: (i, 0))
    return pl.pallas_call(
        _ssm_prep_kernel,
        grid=(g // GB,),
        in_specs=[vec, vec, vec, blk(c, SW), blk(c, SW), blk(c, SW), blk(c, SW), slab],
        out_specs=[blk(CW, CW), blk(CW, SW), blk(CW, SW), blk(SW, SW), blk(SW, SW), blk(4, SW), slab],
        out_shape=[jax.ShapeDtypeStruct((g, CW, CW), BF16),
                   jax.ShapeDtypeStruct((g, CW, SW), BF16),
                   jax.ShapeDtypeStruct((g, CW, SW), BF16),
                   jax.ShapeDtypeStruct((g, SW, SW), BF16),
                   jax.ShapeDtypeStruct((g, SW, SW), BF16),
                   jax.ShapeDtypeStruct((g, 4, SW), F32),
                   jax.ShapeDtypeStruct(w_in.shape, BF16)],
        name="ssm_prep",
    )(jnp.broadcast_to(log_dt[:, None], (g, SW)), dup(a_re), dup(a_im),
      cat(bt_re, bt_im), cat(bt_im, bt_re), cat(c_re, c_im), cat(c_im, c_re), w_in)


SUB_IN = EXTRA_ROWS

TILES_PER_SEQ = SEQ // TM


def _in_proj_kernel(xp_ref, xe_ref, g0_ref, b0_ref, w_ref, p1_ref, p2_ref, cw_ref, gc_ref, wo_ref, wgl_ref,
                    h0_ref, u_ref, uc_ref, rc_ref, vt_ref, vs_ref, wob_ref, wglb_ref,
                    vbuf_scr, meta_scr, u_scr):
    s = pl.program_id(0)

    @pl.when(s == 0)
    def _():
        vbuf_scr[0:ROW8, :] = jnp.zeros((ROW8, D_CONV), F32)

    def tile(x_ref, is_extra):
        sub = SUB_IN if is_extra else TM
        rs = slice(0, sub)
        if is_extra:
            pad = slice(sub, TM)
            h0_ref[pad, :] = jnp.zeros((TM - sub, D_MODEL), F32)
            u_ref[pad, :] = jnp.zeros((TM - sub, D_SSM), F32)
            rc_ref[pad, :] = jnp.zeros((TM - sub, D_CONV), BF16)
        h0 = _layer_norm(x_ref[rs, :], g0_ref[...], b0_ref[...])
        h0_ref[rs, :] = h0
        proj = _dot(h0.astype(BF16), w_ref[...])
        part = lambda k: proj[:, k * D_SSM:(k + 1) * D_SSM]
        u_ref[rs, :] = part(0)
        for b in range(N_LANE_BLOCKS):
            u_scr[b, rs, :] = proj[:, b * SW:(b + 1) * SW]
            if is_extra:
                u_scr[b, sub:TM, :] = jnp.zeros((TM - sub, SW), F32)
        for b in range(N_LANE_BLOCKS):
            for h in range(2):
                src = [u_scr[b, pl.ds(GB * h + j, TILE_CHUNKS, stride=CHUNK), :] for j in range(GB)]
                for gl, w in enumerate(_block_transpose(src)):
                    uc_ref[b * GB + gl, :, h * SW:(h + 1) * SW] = w.astype(BF16)
        b_gate = part(1)
        v = part(2) * part(3)
        vbuf_scr[ROW8:ROW8 + sub, :] = v
        prev = vbuf_scr[0:ROW8 + sub, :]
        vm1 = pltpu.roll(prev, 1, axis=0)[ROW8:, :]
        vm2 = pltpu.roll(prev, 2, axis=0)[ROW8:, :]
        if is_extra:
            sample = lax.broadcasted_iota(jnp.int32, (sub, 1), 0) >= SAMPLE_ROW0
            vm1 = jnp.where(sample, p1_ref[...], vm1)
            vm2 = jnp.where(sample, p2_ref[...], vm2)
        conv = cw_ref[0:1, :] * vm2 + cw_ref[1:2, :] * vm1 + cw_ref[2:3, :] * v
        rc_ref[rs, :] = _rms_norm(b_gate * conv, gc_ref[...]).astype(BF16)

    @pl.when(s == 0)
    def _():
        tile(xe_ref, True)
        vs_ref[...] = vbuf_scr[ROW8:ROW8 + SUB_IN, :]
        meta_scr[...] = vbuf_scr[ROW8 + N_META - ROW8:ROW8 + N_META, :]
        vbuf_scr[0:ROW8, :] = meta_scr[...]

    @pl.when(s > 0)
    def _():
        tile(xp_ref, False)
        wob_ref[...] = wo_ref[...].astype(BF16)
        wglb_ref[...] = wgl_ref[...].astype(BF16)
        tail = vbuf_scr[TM:TM + ROW8, :]
        vt_ref[0] = tail
        vbuf_scr[0:ROW8, :] = jnp.where(s % TILES_PER_SEQ == 0, meta_scr[...], tail)


def _in_proj(xp, xe, g0, b0, w_in, p1, p2, cw, gc, w_out, w_glu):
    const = lambda s: (0, 0)
    slab = lambda s: (jnp.maximum(s - 1, 0), 0)
    row = lambda s: ((s + N_PROMPT_TILES) % N_TILES, 0)
    vec = lambda n: pl.BlockSpec((1, n), const)
    return pl.pallas_call(
        _in_proj_kernel,
        grid=(N_TILES,),
        in_specs=[pl.BlockSpec((TM, D_MODEL), lambda s: (jnp.maximum(s - 1, 0), 0)),
                  pl.BlockSpec((SUB_IN, D_MODEL), const),
                  vec(D_MODEL), vec(D_MODEL),
                  pl.BlockSpec((D_MODEL, 4 * D_SSM), const, pipeline_mode=pl.Buffered(1)),
                  pl.BlockSpec((SUB_IN, D_CONV), const), pl.BlockSpec((SUB_IN, D_CONV), const),
                  pl.BlockSpec((CONV_W, D_CONV), const), vec(D_CONV),
                  pl.BlockSpec((D_MODEL // N_PROMPT_TILES, D_MODEL), slab),
                  pl.BlockSpec((D_SSM // N_PROMPT_TILES, D_SSM), slab)],
        out_specs=[pl.BlockSpec((TM, D_MODEL), row), pl.BlockSpec((TM, D_SSM), row),
                   pl.BlockSpec((N_GROUPS, TILE_CHUNKS, CW), lambda s: (0, (s + N_PROMPT_TILES) % N_TILES, 0)),
                   pl.BlockSpec((TM, D_CONV), row),
                   pl.BlockSpec((1, ROW8, D_CONV), lambda s: (jnp.maximum(s - 1, 0), 0, 0)),
                   pl.BlockSpec((SUB_IN, D_CONV), const),
                   pl.BlockSpec((D_MODEL // N_PROMPT_TILES, D_MODEL), slab),
                   pl.BlockSpec((D_SSM // N_PROMPT_TILES, D_SSM), slab)],
        out_shape=[jax.ShapeDtypeStruct((M_PAD, D_MODEL), F32),
                   jax.ShapeDtypeStruct((M_PAD, D_SSM), F32),
                   jax.ShapeDtypeStruct((N_GROUPS, N_TILES * TILE_CHUNKS, CW), BF16),
                   jax.ShapeDtypeStruct((M_PAD, D_CONV), BF16),
                   jax.ShapeDtypeStruct((N_PROMPT_TILES, ROW8, D_CONV), F32),
                   jax.ShapeDtypeStruct((SUB_IN, D_CONV), F32),
                   jax.ShapeDtypeStruct((D_MODEL, D_MODEL), BF16),
                   jax.ShapeDtypeStruct((D_SSM, D_SSM), BF16)],
        scratch_shapes=[pltpu.VMEM((ROW8 + TM, D_CONV), F32), pltpu.VMEM((ROW8, D_CONV), F32),
                        pltpu.VMEM((N_LANE_BLOCKS, TM, SW), F32)],
        compiler_params=pltpu.CompilerParams(
            dimension_semantics=("arbitrary",), vmem_limit_bytes=VMEM_LIMIT),
        name="in_proj",
    )(xp, xe, g0, b0, w_in, p1, p2, cw, gc, w_out, w_glu)


SEQ_CHUNKS = SEQ // CHUNK
LHS_ROWS = N_TILES * TILE_CHUNKS
META_LHS = N_BATCH * SEQ_CHUNKS
PITCH = SEQ_CHUNKS + ROW8
REC_ROWS = N_BATCH * PITCH + ROW8
META_REC = N_BATCH * PITCH


def _ssm_scan_kernel(uc_ref, ue_ref, m_ref, wst_ref, cpow_ref, bpad_ref, cpad_ref, avec_ref, sre_ref, sim_ref,
                     yc_ref, ys_ref, sfin_ref, nre_ref, nim_ref, x_scr, xs_scr, sp_scr):
    def swap_halves(s):
        return pltpu.roll(s, N_STATE, axis=1)

    def to_rec(ref, gl, x):
        for n in range(N_BATCH):
            ref[gl, n * PITCH:n * PITCH + SEQ_CHUNKS, :] = x[n * SEQ_CHUNKS:(n + 1) * SEQ_CHUNKS]
        ref[gl, META_REC:META_REC + ROW8, :] = x[META_LHS:META_LHS + ROW8]

    for gl in range(GB):
        x = _dot(uc_ref[gl], wst_ref[gl])
        to_rec(x_scr, gl, x)
        to_rec(xs_scr, gl, swap_halves(x))

    ar16 = [avec_ref[gl, 0:1, :] for gl in range(GB)]
    ax16 = [avec_ref[gl, 1:2, :] for gl in range(GB)]

    def step(k, carry):
        new = []
        for gl in range(GB):
            s, t = carry[gl]
            rows = pl.ds(k, N_BATCH, stride=PITCH)
            sp_scr[gl, rows, :] = s
            new.append((ar16[gl] * s + ax16[gl] * t + x_scr[gl, rows, :],
                        ar16[gl] * t - ax16[gl] * s + xs_scr[gl, rows, :]))
        return tuple(new)

    first = lambda ref, gl: jnp.broadcast_to(ref[gl, META_REC:META_REC + 1, :], (N_BATCH, SW))
    init = tuple((first(x_scr, gl), first(xs_scr, gl)) for gl in range(GB))
    fin = lax.fori_loop(0, SEQ_CHUNKS, step, init)

    us = ue_ref[SAMPLE_ROW0:SAMPLE_ROW0 + N_SAMPLE, :].astype(BF16)
    ys = jnp.zeros((N_SAMPLE, SW), F32)
    for gl in range(GB):
        sfin_ref[gl] = fin[gl][0]
        sp = jnp.concatenate([sp_scr[gl, n * PITCH:n * PITCH + SEQ_CHUNKS, :] for n in range(N_BATCH)]
                             + [jnp.zeros((LHS_ROWS - META_LHS, SW), F32)], axis=0)
        yc_ref[gl] = _dot(uc_ref[gl], m_ref[gl]) + _dot_nt(sp.astype(BF16), cpow_ref[gl])
        lanes = slice(gl * N_STATE, (gl + 1) * N_STATE)
        s0 = jnp.concatenate([sre_ref[:, lanes], sim_ref[:, lanes]], axis=1)
        sn = (avec_ref[gl, 2:3, :] * s0 + avec_ref[gl, 3:4, :] * swap_halves(s0)
              + _dot(us, bpad_ref[gl]))
        nre_ref[:, lanes] = sn[:, 0:N_STATE]
        nim_ref[:, lanes] = sn[:, N_STATE:SW]
        ys = ys + _dot_nt(sn.astype(BF16), cpad_ref[gl])
    ys_ref[...] = ys


def _ssm_scan(uc, u, m, wst, cpow, bpad, cpad, avec, s_re, s_im):
    blk = lambda *s: pl.BlockSpec((GB,) + s, lambda i: (i,) + (0,) * len(s))
    extra = pl.BlockSpec((EXTRA_ROWS, SW), lambda i: (N_PROMPT_ROWS // EXTRA_ROWS, i))
    st = pl.BlockSpec((N_SAMPLE, GB * N_STATE), lambda i: (0, i))
    st_shape = jax.ShapeDtypeStruct((N_SAMPLE, N_GROUPS * N_STATE), F32)
    return pl.pallas_call(
        _ssm_scan_kernel,
        grid=(N_GROUPS // GB,),
        in_specs=[blk(LHS_ROWS, CW), extra, blk(CW, CW), blk(CW, SW), blk(CW, SW), blk(SW, SW), blk(SW, SW),
                  blk(4, SW), st, st],
        out_specs=[blk(LHS_ROWS, CW), pl.BlockSpec((N_SAMPLE, SW), lambda i: (0, i)), blk(N_BATCH, SW), st, st],
        out_shape=[jax.ShapeDtypeStruct((N_GROUPS, LHS_ROWS, CW), F32),
                   jax.ShapeDtypeStruct((N_SAMPLE, D_SSM), F32),
                   jax.ShapeDtypeStruct((N_GROUPS, N_BATCH, SW), F32), st_shape, st_shape],
        scratch_shapes=[pltpu.VMEM((GB, REC_ROWS, SW), F32), pltpu.VMEM((GB, REC_ROWS, SW), F32),
                        pltpu.VMEM((GB, REC_ROWS, SW), F32)],
        compiler_params=pltpu.CompilerParams(
            dimension_semantics=("arbitrary",), vmem_limit_bytes=VMEM_LIMIT),
        name="ssm_scan",
    )(uc, u, m, wst, cpow, bpad, cpad, avec, s_re, s_im)


SUB_MIX = 512


def _mix_out_kernel(yc0_ref, ycn_ref, ys_ref, u_ref, rc_ref, h0_ref, dskip_ref, wglu_ref, gs_ref, wout_ref,
                    g1_ref, b1_ref, x1_ref, x1b_ref, merged_scr, y_scr, ynext_scr):
    i = pl.program_id(0)

    def to_rows(yc_ref, dst_scr):
        for b in range(N_LANE_BLOCKS):
            for h in range(2):
                src = [yc_ref[b * GB + gl, :, h * SW:(h + 1) * SW] for gl in range(GB)]
                for j, w in enumerate(_block_transpose(src)):
                    dst_scr[b, pl.ds(GB * h + j, TILE_CHUNKS, stride=CHUNK), :] = w

    def sub_tile(r0, n):
        rs = slice(r0, r0 + n)
        y = jnp.concatenate([y_scr[b, rs, :] for b in range(N_LANE_BLOCKS)], axis=1)
        ys = _gelu_tanh(y + dskip_ref[...] * u_ref[rs, :])
        gate = _dot(ys.astype(BF16), wglu_ref[...])
        merged_scr[rs, 0:D_SSM] = _rms_norm(ys * _sigmoid(gate), gs_ref[...]).astype(BF16)
        merged_scr[rs, D_SSM:D_MODEL] = rc_ref[rs, :]
        mo = _dot(merged_scr[rs, :], wout_ref[...])
        x1 = _layer_norm(ALPHA * h0_ref[rs, :] + mo, g1_ref[...], b1_ref[...])
        x1_ref[rs, :] = x1
        x1b_ref[rs, :] = x1.astype(BF16)

    @pl.when(i == 0)
    def _():
        to_rows(yc0_ref, y_scr)

    @pl.when(i < N_PROMPT_TILES)
    def _():
        for r0 in range(0, TM, SUB_MIX):
            sub_tile(r0, SUB_MIX)
        to_rows(ycn_ref, ynext_scr)
        y_scr[...] = ynext_scr[...]

    @pl.when(i == N_PROMPT_TILES - 1)
    def _():
        for b in range(N_LANE_BLOCKS):
            y_scr[b, SAMPLE_ROW0:SAMPLE_ROW0 + N_SAMPLE, :] = ys_ref[:, b * SW:(b + 1) * SW]

    @pl.when(i == N_PROMPT_TILES)
    def _():
        sub_tile(0, EXTRA_ROWS)
        x1_ref[EXTRA_ROWS:TM, :] = jnp.zeros((TM - EXTRA_ROWS, D_MODEL), F32)
        x1b_ref[EXTRA_ROWS:TM, :] = jnp.zeros((TM - EXTRA_ROWS, D_MODEL), BF16)


def _mix_out(yc, ys, u, rc, h0, dskip, wglu, gs, wout, g1, b1):
    row = lambda i: (i, 0)
    const = lambda i: (0, 0)
    vec = lambda n: pl.BlockSpec((1, n), const)
    chunk_blk = lambda f: pl.BlockSpec((N_GROUPS, TILE_CHUNKS, CW), f)
    return pl.pallas_call(
        _mix_out_kernel,
        grid=(N_TILES,),
        in_specs=[chunk_blk(lambda i: (0, 0, 0)),
                  chunk_blk(lambda i: (0, jnp.minimum(i + 1, N_PROMPT_TILES), 0)),
                  pl.BlockSpec((N_SAMPLE, D_SSM), const), pl.BlockSpec((TM, D_SSM), row),
                  pl.BlockSpec((TM, D_CONV), row), pl.BlockSpec((TM, D_MODEL), row), vec(D_SSM),
                  pl.BlockSpec((D_SSM, D_SSM), const, pipeline_mode=pl.Buffered(1)),
                  vec(D_SSM),
                  pl.BlockSpec((D_MODEL, D_MODEL), const, pipeline_mode=pl.Buffered(1)),
                  vec(D_MODEL), vec(D_MODEL)],
        out_specs=[pl.BlockSpec((TM, D_MODEL), row), pl.BlockSpec((TM, D_MODEL), row)],
        out_shape=[jax.ShapeDtypeStruct((M_PAD, D_MODEL), F32),
                   jax.ShapeDtypeStruct((M_PAD, D_MODEL), BF16)],
        scratch_shapes=[pltpu.VMEM((TM, D_MODEL), BF16), pltpu.VMEM((N_LANE_BLOCKS, TM, SW), F32),
                        pltpu.VMEM((N_LANE_BLOCKS, TM, SW), F32)],
        compiler_params=pltpu.CompilerParams(
            dimension_semantics=("arbitrary",), vmem_limit_bytes=VMEM_LIMIT),
        name="mix_out",
    )(yc, yc, ys, u, rc, h0, dskip, wglu, gs, wout, g1, b1)


M_HALF = M_PAD // 2
SUB_UP = M_HALF // 4
TF_UP = 256
SUB_DOWN = 256
M_REAL_HALF = M_HALF - (TM - EXTRA_ROWS)


def _ffn_up_kernel(xb_ref, wg_ref, wu_ref, wd_ref, act_ref, wdb_ref):
    @pl.when(pl.program_id(0) == 0)
    def _():
        wdb_ref[...] = wd_ref[...].astype(BF16)

    wgb = wg_ref[...].astype(BF16)
    wub = wu_ref[...].astype(BF16)

    def rows(lo, hi):
        xb = xb_ref[lo:hi, :]
        gate = _dot(xb, wgb)
        up = _dot(xb, wub)
        act_ref[lo:hi, :] = (gate * _sigmoid(gate) * up).astype(BF16)

    @pl.when(pl.program_id(0) == 0)
    def _():
        for r0 in range(0, M_HALF, SUB_UP):
            rows(r0, r0 + SUB_UP)

    @pl.when(pl.program_id(0) == 1)
    def _():
        for r0 in range(0, M_REAL_HALF, SUB_UP):
            rows(r0, min(r0 + SUB_UP, M_REAL_HALF))
        act_ref[M_REAL_HALF:M_HALF, :] = jnp.zeros((M_HALF - M_REAL_HALF, TF_UP), BF16)


def _ffn_up(x1b, wg, wu, wd):
    wcol = pl.BlockSpec((D_MODEL, TF_UP), lambda h, f: (0, f))
    wrow = pl.BlockSpec((TF_UP, D_MODEL), lambda h, f: (jnp.where(h == 0, f, D_FF // TF_UP - 1), 0))
    return pl.pallas_call(
        _ffn_up_kernel,
        grid=(M_PAD // M_HALF, D_FF // TF_UP),
        in_specs=[pl.BlockSpec((M_HALF, D_MODEL), lambda h, f: (h, 0), pipeline_mode=pl.Buffered(1)),
                  wcol, wcol, wrow],
        out_specs=[pl.BlockSpec((M_HALF, TF_UP), lambda h, f: (h, f)), wrow],
        out_shape=[jax.ShapeDtypeStruct((M_PAD, D_FF), BF16),
                   jax.ShapeDtypeStruct((D_FF, D_MODEL), BF16)],
        compiler_params=pltpu.CompilerParams(
            dimension_semantics=("arbitrary", "arbitrary"), vmem_limit_bytes=VMEM_LIMIT),
        name="ffn_up",
    )(x1b, wg, wu, wd)


def _ffn_down_kernel(act_ref, x_ref, wd_ref, g2_ref, b2_ref, yp_ref, ye_ref):
    i = pl.program_id(0)

    def tile(o_ref, rows):
        for r0 in range(0, rows, SUB_DOWN):
            rs = slice(r0, r0 + SUB_DOWN)
            z = ALPHA * x_ref[rs, :] + _dot(act_ref[rs, :], wd_ref[...])
            o_ref[rs, :] = _layer_norm(z, g2_ref[...], b2_ref[...])

    @pl.when(i < N_PROMPT_TILES)
    def _():
        tile(yp_ref, TM)

    @pl.when(i == N_PROMPT_TILES)
    def _():
        tile(ye_ref, EXTRA_ROWS)


def _ffn_down(act, x1, wd, g2, b2):
    row = lambda i: (i, 0)
    const = lambda i: (0, 0)
    return pl.pallas_call(
        _ffn_down_kernel,
        grid=(N_TILES,),
        in_specs=[pl.BlockSpec((TM, D_FF), row), pl.BlockSpec((TM, D_MODEL), row),
                  pl.BlockSpec((D_FF, D_MODEL), const, pipeline_mode=pl.Buffered(1)),
                  pl.BlockSpec((1, D_MODEL), const), pl.BlockSpec((1, D_MODEL), const)],
        out_specs=[pl.BlockSpec((TM, D_MODEL), lambda i: (jnp.minimum(i, N_PROMPT_TILES - 1), 0)),
                   pl.BlockSpec((EXTRA_ROWS, D_MODEL), const)],
        out_shape=[jax.ShapeDtypeStruct((N_PROMPT_ROWS, D_MODEL), F32),
                   jax.ShapeDtypeStruct((EXTRA_ROWS, D_MODEL), F32)],
        compiler_params=pltpu.CompilerParams(
            dimension_semantics=("arbitrary",), vmem_limit_bytes=VMEM_LIMIT),
        name="ffn_down",
    )(act, x1, wd, g2, b2)


def _pad_rows(a, n):
    return jnp.pad(a, ((0, n - a.shape[0]), (0, 0)))


def kernel(x_prompt, x_sample, state_ssm_re, state_ssm_im, state_conv, meta_tokens, ln0_g, ln0_b,
           w_in, ssm_a_re, ssm_a_im, ssm_log_dt, ssm_b_re, ssm_b_im, ssm_c_re, ssm_c_im, ssm_d,
           ssm_w_glu, conv_w, g_ssm_out, g_conv_out, w_out, ln1_g, ln1_b, w_gate, w_up, w_down,
           ln2_g, ln2_b):
    assert x_prompt.shape == (N_BATCH, SEQ, D_MODEL) and x_sample.shape == (N_SAMPLE, 1, D_MODEL)
    assert w_in.shape[0] == 1, "single layer"
    g, p, c = N_GROUPS, N_STATE, GROUP
    vec = lambda a: a.reshape(1, -1)

    xp = x_prompt.reshape(N_PROMPT_ROWS, D_MODEL)
    xe = _pad_rows(jnp.concatenate([meta_tokens, x_sample[:, 0, :]], axis=0), SUB_IN)
    g0, b0 = vec(ln0_g), vec(ln0_b)

    m, wst, cpow, bpad, cpad, avec, w_in_bf = _ssm_prep(ssm_log_dt[0], ssm_a_re[0], ssm_a_im[0], ssm_b_re[0],
                                                        ssm_b_im[0], ssm_c_re[0], ssm_c_im[0], w_in[0])

    prev = jnp.pad(state_conv[0], ((SAMPLE_ROW0, SUB_IN - SAMPLE_ROW0 - N_SAMPLE), (0, 0), (0, 0)))
    h0, u, uc, rc, v_tail, v_extra, wout_bf, wglu_bf = _in_proj(
        xp, xe, g0, b0, w_in_bf, prev[:, 1], prev[:, 0], conv_w[0], vec(g_conv_out[0]),
        w_out[0], ssm_w_glu[0])

    flat = lambda a: a.reshape(N_SAMPLE, g * p)
    yc, ys, s_fin, ns_re, ns_im = _ssm_scan(uc, u, m, wst, cpow, bpad, cpad, avec,
                                            flat(state_ssm_re[0]), flat(state_ssm_im[0]))

    x1, x1b = _mix_out(yc, ys, u, rc, h0, vec(ssm_d[0]), wglu_bf, vec(g_ssm_out[0]),
                       wout_bf, vec(ln1_g[0]), vec(ln1_b[0]))

    act, wd_bf = _ffn_up(x1b, w_gate[0], w_up[0], w_down[0])
    yp, ye = _ffn_down(act, x1, wd_bf, vec(ln2_g[0]), vec(ln2_b[0]))

    y_prompt = yp.reshape(N_BATCH, SEQ, D_MODEL)
    y_sample = ye[SAMPLE_ROW0:SAMPLE_ROW0 + N_SAMPLE].reshape(N_SAMPLE, 1, D_MODEL)

    sd = state_ssm_re.dtype
    s_p = s_fin.transpose(1, 0, 2)
    unflat = lambda a: a.reshape(1, N_SAMPLE, g, p).astype(sd)
    cd = state_conv.dtype
    conv_p = v_tail[TILES_PER_SEQ - 1::TILES_PER_SEQ, ROW8 - (CONV_W - 1):].astype(cd)
    conv_s = jnp.stack([state_conv[0][:, 1], v_extra[SAMPLE_ROW0:SAMPLE_ROW0 + N_SAMPLE].astype(cd)], axis=1)
    return (y_prompt, y_sample,
            s_p[None, ..., :p].astype(sd), s_p[None, ..., p:].astype(sd), conv_p[None],
            unflat(ns_re), unflat(ns_im), conv_s[None])
```

```python
import jax
import jax.numpy as jnp
import numpy as np
from jax import lax
from jax.experimental import pallas as pl
from jax.experimental.pallas import tpu as pltpu

D_MODEL = 2048
D_SSM = 1024
D_CONV = 1024
N_GROUPS = 64
GROUP = 16
N_STATE = 64
N_META = 16
CONV_W = 3
D_FF = 5632
LN_EPS = 1e-5
RMS_EPS = 1e-6
ALPHA = 2.0 ** 0.25

N_BATCH = 4
SEQ = 2048
N_SAMPLE = 128
CHUNK = 16
ROW8 = 8
CW = CHUNK * GROUP
SW = 2 * N_STATE

TM = 512
N_PROMPT_ROWS = N_BATCH * SEQ
N_PROMPT_TILES = N_PROMPT_ROWS // TM
N_TILES = N_PROMPT_TILES + 1
M_PAD = N_TILES * TM
SAMPLE_ROW0 = N_META
GB = SW // GROUP
N_LANE_BLOCKS = D_SSM // SW
TILE_CHUNKS = TM // CHUNK
EXTRA_ROWS = 256
VMEM_LIMIT = 60 * 1024 * 1024

F32 = jnp.float32
BF16 = jnp.bfloat16


def _layer_norm(x, g, b):
    mu = jnp.mean(x, axis=-1, keepdims=True)
    xc = x - mu
    var = jnp.mean(xc * xc, axis=-1, keepdims=True)
    return xc * lax.rsqrt(var + LN_EPS) * g + b


def _rms_norm(x, g):
    return x * lax.rsqrt(jnp.mean(x * x, axis=-1, keepdims=True) + RMS_EPS) * g


def _gelu_tanh(x):
    c = np.sqrt(2.0 / np.pi)
    hx = 0.5 * x
    return hx + hx * jnp.tanh(x * (np.float32(c) + np.float32(c * 0.044715) * (x * x)))


def _sigmoid(x):
    return 1.0 / (1.0 + jnp.exp2(x * np.float32(-np.log2(np.e))))


def _dot(a, b):
    return jnp.dot(a, b, preferred_element_type=F32)


def _dot_nt(a, b):
    return lax.dot_general(a, b, (((1,), (1,)), ((), ())), preferred_element_type=F32)


def _block_transpose(a):
    a = list(a)
    blk = lax.broadcasted_iota(jnp.int32, a[0].shape, 1) // GROUP
    for d in (4, 2, 1):
        w = d * GROUP
        keep = (blk & d) == 0
        for j in range(GB):
            if j & d == 0:
                lo, hi = a[j], a[j + d]
                a[j] = jnp.where(keep, lo, pltpu.roll(hi, w, axis=1))
                a[j + d] = jnp.where(keep, pltpu.roll(lo, SW - w, axis=1), hi)
    return a


def _ssm_prep_kernel(logdt_ref, are_ref, aim_ref, bre_ref, bim_ref, cre_ref, cim_ref, win_ref,
                     m_ref, wst_ref, cpow_ref, bpad_ref, cpad_ref, avec_ref, winb_ref):
    winb_ref[...] = win_ref[...].astype(BF16)
    halves = lambda x, y: jnp.concatenate([x, y], axis=1)

    dt = jnp.exp(jnp.broadcast_to(logdt_ref[...], (GB, SW)))
    a_re = halves(are_ref[...], are_ref[...])
    a_im = halves(aim_ref[...], aim_ref[...])
    mag = jnp.exp(dt * a_re)
    ang = dt * a_im
    ab_re = mag * jnp.cos(ang)
    ab_im = mag * jnp.sin(ang)
    n_re = ab_re - 1.0
    den = a_re * a_re + a_im * a_im
    f_re = (n_re * a_re + ab_im * a_im) / den
    f_im = (ab_im * a_re - n_re * a_im) / den
    sgn = jnp.where(lax.broadcasted_iota(jnp.int32, (GB, SW), 1) < N_STATE, -1.0, 1.0)

    pr, pi = jnp.ones_like(ab_re), jnp.zeros_like(ab_re)
    pw = [(pr, pi)]
    for _ in range(CHUNK):
        pr, pi = pr * ab_re - pi * ab_im, pr * ab_im + pi * ab_re
        pw.append((pr, sgn * pi))
    fx = sgn * f_im
    lane = lax.broadcasted_iota(jnp.int32, (GROUP, CW), 1)

    for gl in range(GB):
        row = lambda x: x[gl:gl + 1, :]
        bt_re, bt_im = bre_ref[gl].T, bim_ref[gl].T
        b1, b2 = halves(bt_re, bt_im), halves(bt_im, bt_re)
        c1, c2 = halves(cre_ref[gl], cim_ref[gl]), halves(cim_ref[gl], cre_ref[gl])
        bb1 = row(f_re) * b1 + row(fx) * b2
        bb2 = row(f_re) * b2 - row(fx) * b1
        cc1 = -row(sgn) * c1
        cc2 = -row(sgn) * c2

        cq = []
        for s in range(CHUNK):
            wr, wx = pw[CHUNK - 1 - s]
            wst_ref[gl, s * GROUP:(s + 1) * GROUP, :] = (row(wr) * bb1 + row(wx) * bb2).astype(BF16)
            qr, qx = pw[s + 1]
            cpow_ref[gl, s * GROUP:(s + 1) * GROUP, :] = (row(qr) * cc1 + row(qx) * cc2).astype(BF16)
            dr, dx = pw[s]
            cq.append(row(dr) * cc1 + row(dx) * cc2)
        kt = lax.dot_general(bb1, jnp.concatenate(cq, axis=0), (((1,), (1,)), ((), ())),
                             precision=lax.Precision.HIGHEST, preferred_element_type=F32)
        for i in range(CHUNK):
            blk = kt if i == 0 else jnp.where(lane >= i * GROUP, pltpu.roll(kt, i * GROUP, axis=1), 0.0)
            m_ref[gl, i * GROUP:(i + 1) * GROUP, :] = blk.astype(BF16)

        zeros = jnp.zeros((SW, SW), BF16)
        bpad_ref[gl] = zeros
        cpad_ref[gl] = zeros
        bpad_ref[gl, gl * GROUP:(gl + 1) * GROUP, :] = bb1.astype(BF16)
        cpad_ref[gl, gl * GROUP:(gl + 1) * GROUP, :] = cq[0].astype(BF16)

        avec_ref[gl, 0:1, :] = row(pw[CHUNK][0])
        avec_ref[gl, 1:2, :] = row(pw[CHUNK][1])
        avec_ref[gl, 2:3, :] = row(pw[1][0])
        avec_ref[gl, 3:4, :] = row(pw[1][1])


def _ssm_prep(log_dt, a_re, a_im, b_re, b_im, c_re, c_im, w_in):
    g, c, p = N_GROUPS, GROUP, N_STATE
    vec = lambda n: pl.BlockSpec((GB, n), lambda i: (i, 0))
    blk = lambda *s: pl.BlockSpec((GB,) + s, lambda i: (i,) + (0,) * len(s))
    slab = pl.BlockSpec((w_in.shape[0] // (g // GB), w_in.shape[1]), lambda i: (i, 0))
    return pl.pallas_call(
        _ssm_prep_kernel,
        grid=(g // GB,),
        in_specs=[vec(1), vec(p), vec(p), blk(p, c), blk(p, c), blk(c, p), blk(c, p), slab],
        out_specs=[blk(CW, CW), blk(CW, SW), blk(CW, SW), blk(SW, SW), blk(SW, SW), blk(4, SW), slab],
        out_shape=[jax.ShapeDtypeStruct((g, CW, CW), BF16),
                   jax.ShapeDtypeStruct((g, CW, SW), BF16),
                   jax.ShapeDtypeStruct((g, CW, SW), BF16),
                   jax.ShapeDtypeStruct((g, SW, SW), BF16),
                   jax.ShapeDtypeStruct((g, SW, SW), BF16),
                   jax.ShapeDtypeStruct((g, 4, SW), F32),
                   jax.ShapeDtypeStruct(w_in.shape, BF16)],
        name="ssm_prep",
    )(log_dt[:, None], a_re, a_im, b_re, b_im, c_re, c_im, w_in)


SUB_IN = EXTRA_ROWS

TILES_PER_SEQ = SEQ // TM


def _in_proj_kernel(xp_ref, xe_ref, g0_ref, b0_ref, w_ref, p1_ref, p2_ref, cw_ref, gc_ref, wo_ref, wgl_ref,
                    h0_ref, u_ref, uc_ref, rc_ref, vt_ref, vs_ref, wob_ref, wglb_ref,
                    vbuf_scr, meta_scr, u_scr):
    s = pl.program_id(0)

    @pl.when(s == 0)
    def _():
        vbuf_scr[0:ROW8, :] = jnp.zeros((ROW8, D_CONV), F32)

    def tile(x_ref, is_extra):
        sub = SUB_IN if is_extra else TM
        rs = slice(0, sub)
        if is_extra:
            pad = slice(sub, TM)
            h0_ref[pad, :] = jnp.zeros((TM - sub, D_MODEL), F32)
            u_ref[pad, :] = jnp.zeros((TM - sub, D_SSM), F32)
            rc_ref[pad, :] = jnp.zeros((TM - sub, D_CONV), BF16)
        h0 = _layer_norm(x_ref[rs, :], g0_ref[...], b0_ref[...])
        h0_ref[rs, :] = h0
        proj = _dot(h0.astype(BF16), w_ref[...])
        part = lambda k: proj[:, k * D_SSM:(k + 1) * D_SSM]
        u_ref[rs, :] = part(0)
        for b in range(N_LANE_BLOCKS):
            u_scr[b, rs, :] = proj[:, b * SW:(b + 1) * SW]
            if is_extra:
                u_scr[b, sub:TM, :] = jnp.zeros((TM - sub, SW), F32)
        for b in range(N_LANE_BLOCKS):
            for h in range(2):
                src = [u_scr[b, pl.ds(GB * h + j, TILE_CHUNKS, stride=CHUNK), :] for j in range(GB)]
                for gl, w in enumerate(_block_transpose(src)):
                    uc_ref[b * GB + gl, :, h * SW:(h + 1) * SW] = w.astype(BF16)
        b_gate = part(1)
        v = part(2) * part(3)
        vbuf_scr[ROW8:ROW8 + sub, :] = v
        prev = vbuf_scr[0:ROW8 + sub, :]
        vm1 = pltpu.roll(prev, 1, axis=0)[ROW8:, :]
        vm2 = pltpu.roll(prev, 2, axis=0)[ROW8:, :]
        if is_extra:
            sample = lax.broadcasted_iota(jnp.int32, (sub, 1), 0) >= SAMPLE_ROW0
            vm1 = jnp.where(sample, p1_ref[...], vm1)
            vm2 = jnp.where(sample, p2_ref[...], vm2)
        conv = cw_ref[0:1, :] * vm2 + cw_ref[1:2, :] * vm1 + cw_ref[2:3, :] * v
        rc_ref[rs, :] = _rms_norm(b_gate * conv, gc_ref[...]).astype(BF16)

    @pl.when(s == 0)
    def _():
        tile(xe_ref, True)
        vs_ref[...] = vbuf_scr[ROW8:ROW8 + SUB_IN, :]
        meta_scr[...] = vbuf_scr[ROW8 + N_META - ROW8:ROW8 + N_META, :]
        vbuf_scr[0:ROW8, :] = meta_scr[...]

    @pl.when(s > 0)
    def _():
        tile(xp_ref, False)
        wob_ref[...] = wo_ref[...].astype(BF16)
        wglb_ref[...] = wgl_ref[...].astype(BF16)
        tail = vbuf_scr[TM:TM + ROW8, :]
        vt_ref[0] = tail
        vbuf_scr[0:ROW8, :] = jnp.where(s % TILES_PER_SEQ == 0, meta_scr[...], tail)


def _in_proj(xp, xe, g0, b0, w_in, p1, p2, cw, gc, w_out, w_glu):
    const = lambda s: (0, 0)
    slab = lambda s: (jnp.maximum(s - 1, 0), 0)
    row = lambda s: ((s + N_PROMPT_TILES) % N_TILES, 0)
    vec = lambda n: pl.BlockSpec((1, n), const)
    return pl.pallas_call(
        _in_proj_kernel,
        grid=(N_TILES,),
        in_specs=[pl.BlockSpec((TM, D_MODEL), lambda s: (jnp.maximum(s - 1, 0), 0)),
                  pl.BlockSpec((SUB_IN, D_MODEL), const),
                  vec(D_MODEL), vec(D_MODEL),
                  pl.BlockSpec((D_MODEL, 4 * D_SSM), const, pipeline_mode=pl.Buffered(1)),
                  pl.BlockSpec((SUB_IN, D_CONV), const), pl.BlockSpec((SUB_IN, D_CONV), const),
                  pl.BlockSpec((CONV_W, D_CONV), const), vec(D_CONV),
                  pl.BlockSpec((D_MODEL // N_PROMPT_TILES, D_MODEL), slab),
                  pl.BlockSpec((D_SSM // N_PROMPT_TILES, D_SSM), slab)],
        out_specs=[pl.BlockSpec((TM, D_MODEL), row), pl.BlockSpec((TM, D_SSM), row),
                   pl.BlockSpec((N_GROUPS, TILE_CHUNKS, CW), lambda s: (0, (s + N_PROMPT_TILES) % N_TILES, 0)),
                   pl.BlockSpec((TM, D_CONV), row),
                   pl.BlockSpec((1, ROW8, D_CONV), lambda s: (jnp.maximum(s - 1, 0), 0, 0)),
                   pl.BlockSpec((SUB_IN, D_CONV), const),
                   pl.BlockSpec((D_MODEL // N_PROMPT_TILES, D_MODEL), slab),
                   pl.BlockSpec((D_SSM // N_PROMPT_TILES, D_SSM), slab)],
        out_shape=[jax.ShapeDtypeStruct((M_PAD, D_MODEL), F32),
                   jax.ShapeDtypeStruct((M_PAD, D_SSM), F32),
                   jax.ShapeDtypeStruct((N_GROUPS, N_TILES * TILE_CHUNKS, CW), BF16),
                   jax.ShapeDtypeStruct((M_PAD, D_CONV), BF16),
                   jax.ShapeDtypeStruct((N_PROMPT_TILES, ROW8, D_CONV), F32),
                   jax.ShapeDtypeStruct((SUB_IN, D_CONV), F32),
                   jax.ShapeDtypeStruct((D_MODEL, D_MODEL), BF16),
                   jax.ShapeDtypeStruct((D_SSM, D_SSM), BF16)],
        scratch_shapes=[pltpu.VMEM((ROW8 + TM, D_CONV), F32), pltpu.VMEM((ROW8, D_CONV), F32),
                        pltpu.VMEM((N_LANE_BLOCKS, TM, SW), F32)],
        compiler_params=pltpu.CompilerParams(
            dimension_semantics=("arbitrary",), vmem_limit_bytes=VMEM_LIMIT),
        name="in_proj",
    )(xp, xe, g0, b0, w_in, p1, p2, cw, gc, w_out, w_glu)


SEQ_CHUNKS = SEQ // CHUNK
LHS_ROWS = N_TILES * TILE_CHUNKS
META_LHS = N_BATCH * SEQ_CHUNKS
PITCH = SEQ_CHUNKS + ROW8
REC_ROWS = N_BATCH * PITCH + ROW8
META_REC = N_BATCH * PITCH


def _ssm_scan_kernel(uc_ref, ue_ref, m_ref, wst_ref, cpow_ref, bpad_ref, cpad_ref, avec_ref, sre_ref, sim_ref,
                     yc_ref, ys_ref, sfin_ref, nre_ref, nim_ref, x_scr, xs_scr, sp_scr):
    def swap_halves(s):
        return pltpu.roll(s, N_STATE, axis=1)

    def to_rec(ref, gl, x):
        for n in range(N_BATCH):
            ref[gl, n * PITCH:n * PITCH + SEQ_CHUNKS, :] = x[n * SEQ_CHUNKS:(n + 1) * SEQ_CHUNKS]
        ref[gl, META_REC:META_REC + ROW8, :] = x[META_LHS:META_LHS + ROW8]

    for gl in range(GB):
        x = _dot(uc_ref[gl], wst_ref[gl])
        to_rec(x_scr, gl, x)
        to_rec(xs_scr, gl, swap_halves(x))

    ar16 = [avec_ref[gl, 0:1, :] for gl in range(GB)]
    ax16 = [avec_ref[gl, 1:2, :] for gl in range(GB)]

    def step(k, carry):
        new = []
        for gl in range(GB):
            s, t = carry[gl]
            rows = pl.ds(k, N_BATCH, stride=PITCH)
            sp_scr[gl, rows, :] = s
            new.append((ar16[gl] * s + ax16[gl] * t + x_scr[gl, rows, :],
                        ar16[gl] * t - ax16[gl] * s + xs_scr[gl, rows, :]))
        return tuple(new)

    first = lambda ref, gl: jnp.broadcast_to(ref[gl, META_REC:META_REC + 1, :], (N_BATCH, SW))
    init = tuple((first(x_scr, gl), first(xs_scr, gl)) for gl in range(GB))
    fin = lax.fori_loop(0, SEQ_CHUNKS, step, init)

    us = ue_ref[SAMPLE_ROW0:SAMPLE_ROW0 + N_SAMPLE, :].astype(BF16)
    ys = jnp.zeros((N_SAMPLE, SW), F32)
    for gl in range(GB):
        sfin_ref[gl] = fin[gl][0]
        sp = jnp.concatenate([sp_scr[gl, n * PITCH:n * PITCH + SEQ_CHUNKS, :] for n in range(N_BATCH)]
                             + [jnp.zeros((LHS_ROWS - META_LHS, SW), F32)], axis=0)
        yc_ref[gl] = _dot(uc_ref[gl], m_ref[gl]) + _dot_nt(sp.astype(BF16), cpow_ref[gl])
        lanes = slice(gl * N_STATE, (gl + 1) * N_STATE)
        s0 = jnp.concatenate([sre_ref[:, lanes], sim_ref[:, lanes]], axis=1)
        sn = (avec_ref[gl, 2:3, :] * s0 + avec_ref[gl, 3:4, :] * swap_halves(s0)
              + _dot(us, bpad_ref[gl]))
        nre_ref[:, lanes] = sn[:, 0:N_STATE]
        nim_ref[:, lanes] = sn[:, N_STATE:SW]
        ys = ys + _dot_nt(sn.astype(BF16), cpad_ref[gl])
    ys_ref[...] = ys


def _ssm_scan(uc, u, m, wst, cpow, bpad, cpad, avec, s_re, s_im):
    blk = lambda *s: pl.BlockSpec((GB,) + s, lambda i: (i,) + (0,) * len(s))
    extra = pl.BlockSpec((EXTRA_ROWS, SW), lambda i: (N_PROMPT_ROWS // EXTRA_ROWS, i))
    st = pl.BlockSpec((N_SAMPLE, GB * N_STATE), lambda i: (0, i))
    st_shape = jax.ShapeDtypeStruct((N_SAMPLE, N_GROUPS * N_STATE), F32)
    return pl.pallas_call(
        _ssm_scan_kernel,
        grid=(N_GROUPS // GB,),
        in_specs=[blk(LHS_ROWS, CW), extra, blk(CW, CW), blk(CW, SW), blk(CW, SW), blk(SW, SW), blk(SW, SW),
                  blk(4, SW), st, st],
        out_specs=[blk(LHS_ROWS, CW), pl.BlockSpec((N_SAMPLE, SW), lambda i: (0, i)), blk(N_BATCH, SW), st, st],
        out_shape=[jax.ShapeDtypeStruct((N_GROUPS, LHS_ROWS, CW), F32),
                   jax.ShapeDtypeStruct((N_SAMPLE, D_SSM), F32),
                   jax.ShapeDtypeStruct((N_GROUPS, N_BATCH, SW), F32), st_shape, st_shape],
        scratch_shapes=[pltpu.VMEM((GB, REC_ROWS, SW), F32), pltpu.VMEM((GB, REC_ROWS, SW), F32),
                        pltpu.VMEM((GB, REC_ROWS, SW), F32)],
        compiler_params=pltpu.CompilerParams(
            dimension_semantics=("arbitrary",), vmem_limit_bytes=VMEM_LIMIT),
        name="ssm_scan",
    )(uc, u, m, wst, cpow, bpad, cpad, avec, s_re, s_im)


SUB_MIX = 512


def _mix_out_kernel(yc0_ref, ycn_ref, ys_ref, u_ref, rc_ref, h0_ref, dskip_ref, wglu_ref, gs_ref, wout_ref,
                    g1_ref, b1_ref, x1_ref, x1b_ref, merged_scr, y_scr, ynext_scr):
    i = pl.program_id(0)

    def to_rows(yc_ref, dst_scr):
        for b in range(N_LANE_BLOCKS):
            for h in range(2):
                src = [yc_ref[b * GB + gl, :, h * SW:(h + 1) * SW] for gl in range(GB)]
                for j, w in enumerate(_block_transpose(src)):
                    dst_scr[b, pl.ds(GB * h + j, TILE_CHUNKS, stride=CHUNK), :] = w

    def sub_tile(r0, n):
        rs = slice(r0, r0 + n)
        y = jnp.concatenate([y_scr[b, rs, :] for b in range(N_LANE_BLOCKS)], axis=1)
        ys = _gelu_tanh(y + dskip_ref[...] * u_ref[rs, :])
        gate = _dot(ys.astype(BF16), wglu_ref[...])
        merged_scr[rs, 0:D_SSM] = _rms_norm(ys * _sigmoid(gate), gs_ref[...]).astype(BF16)
        merged_scr[rs, D_SSM:D_MODEL] = rc_ref[rs, :]
        mo = _dot(merged_scr[rs, :], wout_ref[...])
        x1 = _layer_norm(ALPHA * h0_ref[rs, :] + mo, g1_ref[...], b1_ref[...])
        x1_ref[rs, :] = x1
        x1b_ref[rs, :] = x1.astype(BF16)

    @pl.when(i == 0)
    def _():
        to_rows(yc0_ref, y_scr)

    @pl.when(i < N_PROMPT_TILES)
    def _():
        for r0 in range(0, TM, SUB_MIX):
            sub_tile(r0, SUB_MIX)
        to_rows(ycn_ref, ynext_scr)
        y_scr[...] = ynext_scr[...]

    @pl.when(i == N_PROMPT_TILES - 1)
    def _():
        for b in range(N_LANE_BLOCKS):
            y_scr[b, SAMPLE_ROW0:SAMPLE_ROW0 + N_SAMPLE, :] = ys_ref[:, b * SW:(b + 1) * SW]

    @pl.when(i == N_PROMPT_TILES)
    def _():
        sub_tile(0, EXTRA_ROWS)
        x1_ref[EXTRA_ROWS:TM, :] = jnp.zeros((TM - EXTRA_ROWS, D_MODEL), F32)
        x1b_ref[EXTRA_ROWS:TM, :] = jnp.zeros((TM - EXTRA_ROWS, D_MODEL), BF16)


def _mix_out(yc, ys, u, rc, h0, dskip, wglu, gs, wout, g1, b1):
    row = lambda i: (i, 0)
    const = lambda i: (0, 0)
    vec = lambda n: pl.BlockSpec((1, n), const)
    chunk_blk = lambda f: pl.BlockSpec((N_GROUPS, TILE_CHUNKS, CW), f)
    return pl.pallas_call(
        _mix_out_kernel,
        grid=(N_TILES,),
        in_specs=[chunk_blk(lambda i: (0, 0, 0)),
                  chunk_blk(lambda i: (0, jnp.minimum(i + 1, N_PROMPT_TILES), 0)),
                  pl.BlockSpec((N_SAMPLE, D_SSM), const), pl.BlockSpec((TM, D_SSM), row),
                  pl.BlockSpec((TM, D_CONV), row), pl.BlockSpec((TM, D_MODEL), row), vec(D_SSM),
                  pl.BlockSpec((D_SSM, D_SSM), const, pipeline_mode=pl.Buffered(1)),
                  vec(D_SSM),
                  pl.BlockSpec((D_MODEL, D_MODEL), const, pipeline_mode=pl.Buffered(1)),
                  vec(D_MODEL), vec(D_MODEL)],
        out_specs=[pl.BlockSpec((TM, D_MODEL), row), pl.BlockSpec((TM, D_MODEL), row)],
        out_shape=[jax.ShapeDtypeStruct((M_PAD, D_MODEL), F32),
                   jax.ShapeDtypeStruct((M_PAD, D_MODEL), BF16)],
        scratch_shapes=[pltpu.VMEM((TM, D_MODEL), BF16), pltpu.VMEM((N_LANE_BLOCKS, TM, SW), F32),
                        pltpu.VMEM((N_LANE_BLOCKS, TM, SW), F32)],
        compiler_params=pltpu.CompilerParams(
            dimension_semantics=("arbitrary",), vmem_limit_bytes=VMEM_LIMIT),
        name="mix_out",
    )(yc, yc, ys, u, rc, h0, dskip, wglu, gs, wout, g1, b1)


M_HALF = M_PAD // 2
SUB_UP = M_HALF // 4
TF_UP = 256
SUB_DOWN = 256
M_REAL_HALF = M_HALF - (TM - EXTRA_ROWS)


def _ffn_up_kernel(xb_ref, wg_ref, wu_ref, wd_ref, act_ref, wdb_ref):
    @pl.when(pl.program_id(0) == 0)
    def _():
        wdb_ref[...] = wd_ref[...].astype(BF16)

    wgb = wg_ref[...].astype(BF16)
    wub = wu_ref[...].astype(BF16)

    def rows(lo, hi):
        xb = xb_ref[lo:hi, :]
        gate = _dot(xb, wgb)
        up = _dot(xb, wub)
        act_ref[lo:hi, :] = (gate * _sigmoid(gate) * up).astype(BF16)

    @pl.when(pl.program_id(0) == 0)
    def _():
        for r0 in range(0, M_HALF, SUB_UP):
            rows(r0, r0 + SUB_UP)

    @pl.when(pl.program_id(0) == 1)
    def _():
        for r0 in range(0, M_REAL_HALF, SUB_UP):
            rows(r0, min(r0 + SUB_UP, M_REAL_HALF))
        act_ref[M_REAL_HALF:M_HALF, :] = jnp.zeros((M_HALF - M_REAL_HALF, TF_UP), BF16)


def _ffn_up(x1b, wg, wu, wd):
    wcol = pl.BlockSpec((D_MODEL, TF_UP), lambda h, f: (0, f))
    wrow = pl.BlockSpec((TF_UP, D_MODEL), lambda h, f: (jnp.where(h == 0, f, D_FF // TF_UP - 1), 0))
    return pl.pallas_call(
        _ffn_up_kernel,
        grid=(M_PAD // M_HALF, D_FF // TF_UP),
        in_specs=[pl.BlockSpec((M_HALF, D_MODEL), lambda h, f: (h, 0), pipeline_mode=pl.Buffered(1)),
                  wcol, wcol, wrow],
        out_specs=[pl.BlockSpec((M_HALF, TF_UP), lambda h, f: (h, f)), wrow],
        out_shape=[jax.ShapeDtypeStruct((M_PAD, D_FF), BF16),
                   jax.ShapeDtypeStruct((D_FF, D_MODEL), BF16)],
        compiler_params=pltpu.CompilerParams(
            dimension_semantics=("arbitrary", "arbitrary"), vmem_limit_bytes=VMEM_LIMIT),
        name="ffn_up",
    )(x1b, wg, wu, wd)


def _ffn_down_kernel(act_ref, x_ref, wd_ref, g2_ref, b2_ref, yp_ref, ye_ref):
    i = pl.program_id(0)

    def tile(o_ref, rows):
        for r0 in range(0, rows, SUB_DOWN):
            rs = slice(r0, r0 + SUB_DOWN)
            z = ALPHA * x_ref[rs, :] + _dot(act_ref[rs, :], wd_ref[...])
            o_ref[rs, :] = _layer_norm(z, g2_ref[...], b2_ref[...])

    @pl.when(i < N_PROMPT_TILES)
    def _():
        tile(yp_ref, TM)

    @pl.when(i == N_PROMPT_TILES)
    def _():
        tile(ye_ref, EXTRA_ROWS)


def _ffn_down(act, x1, wd, g2, b2):
    row = lambda i: (i, 0)
    const = lambda i: (0, 0)
    return pl.pallas_call(
        _ffn_down_kernel,
        grid=(N_TILES,),
        in_specs=[pl.BlockSpec((TM, D_FF), row), pl.BlockSpec((TM, D_MODEL), row),
                  pl.BlockSpec((D_FF, D_MODEL), const, pipeline_mode=pl.Buffered(1)),
                  pl.BlockSpec((1, D_MODEL), const), pl.BlockSpec((1, D_MODEL), const)],
        out_specs=[pl.BlockSpec((TM, D_MODEL), lambda i: (jnp.minimum(i, N_PROMPT_TILES - 1), 0)),
                   pl.BlockSpec((EXTRA_ROWS, D_MODEL), const)],
        out_shape=[jax.ShapeDtypeStruct((N_PROMPT_ROWS, D_MODEL), F32),
                   jax.ShapeDtypeStruct((EXTRA_ROWS, D_MODEL), F32)],
        compiler_params=pltpu.CompilerParams(
            dimension_semantics=("arbitrary",), vmem_limit_bytes=VMEM_LIMIT),
        name="ffn_down",
    )(act, x1, wd, g2, b2)


def _pad_rows(a, n):
    return jnp.pad(a, ((0, n - a.shape[0]), (0, 0)))


def kernel(x_prompt, x_sample, state_ssm_re, state_ssm_im, state_conv, meta_tokens, ln0_g, ln0_b,
           w_in, ssm_a_re, ssm_a_im, ssm_log_dt, ssm_b_re, ssm_b_im, ssm_c_re, ssm_c_im, ssm_d,
           ssm_w_glu, conv_w, g_ssm_out, g_conv_out, w_out, ln1_g, ln1_b, w_gate, w_up, w_down,
           ln2_g, ln2_b):
    assert x_prompt.shape == (N_BATCH, SEQ, D_MODEL) and x_sample.shape == (N_SAMPLE, 1, D_MODEL)
    assert w_in.shape[0] == 1, "single layer"
    g, p, c = N_GROUPS, N_STATE, GROUP
    vec = lambda a: a.reshape(1, -1)

    xp = x_prompt.reshape(N_PROMPT_ROWS, D_MODEL)
    xe = _pad_rows(jnp.concatenate([meta_tokens, x_sample[:, 0, :]], axis=0), SUB_IN)
    g0, b0 = vec(ln0_g), vec(ln0_b)

    m, wst, cpow, bpad, cpad, avec, w_in_bf = _ssm_prep(ssm_log_dt[0], ssm_a_re[0], ssm_a_im[0], ssm_b_re[0],
                                                        ssm_b_im[0], ssm_c_re[0], ssm_c_im[0], w_in[0])

    prev = jnp.pad(state_conv[0], ((SAMPLE_ROW0, SUB_IN - SAMPLE_ROW0 - N_SAMPLE), (0, 0), (0, 0)))
    h0, u, uc, rc, v_tail, v_extra, wout_bf, wglu_bf = _in_proj(
        xp, xe, g0, b0, w_in_bf, prev[:, 1], prev[:, 0], conv_w[0], vec(g_conv_out[0]),
        w_out[0], ssm_w_glu[0])

    flat = lambda a: a.reshape(N_SAMPLE, g * p)
    yc, ys, s_fin, ns_re, ns_im = _ssm_scan(uc, u, m, wst, cpow, bpad, cpad, avec,
                                            flat(state_ssm_re[0]), flat(state_ssm_im[0]))

    x1, x1b = _mix_out(yc, ys, u, rc, h0, vec(ssm_d[0]), wglu_bf, vec(g_ssm_out[0]),
                       wout_bf, vec(ln1_g[0]), vec(ln1_b[0]))

    act, wd_bf = _ffn_up(x1b, w_gate[0], w_up[0], w_down[0])
    yp, ye = _ffn_down(act, x1, wd_bf, vec(ln2_g[0]), vec(ln2_b[0]))

    y_prompt = yp.reshape(N_BATCH, SEQ, D_MODEL)
    y_sample = ye[SAMPLE_ROW0:SAMPLE_ROW0 + N_SAMPLE].reshape(N_SAMPLE, 1, D_MODEL)

    sd = state_ssm_re.dtype
    s_p = s_fin.transpose(1, 0, 2)
    unflat = lambda a: a.reshape(1, N_SAMPLE, g, p).astype(sd)
    cd = state_conv.dtype
    conv_p = v_tail[TILES_PER_SEQ - 1::TILES_PER_SEQ, ROW8 - (CONV_W - 1):].astype(cd)
    conv_s = jnp.stack([state_conv[0][:, 1], v_extra[SAMPLE_ROW0:SAMPLE_ROW0 + N_SAMPLE].astype(cd)], axis=1)
    return (y_prompt, y_sample,
            s_p[None, ..., :p].astype(sd), s_p[None, ..., p:].astype(sd), conv_p[None],
            unflat(ns_re), unflat(ns_im), conv_s[None])
```

```python
import jax
import jax.numpy as jnp
import numpy as np
from jax import lax
from jax.experimental import pallas as pl
from jax.experimental.pallas import tpu as pltpu

D_MODEL = 2048
D_SSM = 1024
D_CONV = 1024
N_GROUPS = 64
GROUP = 16
N_STATE = 64
N_META = 16
CONV_W = 3
D_FF = 5632
LN_EPS = 1e-5
RMS_EPS = 1e-6
ALPHA = 2.0 ** 0.25

N_BATCH = 4
SEQ = 2048
N_SAMPLE = 128
CHUNK = 16
ROW8 = 8
CW = CHUNK * GROUP
SW = 2 * N_STATE

TM = 512
N_PROMPT_ROWS = N_BATCH * SEQ
N_PROMPT_TILES = N_PROMPT_ROWS // TM
N_TILES = N_PROMPT_TILES + 1
M_PAD = N_TILES * TM
SAMPLE_ROW0 = N_META
GB = SW // GROUP
N_LANE_BLOCKS = D_SSM // SW
TILE_CHUNKS = TM // CHUNK
EXTRA_ROWS = 256
VMEM_LIMIT = 60 * 1024 * 1024

F32 = jnp.float32
BF16 = jnp.bfloat16


def _layer_norm(x, g, b):
    mu = jnp.mean(x, axis=-1, keepdims=True)
    xc = x - mu
    var = jnp.mean(xc * xc, axis=-1, keepdims=True)
    return xc * lax.rsqrt(var + LN_EPS) * g + b


def _rms_norm(x, g):
    return x * lax.rsqrt(jnp.mean(x * x, axis=-1, keepdims=True) + RMS_EPS) * g


def _gelu_tanh(x):
    c = np.sqrt(2.0 / np.pi)
    hx = 0.5 * x
    return hx + hx * jnp.tanh(x * (np.float32(c) + np.float32(c * 0.044715) * (x * x)))


def _sigmoid(x):
    return 1.0 / (1.0 + jnp.exp2(x * np.float32(-np.log2(np.e))))


def _dot(a, b):
    return jnp.dot(a, b, preferred_element_type=F32)


def _dot_nt(a, b):
    return lax.dot_general(a, b, (((1,), (1,)), ((), ())), preferred_element_type=F32)


def _block_transpose(a):
    a = list(a)
    blk = lax.broadcasted_iota(jnp.int32, a[0].shape, 1) // GROUP
    for d in (4, 2, 1):
        w = d * GROUP
        keep = (blk & d) == 0
        for j in range(GB):
            if j & d == 0:
                lo, hi = a[j], a[j + d]
                a[j] = jnp.where(keep, lo, pltpu.roll(hi, w, axis=1))
                a[j + d] = jnp.where(keep, pltpu.roll(lo, SW - w, axis=1), hi)
    return a


def _ssm_prep_kernel(logdt_ref, are_ref, aim_ref, bre_ref, bim_ref, cre_ref, cim_ref, win_ref,
                     m_ref, wst_ref, cpow_ref, bpad_ref, cpad_ref, avec_ref, winb_ref):
    winb_ref[...] = win_ref[...].astype(BF16)
    halves = lambda x, y: jnp.concatenate([x, y], axis=1)

    dt = jnp.exp(jnp.broadcast_to(logdt_ref[...], (GB, SW)))
    a_re = halves(are_ref[...], are_ref[...])
    a_im = halves(aim_ref[...], aim_ref[...])
    mag = jnp.exp(dt * a_re)
    ang = dt * a_im
    ab_re = mag * jnp.cos(ang)
    ab_im = mag * jnp.sin(ang)
    n_re = ab_re - 1.0
    den = a_re * a_re + a_im * a_im
    f_re = (n_re * a_re + ab_im * a_im) / den
    f_im = (ab_im * a_re - n_re * a_im) / den
    sgn = jnp.where(lax.broadcasted_iota(jnp.int32, (GB, SW), 1) < N_STATE, -1.0, 1.0)

    pr, pi = jnp.ones_like(ab_re), jnp.zeros_like(ab_re)
    pw = [(pr, pi)]
    for _ in range(CHUNK):
        pr, pi = pr * ab_re - pi * ab_im, pr * ab_im + pi * ab_re
        pw.append((pr, sgn * pi))
    fx = sgn * f_im
    lane = lax.broadcasted_iota(jnp.int32, (GROUP, CW), 1)

    for gl in range(GB):
        row = lambda x: x[gl:gl + 1, :]
        bt_re, bt_im = bre_ref[gl].T, bim_ref[gl].T
        b1, b2 = halves(bt_re, bt_im), halves(bt_im, bt_re)
        c1, c2 = halves(cre_ref[gl], cim_ref[gl]), halves(cim_ref[gl], cre_ref[gl])
        bb1 = row(f_re) * b1 + row(fx) * b2
        bb2 = row(f_re) * b2 - row(fx) * b1
        cc1 = -row(sgn) * c1
        cc2 = -row(sgn) * c2

        cq = []
        for s in range(CHUNK):
            wr, wx = pw[CHUNK - 1 - s]
            wst_ref[gl, s * GROUP:(s + 1) * GROUP, :] = (row(wr) * bb1 + row(wx) * bb2).astype(BF16)
            qr, qx = pw[s + 1]
            cpow_ref[gl, s * GROUP:(s + 1) * GROUP, :] = (row(qr) * cc1 + row(qx) * cc2).astype(BF16)
            dr, dx = pw[s]
            cq.append(row(dr) * cc1 + row(dx) * cc2)
        kt = lax.dot_general(bb1, jnp.concatenate(cq, axis=0), (((1,), (1,)), ((), ())),
                             precision=lax.Precision.HIGHEST, preferred_element_type=F32)
        for i in range(CHUNK):
            blk = kt if i == 0 else jnp.where(lane >= i * GROUP, pltpu.roll(kt, i * GROUP, axis=1), 0.0)
            m_ref[gl, i * GROUP:(i + 1) * GROUP, :] = blk.astype(BF16)

        zeros = jnp.zeros((SW, SW), BF16)
        bpad_ref[gl] = zeros
        cpad_ref[gl] = zeros
        bpad_ref[gl, gl * GROUP:(gl + 1) * GROUP, :] = bb1.astype(BF16)
        cpad_ref[gl, gl * GROUP:(gl + 1) * GROUP, :] = cq[0].astype(BF16)

        avec_ref[gl, 0:1, :] = row(pw[CHUNK][0])
        avec_ref[gl, 1:2, :] = row(pw[CHUNK][1])
        avec_ref[gl, 2:3, :] = row(pw[1][0])
        avec_ref[gl, 3:4, :] = row(pw[1][1])


def _ssm_prep(log_dt, a_re, a_im, b_re, b_im, c_re, c_im, w_in):
    g, c, p = N_GROUPS, GROUP, N_STATE
    vec = lambda n: pl.BlockSpec((GB, n), lambda i: (i, 0))
    blk = lambda *s: pl.BlockSpec((GB,) + s, lambda i: (i,) + (0,) * len(s))
    slab = pl.BlockSpec((w_in.shape[0] // (g // GB), w_in.shape[1]), lambda i: (i, 0))
    return pl.pallas_call(
        _ssm_prep_kernel,
        grid=(g // GB,),
        in_specs=[vec(1), vec(p), vec(p), blk(p, c), blk(p, c), blk(c, p), blk(c, p), slab],
        out_specs=[blk(CW, CW), blk(CW, SW), blk(CW, SW), blk(SW, SW), blk(SW, SW), blk(4, SW), slab],
        out_shape=[jax.ShapeDtypeStruct((g, CW, CW), BF16),
                   jax.ShapeDtypeStruct((g, CW, SW), BF16),
                   jax.ShapeDtypeStruct((g, CW, SW), BF16),
                   jax.ShapeDtypeStruct((g, SW, SW), BF16),
                   jax.ShapeDtypeStruct((g, SW, SW), BF16),
                   jax.ShapeDtypeStruct((g, 4, SW), F32),
                   jax.ShapeDtypeStruct(w_in.shape, BF16)],
        name="ssm_prep",
    )(log_dt[:, None], a_re, a_im, b_re, b_im, c_re, c_im, w_in)


SUB_IN = EXTRA_ROWS

TILES_PER_SEQ = SEQ // TM


def _in_proj_kernel(xp_ref, xe_ref, g0_ref, b0_ref, w_ref, p1_ref, p2_ref, cw_ref, gc_ref, wo_ref, wgl_ref,
                    h0_ref, u_ref, uc_ref, rc_ref, vt_ref, vs_ref, wob_ref, wglb_ref,
                    vbuf_scr, meta_scr, u_scr):
    s = pl.program_id(0)

    @pl.when(s == 0)
    def _():
        vbuf_scr[0:ROW8, :] = jnp.zeros((ROW8, D_CONV), F32)

    def tile(x_ref, is_extra):
        sub = SUB_IN if is_extra else TM
        rs = slice(0, sub)
        if is_extra:
            pad = slice(sub, TM)
            h0_ref[pad, :] = jnp.zeros((TM - sub, D_MODEL), F32)
            u_ref[pad, :] = jnp.zeros((TM - sub, D_SSM), F32)
            rc_ref[pad, :] = jnp.zeros((TM - sub, D_CONV), BF16)
        h0 = _layer_norm(x_ref[rs, :], g0_ref[...], b0_ref[...])
        h0_ref[rs, :] = h0
        proj = _dot(h0.astype(BF16), w_ref[...])
        part = lambda k: proj[:, k * D_SSM:(k + 1) * D_SSM]
        u_ref[rs, :] = part(0)
        for b in range(N_LANE_BLOCKS):
            u_scr[b, rs, :] = proj[:, b * SW:(b + 1) * SW]
            if is_extra:
                u_scr[b, sub:TM, :] = jnp.zeros((TM - sub, SW), F32)
        for b in range(N_LANE_BLOCKS):
            for h in range(2):
                src = [u_scr[b, pl.ds(GB * h + j, TILE_CHUNKS, stride=CHUNK), :] for j in range(GB)]
                for gl, w in enumerate(_block_transpose(src)):
                    uc_ref[b * GB + gl, :, h * SW:(h + 1) * SW] = w.astype(BF16)
        b_gate = part(1)
        v = part(2) * part(3)
        vbuf_scr[ROW8:ROW8 + sub, :] = v
        prev = vbuf_scr[0:ROW8 + sub, :]
        vm1 = pltpu.roll(prev, 1, axis=0)[ROW8:, :]
        vm2 = pltpu.roll(prev, 2, axis=0)[ROW8:, :]
        if is_extra:
            sample = lax.broadcasted_iota(jnp.int32, (sub, 1), 0) >= SAMPLE_ROW0
            vm1 = jnp.where(sample, p1_ref[...], vm1)
            vm2 = jnp.where(sample, p2_ref[...], vm2)
        conv = cw_ref[0:1, :] * vm2 + cw_ref[1:2, :] * vm1 + cw_ref[2:3, :] * v
        rc_ref[rs, :] = _rms_norm(b_gate * conv, gc_ref[...]).astype(BF16)

    @pl.when(s == 0)
    def _():
        tile(xe_ref, True)
        vs_ref[...] = vbuf_scr[ROW8:ROW8 + SUB_IN, :]
        meta_scr[...] = vbuf_scr[ROW8 + N_META - ROW8:ROW8 + N_META, :]
        vbuf_scr[0:ROW8, :] = meta_scr[...]

    @pl.when(s > 0)
    def _():
        tile(xp_ref, False)
        wob_ref[...] = wo_ref[...].astype(BF16)
        wglb_ref[...] = wgl_ref[...].astype(BF16)
        tail = vbuf_scr[TM:TM + ROW8, :]
        vt_ref[0] = tail
        vbuf_scr[0:ROW8, :] = jnp.where(s % TILES_PER_SEQ == 0, meta_scr[...], tail)


def _in_proj(xp, xe, g0, b0, w_in, p1, p2, cw, gc, w_out, w_glu):
    const = lambda s: (0, 0)
    slab = lambda s: (jnp.maximum(s - 1, 0), 0)
    row = lambda s: ((s + N_PROMPT_TILES) % N_TILES, 0)
    vec = lambda n: pl.BlockSpec((1, n), const)
    return pl.pallas_call(
        _in_proj_kernel,
        grid=(N_TILES,),
        in_specs=[pl.BlockSpec((TM, D_MODEL), lambda s: (jnp.maximum(s - 1, 0), 0)),
                  pl.BlockSpec((SUB_IN, D_MODEL), const),
                  vec(D_MODEL), vec(D_MODEL),
                  pl.BlockSpec((D_MODEL, 4 * D_SSM), const, pipeline_mode=pl.Buffered(1)),
                  pl.BlockSpec((SUB_IN, D_CONV), const), pl.BlockSpec((SUB_IN, D_CONV), const),
                  pl.BlockSpec((CONV_W, D_CONV), const), vec(D_CONV),
                  pl.BlockSpec((D_MODEL // N_PROMPT_TILES, D_MODEL), slab),
                  pl.BlockSpec((D_SSM // N_PROMPT_TILES, D_SSM), slab)],
        out_specs=[pl.BlockSpec((TM, D_MODEL), row), pl.BlockSpec((TM, D_SSM), row),
                   pl.BlockSpec((N_GROUPS, TILE_CHUNKS, CW), lambda s: (0, (s + N_PROMPT_TILES) % N_TILES, 0)),
                   pl.BlockSpec((TM, D_CONV), row),
                   pl.BlockSpec((1, ROW8, D_CONV), lambda s: (jnp.maximum(s - 1, 0), 0, 0)),
                   pl.BlockSpec((SUB_IN, D_CONV), const),
                   pl.BlockSpec((D_MODEL // N_PROMPT_TILES, D_MODEL), slab),
                   pl.BlockSpec((D_SSM // N_PROMPT_TILES, D_SSM), slab)],
        out_shape=[jax.ShapeDtypeStruct((M_PAD, D_MODEL), F32),
                   jax.ShapeDtypeStruct((M_PAD, D_SSM), F32),
                   jax.ShapeDtypeStruct((N_GROUPS, N_TILES * TILE_CHUNKS, CW), BF16),
                   jax.ShapeDtypeStruct((M_PAD, D_CONV), BF16),
                   jax.ShapeDtypeStruct((N_PROMPT_TILES, ROW8, D_CONV), F32),
                   jax.ShapeDtypeStruct((SUB_IN, D_CONV), F32),
                   jax.ShapeDtypeStruct((D_MODEL, D_MODEL), BF16),
                   jax.ShapeDtypeStruct((D_SSM, D_SSM), BF16)],
        scratch_shapes=[pltpu.VMEM((ROW8 + TM, D_CONV), F32), pltpu.VMEM((ROW8, D_CONV), F32),
                        pltpu.VMEM((N_LANE_BLOCKS, TM, SW), F32)],
        compiler_params=pltpu.CompilerParams(
            dimension_semantics=("arbitrary",), vmem_limit_bytes=VMEM_LIMIT),
        name="in_proj",
    )(xp, xe, g0, b0, w_in, p1, p2, cw, gc, w_out, w_glu)


SEQ_CHUNKS = SEQ // CHUNK
LHS_ROWS = N_TILES * TILE_CHUNKS
META_LHS = N_BATCH * SEQ_CHUNKS
PITCH = SEQ_CHUNKS + ROW8
REC_ROWS = N_BATCH * PITCH + ROW8
META_REC = N_BATCH * PITCH


def _ssm_scan_kernel(uc_ref, ue_ref, m_ref, wst_ref, cpow_ref, bpad_ref, cpad_ref, avec_ref, sre_ref, sim_ref,
                     yc_ref, ys_ref, sfin_ref, nre_ref, nim_ref, x_scr, xs_scr, sp_scr):
    def swap_halves(s):
        return pltpu.roll(s, N_STATE, axis=1)

    def to_rec(ref, gl, x):
        for n in range(N_BATCH):
            ref[gl, n * PITCH:n * PITCH + SEQ_CHUNKS, :] = x[n * SEQ_CHUNKS:(n + 1) * SEQ_CHUNKS]
        ref[gl, META_REC:META_REC + ROW8, :] = x[META_LHS:META_LHS + ROW8]

    for gl in range(GB):
        x = _dot(uc_ref[gl], wst_ref[gl])
        to_rec(x_scr, gl, x)
        to_rec(xs_scr, gl, swap_halves(x))

    ar16 = [avec_ref[gl, 0:1, :] for gl in range(GB)]
    ax16 = [avec_ref[gl, 1:2, :] for gl in range(GB)]

    def step(k, carry):
        new = []
        for gl in range(GB):
            s, t = carry[gl]
            rows = pl.ds(k, N_BATCH, stride=PITCH)
            sp_scr[gl, rows, :] = s
            new.append((ar16[gl] * s + ax16[gl] * t + x_scr[gl, rows, :],
                        ar16[gl] * t - ax16[gl] * s + xs_scr[gl, rows, :]))
        return tuple(new)

    first = lambda ref, gl: jnp.broadcast_to(ref[gl, META_REC:META_REC + 1, :], (N_BATCH, SW))
    init = tuple((first(x_scr, gl), first(xs_scr, gl)) for gl in range(GB))
    fin = lax.fori_loop(0, SEQ_CHUNKS, step, init)

    us = ue_ref[SAMPLE_ROW0:SAMPLE_ROW0 + N_SAMPLE, :].astype(BF16)
    ys = jnp.zeros((N_SAMPLE, SW), F32)
    for gl in range(GB):
        sfin_ref[gl] = fin[gl][0]
        sp = jnp.concatenate([sp_scr[gl, n * PITCH:n * PITCH + SEQ_CHUNKS, :] for n in range(N_BATCH)]
                             + [jnp.zeros((LHS_ROWS - META_LHS, SW), F32)], axis=0)
        yc_ref[gl] = _dot(uc_ref[gl], m_ref[gl]) + _dot_nt(sp.astype(BF16), cpow_ref[gl])
        s0 = jnp.concatenate([sre_ref[:, gl, :], sim_ref[:, gl, :]], axis=1)
        sn = (avec_ref[gl, 2:3, :] * s0 + avec_ref[gl, 3:4, :] * swap_halves(s0)
              + _dot(us, bpad_ref[gl]))
        nre_ref[:, gl, :] = sn[:, 0:N_STATE]
        nim_ref[:, gl, :] = sn[:, N_STATE:SW]
        ys = ys + _dot_nt(sn.astype(BF16), cpad_ref[gl])
    ys_ref[...] = ys


def _ssm_scan(uc, u, m, wst, cpow, bpad, cpad, avec, s_re, s_im):
    blk = lambda *s: pl.BlockSpec((GB,) + s, lambda i: (i,) + (0,) * len(s))
    extra = pl.BlockSpec((EXTRA_ROWS, SW), lambda i: (N_PROMPT_ROWS // EXTRA_ROWS, i))
    st = pl.BlockSpec((N_SAMPLE, GB, N_STATE), lambda i: (0, i, 0))
    st_shape = jax.ShapeDtypeStruct((N_SAMPLE, N_GROUPS, N_STATE), F32)
    return pl.pallas_call(
        _ssm_scan_kernel,
        grid=(N_GROUPS // GB,),
        in_specs=[blk(LHS_ROWS, CW), extra, blk(CW, CW), blk(CW, SW), blk(CW, SW), blk(SW, SW), blk(SW, SW),
                  blk(4, SW), st, st],
        out_specs=[blk(LHS_ROWS, CW), pl.BlockSpec((N_SAMPLE, SW), lambda i: (0, i)), blk(N_BATCH, SW), st, st],
        out_shape=[jax.ShapeDtypeStruct((N_GROUPS, LHS_ROWS, CW), F32),
                   jax.ShapeDtypeStruct((N_SAMPLE, D_SSM), F32),
                   jax.ShapeDtypeStruct((N_GROUPS, N_BATCH, SW), F32), st_shape, st_shape],
        scratch_shapes=[pltpu.VMEM((GB, REC_ROWS, SW), F32), pltpu.VMEM((GB, REC_ROWS, SW), F32),
                        pltpu.VMEM((GB, REC_ROWS, SW), F32)],
        compiler_params=pltpu.CompilerParams(
            dimension_semantics=("arbitrary",), vmem_limit_bytes=VMEM_LIMIT),
        name="ssm_scan",
    )(uc, u, m, wst, cpow, bpad, cpad, avec, s_re, s_im)


SUB_MIX = 512


def _mix_out_kernel(yc0_ref, ycn_ref, ys_ref, u_ref, rc_ref, h0_ref, dskip_ref, wglu_ref, gs_ref, wout_ref,
                    g1_ref, b1_ref, x1_ref, x1b_ref, merged_scr, y_scr, ynext_scr):
    i = pl.program_id(0)

    def to_rows(yc_ref, dst_scr):
        for b in range(N_LANE_BLOCKS):
            for h in range(2):
                src = [yc_ref[b * GB + gl, :, h * SW:(h + 1) * SW] for gl in range(GB)]
                for j, w in enumerate(_block_transpose(src)):
                    dst_scr[b, pl.ds(GB * h + j, TILE_CHUNKS, stride=CHUNK), :] = w

    def sub_tile(r0, n):
        rs = slice(r0, r0 + n)
        y = jnp.concatenate([y_scr[b, rs, :] for b in range(N_LANE_BLOCKS)], axis=1)
        ys = _gelu_tanh(y + dskip_ref[...] * u_ref[rs, :])
        gate = _dot(ys.astype(BF16), wglu_ref[...])
        merged_scr[rs, 0:D_SSM] = _rms_norm(ys * _sigmoid(gate), gs_ref[...]).astype(BF16)
        merged_scr[rs, D_SSM:D_MODEL] = rc_ref[rs, :]
        mo = _dot(merged_scr[rs, :], wout_ref[...])
        x1 = _layer_norm(ALPHA * h0_ref[rs, :] + mo, g1_ref[...], b1_ref[...])
        x1_ref[rs, :] = x1
        x1b_ref[rs, :] = x1.astype(BF16)

    @pl.when(i == 0)
    def _():
        to_rows(yc0_ref, y_scr)

    @pl.when(i < N_PROMPT_TILES)
    def _():
        for r0 in range(0, TM, SUB_MIX):
            sub_tile(r0, SUB_MIX)
        to_rows(ycn_ref, ynext_scr)
        y_scr[...] = ynext_scr[...]

    @pl.when(i == N_PROMPT_TILES - 1)
    def _():
        for b in range(N_LANE_BLOCKS):
            y_scr[b, SAMPLE_ROW0:SAMPLE_ROW0 + N_SAMPLE, :] = ys_ref[:, b * SW:(b + 1) * SW]

    @pl.when(i == N_PROMPT_TILES)
    def _():
        sub_tile(0, EXTRA_ROWS)
        x1_ref[EXTRA_ROWS:TM, :] = jnp.zeros((TM - EXTRA_ROWS, D_MODEL), F32)
        x1b_ref[EXTRA_ROWS:TM, :] = jnp.zeros((TM - EXTRA_ROWS, D_MODEL), BF16)


def _mix_out(yc, ys, u, rc, h0, dskip, wglu, gs, wout, g1, b1):
    row = lambda i: (i, 0)
    const = lambda i: (0, 0)
    vec = lambda n: pl.BlockSpec((1, n), const)
    chunk_blk = lambda f: pl.BlockSpec((N_GROUPS, TILE_CHUNKS, CW), f)
    return pl.pallas_call(
        _mix_out_kernel,
        grid=(N_TILES,),
        in_specs=[chunk_blk(lambda i: (0, 0, 0)),
                  chunk_blk(lambda i: (0, jnp.minimum(i + 1, N_PROMPT_TILES), 0)),
                  pl.BlockSpec((N_SAMPLE, D_SSM), const), pl.BlockSpec((TM, D_SSM), row),
                  pl.BlockSpec((TM, D_CONV), row), pl.BlockSpec((TM, D_MODEL), row), vec(D_SSM),
                  pl.BlockSpec((D_SSM, D_SSM), const, pipeline_mode=pl.Buffered(1)),
                  vec(D_SSM),
                  pl.BlockSpec((D_MODEL, D_MODEL), const, pipeline_mode=pl.Buffered(1)),
                  vec(D_MODEL), vec(D_MODEL)],
        out_specs=[pl.BlockSpec((TM, D_MODEL), row), pl.BlockSpec((TM, D_MODEL), row)],
        out_shape=[jax.ShapeDtypeStruct((M_PAD, D_MODEL), F32),
                   jax.ShapeDtypeStruct((M_PAD, D_MODEL), BF16)],
        scratch_shapes=[pltpu.VMEM((TM, D_MODEL), BF16), pltpu.VMEM((N_LANE_BLOCKS, TM, SW), F32),
                        pltpu.VMEM((N_LANE_BLOCKS, TM, SW), F32)],
        compiler_params=pltpu.CompilerParams(
            dimension_semantics=("arbitrary",), vmem_limit_bytes=VMEM_LIMIT),
        name="mix_out",
    )(yc, yc, ys, u, rc, h0, dskip, wglu, gs, wout, g1, b1)


M_HALF = M_PAD // 2
SUB_UP = M_HALF // 4
TF_UP = 256
SUB_DOWN = 256
M_REAL_HALF = M_HALF - (TM - EXTRA_ROWS)


def _ffn_up_kernel(xb_ref, wg_ref, wu_ref, wd_ref, act_ref, wdb_ref):
    @pl.when(pl.program_id(0) == 0)
    def _():
        wdb_ref[...] = wd_ref[...].astype(BF16)

    wgb = wg_ref[...].astype(BF16)
    wub = wu_ref[...].astype(BF16)

    def rows(lo, hi):
        xb = xb_ref[lo:hi, :]
        gate = _dot(xb, wgb)
        up = _dot(xb, wub)
        act_ref[lo:hi, :] = (gate * _sigmoid(gate) * up).astype(BF16)

    @pl.when(pl.program_id(0) == 0)
    def _():
        for r0 in range(0, M_HALF, SUB_UP):
            rows(r0, r0 + SUB_UP)

    @pl.when(pl.program_id(0) == 1)
    def _():
        for r0 in range(0, M_REAL_HALF, SUB_UP):
            rows(r0, min(r0 + SUB_UP, M_REAL_HALF))
        act_ref[M_REAL_HALF:M_HALF, :] = jnp.zeros((M_HALF - M_REAL_HALF, TF_UP), BF16)


def _ffn_up(x1b, wg, wu, wd):
    wcol = pl.BlockSpec((D_MODEL, TF_UP), lambda h, f: (0, f))
    wrow = pl.BlockSpec((TF_UP, D_MODEL), lambda h, f: (jnp.where(h == 0, f, D_FF // TF_UP - 1), 0))
    return pl.pallas_call(
        _ffn_up_kernel,
        grid=(M_PAD // M_HALF, D_FF // TF_UP),
        in_specs=[pl.BlockSpec((M_HALF, D_MODEL), lambda h, f: (h, 0), pipeline_mode=pl.Buffered(1)),
                  wcol, wcol, wrow],
        out_specs=[pl.BlockSpec((M_HALF, TF_UP), lambda h, f: (h, f)), wrow],
        out_shape=[jax.ShapeDtypeStruct((M_PAD, D_FF), BF16),
                   jax.ShapeDtypeStruct((D_FF, D_MODEL), BF16)],
        compiler_params=pltpu.CompilerParams(
            dimension_semantics=("arbitrary", "arbitrary"), vmem_limit_bytes=VMEM_LIMIT),
        name="ffn_up",
    )(x1b, wg, wu, wd)


def _ffn_down_kernel(act_ref, x_ref, wd_ref, g2_ref, b2_ref, yp_ref, ye_ref):
    i = pl.program_id(0)

    def tile(o_ref, rows):
        for r0 in range(0, rows, SUB_DOWN):
            rs = slice(r0, r0 + SUB_DOWN)
            z = ALPHA * x_ref[rs, :] + _dot(act_ref[rs, :], wd_ref[...])
            o_ref[rs, :] = _layer_norm(z, g2_ref[...], b2_ref[...])

    @pl.when(i < N_PROMPT_TILES)
    def _():
        tile(yp_ref, TM)

    @pl.when(i == N_PROMPT_TILES)
    def _():
        tile(ye_ref, EXTRA_ROWS)


def _ffn_down(act, x1, wd, g2, b2):
    row = lambda i: (i, 0)
    const = lambda i: (0, 0)
    return pl.pallas_call(
        _ffn_down_kernel,
        grid=(N_TILES,),
        in_specs=[pl.BlockSpec((TM, D_FF), row), pl.BlockSpec((TM, D_MODEL), row),
                  pl.BlockSpec((D_FF, D_MODEL), const, pipeline_mode=pl.Buffered(1)),
                  pl.BlockSpec((1, D_MODEL), const), pl.BlockSpec((1, D_MODEL), const)],
        out_specs=[pl.BlockSpec((TM, D_MODEL), lambda i: (jnp.minimum(i, N_PROMPT_TILES - 1), 0)),
                   pl.BlockSpec((EXTRA_ROWS, D_MODEL), const)],
        out_shape=[jax.ShapeDtypeStruct((N_PROMPT_ROWS, D_MODEL), F32),
                   jax.ShapeDtypeStruct((EXTRA_ROWS, D_MODEL), F32)],
        compiler_params=pltpu.CompilerParams(
            dimension_semantics=("arbitrary",), vmem_limit_bytes=VMEM_LIMIT),
        name="ffn_down",
    )(act, x1, wd, g2, b2)


def _pad_rows(a, n):
    return jnp.pad(a, ((0, n - a.shape[0]), (0, 0)))


def kernel(x_prompt, x_sample, state_ssm_re, state_ssm_im, state_conv, meta_tokens, ln0_g, ln0_b,
           w_in, ssm_a_re, ssm_a_im, ssm_log_dt, ssm_b_re, ssm_b_im, ssm_c_re, ssm_c_im, ssm_d,
           ssm_w_glu, conv_w, g_ssm_out, g_conv_out, w_out, ln1_g, ln1_b, w_gate, w_up, w_down,
           ln2_g, ln2_b):
    assert x_prompt.shape == (N_BATCH, SEQ, D_MODEL) and x_sample.shape == (N_SAMPLE, 1, D_MODEL)
    assert w_in.shape[0] == 1, "single layer"
    g, p, c = N_GROUPS, N_STATE, GROUP
    vec = lambda a: a.reshape(1, -1)

    xp = x_prompt.reshape(N_PROMPT_ROWS, D_MODEL)
    xe = _pad_rows(jnp.concatenate([meta_tokens, x_sample[:, 0, :]], axis=0), SUB_IN)
    g0, b0 = vec(ln0_g), vec(ln0_b)

    m, wst, cpow, bpad, cpad, avec, w_in_bf = _ssm_prep(ssm_log_dt[0], ssm_a_re[0], ssm_a_im[0], ssm_b_re[0],
                                                        ssm_b_im[0], ssm_c_re[0], ssm_c_im[0], w_in[0])

    prev = jnp.pad(state_conv[0], ((SAMPLE_ROW0, SUB_IN - SAMPLE_ROW0 - N_SAMPLE), (0, 0), (0, 0)))
    h0, u, uc, rc, v_tail, v_extra, wout_bf, wglu_bf = _in_proj(
        xp, xe, g0, b0, w_in_bf, prev[:, 1], prev[:, 0], conv_w[0], vec(g_conv_out[0]),
        w_out[0], ssm_w_glu[0])

    yc, ys, s_fin, ns_re, ns_im = _ssm_scan(uc, u, m, wst, cpow, bpad, cpad, avec,
                                            state_ssm_re[0], state_ssm_im[0])

    x1, x1b = _mix_out(yc, ys, u, rc, h0, vec(ssm_d[0]), wglu_bf, vec(g_ssm_out[0]),
                       wout_bf, vec(ln1_g[0]), vec(ln1_b[0]))

    act, wd_bf = _ffn_up(x1b, w_gate[0], w_up[0], w_down[0])
    yp, ye = _ffn_down(act, x1, wd_bf, vec(ln2_g[0]), vec(ln2_b[0]))

    y_prompt = yp.reshape(N_BATCH, SEQ, D_MODEL)
    y_sample = ye[SAMPLE_ROW0:SAMPLE_ROW0 + N_SAMPLE].reshape(N_SAMPLE, 1, D_MODEL)

    sd = state_ssm_re.dtype
    s_p = s_fin.transpose(1, 0, 2)
    cd = state_conv.dtype
    conv_p = v_tail[TILES_PER_SEQ - 1::TILES_PER_SEQ, ROW8 - (CONV_W - 1):].astype(cd)
    conv_s = jnp.stack([state_conv[0][:, 1], v_extra[SAMPLE_ROW0:SAMPLE_ROW0 + N_SAMPLE].astype(cd)], axis=1)
    return (y_prompt, y_sample,
            s_p[None, ..., :p].astype(sd), s_p[None, ..., p:].astype(sd), conv_p[None],
            ns_re[None].astype(sd), ns_im[None].astype(sd), conv_s[None])
```

```python
import jax
import jax.numpy as jnp
import numpy as np
from jax import lax
from jax.experimental import pallas as pl
from jax.experimental.pallas import tpu as pltpu

D_MODEL = 2048
D_SSM = 1024
D_CONV = 1024
N_GROUPS = 64
GROUP = 16
N_STATE = 64
N_META = 16
CONV_W = 3
D_FF = 5632
LN_EPS = 1e-5
RMS_EPS = 1e-6
ALPHA = 2.0 ** 0.25

N_BATCH = 4
SEQ = 2048
N_SAMPLE = 128
CHUNK = 16
ROW8 = 8
CW = CHUNK * GROUP
SW = 2 * N_STATE

TM = 512
N_PROMPT_ROWS = N_BATCH * SEQ
N_PROMPT_TILES = N_PROMPT_ROWS // TM
N_TILES = N_PROMPT_TILES + 1
M_PAD = N_TILES * TM
SAMPLE_ROW0 = N_META
GB = SW // GROUP
N_LANE_BLOCKS = D_SSM // SW
TILE_CHUNKS = TM // CHUNK
EXTRA_ROWS = 256
VMEM_LIMIT = 60 * 1024 * 1024

F32 = jnp.float32
BF16 = jnp.bfloat16


def _layer_norm(x, g, b):
    mu = jnp.mean(x, axis=-1, keepdims=True)
    xc = x - mu
    var = jnp.mean(xc * xc, axis=-1, keepdims=True)
    return xc * lax.rsqrt(var + LN_EPS) * g + b


def _rms_norm(x, g):
    return x * lax.rsqrt(jnp.mean(x * x, axis=-1, keepdims=True) + RMS_EPS) * g


def _gelu_tanh(x):
    c = np.sqrt(2.0 / np.pi)
    hx = 0.5 * x
    return hx + hx * jnp.tanh(x * (np.float32(c) + np.float32(c * 0.044715) * (x * x)))


def _sigmoid(x):
    return 1.0 / (1.0 + jnp.exp2(x * np.float32(-np.log2(np.e))))


def _dot(a, b):
    return jnp.dot(a, b, preferred_element_type=F32)


def _dot_nt(a, b):
    return lax.dot_general(a, b, (((1,), (1,)), ((), ())), preferred_element_type=F32)


def _block_transpose(a):
    a = list(a)
    blk = lax.broadcasted_iota(jnp.int32, a[0].shape, 1) // GROUP
    for d in (4, 2, 1):
        w = d * GROUP
        keep = (blk & d) == 0
        for j in range(GB):
            if j & d == 0:
                lo, hi = a[j], a[j + d]
                a[j] = jnp.where(keep, lo, pltpu.roll(hi, w, axis=1))
                a[j + d] = jnp.where(keep, pltpu.roll(lo, SW - w, axis=1), hi)
    return a


W_RING = 3


def _ssm_prep_kernel(logdt_ref, are_ref, aim_ref, bre_ref, bim_ref, cre_ref, cim_ref, win_hbm,
                     m_ref, wst_ref, cpow_ref, bpad_ref, cpad_ref, avec_ref, winb_hbm,
                     in_buf, out_buf, in_sem, out_sem):
    step = pl.program_id(0)
    n_slabs = N_GROUPS // GB
    slab_rows = in_buf.shape[1]

    def read(slab, slot):
        return pltpu.make_async_copy(win_hbm.at[pl.ds(slab * slab_rows, slab_rows), :],
                                     in_buf.at[slot], in_sem.at[slot])

    def write(slab, slot):
        return pltpu.make_async_copy(out_buf.at[slot],
                                     winb_hbm.at[pl.ds(slab * slab_rows, slab_rows), :], out_sem.at[slot])

    @pl.when(step == 0)
    def _():
        for k in range(W_RING):
            read(k, k).start()

    slot = step % W_RING
    read(step, slot).wait()

    @pl.when(step >= W_RING)
    def _():
        write(step - W_RING, slot).wait()

    out_buf[slot] = in_buf[slot].astype(BF16)
    write(step, slot).start()

    @pl.when(step + W_RING < n_slabs)
    def _():
        read(step + W_RING, slot).start()

    halves = lambda x, y: jnp.concatenate([x, y], axis=1)

    dt = jnp.exp(jnp.broadcast_to(logdt_ref[...], (GB, SW)))
    a_re = halves(are_ref[...], are_ref[...])
    a_im = halves(aim_ref[...], aim_ref[...])
    mag = jnp.exp(dt * a_re)
    ang = dt * a_im
    ab_re = mag * jnp.cos(ang)
    ab_im = mag * jnp.sin(ang)
    n_re = ab_re - 1.0
    den = a_re * a_re + a_im * a_im
    f_re = (n_re * a_re + ab_im * a_im) / den
    f_im = (ab_im * a_re - n_re * a_im) / den
    sgn = jnp.where(lax.broadcasted_iota(jnp.int32, (GB, SW), 1) < N_STATE, -1.0, 1.0)

    pr, pi = jnp.ones_like(ab_re), jnp.zeros_like(ab_re)
    pw = [(pr, pi)]
    for _ in range(CHUNK):
        pr, pi = pr * ab_re - pi * ab_im, pr * ab_im + pi * ab_re
        pw.append((pr, sgn * pi))
    fx = sgn * f_im
    lane = lax.broadcasted_iota(jnp.int32, (GROUP, CW), 1)

    for gl in range(GB):
        row = lambda x: x[gl:gl + 1, :]
        bt_re, bt_im = bre_ref[gl].T, bim_ref[gl].T
        b1, b2 = halves(bt_re, bt_im), halves(bt_im, bt_re)
        c1, c2 = halves(cre_ref[gl], cim_ref[gl]), halves(cim_ref[gl], cre_ref[gl])
        bb1 = row(f_re) * b1 + row(fx) * b2
        bb2 = row(f_re) * b2 - row(fx) * b1
        cc1 = -row(sgn) * c1
        cc2 = -row(sgn) * c2

        cq = []
        for s in range(CHUNK):
            wr, wx = pw[CHUNK - 1 - s]
            wst_ref[gl, s * GROUP:(s + 1) * GROUP, :] = (row(wr) * bb1 + row(wx) * bb2).astype(BF16)
            qr, qx = pw[s + 1]
            cpow_ref[gl, s * GROUP:(s + 1) * GROUP, :] = (row(qr) * cc1 + row(qx) * cc2).astype(BF16)
            dr, dx = pw[s]
            cq.append(row(dr) * cc1 + row(dx) * cc2)
        kt = lax.dot_general(bb1, jnp.concatenate(cq, axis=0), (((1,), (1,)), ((), ())),
                             precision=lax.Precision.HIGHEST, preferred_element_type=F32)
        for i in range(CHUNK):
            blk = kt if i == 0 else jnp.where(lane >= i * GROUP, pltpu.roll(kt, i * GROUP, axis=1), 0.0)
            m_ref[gl, i * GROUP:(i + 1) * GROUP, :] = blk.astype(BF16)

        zeros = jnp.zeros((SW, SW), BF16)
        bpad_ref[gl] = zeros
        cpad_ref[gl] = zeros
        bpad_ref[gl, gl * GROUP:(gl + 1) * GROUP, :] = bb1.astype(BF16)
        cpad_ref[gl, gl * GROUP:(gl + 1) * GROUP, :] = cq[0].astype(BF16)

        avec_ref[gl, 0:1, :] = row(pw[CHUNK][0])
        avec_ref[gl, 1:2, :] = row(pw[CHUNK][1])
        avec_ref[gl, 2:3, :] = row(pw[1][0])
        avec_ref[gl, 3:4, :] = row(pw[1][1])

    @pl.when(step == n_slabs - 1)
    def _():
        for t in range(n_slabs - W_RING, n_slabs):
            write(t, t % W_RING).wait()


def _ssm_prep(log_dt, a_re, a_im, b_re, b_im, c_re, c_im, w_in):
    g, c, p = N_GROUPS, GROUP, N_STATE
    vec = lambda n: pl.BlockSpec((GB, n), lambda i: (i, 0))
    blk = lambda *s: pl.BlockSpec((GB,) + s, lambda i: (i,) + (0,) * len(s))
    hbm = pl.BlockSpec(memory_space=pl.ANY)
    slab = (w_in.shape[0] // (g // GB), w_in.shape[1])
    return pl.pallas_call(
        _ssm_prep_kernel,
        grid=(g // GB,),
        in_specs=[vec(1), vec(p), vec(p), blk(p, c), blk(p, c), blk(c, p), blk(c, p), hbm],
        out_specs=[blk(CW, CW), blk(CW, SW), blk(CW, SW), blk(SW, SW), blk(SW, SW), blk(4, SW), hbm],
        out_shape=[jax.ShapeDtypeStruct((g, CW, CW), BF16),
                   jax.ShapeDtypeStruct((g, CW, SW), BF16),
                   jax.ShapeDtypeStruct((g, CW, SW), BF16),
                   jax.ShapeDtypeStruct((g, SW, SW), BF16),
                   jax.ShapeDtypeStruct((g, SW, SW), BF16),
                   jax.ShapeDtypeStruct((g, 4, SW), F32),
                   jax.ShapeDtypeStruct(w_in.shape, BF16)],
        scratch_shapes=[pltpu.VMEM((W_RING,) + slab, F32), pltpu.VMEM((W_RING,) + slab, BF16),
                        pltpu.SemaphoreType.DMA((W_RING,)), pltpu.SemaphoreType.DMA((W_RING,))],
        compiler_params=pltpu.CompilerParams(dimension_semantics=("arbitrary",)),
        name="ssm_prep",
    )(log_dt[:, None], a_re, a_im, b_re, b_im, c_re, c_im, w_in)


SUB_IN = EXTRA_ROWS

TILES_PER_SEQ = SEQ // TM


def _in_proj_kernel(xp_ref, xe_ref, g0_ref, b0_ref, w_ref, p1_ref, p2_ref, cw_ref, gc_ref, wo_ref, wgl_ref,
                    h0_ref, u_ref, uc_ref, rc_ref, vt_ref, vs_ref, wob_ref, wglb_ref,
                    vbuf_scr, meta_scr, u_scr):
    s = pl.program_id(0)

    @pl.when(s == 0)
    def _():
        vbuf_scr[0:ROW8, :] = jnp.zeros((ROW8, D_CONV), F32)

    def tile(x_ref, is_extra):
        sub = SUB_IN if is_extra else TM
        rs = slice(0, sub)
        if is_extra:
            pad = slice(sub, TM)
            h0_ref[pad, :] = jnp.zeros((TM - sub, D_MODEL), F32)
            u_ref[pad, :] = jnp.zeros((TM - sub, D_SSM), F32)
            rc_ref[pad, :] = jnp.zeros((TM - sub, D_CONV), BF16)
        h0 = _layer_norm(x_ref[rs, :], g0_ref[...], b0_ref[...])
        h0_ref[rs, :] = h0
        proj = _dot(h0.astype(BF16), w_ref[...])
        part = lambda k: proj[:, k * D_SSM:(k + 1) * D_SSM]
        u_ref[rs, :] = part(0)
        for b in range(N_LANE_BLOCKS):
            u_scr[b, rs, :] = proj[:, b * SW:(b + 1) * SW]
            if is_extra:
                u_scr[b, sub:TM, :] = jnp.zeros((TM - sub, SW), F32)
        for b in range(N_LANE_BLOCKS):
            for h in range(2):
                src = [u_scr[b, pl.ds(GB * h + j, TILE_CHUNKS, stride=CHUNK), :] for j in range(GB)]
                for gl, w in enumerate(_block_transpose(src)):
                    uc_ref[b * GB + gl, :, h * SW:(h + 1) * SW] = w.astype(BF16)
        b_gate = part(1)
        v = part(2) * part(3)
        vbuf_scr[ROW8:ROW8 + sub, :] = v
        prev = vbuf_scr[0:ROW8 + sub, :]
        vm1 = pltpu.roll(prev, 1, axis=0)[ROW8:, :]
        vm2 = pltpu.roll(prev, 2, axis=0)[ROW8:, :]
        if is_extra:
            sample = lax.broadcasted_iota(jnp.int32, (sub, 1), 0) >= SAMPLE_ROW0
            vm1 = jnp.where(sample, p1_ref[...], vm1)
            vm2 = jnp.where(sample, p2_ref[...], vm2)
        conv = cw_ref[0:1, :] * vm2 + cw_ref[1:2, :] * vm1 + cw_ref[2:3, :] * v
        rc_ref[rs, :] = _rms_norm(b_gate * conv, gc_ref[...]).astype(BF16)

    @pl.when(s == 0)
    def _():
        tile(xe_ref, True)
        vs_ref[...] = vbuf_scr[ROW8:ROW8 + SUB_IN, :]
        meta_scr[...] = vbuf_scr[ROW8 + N_META - ROW8:ROW8 + N_META, :]
        vbuf_scr[0:ROW8, :] = meta_scr[...]

    @pl.when(s > 0)
    def _():
        tile(xp_ref, False)
        wob_ref[...] = wo_ref[...].astype(BF16)
        wglb_ref[...] = wgl_ref[...].astype(BF16)
        tail = vbuf_scr[TM:TM + ROW8, :]
        vt_ref[0] = tail
        vbuf_scr[0:ROW8, :] = jnp.where(s % TILES_PER_SEQ == 0, meta_scr[...], tail)


def _in_proj(xp, xe, g0, b0, w_in, p1, p2, cw, gc, w_out, w_glu):
    const = lambda s: (0, 0)
    slab = lambda s: (jnp.maximum(s - 1, 0), 0)
    row = lambda s: ((s + N_PROMPT_TILES) % N_TILES, 0)
    vec = lambda n: pl.BlockSpec((1, n), const)
    return pl.pallas_call(
        _in_proj_kernel,
        grid=(N_TILES,),
        in_specs=[pl.BlockSpec((TM, D_MODEL), lambda s: (jnp.maximum(s - 1, 0), 0)),
                  pl.BlockSpec((SUB_IN, D_MODEL), const),
                  vec(D_MODEL), vec(D_MODEL),
                  pl.BlockSpec((D_MODEL, 4 * D_SSM), const, pipeline_mode=pl.Buffered(1)),
                  pl.BlockSpec((SUB_IN, D_CONV), const), pl.BlockSpec((SUB_IN, D_CONV), const),
                  pl.BlockSpec((CONV_W, D_CONV), const), vec(D_CONV),
                  pl.BlockSpec((D_MODEL // N_PROMPT_TILES, D_MODEL), slab),
                  pl.BlockSpec((D_SSM // N_PROMPT_TILES, D_SSM), slab)],
        out_specs=[pl.BlockSpec((TM, D_MODEL), row), pl.BlockSpec((TM, D_SSM), row),
                   pl.BlockSpec((N_GROUPS, TILE_CHUNKS, CW), lambda s: (0, (s + N_PROMPT_TILES) % N_TILES, 0)),
                   pl.BlockSpec((TM, D_CONV), row),
                   pl.BlockSpec((1, ROW8, D_CONV), lambda s: (jnp.maximum(s - 1, 0), 0, 0)),
                   pl.BlockSpec((SUB_IN, D_CONV), const),
                   pl.BlockSpec((D_MODEL // N_PROMPT_TILES, D_MODEL), slab),
                   pl.BlockSpec((D_SSM // N_PROMPT_TILES, D_SSM), slab)],
        out_shape=[jax.ShapeDtypeStruct((M_PAD, D_MODEL), F32),
                   jax.ShapeDtypeStruct((M_PAD, D_SSM), F32),
                   jax.ShapeDtypeStruct((N_GROUPS, N_TILES * TILE_CHUNKS, CW), BF16),
                   jax.ShapeDtypeStruct((M_PAD, D_CONV), BF16),
                   jax.ShapeDtypeStruct((N_PROMPT_TILES, ROW8, D_CONV), F32),
                   jax.ShapeDtypeStruct((SUB_IN, D_CONV), F32),
                   jax.ShapeDtypeStruct((D_MODEL, D_MODEL), BF16),
                   jax.ShapeDtypeStruct((D_SSM, D_SSM), BF16)],
        scratch_shapes=[pltpu.VMEM((ROW8 + TM, D_CONV), F32), pltpu.VMEM((ROW8, D_CONV), F32),
                        pltpu.VMEM((N_LANE_BLOCKS, TM, SW), F32)],
        compiler_params=pltpu.CompilerParams(
            dimension_semantics=("arbitrary",), vmem_limit_bytes=VMEM_LIMIT),
        name="in_proj",
    )(xp, xe, g0, b0, w_in, p1, p2, cw, gc, w_out, w_glu)


SEQ_CHUNKS = SEQ // CHUNK
LHS_ROWS = N_TILES * TILE_CHUNKS
META_LHS = N_BATCH * SEQ_CHUNKS
PITCH = SEQ_CHUNKS + ROW8
REC_ROWS = N_BATCH * PITCH + ROW8
META_REC = N_BATCH * PITCH


def _ssm_scan_kernel(uc_ref, ue_ref, m_ref, wst_ref, cpow_ref, bpad_ref, cpad_ref, avec_ref, sre_ref, sim_ref,
                     yc_ref, ys_ref, sfin_ref, nre_ref, nim_ref, x_scr, xs_scr, sp_scr):
    def swap_halves(s):
        return pltpu.roll(s, N_STATE, axis=1)

    def to_rec(ref, gl, x):
        for n in range(N_BATCH):
            ref[gl, n * PITCH:n * PITCH + SEQ_CHUNKS, :] = x[n * SEQ_CHUNKS:(n + 1) * SEQ_CHUNKS]
        ref[gl, META_REC:META_REC + ROW8, :] = x[META_LHS:META_LHS + ROW8]

    for gl in range(GB):
        x = _dot(uc_ref[gl], wst_ref[gl])
        to_rec(x_scr, gl, x)
        to_rec(xs_scr, gl, swap_halves(x))

    ar16 = [avec_ref[gl, 0:1, :] for gl in range(GB)]
    ax16 = [avec_ref[gl, 1:2, :] for gl in range(GB)]

    def step(k, carry):
        new = []
        for gl in range(GB):
            s, t = carry[gl]
            rows = pl.ds(k, N_BATCH, stride=PITCH)
            sp_scr[gl, rows, :] = s
            new.append((ar16[gl] * s + ax16[gl] * t + x_scr[gl, rows, :],
                        ar16[gl] * t - ax16[gl] * s + xs_scr[gl, rows, :]))
        return tuple(new)

    first = lambda ref, gl: jnp.broadcast_to(ref[gl, META_REC:META_REC + 1, :], (N_BATCH, SW))
    init = tuple((first(x_scr, gl), first(xs_scr, gl)) for gl in range(GB))
    fin = lax.fori_loop(0, SEQ_CHUNKS, step, init)

    us = ue_ref[SAMPLE_ROW0:SAMPLE_ROW0 + N_SAMPLE, :].astype(BF16)
    ys = jnp.zeros((N_SAMPLE, SW), F32)
    for gl in range(GB):
        sfin_ref[gl] = fin[gl][0]
        sp = jnp.concatenate([sp_scr[gl, n * PITCH:n * PITCH + SEQ_CHUNKS, :] for n in range(N_BATCH)]
                             + [jnp.zeros((LHS_ROWS - META_LHS, SW), F32)], axis=0)
        yc_ref[gl] = _dot(uc_ref[gl], m_ref[gl]) + _dot_nt(sp.astype(BF16), cpow_ref[gl])
        lanes = slice(gl * N_STATE, (gl + 1) * N_STATE)
        s0 = jnp.concatenate([sre_ref[:, lanes], sim_ref[:, lanes]], axis=1)
        sn = (avec_ref[gl, 2:3, :] * s0 + avec_ref[gl, 3:4, :] * swap_halves(s0)
              + _dot(us, bpad_ref[gl]))
        nre_ref[:, lanes] = sn[:, 0:N_STATE]
        nim_ref[:, lanes] = sn[:, N_STATE:SW]
        ys = ys + _dot_nt(sn.astype(BF16), cpad_ref[gl])
    ys_ref[...] = ys


def _ssm_scan(uc, u, m, wst, cpow, bpad, cpad, avec, s_re, s_im):
    blk = lambda *s: pl.BlockSpec((GB,) + s, lambda i: (i,) + (0,) * len(s))
    extra = pl.BlockSpec((EXTRA_ROWS, SW), lambda i: (N_PROMPT_ROWS // EXTRA_ROWS, i))
    st = pl.BlockSpec((N_SAMPLE, GB * N_STATE), lambda i: (0, i))
    st_shape = jax.ShapeDtypeStruct((N_SAMPLE, N_GROUPS * N_STATE), F32)
    return pl.pallas_call(
        _ssm_scan_kernel,
        grid=(N_GROUPS // GB,),
        in_specs=[blk(LHS_ROWS, CW), extra, blk(CW, CW), blk(CW, SW), blk(CW, SW), blk(SW, SW), blk(SW, SW),
                  blk(4, SW), st, st],
        out_specs=[blk(LHS_ROWS, CW), pl.BlockSpec((N_SAMPLE, SW), lambda i: (0, i)), blk(N_BATCH, SW), st, st],
        out_shape=[jax.ShapeDtypeStruct((N_GROUPS, LHS_ROWS, CW), F32),
                   jax.ShapeDtypeStruct((N_SAMPLE, D_SSM), F32),
                   jax.ShapeDtypeStruct((N_GROUPS, N_BATCH, SW), F32), st_shape, st_shape],
        scratch_shapes=[pltpu.VMEM((GB, REC_ROWS, SW), F32), pltpu.VMEM((GB, REC_ROWS, SW), F32),
                        pltpu.VMEM((GB, REC_ROWS, SW), F32)],
        compiler_params=pltpu.CompilerParams(
            dimension_semantics=("arbitrary",), vmem_limit_bytes=VMEM_LIMIT),
        name="ssm_scan",
    )(uc, u, m, wst, cpow, bpad, cpad, avec, s_re, s_im)


SUB_MIX = 512


def _mix_out_kernel(yc0_ref, ycn_ref, ys_ref, u_ref, rc_ref, h0_ref, dskip_ref, wglu_ref, gs_ref, wout_ref,
                    g1_ref, b1_ref, x1_ref, x1b_ref, merged_scr, y_scr, ynext_scr):
    i = pl.program_id(0)

    def to_rows(yc_ref, dst_scr):
        for b in range(N_LANE_BLOCKS):
            for h in range(2):
                src = [yc_ref[b * GB + gl, :, h * SW:(h + 1) * SW] for gl in range(GB)]
                for j, w in enumerate(_block_transpose(src)):
                    dst_scr[b, pl.ds(GB * h + j, TILE_CHUNKS, stride=CHUNK), :] = w

    def sub_tile(r0, n):
        rs = slice(r0, r0 + n)
        y = jnp.concatenate([y_scr[b, rs, :] for b in range(N_LANE_BLOCKS)], axis=1)
        ys = _gelu_tanh(y + dskip_ref[...] * u_ref[rs, :])
        gate = _dot(ys.astype(BF16), wglu_ref[...])
        merged_scr[rs, 0:D_SSM] = _rms_norm(ys * _sigmoid(gate), gs_ref[...]).astype(BF16)
        merged_scr[rs, D_SSM:D_MODEL] = rc_ref[rs, :]
        mo = _dot(merged_scr[rs, :], wout_ref[...])
        x1 = _layer_norm(ALPHA * h0_ref[rs, :] + mo, g1_ref[...], b1_ref[...])
        x1_ref[rs, :] = x1
        x1b_ref[rs, :] = x1.astype(BF16)

    @pl.when(i == 0)
    def _():
        to_rows(yc0_ref, y_scr)

    @pl.when(i < N_PROMPT_TILES)
    def _():
        for r0 in range(0, TM, SUB_MIX):
            sub_tile(r0, SUB_MIX)
        to_rows(ycn_ref, ynext_scr)
        y_scr[...] = ynext_scr[...]

    @pl.when(i == N_PROMPT_TILES - 1)
    def _():
        for b in range(N_LANE_BLOCKS):
            y_scr[b, SAMPLE_ROW0:SAMPLE_ROW0 + N_SAMPLE, :] = ys_ref[:, b * SW:(b + 1) * SW]

    @pl.when(i == N_PROMPT_TILES)
    def _():
        sub_tile(0, EXTRA_ROWS)
        x1_ref[EXTRA_ROWS:TM, :] = jnp.zeros((TM - EXTRA_ROWS, D_MODEL), F32)
        x1b_ref[EXTRA_ROWS:TM, :] = jnp.zeros((TM - EXTRA_ROWS, D_MODEL), BF16)


def _mix_out(yc, ys, u, rc, h0, dskip, wglu, gs, wout, g1, b1):
    row = lambda i: (i, 0)
    const = lambda i: (0, 0)
    vec = lambda n: pl.BlockSpec((1, n), const)
    chunk_blk = lambda f: pl.BlockSpec((N_GROUPS, TILE_CHUNKS, CW), f)
    return pl.pallas_call(
        _mix_out_kernel,
        grid=(N_TILES,),
        in_specs=[chunk_blk(lambda i: (0, 0, 0)),
                  chunk_blk(lambda i: (0, jnp.minimum(i + 1, N_PROMPT_TILES), 0)),
                  pl.BlockSpec((N_SAMPLE, D_SSM), const), pl.BlockSpec((TM, D_SSM), row),
                  pl.BlockSpec((TM, D_CONV), row), pl.BlockSpec((TM, D_MODEL), row), vec(D_SSM),
                  pl.BlockSpec((D_SSM, D_SSM), const, pipeline_mode=pl.Buffered(1)),
                  vec(D_SSM),
                  pl.BlockSpec((D_MODEL, D_MODEL), const, pipeline_mode=pl.Buffered(1)),
                  vec(D_MODEL), vec(D_MODEL)],
        out_specs=[pl.BlockSpec((TM, D_MODEL), row), pl.BlockSpec((TM, D_MODEL), row)],
        out_shape=[jax.ShapeDtypeStruct((M_PAD, D_MODEL), F32),
                   jax.ShapeDtypeStruct((M_PAD, D_MODEL), BF16)],
        scratch_shapes=[pltpu.VMEM((TM, D_MODEL), BF16), pltpu.VMEM((N_LANE_BLOCKS, TM, SW), F32),
                        pltpu.VMEM((N_LANE_BLOCKS, TM, SW), F32)],
        compiler_params=pltpu.CompilerParams(
            dimension_semantics=("arbitrary",), vmem_limit_bytes=VMEM_LIMIT),
        name="mix_out",
    )(yc, yc, ys, u, rc, h0, dskip, wglu, gs, wout, g1, b1)


M_HALF = M_PAD // 2
SUB_UP = M_HALF // 4
TF_UP = 256
SUB_DOWN = 256
M_REAL_HALF = M_HALF - (TM - EXTRA_ROWS)


def _ffn_up_kernel(xb_ref, wg_ref, wu_ref, wd_ref, act_ref, wdb_ref):
    @pl.when(pl.program_id(0) == 0)
    def _():
        wdb_ref[...] = wd_ref[...].astype(BF16)

    wgb = wg_ref[...].astype(BF16)
    wub = wu_ref[...].astype(BF16)

    def rows(lo, hi):
        xb = xb_ref[lo:hi, :]
        gate = _dot(xb, wgb)
        up = _dot(xb, wub)
        act_ref[lo:hi, :] = (gate * _sigmoid(gate) * up).astype(BF16)

    @pl.when(pl.program_id(0) == 0)
    def _():
        for r0 in range(0, M_HALF, SUB_UP):
            rows(r0, r0 + SUB_UP)

    @pl.when(pl.program_id(0) == 1)
    def _():
        for r0 in range(0, M_REAL_HALF, SUB_UP):
            rows(r0, min(r0 + SUB_UP, M_REAL_HALF))
        act_ref[M_REAL_HALF:M_HALF, :] = jnp.zeros((M_HALF - M_REAL_HALF, TF_UP), BF16)


def _ffn_up(x1b, wg, wu, wd):
    wcol = pl.BlockSpec((D_MODEL, TF_UP), lambda h, f: (0, f))
    wrow = pl.BlockSpec((TF_UP, D_MODEL), lambda h, f: (jnp.where(h == 0, f, D_FF // TF_UP - 1), 0))
    return pl.pallas_call(
        _ffn_up_kernel,
        grid=(M_PAD // M_HALF, D_FF // TF_UP),
        in_specs=[pl.BlockSpec((M_HALF, D_MODEL), lambda h, f: (h, 0), pipeline_mode=pl.Buffered(1)),
                  wcol, wcol, wrow],
        out_specs=[pl.BlockSpec((M_HALF, TF_UP), lambda h, f: (h, f)), wrow],
        out_shape=[jax.ShapeDtypeStruct((M_PAD, D_FF), BF16),
                   jax.ShapeDtypeStruct((D_FF, D_MODEL), BF16)],
        compiler_params=pltpu.CompilerParams(
            dimension_semantics=("arbitrary", "arbitrary"), vmem_limit_bytes=VMEM_LIMIT),
        name="ffn_up",
    )(x1b, wg, wu, wd)


def _ffn_down_kernel(act_ref, x_ref, wd_ref, g2_ref, b2_ref, yp_ref, ye_ref):
    i = pl.program_id(0)

    def tile(o_ref, rows):
        for r0 in range(0, rows, SUB_DOWN):
            rs = slice(r0, r0 + SUB_DOWN)
            z = ALPHA * x_ref[rs, :] + _dot(act_ref[rs, :], wd_ref[...])
            o_ref[rs, :] = _layer_norm(z, g2_ref[...], b2_ref[...])

    @pl.when(i < N_PROMPT_TILES)
    def _():
        tile(yp_ref, TM)

    @pl.when(i == N_PROMPT_TILES)
    def _():
        tile(ye_ref, EXTRA_ROWS)


def _ffn_down(act, x1, wd, g2, b2):
    row = lambda i: (i, 0)
    const = lambda i: (0, 0)
    return pl.pallas_call(
        _ffn_down_kernel,
        grid=(N_TILES,),
        in_specs=[pl.BlockSpec((TM, D_FF), row), pl.BlockSpec((TM, D_MODEL), row),
                  pl.BlockSpec((D_FF, D_MODEL), const, pipeline_mode=pl.Buffered(1)),
                  pl.BlockSpec((1, D_MODEL), const), pl.BlockSpec((1, D_MODEL), const)],
        out_specs=[pl.BlockSpec((TM, D_MODEL), lambda i: (jnp.minimum(i, N_PROMPT_TILES - 1), 0)),
                   pl.BlockSpec((EXTRA_ROWS, D_MODEL), const)],
        out_shape=[jax.ShapeDtypeStruct((N_PROMPT_ROWS, D_MODEL), F32),
                   jax.ShapeDtypeStruct((EXTRA_ROWS, D_MODEL), F32)],
        compiler_params=pltpu.CompilerParams(
            dimension_semantics=("arbitrary",), vmem_limit_bytes=VMEM_LIMIT),
        name="ffn_down",
    )(act, x1, wd, g2, b2)


def _pad_rows(a, n):
    return jnp.pad(a, ((0, n - a.shape[0]), (0, 0)))


def kernel(x_prompt, x_sample, state_ssm_re, state_ssm_im, state_conv, meta_tokens, ln0_g, ln0_b,
           w_in, ssm_a_re, ssm_a_im, ssm_log_dt, ssm_b_re, ssm_b_im, ssm_c_re, ssm_c_im, ssm_d,
           ssm_w_glu, conv_w, g_ssm_out, g_conv_out, w_out, ln1_g, ln1_b, w_gate, w_up, w_down,
           ln2_g, ln2_b):
    assert x_prompt.shape == (N_BATCH, SEQ, D_MODEL) and x_sample.shape == (N_SAMPLE, 1, D_MODEL)
    assert w_in.shape[0] == 1, "single layer"
    g, p, c = N_GROUPS, N_STATE, GROUP
    vec = lambda a: a.reshape(1, -1)

    xp = x_prompt.reshape(N_PROMPT_ROWS, D_MODEL)
    xe = _pad_rows(jnp.concatenate([meta_tokens, x_sample[:, 0, :]], axis=0), SUB_IN)
    g0, b0 = vec(ln0_g), vec(ln0_b)

    m, wst, cpow, bpad, cpad, avec, w_in_bf = _ssm_prep(ssm_log_dt[0], ssm_a_re[0], ssm_a_im[0], ssm_b_re[0],
                                                        ssm_b_im[0], ssm_c_re[0], ssm_c_im[0], w_in[0])

    prev = jnp.pad(state_conv[0], ((SAMPLE_ROW0, SUB_IN - SAMPLE_ROW0 - N_SAMPLE), (0, 0), (0, 0)))
    h0, u, uc, rc, v_tail, v_extra, wout_bf, wglu_bf = _in_proj(
        xp, xe, g0, b0, w_in_bf, prev[:, 1], prev[:, 0], conv_w[0], vec(g_conv_out[0]),
        w_out[0], ssm_w_glu[0])

    flat = lambda a: a.reshape(N_SAMPLE, g * p)
    yc, ys, s_fin, ns_re, ns_im = _ssm_scan(uc, u, m, wst, cpow, bpad, cpad, avec,
                                            flat(state_ssm_re[0]), flat(state_ssm_im[0]))

    x1, x1b = _mix_out(yc, ys, u, rc, h0, vec(ssm_d[0]), wglu_bf, vec(g_ssm_out[0]),
                       wout_bf, vec(ln1_g[0]), vec(ln1_b[0]))

    act, wd_bf = _ffn_up(x1b, w_gate[0], w_up[0], w_down[0])
    yp, ye = _ffn_down(act, x1, wd_bf, vec(ln2_g[0]), vec(ln2_b[0]))

    y_prompt = yp.reshape(N_BATCH, SEQ, D_MODEL)
    y_sample = ye[SAMPLE_ROW0:SAMPLE_ROW0 + N_SAMPLE].reshape(N_SAMPLE, 1, D_MODEL)

    sd = state_ssm_re.dtype
    s_p = s_fin.transpose(1, 0, 2)
    unflat = lambda a: a.reshape(1, N_SAMPLE, g, p).astype(sd)
    cd = state_conv.dtype
    conv_p = v_tail[TILES_PER_SEQ - 1::TILES_PER_SEQ, ROW8 - (CONV_W - 1):].astype(cd)
    conv_s = jnp.stack([state_conv[0][:, 1], v_extra[SAMPLE_ROW0:SAMPLE_ROW0 + N_SAMPLE].astype(cd)], axis=1)
    return (y_prompt, y_sample,
            s_p[None, ..., :p].astype(sd), s_p[None, ..., p:].astype(sd), conv_p[None],
            unflat(ns_re), unflat(ns_im), conv_s[None])
```

```python
import jax
import jax.numpy as jnp
import numpy as np
from jax import lax
from jax.experimental import pallas as pl
from jax.experimental.pallas import tpu as pltpu

D_MODEL = 2048
D_SSM = 1024
D_CONV = 1024
N_GROUPS = 64
GROUP = 16
N_STATE = 64
N_META = 16
CONV_W = 3
D_FF = 5632
LN_EPS = 1e-5
RMS_EPS = 1e-6
ALPHA = 2.0 ** 0.25

N_BATCH = 4
SEQ = 2048
N_SAMPLE = 128
CHUNK = 16
ROW8 = 8
CW = CHUNK * GROUP
SW = 2 * N_STATE

TM = 512
N_PROMPT_ROWS = N_BATCH * SEQ
N_PROMPT_TILES = N_PROMPT_ROWS // TM
N_TILES = N_PROMPT_TILES + 1
M_PAD = N_TILES * TM
SAMPLE_ROW0 = N_META
GB = SW // GROUP
N_LANE_BLOCKS = D_SSM // SW
TILE_CHUNKS = TM // CHUNK
EXTRA_ROWS = 256
VMEM_LIMIT = 60 * 1024 * 1024

F32 = jnp.float32
BF16 = jnp.bfloat16


def _layer_norm(x, g, b):
    mu = jnp.mean(x, axis=-1, keepdims=True)
    xc = x - mu
    var = jnp.mean(xc * xc, axis=-1, keepdims=True)
    return xc * lax.rsqrt(var + LN_EPS) * g + b


def _rms_norm(x, g):
    return x * lax.rsqrt(jnp.mean(x * x, axis=-1, keepdims=True) + RMS_EPS) * g


def _gelu_tanh(x):
    c = np.sqrt(2.0 / np.pi)
    hx = 0.5 * x
    return hx + hx * jnp.tanh(x * (np.float32(c) + np.float32(c * 0.044715) * (x * x)))


def _sigmoid(x):
    return 1.0 / (1.0 + jnp.exp2(x * np.float32(-np.log2(np.e))))


def _dot(a, b):
    return jnp.dot(a, b, preferred_element_type=F32)


def _dot_nt(a, b):
    return lax.dot_general(a, b, (((1,), (1,)), ((), ())), preferred_element_type=F32)


def _block_transpose(a):
    a = list(a)
    blk = lax.broadcasted_iota(jnp.int32, a[0].shape, 1) // GROUP
    for d in (4, 2, 1):
        w = d * GROUP
        keep = (blk & d) == 0
        for j in range(GB):
            if j & d == 0:
                lo, hi = a[j], a[j + d]
                a[j] = jnp.where(keep, lo, pltpu.roll(hi, w, axis=1))
                a[j + d] = jnp.where(keep, pltpu.roll(lo, SW - w, axis=1), hi)
    return a


W_RING = 3
W_PARTS = 4


def _ssm_prep_kernel(logdt_ref, are_ref, aim_ref, bre_ref, bim_ref, cre_ref, cim_ref, win_hbm,
                     m_ref, wst_ref, cpow_ref, bpad_ref, cpad_ref, avec_ref, winb_hbm,
                     in_buf, out_buf, in_sem, out_sem):
    step = pl.program_id(0)
    n_slabs = N_GROUPS // GB
    slab_rows = in_buf.shape[1]

    part_rows = slab_rows // W_PARTS

    class _Copies:
        def __init__(self, copies):
            self.copies = copies

        def start(self):
            for cp in self.copies:
                cp.start()

        def wait(self):
            for cp in self.copies:
                cp.wait()

    def read(slab, slot):
        return _Copies([pltpu.make_async_copy(
            win_hbm.at[pl.ds(slab * slab_rows + q * part_rows, part_rows), :],
            in_buf.at[slot, pl.ds(q * part_rows, part_rows), :], in_sem.at[slot, q]) for q in range(W_PARTS)])

    def write(slab, slot):
        return _Copies([pltpu.make_async_copy(
            out_buf.at[slot, pl.ds(q * part_rows, part_rows), :],
            winb_hbm.at[pl.ds(slab * slab_rows + q * part_rows, part_rows), :],
            out_sem.at[slot, q]) for q in range(W_PARTS)])

    @pl.when(step == 0)
    def _():
        for k in range(W_RING):
            read(k, k).start()

    slot = step % W_RING
    read(step, slot).wait()

    @pl.when(step >= W_RING)
    def _():
        write(step - W_RING, slot).wait()

    out_buf[slot] = in_buf[slot].astype(BF16)
    write(step, slot).start()

    @pl.when(step + W_RING < n_slabs)
    def _():
        read(step + W_RING, slot).start()

    halves = lambda x, y: jnp.concatenate([x, y], axis=1)

    dt = jnp.exp(jnp.broadcast_to(logdt_ref[...], (GB, SW)))
    a_re = halves(are_ref[...], are_ref[...])
    a_im = halves(aim_ref[...], aim_ref[...])
    mag = jnp.exp(dt * a_re)
    ang = dt * a_im
    ab_re = mag * jnp.cos(ang)
    ab_im = mag * jnp.sin(ang)
    n_re = ab_re - 1.0
    den = a_re * a_re + a_im * a_im
    f_re = (n_re * a_re + ab_im * a_im) / den
    f_im = (ab_im * a_re - n_re * a_im) / den
    sgn = jnp.where(lax.broadcasted_iota(jnp.int32, (GB, SW), 1) < N_STATE, -1.0, 1.0)

    pr, pi = jnp.ones_like(ab_re), jnp.zeros_like(ab_re)
    pw = [(pr, pi)]
    for _ in range(CHUNK):
        pr, pi = pr * ab_re - pi * ab_im, pr * ab_im + pi * ab_re
        pw.append((pr, sgn * pi))
    fx = sgn * f_im
    lane = lax.broadcasted_iota(jnp.int32, (GROUP, CW), 1)

    for gl in range(GB):
        row = lambda x: x[gl:gl + 1, :]
        bt_re, bt_im = bre_ref[gl].T, bim_ref[gl].T
        b1, b2 = halves(bt_re, bt_im), halves(bt_im, bt_re)
        c1, c2 = halves(cre_ref[gl], cim_ref[gl]), halves(cim_ref[gl], cre_ref[gl])
        bb1 = row(f_re) * b1 + row(fx) * b2
        bb2 = row(f_re) * b2 - row(fx) * b1
        cc1 = -row(sgn) * c1
        cc2 = -row(sgn) * c2

        cq = []
        for s in range(CHUNK):
            wr, wx = pw[CHUNK - 1 - s]
            wst_ref[gl, s * GROUP:(s + 1) * GROUP, :] = (row(wr) * bb1 + row(wx) * bb2).astype(BF16)
            qr, qx = pw[s + 1]
            cpow_ref[gl, s * GROUP:(s + 1) * GROUP, :] = (row(qr) * cc1 + row(qx) * cc2).astype(BF16)
            dr, dx = pw[s]
            cq.append(row(dr) * cc1 + row(dx) * cc2)
        kt = lax.dot_general(bb1, jnp.concatenate(cq, axis=0), (((1,), (1,)), ((), ())),
                             precision=lax.Precision.HIGHEST, preferred_element_type=F32)
        for i in range(CHUNK):
            blk = kt if i == 0 else jnp.where(lane >= i * GROUP, pltpu.roll(kt, i * GROUP, axis=1), 0.0)
            m_ref[gl, i * GROUP:(i + 1) * GROUP, :] = blk.astype(BF16)

        zeros = jnp.zeros((SW, SW), BF16)
        bpad_ref[gl] = zeros
        cpad_ref[gl] = zeros
        bpad_ref[gl, gl * GROUP:(gl + 1) * GROUP, :] = bb1.astype(BF16)
        cpad_ref[gl, gl * GROUP:(gl + 1) * GROUP, :] = cq[0].astype(BF16)

        avec_ref[gl, 0:1, :] = row(pw[CHUNK][0])
        avec_ref[gl, 1:2, :] = row(pw[CHUNK][1])
        avec_ref[gl, 2:3, :] = row(pw[1][0])
        avec_ref[gl, 3:4, :] = row(pw[1][1])

    @pl.when(step == n_slabs - 1)
    def _():
        for t in range(n_slabs - W_RING, n_slabs):
            write(t, t % W_RING).wait()


def _ssm_prep(log_dt, a_re, a_im, b_re, b_im, c_re, c_im, w_in):
    g, c, p = N_GROUPS, GROUP, N_STATE
    vec = lambda n: pl.BlockSpec((GB, n), lambda i: (i, 0))
    blk = lambda *s: pl.BlockSpec((GB,) + s, lambda i: (i,) + (0,) * len(s))
    hbm = pl.BlockSpec(memory_space=pl.ANY)
    slab = (w_in.shape[0] // (g // GB), w_in.shape[1])
    return pl.pallas_call(
        _ssm_prep_kernel,
        grid=(g // GB,),
        in_specs=[vec(1), vec(p), vec(p), blk(p, c), blk(p, c), blk(c, p), blk(c, p), hbm],
        out_specs=[blk(CW, CW), blk(CW, SW), blk(CW, SW), blk(SW, SW), blk(SW, SW), blk(4, SW), hbm],
        out_shape=[jax.ShapeDtypeStruct((g, CW, CW), BF16),
                   jax.ShapeDtypeStruct((g, CW, SW), BF16),
                   jax.ShapeDtypeStruct((g, CW, SW), BF16),
                   jax.ShapeDtypeStruct((g, SW, SW), BF16),
                   jax.ShapeDtypeStruct((g, SW, SW), BF16),
                   jax.ShapeDtypeStruct((g, 4, SW), F32),
                   jax.ShapeDtypeStruct(w_in.shape, BF16)],
        scratch_shapes=[pltpu.VMEM((W_RING,) + slab, F32), pltpu.VMEM((W_RING,) + slab, BF16),
                        pltpu.SemaphoreType.DMA((W_RING, W_PARTS)), pltpu.SemaphoreType.DMA((W_RING, W_PARTS))],
        compiler_params=pltpu.CompilerParams(dimension_semantics=("arbitrary",)),
        name="ssm_prep",
    )(log_dt[:, None], a_re, a_im, b_re, b_im, c_re, c_im, w_in)


SUB_IN = EXTRA_ROWS

TILES_PER_SEQ = SEQ // TM


def _in_proj_kernel(xp_ref, xe_ref, g0_ref, b0_ref, w_ref, p1_ref, p2_ref, cw_ref, gc_ref, wo_ref, wgl_ref,
                    h0_ref, u_ref, uc_ref, rc_ref, vt_ref, vs_ref, wob_ref, wglb_ref,
                    vbuf_scr, meta_scr, u_scr):
    s = pl.program_id(0)

    @pl.when(s == 0)
    def _():
        vbuf_scr[0:ROW8, :] = jnp.zeros((ROW8, D_CONV), F32)

    def tile(x_ref, is_extra):
        sub = SUB_IN if is_extra else TM
        rs = slice(0, sub)
        if is_extra:
            pad = slice(sub, TM)
            h0_ref[pad, :] = jnp.zeros((TM - sub, D_MODEL), F32)
            u_ref[pad, :] = jnp.zeros((TM - sub, D_SSM), F32)
            rc_ref[pad, :] = jnp.zeros((TM - sub, D_CONV), BF16)
        h0 = _layer_norm(x_ref[rs, :], g0_ref[...], b0_ref[...])
        h0_ref[rs, :] = h0
        proj = _dot(h0.astype(BF16), w_ref[...])
        part = lambda k: proj[:, k * D_SSM:(k + 1) * D_SSM]
        u_ref[rs, :] = part(0)
        for b in range(N_LANE_BLOCKS):
            u_scr[b, rs, :] = proj[:, b * SW:(b + 1) * SW]
            if is_extra:
                u_scr[b, sub:TM, :] = jnp.zeros((TM - sub, SW), F32)
        for b in range(N_LANE_BLOCKS):
            for h in range(2):
                src = [u_scr[b, pl.ds(GB * h + j, TILE_CHUNKS, stride=CHUNK), :] for j in range(GB)]
                for gl, w in enumerate(_block_transpose(src)):
                    uc_ref[b * GB + gl, :, h * SW:(h + 1) * SW] = w.astype(BF16)
        b_gate = part(1)
        v = part(2) * part(3)
        vbuf_scr[ROW8:ROW8 + sub, :] = v
        prev = vbuf_scr[0:ROW8 + sub, :]
        vm1 = pltpu.roll(prev, 1, axis=0)[ROW8:, :]
        vm2 = pltpu.roll(prev, 2, axis=0)[ROW8:, :]
        if is_extra:
            sample = lax.broadcasted_iota(jnp.int32, (sub, 1), 0) >= SAMPLE_ROW0
            vm1 = jnp.where(sample, p1_ref[...], vm1)
            vm2 = jnp.where(sample, p2_ref[...], vm2)
        conv = cw_ref[0:1, :] * vm2 + cw_ref[1:2, :] * vm1 + cw_ref[2:3, :] * v
        rc_ref[rs, :] = _rms_norm(b_gate * conv, gc_ref[...]).astype(BF16)

    @pl.when(s == 0)
    def _():
        tile(xe_ref, True)
        vs_ref[...] = vbuf_scr[ROW8:ROW8 + SUB_IN, :]
        meta_scr[...] = vbuf_scr[ROW8 + N_META - ROW8:ROW8 + N_META, :]
        vbuf_scr[0:ROW8, :] = meta_scr[...]

    @pl.when(s > 0)
    def _():
        tile(xp_ref, False)
        wob_ref[...] = wo_ref[...].astype(BF16)
        wglb_ref[...] = wgl_ref[...].astype(BF16)
        tail = vbuf_scr[TM:TM + ROW8, :]
        vt_ref[0] = tail
        vbuf_scr[0:ROW8, :] = jnp.where(s % TILES_PER_SEQ == 0, meta_scr[...], tail)


def _in_proj(xp, xe, g0, b0, w_in, p1, p2, cw, gc, w_out, w_glu):
    const = lambda s: (0, 0)
    slab = lambda s: (jnp.maximum(s - 1, 0), 0)
    row = lambda s: ((s + N_PROMPT_TILES) % N_TILES, 0)
    vec = lambda n: pl.BlockSpec((1, n), const)
    return pl.pallas_call(
        _in_proj_kernel,
        grid=(N_TILES,),
        in_specs=[pl.BlockSpec((TM, D_MODEL), lambda s: (jnp.maximum(s - 1, 0), 0)),
                  pl.BlockSpec((SUB_IN, D_MODEL), const),
                  vec(D_MODEL), vec(D_MODEL),
                  pl.BlockSpec((D_MODEL, 4 * D_SSM), const, pipeline_mode=pl.Buffered(1)),
                  pl.BlockSpec((SUB_IN, D_CONV), const), pl.BlockSpec((SUB_IN, D_CONV), const),
                  pl.BlockSpec((CONV_W, D_CONV), const), vec(D_CONV),
                  pl.BlockSpec((D_MODEL // N_PROMPT_TILES, D_MODEL), slab),
                  pl.BlockSpec((D_SSM // N_PROMPT_TILES, D_SSM), slab)],
        out_specs=[pl.BlockSpec((TM, D_MODEL), row), pl.BlockSpec((TM, D_SSM), row),
                   pl.BlockSpec((N_GROUPS, TILE_CHUNKS, CW), lambda s: (0, (s + N_PROMPT_TILES) % N_TILES, 0)),
                   pl.BlockSpec((TM, D_CONV), row),
                   pl.BlockSpec((1, ROW8, D_CONV), lambda s: (jnp.maximum(s - 1, 0), 0, 0)),
                   pl.BlockSpec((SUB_IN, D_CONV), const),
                   pl.BlockSpec((D_MODEL // N_PROMPT_TILES, D_MODEL), slab),
                   pl.BlockSpec((D_SSM // N_PROMPT_TILES, D_SSM), slab)],
        out_shape=[jax.ShapeDtypeStruct((M_PAD, D_MODEL), F32),
                   jax.ShapeDtypeStruct((M_PAD, D_SSM), F32),
                   jax.ShapeDtypeStruct((N_GROUPS, N_TILES * TILE_CHUNKS, CW), BF16),
                   jax.ShapeDtypeStruct((M_PAD, D_CONV), BF16),
                   jax.ShapeDtypeStruct((N_PROMPT_TILES, ROW8, D_CONV), F32),
                   jax.ShapeDtypeStruct((SUB_IN, D_CONV), F32),
                   jax.ShapeDtypeStruct((D_MODEL, D_MODEL), BF16),
                   jax.ShapeDtypeStruct((D_SSM, D_SSM), BF16)],
        scratch_shapes=[pltpu.VMEM((ROW8 + TM, D_CONV), F32), pltpu.VMEM((ROW8, D_CONV), F32),
                        pltpu.VMEM((N_LANE_BLOCKS, TM, SW), F32)],
        compiler_params=pltpu.CompilerParams(
            dimension_semantics=("arbitrary",), vmem_limit_bytes=VMEM_LIMIT),
        name="in_proj",
    )(xp, xe, g0, b0, w_in, p1, p2, cw, gc, w_out, w_glu)


SEQ_CHUNKS = SEQ // CHUNK
LHS_ROWS = N_TILES * TILE_CHUNKS
META_LHS = N_BATCH * SEQ_CHUNKS
PITCH = SEQ_CHUNKS + ROW8
REC_ROWS = N_BATCH * PITCH + ROW8
META_REC = N_BATCH * PITCH


def _ssm_scan_kernel(uc_ref, ue_ref, m_ref, wst_ref, cpow_ref, bpad_ref, cpad_ref, avec_ref, sre_ref, sim_ref,
                     yc_ref, ys_ref, sfin_ref, nre_ref, nim_ref, x_scr, xs_scr, sp_scr):
    def swap_halves(s):
        return pltpu.roll(s, N_STATE, axis=1)

    def to_rec(ref, gl, x):
        for n in range(N_BATCH):
            ref[gl, n * PITCH:n * PITCH + SEQ_CHUNKS, :] = x[n * SEQ_CHUNKS:(n + 1) * SEQ_CHUNKS]
        ref[gl, META_REC:META_REC + ROW8, :] = x[META_LHS:META_LHS + ROW8]

    for gl in range(GB):
        x = _dot(uc_ref[gl], wst_ref[gl])
        to_rec(x_scr, gl, x)
        to_rec(xs_scr, gl, swap_halves(x))

    ar16 = [avec_ref[gl, 0:1, :] for gl in range(GB)]
    ax16 = [avec_ref[gl, 1:2, :] for gl in range(GB)]

    def step(k, carry):
        new = []
        for gl in range(GB):
            s, t = carry[gl]
            rows = pl.ds(k, N_BATCH, stride=PITCH)
            sp_scr[gl, rows, :] = s
            new.append((ar16[gl] * s + ax16[gl] * t + x_scr[gl, rows, :],
                        ar16[gl] * t - ax16[gl] * s + xs_scr[gl, rows, :]))
        return tuple(new)

    first = lambda ref, gl: jnp.broadcast_to(ref[gl, META_REC:META_REC + 1, :], (N_BATCH, SW))
    init = tuple((first(x_scr, gl), first(xs_scr, gl)) for gl in range(GB))
    fin = lax.fori_loop(0, SEQ_CHUNKS, step, init)

    us = ue_ref[SAMPLE_ROW0:SAMPLE_ROW0 + N_SAMPLE, :].astype(BF16)
    ys = jnp.zeros((N_SAMPLE, SW), F32)
    for gl in range(GB):
        sfin_ref[gl] = fin[gl][0]
        sp = jnp.concatenate([sp_scr[gl, n * PITCH:n * PITCH + SEQ_CHUNKS, :] for n in range(N_BATCH)]
                             + [jnp.zeros((LHS_ROWS - META_LHS, SW), F32)], axis=0)
        yc_ref[gl] = _dot(uc_ref[gl], m_ref[gl]) + _dot_nt(sp.astype(BF16), cpow_ref[gl])
        lanes = slice(gl * N_STATE, (gl + 1) * N_STATE)
        s0 = jnp.concatenate([sre_ref[:, lanes], sim_ref[:, lanes]], axis=1)
        sn = (avec_ref[gl, 2:3, :] * s0 + avec_ref[gl, 3:4, :] * swap_halves(s0)
              + _dot(us, bpad_ref[gl]))
        nre_ref[:, lanes] = sn[:, 0:N_STATE]
        nim_ref[:, lanes] = sn[:, N_STATE:SW]
        ys = ys + _dot_nt(sn.astype(BF16), cpad_ref[gl])
    ys_ref[...] = ys


def _ssm_scan(uc, u, m, wst, cpow, bpad, cpad, avec, s_re, s_im):
    blk = lambda *s: pl.BlockSpec((GB,) + s, lambda i: (i,) + (0,) * len(s))
    extra = pl.BlockSpec((EXTRA_ROWS, SW), lambda i: (N_PROMPT_ROWS // EXTRA_ROWS, i))
    st = pl.BlockSpec((N_SAMPLE, GB * N_STATE), lambda i: (0, i))
    st_shape = jax.ShapeDtypeStruct((N_SAMPLE, N_GROUPS * N_STATE), F32)
    return pl.pallas_call(
        _ssm_scan_kernel,
        grid=(N_GROUPS // GB,),
        in_specs=[blk(LHS_ROWS, CW), extra, blk(CW, CW), blk(CW, SW), blk(CW, SW), blk(SW, SW), blk(SW, SW),
                  blk(4, SW), st, st],
        out_specs=[blk(LHS_ROWS, CW), pl.BlockSpec((N_SAMPLE, SW), lambda i: (0, i)), blk(N_BATCH, SW), st, st],
        out_shape=[jax.ShapeDtypeStruct((N_GROUPS, LHS_ROWS, CW), F32),
                   jax.ShapeDtypeStruct((N_SAMPLE, D_SSM), F32),
                   jax.ShapeDtypeStruct((N_GROUPS, N_BATCH, SW), F32), st_shape, st_shape],
        scratch_shapes=[pltpu.VMEM((GB, REC_ROWS, SW), F32), pltpu.VMEM((GB, REC_ROWS, SW), F32),
                        pltpu.VMEM((GB, REC_ROWS, SW), F32)],
        compiler_params=pltpu.CompilerParams(
            dimension_semantics=("arbitrary",), vmem_limit_bytes=VMEM_LIMIT),
        name="ssm_scan",
    )(uc, u, m, wst, cpow, bpad, cpad, avec, s_re, s_im)


SUB_MIX = 512


def _mix_out_kernel(yc0_ref, ycn_ref, ys_ref, u_ref, rc_ref, h0_ref, dskip_ref, wglu_ref, gs_ref, wout_ref,
                    g1_ref, b1_ref, x1_ref, x1b_ref, merged_scr, y_scr, ynext_scr):
    i = pl.program_id(0)

    def to_rows(yc_ref, dst_scr):
        for b in range(N_LANE_BLOCKS):
            for h in range(2):
                src = [yc_ref[b * GB + gl, :, h * SW:(h + 1) * SW] for gl in range(GB)]
                for j, w in enumerate(_block_transpose(src)):
                    dst_scr[b, pl.ds(GB * h + j, TILE_CHUNKS, stride=CHUNK), :] = w

    def sub_tile(r0, n):
        rs = slice(r0, r0 + n)
        y = jnp.concatenate([y_scr[b, rs, :] for b in range(N_LANE_BLOCKS)], axis=1)
        ys = _gelu_tanh(y + dskip_ref[...] * u_ref[rs, :])
        gate = _dot(ys.astype(BF16), wglu_ref[...])
        merged_scr[rs, 0:D_SSM] = _rms_norm(ys * _sigmoid(gate), gs_ref[...]).astype(BF16)
        merged_scr[rs, D_SSM:D_MODEL] = rc_ref[rs, :]
        mo = _dot(merged_scr[rs, :], wout_ref[...])
        x1 = _layer_norm(ALPHA * h0_ref[rs, :] + mo, g1_ref[...], b1_ref[...])
        x1_ref[rs, :] = x1
        x1b_ref[rs, :] = x1.astype(BF16)

    @pl.when(i == 0)
    def _():
        to_rows(yc0_ref, y_scr)

    @pl.when(i < N_PROMPT_TILES)
    def _():
        for r0 in range(0, TM, SUB_MIX):
            sub_tile(r0, SUB_MIX)
        to_rows(ycn_ref, ynext_scr)
        y_scr[...] = ynext_scr[...]

    @pl.when(i == N_PROMPT_TILES - 1)
    def _():
        for b in range(N_LANE_BLOCKS):
            y_scr[b, SAMPLE_ROW0:SAMPLE_ROW0 + N_SAMPLE, :] = ys_ref[:, b * SW:(b + 1) * SW]

    @pl.when(i == N_PROMPT_TILES)
    def _():
        sub_tile(0, EXTRA_ROWS)
        x1_ref[EXTRA_ROWS:TM, :] = jnp.zeros((TM - EXTRA_ROWS, D_MODEL), F32)
        x1b_ref[EXTRA_ROWS:TM, :] = jnp.zeros((TM - EXTRA_ROWS, D_MODEL), BF16)


def _mix_out(yc, ys, u, rc, h0, dskip, wglu, gs, wout, g1, b1):
    row = lambda i: (i, 0)
    const = lambda i: (0, 0)
    vec = lambda n: pl.BlockSpec((1, n), const)
    chunk_blk = lambda f: pl.BlockSpec((N_GROUPS, TILE_CHUNKS, CW), f)
    return pl.pallas_call(
        _mix_out_kernel,
        grid=(N_TILES,),
        in_specs=[chunk_blk(lambda i: (0, 0, 0)),
                  chunk_blk(lambda i: (0, jnp.minimum(i + 1, N_PROMPT_TILES), 0)),
                  pl.BlockSpec((N_SAMPLE, D_SSM), const), pl.BlockSpec((TM, D_SSM), row),
                  pl.BlockSpec((TM, D_CONV), row), pl.BlockSpec((TM, D_MODEL), row), vec(D_SSM),
                  pl.BlockSpec((D_SSM, D_SSM), const, pipeline_mode=pl.Buffered(1)),
                  vec(D_SSM),
                  pl.BlockSpec((D_MODEL, D_MODEL), const, pipeline_mode=pl.Buffered(1)),
                  vec(D_MODEL), vec(D_MODEL)],
        out_specs=[pl.BlockSpec((TM, D_MODEL), row), pl.BlockSpec((TM, D_MODEL), row)],
        out_shape=[jax.ShapeDtypeStruct((M_PAD, D_MODEL), F32),
                   jax.ShapeDtypeStruct((M_PAD, D_MODEL), BF16)],
        scratch_shapes=[pltpu.VMEM((TM, D_MODEL), BF16), pltpu.VMEM((N_LANE_BLOCKS, TM, SW), F32),
                        pltpu.VMEM((N_LANE_BLOCKS, TM, SW), F32)],
        compiler_params=pltpu.CompilerParams(
            dimension_semantics=("arbitrary",), vmem_limit_bytes=VMEM_LIMIT),
        name="mix_out",
    )(yc, yc, ys, u, rc, h0, dskip, wglu, gs, wout, g1, b1)


M_HALF = M_PAD // 2
SUB_UP = M_HALF // 4
TF_UP = 256
SUB_DOWN = 256
M_REAL_HALF = M_HALF - (TM - EXTRA_ROWS)


def _ffn_up_kernel(xb_ref, wg_ref, wu_ref, wd_ref, act_ref, wdb_ref):
    @pl.when(pl.program_id(0) == 0)
    def _():
        wdb_ref[...] = wd_ref[...].astype(BF16)

    wgb = wg_ref[...].astype(BF16)
    wub = wu_ref[...].astype(BF16)

    def rows(lo, hi):
        xb = xb_ref[lo:hi, :]
        gate = _dot(xb, wgb)
        up = _dot(xb, wub)
        act_ref[lo:hi, :] = (gate * _sigmoid(gate) * up).astype(BF16)

    @pl.when(pl.program_id(0) == 0)
    def _():
        for r0 in range(0, M_HALF, SUB_UP):
            rows(r0, r0 + SUB_UP)

    @pl.when(pl.program_id(0) == 1)
    def _():
        for r0 in range(0, M_REAL_HALF, SUB_UP):
            rows(r0, min(r0 + SUB_UP, M_REAL_HALF))
        act_ref[M_REAL_HALF:M_HALF, :] = jnp.zeros((M_HALF - M_REAL_HALF, TF_UP), BF16)


def _ffn_up(x1b, wg, wu, wd):
    wcol = pl.BlockSpec((D_MODEL, TF_UP), lambda h, f: (0, f))
    wrow = pl.BlockSpec((TF_UP, D_MODEL), lambda h, f: (jnp.where(h == 0, f, D_FF // TF_UP - 1), 0))
    return pl.pallas_call(
        _ffn_up_kernel,
        grid=(M_PAD // M_HALF, D_FF // TF_UP),
        in_specs=[pl.BlockSpec((M_HALF, D_MODEL), lambda h, f: (h, 0), pipeline_mode=pl.Buffered(1)),
                  wcol, wcol, wrow],
        out_specs=[pl.BlockSpec((M_HALF, TF_UP), lambda h, f: (h, f)), wrow],
        out_shape=[jax.ShapeDtypeStruct((M_PAD, D_FF), BF16),
                   jax.ShapeDtypeStruct((D_FF, D_MODEL), BF16)],
        compiler_params=pltpu.CompilerParams(
            dimension_semantics=("arbitrary", "arbitrary"), vmem_limit_bytes=VMEM_LIMIT),
        name="ffn_up",
    )(x1b, wg, wu, wd)


def _ffn_down_kernel(act_ref, x_ref, wd_ref, g2_ref, b2_ref, yp_ref, ye_ref):
    i = pl.program_id(0)

    def tile(o_ref, rows):
        for r0 in range(0, rows, SUB_DOWN):
            rs = slice(r0, r0 + SUB_DOWN)
            z = ALPHA * x_ref[rs, :] + _dot(act_ref[rs, :], wd_ref[...])
            o_ref[rs, :] = _layer_norm(z, g2_ref[...], b2_ref[...])

    @pl.when(i < N_PROMPT_TILES)
    def _():
        tile(yp_ref, TM)

    @pl.when(i == N_PROMPT_TILES)
    def _():
        tile(ye_ref, EXTRA_ROWS)


def _ffn_down(act, x1, wd, g2, b2):
    row = lambda i: (i, 0)
    const = lambda i: (0, 0)
    return pl.pallas_call(
        _ffn_down_kernel,
        grid=(N_TILES,),
        in_specs=[pl.BlockSpec((TM, D_FF), row), pl.BlockSpec((TM, D_MODEL), row),
                  pl.BlockSpec((D_FF, D_MODEL), const, pipeline_mode=pl.Buffered(1)),
                  pl.BlockSpec((1, D_MODEL), const), pl.BlockSpec((1, D_MODEL), const)],
        out_specs=[pl.BlockSpec((TM, D_MODEL), lambda i: (jnp.minimum(i, N_PROMPT_TILES - 1), 0)),
                   pl.BlockSpec((EXTRA_ROWS, D_MODEL), const)],
        out_shape=[jax.ShapeDtypeStruct((N_PROMPT_ROWS, D_MODEL), F32),
                   jax.ShapeDtypeStruct((EXTRA_ROWS, D_MODEL), F32)],
        compiler_params=pltpu.CompilerParams(
            dimension_semantics=("arbitrary",), vmem_limit_bytes=VMEM_LIMIT),
        name="ffn_down",
    )(act, x1, wd, g2, b2)


def _pad_rows(a, n):
    return jnp.pad(a, ((0, n - a.shape[0]), (0, 0)))


def kernel(x_prompt, x_sample, state_ssm_re, state_ssm_im, state_conv, meta_tokens, ln0_g, ln0_b,
           w_in, ssm_a_re, ssm_a_im, ssm_log_dt, ssm_b_re, ssm_b_im, ssm_c_re, ssm_c_im, ssm_d,
           ssm_w_glu, conv_w, g_ssm_out, g_conv_out, w_out, ln1_g, ln1_b, w_gate, w_up, w_down,
           ln2_g, ln2_b):
    assert x_prompt.shape == (N_BATCH, SEQ, D_MODEL) and x_sample.shape == (N_SAMPLE, 1, D_MODEL)
    assert w_in.shape[0] == 1, "single layer"
    g, p, c = N_GROUPS, N_STATE, GROUP
    vec = lambda a: a.reshape(1, -1)

    xp = x_prompt.reshape(N_PROMPT_ROWS, D_MODEL)
    xe = _pad_rows(jnp.concatenate([meta_tokens, x_sample[:, 0, :]], axis=0), SUB_IN)
    g0, b0 = vec(ln0_g), vec(ln0_b)

    m, wst, cpow, bpad, cpad, avec, w_in_bf = _ssm_prep(ssm_log_dt[0], ssm_a_re[0], ssm_a_im[0], ssm_b_re[0],
                                                        ssm_b_im[0], ssm_c_re[0], ssm_c_im[0], w_in[0])

    prev = jnp.pad(state_conv[0], ((SAMPLE_ROW0, SUB_IN - SAMPLE_ROW0 - N_SAMPLE), (0, 0), (0, 0)))
    h0, u, uc, rc, v_tail, v_extra, wout_bf, wglu_bf = _in_proj(
        xp, xe, g0, b0, w_in_bf, prev[:, 1], prev[:, 0], conv_w[0], vec(g_conv_out[0]),
        w_out[0], ssm_w_glu[0])

    flat = lambda a: a.reshape(N_SAMPLE, g * p)
    yc, ys, s_fin, ns_re, ns_im = _ssm_scan(uc, u, m, wst, cpow, bpad, cpad, avec,
                                            flat(state_ssm_re[0]), flat(state_ssm_im[0]))

    x1, x1b = _mix_out(yc, ys, u, rc, h0, vec(ssm_d[0]), wglu_bf, vec(g_ssm_out[0]),
                       wout_bf, vec(ln1_g[0]), vec(ln1_b[0]))

    act, wd_bf = _ffn_up(x1b, w_gate[0], w_up[0], w_down[0])
    yp, ye = _ffn_down(act, x1, wd_bf, vec(ln2_g[0]), vec(ln2_b[0]))

    y_prompt = yp.reshape(N_BATCH, SEQ, D_MODEL)
    y_sample = ye[SAMPLE_ROW0:SAMPLE_ROW0 + N_SAMPLE].reshape(N_SAMPLE, 1, D_MODEL)

    sd = state_ssm_re.dtype
    s_p = s_fin.transpose(1, 0, 2)
    unflat = lambda a: a.reshape(1, N_SAMPLE, g, p).astype(sd)
    cd = state_conv.dtype
    conv_p = v_tail[TILES_PER_SEQ - 1::TILES_PER_SEQ, ROW8 - (CONV_W - 1):].astype(cd)
    conv_s = jnp.stack([state_conv[0][:, 1], v_extra[SAMPLE_ROW0:SAMPLE_ROW0 + N_SAMPLE].astype(cd)], axis=1)
    return (y_prompt, y_sample,
            s_p[None, ..., :p].astype(sd), s_p[None, ..., p:].astype(sd), conv_p[None],
            unflat(ns_re), unflat(ns_im), conv_s[None])
```

```python
import jax
import jax.numpy as jnp
import numpy as np
from jax import lax
from jax.experimental import pallas as pl
from jax.experimental.pallas import tpu as pltpu

D_MODEL = 2048
D_SSM = 1024
D_CONV = 1024
N_GROUPS = 64
GROUP = 16
N_STATE = 64
N_META = 16
CONV_W = 3
D_FF = 5632
LN_EPS = 1e-5
RMS_EPS = 1e-6
ALPHA = 2.0 ** 0.25

N_BATCH = 4
SEQ = 2048
N_SAMPLE = 128
CHUNK = 16
ROW8 = 8
CW = CHUNK * GROUP
SW = 2 * N_STATE

TM = 512
N_PROMPT_ROWS = N_BATCH * SEQ
N_PROMPT_TILES = N_PROMPT_ROWS // TM
N_TILES = N_PROMPT_TILES + 1
M_PAD = N_TILES * TM
SAMPLE_ROW0 = N_META
GB = SW // GROUP
N_LANE_BLOCKS = D_SSM // SW
TILE_CHUNKS = TM // CHUNK
EXTRA_ROWS = 256
VMEM_LIMIT = 60 * 1024 * 1024

F32 = jnp.float32
BF16 = jnp.bfloat16


def _layer_norm(x, g, b):
    mu = jnp.mean(x, axis=-1, keepdims=True)
    xc = x - mu
    var = jnp.mean(xc * xc, axis=-1, keepdims=True)
    return xc * lax.rsqrt(var + LN_EPS) * g + b


def _rms_norm(x, g):
    return x * lax.rsqrt(jnp.mean(x * x, axis=-1, keepdims=True) + RMS_EPS) * g


def _gelu_tanh(x):
    c = np.sqrt(2.0 / np.pi)
    hx = 0.5 * x
    return hx + hx * jnp.tanh(x * (np.float32(c) + np.float32(c * 0.044715) * (x * x)))


def _sigmoid(x):
    return 1.0 / (1.0 + jnp.exp2(x * np.float32(-np.log2(np.e))))


def _dot(a, b):
    return jnp.dot(a, b, preferred_element_type=F32)


def _dot_nt(a, b):
    return lax.dot_general(a, b, (((1,), (1,)), ((), ())), preferred_element_type=F32)


def _block_transpose(a):
    a = list(a)
    blk = lax.broadcasted_iota(jnp.int32, a[0].shape, 1) // GROUP
    for d in (4, 2, 1):
        w = d * GROUP
        keep = (blk & d) == 0
        for j in range(GB):
            if j & d == 0:
                lo, hi = a[j], a[j + d]
                a[j] = jnp.where(keep, lo, pltpu.roll(hi, w, axis=1))
                a[j + d] = jnp.where(keep, pltpu.roll(lo, SW - w, axis=1), hi)
    return a


def _ssm_prep_kernel(logdt_ref, are_ref, aim_ref, bre_ref, bim_ref, cre_ref, cim_ref, win_ref,
                     m_ref, wst_ref, cpow_ref, bpad_ref, cpad_ref, avec_ref, winb_ref):
    winb_ref[...] = win_ref[...].astype(BF16)
    halves = lambda x, y: jnp.concatenate([x, y], axis=1)

    dt = jnp.exp(jnp.broadcast_to(logdt_ref[...], (GB, SW)))
    a_re = halves(are_ref[...], are_ref[...])
    a_im = halves(aim_ref[...], aim_ref[...])
    mag = jnp.exp(dt * a_re)
    ang = dt * a_im
    ab_re = mag * jnp.cos(ang)
    ab_im = mag * jnp.sin(ang)
    n_re = ab_re - 1.0
    den = a_re * a_re + a_im * a_im
    f_re = (n_re * a_re + ab_im * a_im) / den
    f_im = (ab_im * a_re - n_re * a_im) / den
    sgn = jnp.where(lax.broadcasted_iota(jnp.int32, (GB, SW), 1) < N_STATE, -1.0, 1.0)

    pr, pi = jnp.ones_like(ab_re), jnp.zeros_like(ab_re)
    pw = [(pr, pi)]
    for _ in range(CHUNK):
        pr, pi = pr * ab_re - pi * ab_im, pr * ab_im + pi * ab_re
        pw.append((pr, sgn * pi))
    fx = sgn * f_im
    lane = lax.broadcasted_iota(jnp.int32, (GROUP, CW), 1)

    for gl in range(GB):
        row = lambda x: x[gl:gl + 1, :]
        bt_re, bt_im = bre_ref[gl].T, bim_ref[gl].T
        b1, b2 = halves(bt_re, bt_im), halves(bt_im, bt_re)
        c1, c2 = halves(cre_ref[gl], cim_ref[gl]), halves(cim_ref[gl], cre_ref[gl])
        bb1 = row(f_re) * b1 + row(fx) * b2
        bb2 = row(f_re) * b2 - row(fx) * b1
        cc1 = -row(sgn) * c1
        cc2 = -row(sgn) * c2

        cq = []
        for s in range(CHUNK):
            wr, wx = pw[CHUNK - 1 - s]
            wst_ref[gl, s * GROUP:(s + 1) * GROUP, :] = (row(wr) * bb1 + row(wx) * bb2).astype(BF16)
            qr, qx = pw[s + 1]
            cpow_ref[gl, s * GROUP:(s + 1) * GROUP, :] = (row(qr) * cc1 + row(qx) * cc2).astype(BF16)
            dr, dx = pw[s]
            cq.append(row(dr) * cc1 + row(dx) * cc2)
        kt = lax.dot_general(bb1, jnp.concatenate(cq, axis=0), (((1,), (1,)), ((), ())),
                             precision=lax.Precision.HIGHEST, preferred_element_type=F32)
        for i in range(CHUNK):
            blk = kt if i == 0 else jnp.where(lane >= i * GROUP, pltpu.roll(kt, i * GROUP, axis=1), 0.0)
            m_ref[gl, i * GROUP:(i + 1) * GROUP, :] = blk.astype(BF16)

        bpad_ref[gl] = bb1.astype(BF16)
        cpad_ref[gl] = cq[0].astype(BF16)

        avec_ref[gl, 0:1, :] = row(pw[CHUNK][0])
        avec_ref[gl, 1:2, :] = row(pw[CHUNK][1])
        avec_ref[gl, 2:3, :] = row(pw[1][0])
        avec_ref[gl, 3:4, :] = row(pw[1][1])


def _ssm_prep(log_dt, a_re, a_im, b_re, b_im, c_re, c_im, w_in):
    g, c, p = N_GROUPS, GROUP, N_STATE
    vec = lambda n: pl.BlockSpec((GB, n), lambda i: (i, 0))
    blk = lambda *s: pl.BlockSpec((GB,) + s, lambda i: (i,) + (0,) * len(s))
    slab = pl.BlockSpec((w_in.shape[0] // (g // GB), w_in.shape[1]), lambda i: (i, 0))
    return pl.pallas_call(
        _ssm_prep_kernel,
        grid=(g // GB,),
        in_specs=[vec(1), vec(p), vec(p), blk(p, c), blk(p, c), blk(c, p), blk(c, p), slab],
        out_specs=[blk(CW, CW), blk(CW, SW), blk(CW, SW), blk(c, SW), blk(c, SW), blk(4, SW), slab],
        out_shape=[jax.ShapeDtypeStruct((g, CW, CW), BF16),
                   jax.ShapeDtypeStruct((g, CW, SW), BF16),
                   jax.ShapeDtypeStruct((g, CW, SW), BF16),
                   jax.ShapeDtypeStruct((g, c, SW), BF16),
                   jax.ShapeDtypeStruct((g, c, SW), BF16),
                   jax.ShapeDtypeStruct((g, 4, SW), F32),
                   jax.ShapeDtypeStruct(w_in.shape, BF16)],
        name="ssm_prep",
    )(log_dt[:, None], a_re, a_im, b_re, b_im, c_re, c_im, w_in)


SUB_IN = EXTRA_ROWS

TILES_PER_SEQ = SEQ // TM


def _in_proj_kernel(xp_ref, xe_ref, g0_ref, b0_ref, w_ref, p1_ref, p2_ref, cw_ref, gc_ref, wo_ref, wgl_ref,
                    h0_ref, u_ref, uc_ref, rc_ref, vt_ref, vs_ref, wob_ref, wglb_ref,
                    vbuf_scr, meta_scr, u_scr):
    s = pl.program_id(0)

    @pl.when(s == 0)
    def _():
        vbuf_scr[0:ROW8, :] = jnp.zeros((ROW8, D_CONV), F32)

    def tile(x_ref, is_extra):
        sub = SUB_IN if is_extra else TM
        rs = slice(0, sub)
        if is_extra:
            pad = slice(sub, TM)
            h0_ref[pad, :] = jnp.zeros((TM - sub, D_MODEL), F32)
            u_ref[pad, :] = jnp.zeros((TM - sub, D_SSM), F32)
            rc_ref[pad, :] = jnp.zeros((TM - sub, D_CONV), BF16)
        h0 = _layer_norm(x_ref[rs, :], g0_ref[...], b0_ref[...])
        h0_ref[rs, :] = h0
        proj = _dot(h0.astype(BF16), w_ref[...])
        part = lambda k: proj[:, k * D_SSM:(k + 1) * D_SSM]
        u_ref[rs, :] = part(0)
        for b in range(N_LANE_BLOCKS):
            u_scr[b, rs, :] = proj[:, b * SW:(b + 1) * SW]
            if is_extra:
                u_scr[b, sub:TM, :] = jnp.zeros((TM - sub, SW), F32)
        for b in range(N_LANE_BLOCKS):
            for h in range(2):
                src = [u_scr[b, pl.ds(GB * h + j, TILE_CHUNKS, stride=CHUNK), :] for j in range(GB)]
                for gl, w in enumerate(_block_transpose(src)):
                    uc_ref[b * GB + gl, :, h * SW:(h + 1) * SW] = w.astype(BF16)
        b_gate = part(1)
        v = part(2) * part(3)
        vbuf_scr[ROW8:ROW8 + sub, :] = v
        prev = vbuf_scr[0:ROW8 + sub, :]
        vm1 = pltpu.roll(prev, 1, axis=0)[ROW8:, :]
        vm2 = pltpu.roll(prev, 2, axis=0)[ROW8:, :]
        if is_extra:
            sample = lax.broadcasted_iota(jnp.int32, (sub, 1), 0) >= SAMPLE_ROW0
            vm1 = jnp.where(sample, p1_ref[...], vm1)
            vm2 = jnp.where(sample, p2_ref[...], vm2)
        conv = cw_ref[0:1, :] * vm2 + cw_ref[1:2, :] * vm1 + cw_ref[2:3, :] * v
        rc_ref[rs, :] = _rms_norm(b_gate * conv, gc_ref[...]).astype(BF16)

    @pl.when(s == 0)
    def _():
        tile(xe_ref, True)
        vs_ref[...] = vbuf_scr[ROW8:ROW8 + SUB_IN, :]
        meta_scr[...] = vbuf_scr[ROW8 + N_META - ROW8:ROW8 + N_META, :]
        vbuf_scr[0:ROW8, :] = meta_scr[...]

    @pl.when(s > 0)
    def _():
        tile(xp_ref, False)
        wob_ref[...] = wo_ref[...].astype(BF16)
        wglb_ref[...] = wgl_ref[...].astype(BF16)
        tail = vbuf_scr[TM:TM + ROW8, :]
        vt_ref[0] = tail
        vbuf_scr[0:ROW8, :] = jnp.where(s % TILES_PER_SEQ == 0, meta_scr[...], tail)


def _in_proj(xp, xe, g0, b0, w_in, p1, p2, cw, gc, w_out, w_glu):
    const = lambda s: (0, 0)
    slab = lambda s: (jnp.maximum(s - 1, 0), 0)
    row = lambda s: ((s + N_PROMPT_TILES) % N_TILES, 0)
    vec = lambda n: pl.BlockSpec((1, n), const)
    return pl.pallas_call(
        _in_proj_kernel,
        grid=(N_TILES,),
        in_specs=[pl.BlockSpec((TM, D_MODEL), lambda s: (jnp.maximum(s - 1, 0), 0)),
                  pl.BlockSpec((SUB_IN, D_MODEL), const),
                  vec(D_MODEL), vec(D_MODEL),
                  pl.BlockSpec((D_MODEL, 4 * D_SSM), const, pipeline_mode=pl.Buffered(1)),
                  pl.BlockSpec((SUB_IN, D_CONV), const), pl.BlockSpec((SUB_IN, D_CONV), const),
                  pl.BlockSpec((CONV_W, D_CONV), const), vec(D_CONV),
                  pl.BlockSpec((D_MODEL // N_PROMPT_TILES, D_MODEL), slab),
                  pl.BlockSpec((D_SSM // N_PROMPT_TILES, D_SSM), slab)],
        out_specs=[pl.BlockSpec((TM, D_MODEL), row), pl.BlockSpec((TM, D_SSM), row),
                   pl.BlockSpec((N_GROUPS, TILE_CHUNKS, CW), lambda s: (0, (s + N_PROMPT_TILES) % N_TILES, 0)),
                   pl.BlockSpec((TM, D_CONV), row),
                   pl.BlockSpec((1, ROW8, D_CONV), lambda s: (jnp.maximum(s - 1, 0), 0, 0)),
                   pl.BlockSpec((SUB_IN, D_CONV), const),
                   pl.BlockSpec((D_MODEL // N_PROMPT_TILES, D_MODEL), slab),
                   pl.BlockSpec((D_SSM // N_PROMPT_TILES, D_SSM), slab)],
        out_shape=[jax.ShapeDtypeStruct((M_PAD, D_MODEL), F32),
                   jax.ShapeDtypeStruct((M_PAD, D_SSM), F32),
                   jax.ShapeDtypeStruct((N_GROUPS, N_TILES * TILE_CHUNKS, CW), BF16),
                   jax.ShapeDtypeStruct((M_PAD, D_CONV), BF16),
                   jax.ShapeDtypeStruct((N_PROMPT_TILES, ROW8, D_CONV), F32),
                   jax.ShapeDtypeStruct((SUB_IN, D_CONV), F32),
                   jax.ShapeDtypeStruct((D_MODEL, D_MODEL), BF16),
                   jax.ShapeDtypeStruct((D_SSM, D_SSM), BF16)],
        scratch_shapes=[pltpu.VMEM((ROW8 + TM, D_CONV), F32), pltpu.VMEM((ROW8, D_CONV), F32),
                        pltpu.VMEM((N_LANE_BLOCKS, TM, SW), F32)],
        compiler_params=pltpu.CompilerParams(
            dimension_semantics=("arbitrary",), vmem_limit_bytes=VMEM_LIMIT),
        name="in_proj",
    )(xp, xe, g0, b0, w_in, p1, p2, cw, gc, w_out, w_glu)


SEQ_CHUNKS = SEQ // CHUNK
LHS_ROWS = N_TILES * TILE_CHUNKS
META_LHS = N_BATCH * SEQ_CHUNKS
PITCH = SEQ_CHUNKS + ROW8
REC_ROWS = N_BATCH * PITCH + ROW8
META_REC = N_BATCH * PITCH


def _ssm_scan_kernel(uc_ref, ue_ref, m_ref, wst_ref, cpow_ref, bpad_ref, cpad_ref, avec_ref, sre_ref, sim_ref,
                     yc_ref, ys_ref, sfin_ref, nre_ref, nim_ref, x_scr, xs_scr, sp_scr, pad_scr):
    def swap_halves(s):
        return pltpu.roll(s, N_STATE, axis=1)

    def to_rec(ref, gl, x):
        for n in range(N_BATCH):
            ref[gl, n * PITCH:n * PITCH + SEQ_CHUNKS, :] = x[n * SEQ_CHUNKS:(n + 1) * SEQ_CHUNKS]
        ref[gl, META_REC:META_REC + ROW8, :] = x[META_LHS:META_LHS + ROW8]

    for gl in range(GB):
        x = _dot(uc_ref[gl], wst_ref[gl])
        to_rec(x_scr, gl, x)
        to_rec(xs_scr, gl, swap_halves(x))

    ar16 = [avec_ref[gl, 0:1, :] for gl in range(GB)]
    ax16 = [avec_ref[gl, 1:2, :] for gl in range(GB)]

    def step(k, carry):
        new = []
        for gl in range(GB):
            s, t = carry[gl]
            rows = pl.ds(k, N_BATCH, stride=PITCH)
            sp_scr[gl, rows, :] = s
            new.append((ar16[gl] * s + ax16[gl] * t + x_scr[gl, rows, :],
                        ar16[gl] * t - ax16[gl] * s + xs_scr[gl, rows, :]))
        return tuple(new)

    first = lambda ref, gl: jnp.broadcast_to(ref[gl, META_REC:META_REC + 1, :], (N_BATCH, SW))
    init = tuple((first(x_scr, gl), first(xs_scr, gl)) for gl in range(GB))
    fin = lax.fori_loop(0, SEQ_CHUNKS, step, init)

    us = ue_ref[SAMPLE_ROW0:SAMPLE_ROW0 + N_SAMPLE, :].astype(BF16)
    ys = jnp.zeros((N_SAMPLE, SW), F32)
    pad_scr[...] = jnp.zeros(pad_scr.shape, BF16)
    for gl in range(GB):
        pad_scr[gl, 0, gl * GROUP:(gl + 1) * GROUP, :] = bpad_ref[gl]
        pad_scr[gl, 1, gl * GROUP:(gl + 1) * GROUP, :] = cpad_ref[gl]
    for gl in range(GB):
        sfin_ref[gl] = fin[gl][0]
        sp = jnp.concatenate([sp_scr[gl, n * PITCH:n * PITCH + SEQ_CHUNKS, :] for n in range(N_BATCH)]
                             + [jnp.zeros((LHS_ROWS - META_LHS, SW), F32)], axis=0)
        yc_ref[gl] = _dot(uc_ref[gl], m_ref[gl]) + _dot_nt(sp.astype(BF16), cpow_ref[gl])
        lanes = slice(gl * N_STATE, (gl + 1) * N_STATE)
        s0 = jnp.concatenate([sre_ref[:, lanes], sim_ref[:, lanes]], axis=1)
        sn = (avec_ref[gl, 2:3, :] * s0 + avec_ref[gl, 3:4, :] * swap_halves(s0)
              + _dot(us, pad_scr[gl, 0]))
        nre_ref[:, lanes] = sn[:, 0:N_STATE]
        nim_ref[:, lanes] = sn[:, N_STATE:SW]
        ys = ys + _dot_nt(sn.astype(BF16), pad_scr[gl, 1])
    ys_ref[...] = ys


def _ssm_scan(uc, u, m, wst, cpow, bpad, cpad, avec, s_re, s_im):
    blk = lambda *s: pl.BlockSpec((GB,) + s, lambda i: (i,) + (0,) * len(s))
    extra = pl.BlockSpec((EXTRA_ROWS, SW), lambda i: (N_PROMPT_ROWS // EXTRA_ROWS, i))
    st = pl.BlockSpec((N_SAMPLE, GB * N_STATE), lambda i: (0, i))
    st_shape = jax.ShapeDtypeStruct((N_SAMPLE, N_GROUPS * N_STATE), F32)
    return pl.pallas_call(
        _ssm_scan_kernel,
        grid=(N_GROUPS // GB,),
        in_specs=[blk(LHS_ROWS, CW), extra, blk(CW, CW), blk(CW, SW), blk(CW, SW), blk(GROUP, SW), blk(GROUP, SW),
                  blk(4, SW), st, st],
        out_specs=[blk(LHS_ROWS, CW), pl.BlockSpec((N_SAMPLE, SW), lambda i: (0, i)), blk(N_BATCH, SW), st, st],
        out_shape=[jax.ShapeDtypeStruct((N_GROUPS, LHS_ROWS, CW), F32),
                   jax.ShapeDtypeStruct((N_SAMPLE, D_SSM), F32),
                   jax.ShapeDtypeStruct((N_GROUPS, N_BATCH, SW), F32), st_shape, st_shape],
        scratch_shapes=[pltpu.VMEM((GB, REC_ROWS, SW), F32), pltpu.VMEM((GB, REC_ROWS, SW), F32),
                        pltpu.VMEM((GB, REC_ROWS, SW), F32), pltpu.VMEM((GB, 2, SW, SW), BF16)],
        compiler_params=pltpu.CompilerParams(
            dimension_semantics=("arbitrary",), vmem_limit_bytes=VMEM_LIMIT),
        name="ssm_scan",
    )(uc, u, m, wst, cpow, bpad, cpad, avec, s_re, s_im)


SUB_MIX = 512


def _mix_out_kernel(yc0_ref, ycn_ref, ys_ref, u_ref, rc_ref, h0_ref, dskip_ref, wglu_ref, gs_ref, wout_ref,
                    g1_ref, b1_ref, x1_ref, x1b_ref, merged_scr, y_scr, ynext_scr):
    i = pl.program_id(0)

    def to_rows(yc_ref, dst_scr):
        for b in range(N_LANE_BLOCKS):
            for h in range(2):
                src = [yc_ref[b * GB + gl, :, h * SW:(h + 1) * SW] for gl in range(GB)]
                for j, w in enumerate(_block_transpose(src)):
                    dst_scr[b, pl.ds(GB * h + j, TILE_CHUNKS, stride=CHUNK), :] = w

    def sub_tile(r0, n):
        rs = slice(r0, r0 + n)
        y = jnp.concatenate([y_scr[b, rs, :] for b in range(N_LANE_BLOCKS)], axis=1)
        ys = _gelu_tanh(y + dskip_ref[...] * u_ref[rs, :])
        gate = _dot(ys.astype(BF16), wglu_ref[...])
        merged_scr[rs, 0:D_SSM] = _rms_norm(ys * _sigmoid(gate), gs_ref[...]).astype(BF16)
        merged_scr[rs, D_SSM:D_MODEL] = rc_ref[rs, :]
        mo = _dot(merged_scr[rs, :], wout_ref[...])
        x1 = _layer_norm(ALPHA * h0_ref[rs, :] + mo, g1_ref[...], b1_ref[...])
        x1_ref[rs, :] = x1
        x1b_ref[rs, :] = x1.astype(BF16)

    @pl.when(i == 0)
    def _():
        to_rows(yc0_ref, y_scr)

    @pl.when(i < N_PROMPT_TILES)
    def _():
        for r0 in range(0, TM, SUB_MIX):
            sub_tile(r0, SUB_MIX)
        to_rows(ycn_ref, ynext_scr)
        y_scr[...] = ynext_scr[...]

    @pl.when(i == N_PROMPT_TILES - 1)
    def _():
        for b in range(N_LANE_BLOCKS):
            y_scr[b, SAMPLE_ROW0:SAMPLE_ROW0 + N_SAMPLE, :] = ys_ref[:, b * SW:(b + 1) * SW]

    @pl.when(i == N_PROMPT_TILES)
    def _():
        sub_tile(0, EXTRA_ROWS)
        x1_ref[EXTRA_ROWS:TM, :] = jnp.zeros((TM - EXTRA_ROWS, D_MODEL), F32)
        x1b_ref[EXTRA_ROWS:TM, :] = jnp.zeros((TM - EXTRA_ROWS, D_MODEL), BF16)


def _mix_out(yc, ys, u, rc, h0, dskip, wglu, gs, wout, g1, b1):
    row = lambda i: (i, 0)
    const = lambda i: (0, 0)
    vec = lambda n: pl.BlockSpec((1, n), const)
    chunk_blk = lambda f: pl.BlockSpec((N_GROUPS, TILE_CHUNKS, CW), f)
    return pl.pallas_call(
        _mix_out_kernel,
        grid=(N_TILES,),
        in_specs=[chunk_blk(lambda i: (0, 0, 0)),
                  chunk_blk(lambda i: (0, jnp.minimum(i + 1, N_PROMPT_TILES), 0)),
                  pl.BlockSpec((N_SAMPLE, D_SSM), const), pl.BlockSpec((TM, D_SSM), row),
                  pl.BlockSpec((TM, D_CONV), row), pl.BlockSpec((TM, D_MODEL), row), vec(D_SSM),
                  pl.BlockSpec((D_SSM, D_SSM), const, pipeline_mode=pl.Buffered(1)),
                  vec(D_SSM),
                  pl.BlockSpec((D_MODEL, D_MODEL), const, pipeline_mode=pl.Buffered(1)),
                  vec(D_MODEL), vec(D_MODEL)],
        out_specs=[pl.BlockSpec((TM, D_MODEL), row), pl.BlockSpec((TM, D_MODEL), row)],
        out_shape=[jax.ShapeDtypeStruct((M_PAD, D_MODEL), F32),
                   jax.ShapeDtypeStruct((M_PAD, D_MODEL), BF16)],
        scratch_shapes=[pltpu.VMEM((TM, D_MODEL), BF16), pltpu.VMEM((N_LANE_BLOCKS, TM, SW), F32),
                        pltpu.VMEM((N_LANE_BLOCKS, TM, SW), F32)],
        compiler_params=pltpu.CompilerParams(
            dimension_semantics=("arbitrary",), vmem_limit_bytes=VMEM_LIMIT),
        name="mix_out",
    )(yc, yc, ys, u, rc, h0, dskip, wglu, gs, wout, g1, b1)


M_HALF = M_PAD // 2
SUB_UP = M_HALF // 4
TF_UP = 256
SUB_DOWN = 256
M_REAL_HALF = M_HALF - (TM - EXTRA_ROWS)


def _ffn_up_kernel(xb_ref, wg_ref, wu_ref, wd_ref, act_ref, wdb_ref):
    @pl.when(pl.program_id(0) == 0)
    def _():
        wdb_ref[...] = wd_ref[...].astype(BF16)

    wgb = wg_ref[...].astype(BF16)
    wub = wu_ref[...].astype(BF16)

    def rows(lo, hi):
        xb = xb_ref[lo:hi, :]
        gate = _dot(xb, wgb)
        up = _dot(xb, wub)
        act_ref[lo:hi, :] = (gate * _sigmoid(gate) * up).astype(BF16)

    @pl.when(pl.program_id(0) == 0)
    def _():
        for r0 in range(0, M_HALF, SUB_UP):
            rows(r0, r0 + SUB_UP)

    @pl.when(pl.program_id(0) == 1)
    def _():
        for r0 in range(0, M_REAL_HALF, SUB_UP):
            rows(r0, min(r0 + SUB_UP, M_REAL_HALF))
        act_ref[M_REAL_HALF:M_HALF, :] = jnp.zeros((M_HALF - M_REAL_HALF, TF_UP), BF16)


def _ffn_up(x1b, wg, wu, wd):
    wcol = pl.BlockSpec((D_MODEL, TF_UP), lambda h, f: (0, f))
    wrow = pl.BlockSpec((TF_UP, D_MODEL), lambda h, f: (jnp.where(h == 0, f, D_FF // TF_UP - 1), 0))
    return pl.pallas_call(
        _ffn_up_kernel,
        grid=(M_PAD // M_HALF, D_FF // TF_UP),
        in_specs=[pl.BlockSpec((M_HALF, D_MODEL), lambda h, f: (h, 0), pipeline_mode=pl.Buffered(1)),
                  wcol, wcol, wrow],
        out_specs=[pl.BlockSpec((M_HALF, TF_UP), lambda h, f: (h, f)), wrow],
        out_shape=[jax.ShapeDtypeStruct((M_PAD, D_FF), BF16),
                   jax.ShapeDtypeStruct((D_FF, D_MODEL), BF16)],
        compiler_params=pltpu.CompilerParams(
            dimension_semantics=("arbitrary", "arbitrary"), vmem_limit_bytes=VMEM_LIMIT),
        name="ffn_up",
    )(x1b, wg, wu, wd)


def _ffn_down_kernel(act_ref, x_ref, wd_ref, g2_ref, b2_ref, yp_ref, ye_ref):
    i = pl.program_id(0)

    def tile(o_ref, rows):
        for r0 in range(0, rows, SUB_DOWN):
            rs = slice(r0, r0 + SUB_DOWN)
            z = ALPHA * x_ref[rs, :] + _dot(act_ref[rs, :], wd_ref[...])
            o_ref[rs, :] = _layer_norm(z, g2_ref[...], b2_ref[...])

    @pl.when(i < N_PROMPT_TILES)
    def _():
        tile(yp_ref, TM)

    @pl.when(i == N_PROMPT_TILES)
    def _():
        tile(ye_ref, EXTRA_ROWS)


def _ffn_down(act, x1, wd, g2, b2):
    row = lambda i: (i, 0)
    const = lambda i: (0, 0)
    return pl.pallas_call(
        _ffn_down_kernel,
        grid=(N_TILES,),
        in_specs=[pl.BlockSpec((TM, D_FF), row), pl.BlockSpec((TM, D_MODEL), row),
                  pl.BlockSpec((D_FF, D_MODEL), const, pipeline_mode=pl.Buffered(1)),
                  pl.BlockSpec((1, D_MODEL), const), pl.BlockSpec((1, D_MODEL), const)],
        out_specs=[pl.BlockSpec((TM, D_MODEL), lambda i: (jnp.minimum(i, N_PROMPT_TILES - 1), 0)),
                   pl.BlockSpec((EXTRA_ROWS, D_MODEL), const)],
        out_shape=[jax.ShapeDtypeStruct((N_PROMPT_ROWS, D_MODEL), F32),
                   jax.ShapeDtypeStruct((EXTRA_ROWS, D_MODEL), F32)],
        compiler_params=pltpu.CompilerParams(
            dimension_semantics=("arbitrary",), vmem_limit_bytes=VMEM_LIMIT),
        name="ffn_down",
    )(act, x1, wd, g2, b2)


def _pad_rows(a, n):
    return jnp.pad(a, ((0, n - a.shape[0]), (0, 0)))


def kernel(x_prompt, x_sample, state_ssm_re, state_ssm_im, state_conv, meta_tokens, ln0_g, ln0_b,
           w_in, ssm_a_re, ssm_a_im, ssm_log_dt, ssm_b_re, ssm_b_im, ssm_c_re, ssm_c_im, ssm_d,
           ssm_w_glu, conv_w, g_ssm_out, g_conv_out, w_out, ln1_g, ln1_b, w_gate, w_up, w_down,
           ln2_g, ln2_b):
    assert x_prompt.shape == (N_BATCH, SEQ, D_MODEL) and x_sample.shape == (N_SAMPLE, 1, D_MODEL)
    assert w_in.shape[0] == 1, "single layer"
    g, p, c = N_GROUPS, N_STATE, GROUP
    vec = lambda a: a.reshape(1, -1)

    xp = x_prompt.reshape(N_PROMPT_ROWS, D_MODEL)
    xe = _pad_rows(jnp.concatenate([meta_tokens, x_sample[:, 0, :]], axis=0), SUB_IN)
    g0, b0 = vec(ln0_g), vec(ln0_b)

    m, wst, cpow, bpad, cpad, avec, w_in_bf = _ssm_prep(ssm_log_dt[0], ssm_a_re[0], ssm_a_im[0], ssm_b_re[0],
                                                        ssm_b_im[0], ssm_c_re[0], ssm_c_im[0], w_in[0])

    prev = jnp.pad(state_conv[0], ((SAMPLE_ROW0, SUB_IN - SAMPLE_ROW0 - N_SAMPLE), (0, 0), (0, 0)))
    h0, u, uc, rc, v_tail, v_extra, wout_bf, wglu_bf = _in_proj(
        xp, xe, g0, b0, w_in_bf, prev[:, 1], prev[:, 0], conv_w[0], vec(g_conv_out[0]),
        w_out[0], ssm_w_glu[0])

    flat = lambda a: a.reshape(N_SAMPLE, g * p)
    yc, ys, s_fin, ns_re, ns_im = _ssm_scan(uc, u, m, wst, cpow, bpad, cpad, avec,
                                            flat(state_ssm_re[0]), flat(state_ssm_im[0]))

    x1, x1b = _mix_out(yc, ys, u, rc, h0, vec(ssm_d[0]), wglu_bf, vec(g_ssm_out[0]),
                       wout_bf, vec(ln1_g[0]), vec(ln1_b[0]))

    act, wd_bf = _ffn_up(x1b, w_gate[0], w_up[0], w_down[0])
    yp, ye = _ffn_down(act, x1, wd_bf, vec(ln2_g[0]), vec(ln2_b[0]))

    y_prompt = yp.reshape(N_BATCH, SEQ, D_MODEL)
    y_sample = ye[SAMPLE_ROW0:SAMPLE_ROW0 + N_SAMPLE].reshape(N_SAMPLE, 1, D_MODEL)

    sd = state_ssm_re.dtype
    s_p = s_fin.transpose(1, 0, 2)
    unflat = lambda a: a.reshape(1, N_SAMPLE, g, p).astype(sd)
    cd = state_conv.dtype
    conv_p = v_tail[TILES_PER_SEQ - 1::TILES_PER_SEQ, ROW8 - (CONV_W - 1):].astype(cd)
    conv_s = jnp.stack([state_conv[0][:, 1], v_extra[SAMPLE_ROW0:SAMPLE_ROW0 + N_SAMPLE].astype(cd)], axis=1)
    return (y_prompt, y_sample,
            s_p[None, ..., :p].astype(sd), s_p[None, ..., p:].astype(sd), conv_p[None],
            unflat(ns_re), unflat(ns_im), conv_s[None])
```

```python
import jax
import jax.numpy as jnp
import numpy as np
from jax import lax
from jax.experimental import pallas as pl
from jax.experimental.pallas import tpu as pltpu

D_MODEL = 2048
D_SSM = 1024
D_CONV = 1024
N_GROUPS = 64
GROUP = 16
N_STATE = 64
N_META = 16
CONV_W = 3
D_FF = 5632
LN_EPS = 1e-5
RMS_EPS = 1e-6
ALPHA = 2.0 ** 0.25

N_BATCH = 4
SEQ = 2048
N_SAMPLE = 128
CHUNK = 16
ROW8 = 8
CW = CHUNK * GROUP
SW = 2 * N_STATE

TM = 512
N_PROMPT_ROWS = N_BATCH * SEQ
N_PROMPT_TILES = N_PROMPT_ROWS // TM
N_TILES = N_PROMPT_TILES + 1
M_PAD = N_TILES * TM
SAMPLE_ROW0 = N_META
GB = SW // GROUP
N_LANE_BLOCKS = D_SSM // SW
TILE_CHUNKS = TM // CHUNK
EXTRA_ROWS = 256
VMEM_LIMIT = 60 * 1024 * 1024

F32 = jnp.float32
BF16 = jnp.bfloat16


def _layer_norm(x, g, b):
    mu = jnp.mean(x, axis=-1, keepdims=True)
    xc = x - mu
    var = jnp.mean(xc * xc, axis=-1, keepdims=True)
    return xc * lax.rsqrt(var + LN_EPS) * g + b


def _rms_norm(x, g):
    return x * lax.rsqrt(jnp.mean(x * x, axis=-1, keepdims=True) + RMS_EPS) * g


def _gelu_tanh(x):
    c = np.sqrt(2.0 / np.pi)
    hx = 0.5 * x
    return hx + hx * jnp.tanh(x * (np.float32(c) + np.float32(c * 0.044715) * (x * x)))


def _sigmoid(x):
    return 1.0 / (1.0 + jnp.exp2(x * np.float32(-np.log2(np.e))))


def _dot(a, b):
    return jnp.dot(a, b, preferred_element_type=F32)


def _dot_nt(a, b):
    return lax.dot_general(a, b, (((1,), (1,)), ((), ())), preferred_element_type=F32)


def _block_transpose(a):
    a = list(a)
    blk = lax.broadcasted_iota(jnp.int32, a[0].shape, 1) // GROUP
    for d in (4, 2, 1):
        w = d * GROUP
        keep = (blk & d) == 0
        for j in range(GB):
            if j & d == 0:
                lo, hi = a[j], a[j + d]
                a[j] = jnp.where(keep, lo, pltpu.roll(hi, w, axis=1))
                a[j + d] = jnp.where(keep, pltpu.roll(lo, SW - w, axis=1), hi)
    return a


def _ssm_prep_kernel(logdt_ref, are_ref, aim_ref, bre_ref, bim_ref, cre_ref, cim_ref, win_ref,
                     m_ref, wst_ref, cpow_ref, bpad_ref, cpad_ref, avec_ref, winb_ref):
    winb_ref[...] = win_ref[...].astype(BF16)
    halves = lambda x, y: jnp.concatenate([x, y], axis=1)

    dt = jnp.exp(jnp.broadcast_to(logdt_ref[...], (GB, SW)))
    a_re = halves(are_ref[...], are_ref[...])
    a_im = halves(aim_ref[...], aim_ref[...])
    mag = jnp.exp(dt * a_re)
    ang = dt * a_im
    ab_re = mag * jnp.cos(ang)
    ab_im = mag * jnp.sin(ang)
    n_re = ab_re - 1.0
    den = a_re * a_re + a_im * a_im
    f_re = (n_re * a_re + ab_im * a_im) / den
    f_im = (ab_im * a_re - n_re * a_im) / den
    sgn = jnp.where(lax.broadcasted_iota(jnp.int32, (GB, SW), 1) < N_STATE, -1.0, 1.0)

    pr, pi = jnp.ones_like(ab_re), jnp.zeros_like(ab_re)
    pw = [(pr, pi)]
    for _ in range(CHUNK):
        pr, pi = pr * ab_re - pi * ab_im, pr * ab_im + pi * ab_re
        pw.append((pr, sgn * pi))
    fx = sgn * f_im
    lane = lax.broadcasted_iota(jnp.int32, (GROUP, CW), 1)

    for gl in range(GB):
        row = lambda x: x[gl:gl + 1, :]
        bt_re, bt_im = bre_ref[gl].T, bim_ref[gl].T
        b1, b2 = halves(bt_re, bt_im), halves(bt_im, bt_re)
        c1, c2 = halves(cre_ref[gl], cim_ref[gl]), halves(cim_ref[gl], cre_ref[gl])
        bb1 = row(f_re) * b1 + row(fx) * b2
        bb2 = row(f_re) * b2 - row(fx) * b1
        cc1 = -row(sgn) * c1
        cc2 = -row(sgn) * c2

        cq = []
        for s in range(CHUNK):
            wr, wx = pw[CHUNK - 1 - s]
            wst_ref[gl, s * GROUP:(s + 1) * GROUP, :] = (row(wr) * bb1 + row(wx) * bb2).astype(BF16)
            qr, qx = pw[s + 1]
            cpow_ref[gl, s * GROUP:(s + 1) * GROUP, :] = (row(qr) * cc1 + row(qx) * cc2).astype(BF16)
            dr, dx = pw[s]
            cq.append(row(dr) * cc1 + row(dx) * cc2)
        kt = lax.dot_general(bb1, jnp.concatenate(cq, axis=0), (((1,), (1,)), ((), ())),
                             precision=lax.Precision.HIGHEST, preferred_element_type=F32)
        for i in range(CHUNK):
            blk = kt if i == 0 else jnp.where(lane >= i * GROUP, pltpu.roll(kt, i * GROUP, axis=1), 0.0)
            m_ref[gl, i * GROUP:(i + 1) * GROUP, :] = blk.astype(BF16)

        zeros = jnp.zeros((SW, SW), BF16)
        bpad_ref[gl] = zeros
        cpad_ref[gl] = zeros
        bpad_ref[gl, gl * GROUP:(gl + 1) * GROUP, :] = bb1.astype(BF16)
        cpad_ref[gl, gl * GROUP:(gl + 1) * GROUP, :] = cq[0].astype(BF16)

        avec_ref[gl, 0:1, :] = row(pw[CHUNK][0])
        avec_ref[gl, 1:2, :] = row(pw[CHUNK][1])
        avec_ref[gl, 2:3, :] = row(pw[1][0])
        avec_ref[gl, 3:4, :] = row(pw[1][1])


def _ssm_prep(log_dt, a_re, a_im, b_re, b_im, c_re, c_im, w_in):
    g, c, p = N_GROUPS, GROUP, N_STATE
    vec = lambda n: pl.BlockSpec((GB, n), lambda i: (i, 0))
    blk = lambda *s: pl.BlockSpec((GB,) + s, lambda i: (i,) + (0,) * len(s))
    slab = pl.BlockSpec((w_in.shape[0] // (g // GB), w_in.shape[1]), lambda i: (i, 0))
    return pl.pallas_call(
        _ssm_prep_kernel,
        grid=(g // GB,),
        in_specs=[vec(1), vec(p), vec(p), blk(p, c), blk(p, c), blk(c, p), blk(c, p), slab],
        out_specs=[blk(CW, CW), blk(CW, SW), blk(CW, SW), blk(SW, SW), blk(SW, SW), blk(4, SW), slab],
        out_shape=[jax.ShapeDtypeStruct((g, CW, CW), BF16),
                   jax.ShapeDtypeStruct((g, CW, SW), BF16),
                   jax.ShapeDtypeStruct((g, CW, SW), BF16),
                   jax.ShapeDtypeStruct((g, SW, SW), BF16),
                   jax.ShapeDtypeStruct((g, SW, SW), BF16),
                   jax.ShapeDtypeStruct((g, 4, SW), F32),
                   jax.ShapeDtypeStruct(w_in.shape, BF16)],
        name="ssm_prep",
    )(log_dt[:, None], a_re, a_im, b_re, b_im, c_re, c_im, w_in)


SUB_IN = EXTRA_ROWS

TILES_PER_SEQ = SEQ // TM


def _in_proj_kernel(xp_ref, xe_ref, g0_ref, b0_ref, w_ref, p1_ref, p2_ref, cw_ref, gc_ref, wo_ref, wgl_ref,
                    h0_ref, u_ref, uc_ref, rc_ref, vt_ref, vs_ref, wob_ref, wglb_ref,
                    vbuf_scr, meta_scr, u_scr):
    s = pl.program_id(0)

    @pl.when(s == 0)
    def _():
        vbuf_scr[0:ROW8, :] = jnp.zeros((ROW8, D_CONV), F32)

    def tile(x_ref, is_extra):
        sub = SUB_IN if is_extra else TM
        rs = slice(0, sub)
        if is_extra:
            pad = slice(sub, TM)
            h0_ref[pad, :] = jnp.zeros((TM - sub, D_MODEL), F32)
            u_ref[pad, :] = jnp.zeros((TM - sub, D_SSM), F32)
            rc_ref[pad, :] = jnp.zeros((TM - sub, D_CONV), BF16)
        h0 = _layer_norm(x_ref[rs, :], g0_ref[...], b0_ref[...])
        h0_ref[rs, :] = h0
        proj = _dot(h0.astype(BF16), w_ref[...])
        part = lambda k: proj[:, k * D_SSM:(k + 1) * D_SSM]
        u_ref[rs, :] = part(0)
        for b in range(N_LANE_BLOCKS):
            u_scr[b, rs, :] = proj[:, b * SW:(b + 1) * SW]
            if is_extra:
                u_scr[b, sub:TM, :] = jnp.zeros((TM - sub, SW), F32)
        for b in range(N_LANE_BLOCKS):
            for h in range(2):
                src = [u_scr[b, pl.ds(GB * h + j, TILE_CHUNKS, stride=CHUNK), :] for j in range(GB)]
                for gl, w in enumerate(_block_transpose(src)):
                    uc_ref[b * GB + gl, :, h * SW:(h + 1) * SW] = w.astype(BF16)
        b_gate = part(1)
        v = part(2) * part(3)
        vbuf_scr[ROW8:ROW8 + sub, :] = v
        prev = vbuf_scr[0:ROW8 + sub, :]
        vm1 = pltpu.roll(prev, 1, axis=0)[ROW8:, :]
        vm2 = pltpu.roll(prev, 2, axis=0)[ROW8:, :]
        if is_extra:
            sample = lax.broadcasted_iota(jnp.int32, (sub, 1), 0) >= SAMPLE_ROW0
            vm1 = jnp.where(sample, p1_ref[...], vm1)
            vm2 = jnp.where(sample, p2_ref[...], vm2)
        conv = cw_ref[0:1, :] * vm2 + cw_ref[1:2, :] * vm1 + cw_ref[2:3, :] * v
        rc_ref[rs, :] = _rms_norm(b_gate * conv, gc_ref[...]).astype(BF16)

    @pl.when(s == 0)
    def _():
        tile(xe_ref, True)
        vs_ref[...] = vbuf_scr[ROW8:ROW8 + SUB_IN, :]
        meta_scr[...] = vbuf_scr[ROW8 + N_META - ROW8:ROW8 + N_META, :]
        vbuf_scr[0:ROW8, :] = meta_scr[...]

    @pl.when(s > 0)
    def _():
        tile(xp_ref, False)
        wob_ref[...] = wo_ref[...].astype(BF16)
        wglb_ref[...] = wgl_ref[...].astype(BF16)
        tail = vbuf_scr[TM:TM + ROW8, :]
        vt_ref[0] = tail
        vbuf_scr[0:ROW8, :] = jnp.where(s % TILES_PER_SEQ == 0, meta_scr[...], tail)


def _in_proj(xp, xe, g0, b0, w_in, p1, p2, cw, gc, w_out, w_glu):
    const = lambda s: (0, 0)
    slab = lambda s: (jnp.maximum(s - 1, 0), 0)
    row = lambda s: ((s + N_PROMPT_TILES) % N_TILES, 0)
    vec = lambda n: pl.BlockSpec((1, n), const)
    return pl.pallas_call(
        _in_proj_kernel,
        grid=(N_TILES,),
        in_specs=[pl.BlockSpec((TM, D_MODEL), lambda s: (jnp.maximum(s - 1, 0), 0)),
                  pl.BlockSpec((SUB_IN, D_MODEL), const),
                  vec(D_MODEL), vec(D_MODEL),
                  pl.BlockSpec((D_MODEL, 4 * D_SSM), const, pipeline_mode=pl.Buffered(1)),
                  pl.BlockSpec((SUB_IN, D_CONV), const), pl.BlockSpec((SUB_IN, D_CONV), const),
                  pl.BlockSpec((CONV_W, D_CONV), const), vec(D_CONV),
                  pl.BlockSpec((D_MODEL // N_PROMPT_TILES, D_MODEL), slab),
                  pl.BlockSpec((D_SSM // N_PROMPT_TILES, D_SSM), slab)],
        out_specs=[pl.BlockSpec((TM, D_MODEL), row), pl.BlockSpec((TM, D_SSM), row),
                   pl.BlockSpec((N_GROUPS, TILE_CHUNKS, CW), lambda s: (0, (s + N_PROMPT_TILES) % N_TILES, 0)),
                   pl.BlockSpec((TM, D_CONV), row),
                   pl.BlockSpec((1, ROW8, D_CONV), lambda s: (jnp.maximum(s - 1, 0), 0, 0)),
                   pl.BlockSpec((SUB_IN, D_CONV), const),
                   pl.BlockSpec((D_MODEL // N_PROMPT_TILES, D_MODEL), slab),
                   pl.BlockSpec((D_SSM // N_PROMPT_TILES, D_SSM), slab)],
        out_shape=[jax.ShapeDtypeStruct((M_PAD, D_MODEL), F32),
                   jax.ShapeDtypeStruct((M_PAD, D_SSM), F32),
                   jax.ShapeDtypeStruct((N_GROUPS, N_TILES * TILE_CHUNKS, CW), BF16),
                   jax.ShapeDtypeStruct((M_PAD, D_CONV), BF16),
                   jax.ShapeDtypeStruct((N_PROMPT_TILES, ROW8, D_CONV), F32),
                   jax.ShapeDtypeStruct((SUB_IN, D_CONV), F32),
                   jax.ShapeDtypeStruct((D_MODEL, D_MODEL), BF16),
                   jax.ShapeDtypeStruct((D_SSM, D_SSM), BF16)],
        scratch_shapes=[pltpu.VMEM((ROW8 + TM, D_CONV), F32), pltpu.VMEM((ROW8, D_CONV), F32),
                        pltpu.VMEM((N_LANE_BLOCKS, TM, SW), F32)],
        compiler_params=pltpu.CompilerParams(
            dimension_semantics=("arbitrary",), vmem_limit_bytes=VMEM_LIMIT),
        name="in_proj",
    )(xp, xe, g0, b0, w_in, p1, p2, cw, gc, w_out, w_glu)


SEQ_CHUNKS = SEQ // CHUNK
LHS_ROWS = N_TILES * TILE_CHUNKS
META_LHS = N_BATCH * SEQ_CHUNKS
PITCH = SEQ_CHUNKS + ROW8
REC_ROWS = N_BATCH * PITCH + ROW8
META_REC = N_BATCH * PITCH


def _ssm_scan_kernel(uc_ref, ue_ref, m_ref, wst_ref, cpow_ref, bpad_ref, cpad_ref, avec_ref, sre_ref, sim_ref,
                     yc_ref, ys_ref, sfin_ref, nre_ref, nim_ref, x_scr, xs_scr, sp_scr):
    def swap_halves(s):
        return pltpu.roll(s, N_STATE, axis=1)

    def to_rec(ref, gl, x):
        for n in range(N_BATCH):
            ref[gl, n * PITCH:n * PITCH + SEQ_CHUNKS, :] = x[n * SEQ_CHUNKS:(n + 1) * SEQ_CHUNKS]
        ref[gl, META_REC:META_REC + ROW8, :] = x[META_LHS:META_LHS + ROW8]

    for gl in range(GB):
        x = _dot(uc_ref[gl], wst_ref[gl])
        to_rec(x_scr, gl, x)
        to_rec(xs_scr, gl, swap_halves(x))

    ar16 = [avec_ref[gl, 0:1, :] for gl in range(GB)]
    ax16 = [avec_ref[gl, 1:2, :] for gl in range(GB)]

    def step(k, carry):
        new = []
        for gl in range(GB):
            s, t = carry[gl]
            rows = pl.ds(k, N_BATCH, stride=PITCH)
            sp_scr[gl, rows, :] = s
            new.append((ar16[gl] * s + ax16[gl] * t + x_scr[gl, rows, :],
                        ar16[gl] * t - ax16[gl] * s + xs_scr[gl, rows, :]))
        return tuple(new)

    first = lambda ref, gl: jnp.broadcast_to(ref[gl, META_REC:META_REC + 1, :], (N_BATCH, SW))
    init = tuple((first(x_scr, gl), first(xs_scr, gl)) for gl in range(GB))
    fin = lax.fori_loop(0, SEQ_CHUNKS, step, init, unroll=4)

    us = ue_ref[SAMPLE_ROW0:SAMPLE_ROW0 + N_SAMPLE, :].astype(BF16)
    ys = jnp.zeros((N_SAMPLE, SW), F32)
    for gl in range(GB):
        sfin_ref[gl] = fin[gl][0]
        sp = jnp.concatenate([sp_scr[gl, n * PITCH:n * PITCH + SEQ_CHUNKS, :] for n in range(N_BATCH)]
                             + [jnp.zeros((LHS_ROWS - META_LHS, SW), F32)], axis=0)
        yc_ref[gl] = _dot(uc_ref[gl], m_ref[gl]) + _dot_nt(sp.astype(BF16), cpow_ref[gl])
        lanes = slice(gl * N_STATE, (gl + 1) * N_STATE)
        s0 = jnp.concatenate([sre_ref[:, lanes], sim_ref[:, lanes]], axis=1)
        sn = (avec_ref[gl, 2:3, :] * s0 + avec_ref[gl, 3:4, :] * swap_halves(s0)
              + _dot(us, bpad_ref[gl]))
        nre_ref[:, lanes] = sn[:, 0:N_STATE]
        nim_ref[:, lanes] = sn[:, N_STATE:SW]
        ys = ys + _dot_nt(sn.astype(BF16), cpad_ref[gl])
    ys_ref[...] = ys


def _ssm_scan(uc, u, m, wst, cpow, bpad, cpad, avec, s_re, s_im):
    blk = lambda *s: pl.BlockSpec((GB,) + s, lambda i: (i,) + (0,) * len(s))
    extra = pl.BlockSpec((EXTRA_ROWS, SW), lambda i: (N_PROMPT_ROWS // EXTRA_ROWS, i))
    st = pl.BlockSpec((N_SAMPLE, GB * N_STATE), lambda i: (0, i))
    st_shape = jax.ShapeDtypeStruct((N_SAMPLE, N_GROUPS * N_STATE), F32)
    return pl.pallas_call(
        _ssm_scan_kernel,
        grid=(N_GROUPS // GB,),
        in_specs=[blk(LHS_ROWS, CW), extra, blk(CW, CW), blk(CW, SW), blk(CW, SW), blk(SW, SW), blk(SW, SW),
                  blk(4, SW), st, st],
        out_specs=[blk(LHS_ROWS, CW), pl.BlockSpec((N_SAMPLE, SW), lambda i: (0, i)), blk(N_BATCH, SW), st, st],
        out_shape=[jax.ShapeDtypeStruct((N_GROUPS, LHS_ROWS, CW), F32),
                   jax.ShapeDtypeStruct((N_SAMPLE, D_SSM), F32),
                   jax.ShapeDtypeStruct((N_GROUPS, N_BATCH, SW), F32), st_shape, st_shape],
        scratch_shapes=[pltpu.VMEM((GB, REC_ROWS, SW), F32), pltpu.VMEM((GB, REC_ROWS, SW), F32),
                        pltpu.VMEM((GB, REC_ROWS, SW), F32)],
        compiler_params=pltpu.CompilerParams(
            dimension_semantics=("arbitrary",), vmem_limit_bytes=VMEM_LIMIT),
        name="ssm_scan",
    )(uc, u, m, wst, cpow, bpad, cpad, avec, s_re, s_im)


SUB_MIX = 512


def _mix_out_kernel(yc0_ref, ycn_ref, ys_ref, u_ref, rc_ref, h0_ref, dskip_ref, wglu_ref, gs_ref, wout_ref,
                    g1_ref, b1_ref, x1_ref, x1b_ref, merged_scr, y_scr, ynext_scr):
    i = pl.program_id(0)

    def to_rows(yc_ref, dst_scr):
        for b in range(N_LANE_BLOCKS):
            for h in range(2):
                src = [yc_ref[b * GB + gl, :, h * SW:(h + 1) * SW] for gl in range(GB)]
                for j, w in enumerate(_block_transpose(src)):
                    dst_scr[b, pl.ds(GB * h + j, TILE_CHUNKS, stride=CHUNK), :] = w

    def sub_tile(r0, n):
        rs = slice(r0, r0 + n)
        y = jnp.concatenate([y_scr[b, rs, :] for b in range(N_LANE_BLOCKS)], axis=1)
        ys = _gelu_tanh(y + dskip_ref[...] * u_ref[rs, :])
        gate = _dot(ys.astype(BF16), wglu_ref[...])
        merged_scr[rs, 0:D_SSM] = _rms_norm(ys * _sigmoid(gate), gs_ref[...]).astype(BF16)
        merged_scr[rs, D_SSM:D_MODEL] = rc_ref[rs, :]
        mo = _dot(merged_scr[rs, :], wout_ref[...])
        x1 = _layer_norm(ALPHA * h0_ref[rs, :] + mo, g1_ref[...], b1_ref[...])
        x1_ref[rs, :] = x1
        x1b_ref[rs, :] = x1.astype(BF16)

    @pl.when(i == 0)
    def _():
        to_rows(yc0_ref, y_scr)

    @pl.when(i < N_PROMPT_TILES)
    def _():
        for r0 in range(0, TM, SUB_MIX):
            sub_tile(r0, SUB_MIX)
        to_rows(ycn_ref, ynext_scr)
        y_scr[...] = ynext_scr[...]

    @pl.when(i == N_PROMPT_TILES - 1)
    def _():
        for b in range(N_LANE_BLOCKS):
            y_scr[b, SAMPLE_ROW0:SAMPLE_ROW0 + N_SAMPLE, :] = ys_ref[:, b * SW:(b + 1) * SW]

    @pl.when(i == N_PROMPT_TILES)
    def _():
        sub_tile(0, EXTRA_ROWS)
        x1_ref[EXTRA_ROWS:TM, :] = jnp.zeros((TM - EXTRA_ROWS, D_MODEL), F32)
        x1b_ref[EXTRA_ROWS:TM, :] = jnp.zeros((TM - EXTRA_ROWS, D_MODEL), BF16)


def _mix_out(yc, ys, u, rc, h0, dskip, wglu, gs, wout, g1, b1):
    row = lambda i: (i, 0)
    const = lambda i: (0, 0)
    vec = lambda n: pl.BlockSpec((1, n), const)
    chunk_blk = lambda f: pl.BlockSpec((N_GROUPS, TILE_CHUNKS, CW), f)
    return pl.pallas_call(
        _mix_out_kernel,
        grid=(N_TILES,),
        in_specs=[chunk_blk(lambda i: (0, 0, 0)),
                  chunk_blk(lambda i: (0, jnp.minimum(i + 1, N_PROMPT_TILES), 0)),
                  pl.BlockSpec((N_SAMPLE, D_SSM), const), pl.BlockSpec((TM, D_SSM), row),
                  pl.BlockSpec((TM, D_CONV), row), pl.BlockSpec((TM, D_MODEL), row), vec(D_SSM),
                  pl.BlockSpec((D_SSM, D_SSM), const, pipeline_mode=pl.Buffered(1)),
                  vec(D_SSM),
                  pl.BlockSpec((D_MODEL, D_MODEL), const, pipeline_mode=pl.Buffered(1)),
                  vec(D_MODEL), vec(D_MODEL)],
        out_specs=[pl.BlockSpec((TM, D_MODEL), row), pl.BlockSpec((TM, D_MODEL), row)],
        out_shape=[jax.ShapeDtypeStruct((M_PAD, D_MODEL), F32),
                   jax.ShapeDtypeStruct((M_PAD, D_MODEL), BF16)],
        scratch_shapes=[pltpu.VMEM((TM, D_MODEL), BF16), pltpu.VMEM((N_LANE_BLOCKS, TM, SW), F32),
                        pltpu.VMEM((N_LANE_BLOCKS, TM, SW), F32)],
        compiler_params=pltpu.CompilerParams(
            dimension_semantics=("arbitrary",), vmem_limit_bytes=VMEM_LIMIT),
        name="mix_out",
    )(yc, yc, ys, u, rc, h0, dskip, wglu, gs, wout, g1, b1)


M_HALF = M_PAD // 2
SUB_UP = M_HALF // 4
TF_UP = 256
SUB_DOWN = 256
M_REAL_HALF = M_HALF - (TM - EXTRA_ROWS)


def _ffn_up_kernel(xb_ref, wg_ref, wu_ref, wd_ref, act_ref, wdb_ref):
    @pl.when(pl.program_id(0) == 0)
    def _():
        wdb_ref[...] = wd_ref[...].astype(BF16)

    wgb = wg_ref[...].astype(BF16)
    wub = wu_ref[...].astype(BF16)

    def rows(lo, hi):
        xb = xb_ref[lo:hi, :]
        gate = _dot(xb, wgb)
        up = _dot(xb, wub)
        act_ref[lo:hi, :] = (gate * _sigmoid(gate) * up).astype(BF16)

    @pl.when(pl.program_id(0) == 0)
    def _():
        for r0 in range(0, M_HALF, SUB_UP):
            rows(r0, r0 + SUB_UP)

    @pl.when(pl.program_id(0) == 1)
    def _():
        for r0 in range(0, M_REAL_HALF, SUB_UP):
            rows(r0, min(r0 + SUB_UP, M_REAL_HALF))
        act_ref[M_REAL_HALF:M_HALF, :] = jnp.zeros((M_HALF - M_REAL_HALF, TF_UP), BF16)


def _ffn_up(x1b, wg, wu, wd):
    wcol = pl.BlockSpec((D_MODEL, TF_UP), lambda h, f: (0, f))
    wrow = pl.BlockSpec((TF_UP, D_MODEL), lambda h, f: (jnp.where(h == 0, f, D_FF // TF_UP - 1), 0))
    return pl.pallas_call(
        _ffn_up_kernel,
        grid=(M_PAD // M_HALF, D_FF // TF_UP),
        in_specs=[pl.BlockSpec((M_HALF, D_MODEL), lambda h, f: (h, 0), pipeline_mode=pl.Buffered(1)),
                  wcol, wcol, wrow],
        out_specs=[pl.BlockSpec((M_HALF, TF_UP), lambda h, f: (h, f)), wrow],
        out_shape=[jax.ShapeDtypeStruct((M_PAD, D_FF), BF16),
                   jax.ShapeDtypeStruct((D_FF, D_MODEL), BF16)],
        compiler_params=pltpu.CompilerParams(
            dimension_semantics=("arbitrary", "arbitrary"), vmem_limit_bytes=VMEM_LIMIT),
        name="ffn_up",
    )(x1b, wg, wu, wd)


def _ffn_down_kernel(act_ref, x_ref, wd_ref, g2_ref, b2_ref, yp_ref, ye_ref):
    i = pl.program_id(0)

    def tile(o_ref, rows):
        for r0 in range(0, rows, SUB_DOWN):
            rs = slice(r0, r0 + SUB_DOWN)
            z = ALPHA * x_ref[rs, :] + _dot(act_ref[rs, :], wd_ref[...])
            o_ref[rs, :] = _layer_norm(z, g2_ref[...], b2_ref[...])

    @pl.when(i < N_PROMPT_TILES)
    def _():
        tile(yp_ref, TM)

    @pl.when(i == N_PROMPT_TILES)
    def _():
        tile(ye_ref, EXTRA_ROWS)


def _ffn_down(act, x1, wd, g2, b2):
    row = lambda i: (i, 0)
    const = lambda i: (0, 0)
    return pl.pallas_call(
        _ffn_down_kernel,
        grid=(N_TILES,),
        in_specs=[pl.BlockSpec((TM, D_FF), row), pl.BlockSpec((TM, D_MODEL), row),
                  pl.BlockSpec((D_FF, D_MODEL), const, pipeline_mode=pl.Buffered(1)),
                  pl.BlockSpec((1, D_MODEL), const), pl.BlockSpec((1, D_MODEL), const)],
        out_specs=[pl.BlockSpec((TM, D_MODEL), lambda i: (jnp.minimum(i, N_PROMPT_TILES - 1), 0)),
                   pl.BlockSpec((EXTRA_ROWS, D_MODEL), const)],
        out_shape=[jax.ShapeDtypeStruct((N_PROMPT_ROWS, D_MODEL), F32),
                   jax.ShapeDtypeStruct((EXTRA_ROWS, D_MODEL), F32)],
        compiler_params=pltpu.CompilerParams(
            dimension_semantics=("arbitrary",), vmem_limit_bytes=VMEM_LIMIT),
        name="ffn_down",
    )(act, x1, wd, g2, b2)


def _pad_rows(a, n):
    return jnp.pad(a, ((0, n - a.shape[0]), (0, 0)))


def kernel(x_prompt, x_sample, state_ssm_re, state_ssm_im, state_conv, meta_tokens, ln0_g, ln0_b,
           w_in, ssm_a_re, ssm_a_im, ssm_log_dt, ssm_b_re, ssm_b_im, ssm_c_re, ssm_c_im, ssm_d,
           ssm_w_glu, conv_w, g_ssm_out, g_conv_out, w_out, ln1_g, ln1_b, w_gate, w_up, w_down,
           ln2_g, ln2_b):
    assert x_prompt.shape == (N_BATCH, SEQ, D_MODEL) and x_sample.shape == (N_SAMPLE, 1, D_MODEL)
    assert w_in.shape[0] == 1, "single layer"
    g, p, c = N_GROUPS, N_STATE, GROUP
    vec = lambda a: a.reshape(1, -1)

    xp = x_prompt.reshape(N_PROMPT_ROWS, D_MODEL)
    xe = _pad_rows(jnp.concatenate([meta_tokens, x_sample[:, 0, :]], axis=0), SUB_IN)
    g0, b0 = vec(ln0_g), vec(ln0_b)

    m, wst, cpow, bpad, cpad, avec, w_in_bf = _ssm_prep(ssm_log_dt[0], ssm_a_re[0], ssm_a_im[0], ssm_b_re[0],
                                                        ssm_b_im[0], ssm_c_re[0], ssm_c_im[0], w_in[0])

    prev = jnp.pad(state_conv[0], ((SAMPLE_ROW0, SUB_IN - SAMPLE_ROW0 - N_SAMPLE), (0, 0), (0, 0)))
    h0, u, uc, rc, v_tail, v_extra, wout_bf, wglu_bf = _in_proj(
        xp, xe, g0, b0, w_in_bf, prev[:, 1], prev[:, 0], conv_w[0], vec(g_conv_out[0]),
        w_out[0], ssm_w_glu[0])

    flat = lambda a: a.reshape(N_SAMPLE, g * p)
    yc, ys, s_fin, ns_re, ns_im = _ssm_scan(uc, u, m, wst, cpow, bpad, cpad, avec,
                                            flat(state_ssm_re[0]), flat(state_ssm_im[0]))

    x1, x1b = _mix_out(yc, ys, u, rc, h0, vec(ssm_d[0]), wglu_bf, vec(g_ssm_out[0]),
                       wout_bf, vec(ln1_g[0]), vec(ln1_b[0]))

    act, wd_bf = _ffn_up(x1b, w_gate[0], w_up[0], w_down[0])
    yp, ye = _ffn_down(act, x1, wd_bf, vec(ln2_g[0]), vec(ln2_b[0]))

    y_prompt = yp.reshape(N_BATCH, SEQ, D_MODEL)
    y_sample = ye[SAMPLE_ROW0:SAMPLE_ROW0 + N_SAMPLE].reshape(N_SAMPLE, 1, D_MODEL)

    sd = state_ssm_re.dtype
    s_p = s_fin.transpose(1, 0, 2)
    unflat = lambda a: a.reshape(1, N_SAMPLE, g, p).astype(sd)
    cd = state_conv.dtype
    conv_p = v_tail[TILES_PER_SEQ - 1::TILES_PER_SEQ, ROW8 - (CONV_W - 1):].astype(cd)
    conv_s = jnp.stack([state_conv[0][:, 1], v_extra[SAMPLE_ROW0:SAMPLE_ROW0 + N_SAMPLE].astype(cd)], axis=1)
    return (y_prompt, y_sample,
            s_p[None, ..., :p].astype(sd), s_p[None, ..., p:].astype(sd), conv_p[None],
            unflat(ns_re), unflat(ns_im), conv_s[None])
```
